```python
import functools
import jax, jax.numpy as jnp
from jax import lax
import numpy as np

D_MODEL = 2048
BATCH = 2
SEQ = 4096
DEPTH = 4
DEC_BATCH = 32
DEC_SEQ = 1
PAST_LEN = 16384
PAGE_SIZE = 128

MIX = D_MODEL
A_HD = 128
A_HEADS = (MIX // 2) // A_HD
A_KV = A_HEADS // 4
A_GROUP = A_HEADS // A_KV
WINDOW = 128
B_DV = 128
B_DK = B_DV // 2
B_HEADS = (MIX // 4) // B_DV
C_DK = 128
C_DV = 128
C_HEADS = (MIX // 4) // C_DV
CHUNK = 64
ROPE_BASE = 10000.0
N_GROUPS = 4
EXP_PER_GROUP = 4
N_EXPERTS = N_GROUPS * EXP_PER_GROUP
TOP_K = 2
D_EXPERT = D_MODEL // 4
EPS = 1e-6
NEG_BIG = -1e30
IN_COLS = (A_HEADS * A_HD, A_KV * A_HD, A_KV * A_HD,
           B_HEADS * B_DK, B_HEADS * B_DK, B_HEADS * B_DV, B_HEADS * B_DV,
           C_HEADS * C_DK, C_HEADS * C_DK, C_HEADS * C_DV, C_HEADS * C_DV)
IN_WIDTH = sum(IN_COLS)

kernel_name = 'hymba_swa_retnet_hgrn2_hmoe_step'


def rms_norm(x, g):
    xf = x.astype(jnp.float32)
    y = xf * lax.rsqrt(jnp.mean(xf * xf, axis=-1, keepdims=True) + EPS)
    return (y * g.astype(jnp.float32)).astype(x.dtype)


def split_cols(p):
    idx, acc = [], 0
    for w in IN_COLS[:-1]:
        acc += w
        idx.append(acc)
    return jnp.split(p, idx, axis=-1)


def rope(x, pos):
    half = x.shape[-1] // 2
    inv = ROPE_BASE ** (-jnp.arange(half, dtype=jnp.float32) / half)
    ang = pos.astype(jnp.float32)[:, None] * inv[None, :]
    cos = jnp.cos(ang)[None, :, None, :]
    sin = jnp.sin(ang)[None, :, None, :]
    xf = x.astype(jnp.float32)
    x1, x2 = xf[..., :half], xf[..., half:]
    return jnp.concatenate([x1 * cos - x2 * sin, x1 * sin + x2 * cos], axis=-1)


def sink_attend(s, sinks, v, eq):
    snk = sinks.astype(jnp.float32).reshape(A_KV, A_GROUP, 1, 1)
    m = jnp.maximum(jnp.max(s, axis=-1, keepdims=True), snk)
    p = jnp.exp(s - m)
    p = p / (jnp.sum(p, axis=-1, keepdims=True) + jnp.exp(snk - m))
    return jnp.einsum(eq, p.astype(v.dtype), v)


def swa_prompt(buf, q, k, v, sinks):
    Bn, T = q.shape[:2]
    nb = T // WINDOW
    qb = q.reshape(Bn, nb, WINDOW, A_KV, A_GROUP, A_HD)
    kb = k.reshape(Bn, nb, WINDOW, A_KV, A_HD)
    vb = v.reshape(Bn, nb, WINDOW, A_KV, A_HD)

    def prev_block(t):
        return jnp.concatenate([jnp.zeros_like(t[:, :1]), t[:, :-1]], axis=1)

    kband = jnp.concatenate([prev_block(kb), kb], axis=2)
    vband = jnp.concatenate([prev_block(vb), vb], axis=2)
    s = jnp.einsum('bnqkgd,bnskd->bnkgqs', qb, kband,
                   preferred_element_type=jnp.float32) * (A_HD ** -0.5)
    qi = jnp.arange(WINDOW)[:, None]
    kj = jnp.arange(2 * WINDOW)[None, :] - WINDOW
    rel = kj - qi
    band = (rel <= 0) & (rel >= -WINDOW)
    blk = jnp.arange(nb)[:, None, None]
    mask = band[None] & ((blk > 0) | (kj[None] >= 0))
    s = jnp.where(mask[None, :, None, None], s, NEG_BIG)
    o = sink_attend(s, sinks, vband, 'bnkgqs,bnskd->bnqkgd')
    return o.reshape(Bn, T, A_HEADS * A_HD), k[:, T - buf:], v[:, T - buf:]


def swa_sample(buf_k, buf_v, q, k, v, sinks):
    Bn, Ts = q.shape[:2]
    L = buf_k.shape[1]
    kk = jnp.concatenate([buf_k.astype(k.dtype), k], axis=1)
    vv = jnp.concatenate([buf_v.astype(v.dtype), v], axis=1)
    qpos = PAST_LEN + jnp.arange(Ts)
    kpos = jnp.concatenate([PAST_LEN - L + jnp.arange(L), qpos])
    mask = (kpos[None, :] <= qpos[:, None]) & (kpos[None, :] >= qpos[:, None] - WINDOW)
    qg = q.reshape(Bn, Ts, A_KV, A_GROUP, A_HD)
    s = jnp.einsum('bqkgd,bskd->bkgqs', qg, kk,
                   preferred_element_type=jnp.float32) * (A_HD ** -0.5)
    s = jnp.where(mask, s, NEG_BIG)
    o = sink_attend(s, sinks, vv, 'bkgqs,bskd->bqkgd')
    return o.reshape(Bn, Ts, A_HEADS * A_HD), kk[:, -L:], vv[:, -L:]


def retention_chunk(S, q, k, v):
    f32 = jnp.float32
    C = q.shape[1]
    lg = jnp.log1p(-(2.0 ** (-5.0 - jnp.arange(B_HEADS, dtype=f32))))
    idx = jnp.arange(C, dtype=f32)
    diff = idx[:, None] - idx[None, :]
    dmat = jnp.where(diff >= 0, jnp.exp(jnp.maximum(diff, 0.0)[None] * lg[:, None, None]), 0.0)
    a = jnp.einsum('bihd,bjhd->bhij', q, k) * dmat[None]
    o = jnp.einsum('bhij,bjhv->bihv', a, v)
    o = o + jnp.einsum('bihd,bhdv->bihv', q, S) * jnp.exp((idx + 1.0)[:, None] * lg[None])[None, :, :, None]
    kdec = k * jnp.exp((C - 1.0 - idx)[:, None] * lg[None])[None, :, :, None]
    S_new = jnp.exp(C * lg)[None, :, None, None] * S + jnp.einsum('bjhd,bjhv->bhdv', kdec, v)
    return S_new, o


def hgrn_chunk(S, q, k, logf, v):
    C = q.shape[1]
    b = jnp.cumsum(logf, axis=1)
    causal = jnp.tril(jnp.ones((C, C), dtype=bool))[None, :, :, None, None]
    diff = b[:, :, None] - b[:, None, :]
    dec = jnp.where(causal, jnp.exp(jnp.minimum(diff, 0.0)), 0.0)
    a = jnp.einsum('bihd,bjhd,bijhd->bhij', q, k, dec)
    o = jnp.einsum('bhij,bjhv->bihv', a, v)
    o = o + jnp.einsum('bihd,bhdv->bihv', q * jnp.exp(b), S)
    blast = b[:, -1]
    kdec = k * jnp.exp(blast[:, None] - b)
    S_new = jnp.exp(blast)[..., None] * S + jnp.einsum('bjhd,bjhv->bhdv', kdec, v)
    return S_new, o


def chunked_scan(fn, S0, *xs):
    Bn, T = xs[0].shape[:2]
    n = T // CHUNK
    xc = tuple(t.reshape(Bn, n, CHUNK, *t.shape[2:]).swapaxes(0, 1) for t in xs)
    S, o = lax.scan(lambda s, c: fn(s, *c), S0, xc)
    return S, o.swapaxes(0, 1).reshape(Bn, T, *o.shape[3:])


def single_chunk(fn, S0, *xs):
    return fn(S0, *xs)


def hier_moe(h, w_rg, b_rg, w_re, b_re, w1, w3, w2):
    Bn, T, D = h.shape
    t = h.reshape(Bn * T, D)
    g_logits = (t @ w_rg + b_rg).astype(jnp.float32)
    g_prob = jax.nn.softmax(g_logits, axis=-1)
    g_sel = jnp.argmax(g_logits, axis=-1)
    g_w = jnp.take_along_axis(g_prob, g_sel[:, None], axis=-1)
    e_logits = (t @ w_re + b_re).astype(jnp.float32).reshape(-1, N_GROUPS, EXP_PER_GROUP)
    e_in = jnp.take_along_axis(e_logits, g_sel[:, None, None], axis=1)[:, 0]
    top_v, top_i = lax.top_k(e_in, TOP_K)
    top_w = jax.nn.softmax(top_v, axis=-1) * g_w
    eid = g_sel[:, None] * EXP_PER_GROUP + top_i
    gate = jnp.sum(jax.nn.one_hot(eid, N_EXPERTS, dtype=jnp.float32) * top_w[..., None], axis=1)
    hid = jax.nn.silu(jnp.einsum('nd,edf->nef', t, w1)) * jnp.einsum('nd,edf->nef', t, w3)
    hid = hid * gate[..., None].astype(hid.dtype)
    y = jnp.einsum('nef,efd->nd', hid, w2)
    return y.reshape(Bn, T, D)


def layer_forward(x, c, pos, swa_fn, rec_fn, s_ret0, s_hg0, lb,
                  w_ada, b_ada, ln1_g, w_in, qn_g, kn_g, sinks, ret_gn_g, ret_gn_b,
                  hgrn_ng, w_out, ln2_g, w_rg, b_rg, w_re, b_re, w1, w3, w2):
    f32 = jnp.float32
    Bn, T, _ = x.shape
    mod = jax.nn.silu(c) @ w_ada + b_ada
    sh1, sc1, g1, sh2, sc2, g2 = jnp.split(mod[:, None, :], 6, axis=-1)
    h = rms_norm(x, ln1_g) * (1 + sc1) + sh1
    aq, ak, av, bq, bk, bv, bg, cq, cf, ci, cg = split_cols(h @ w_in)
    aq = rms_norm(aq.reshape(Bn, T, A_HEADS, A_HD), qn_g)
    ak = rms_norm(ak.reshape(Bn, T, A_KV, A_HD), kn_g)
    av = av.reshape(Bn, T, A_KV, A_HD)
    oa, new_k, new_v = swa_fn(aq, ak, av, sinks)
    bqr = rope(bq.reshape(Bn, T, B_HEADS, B_DK), pos)
    bkr = rope(bk.reshape(Bn, T, B_HEADS, B_DK), pos) * (B_DK ** -0.5)
    s_ret, ob = rec_fn(retention_chunk, s_ret0.astype(f32), bqr, bkr,
                       bv.reshape(Bn, T, B_HEADS, B_DV).astype(f32))
    z = cf.reshape(Bn, T, C_HEADS, C_DK).astype(f32)
    lbh = lb.reshape(C_HEADS, C_DK)
    logf = jnp.log(lbh + (1.0 - lbh) * jax.nn.sigmoid(z))
    ck = (1.0 - lbh) * jax.nn.sigmoid(-z)
    cqf = jax.nn.silu(cq.reshape(Bn, T, C_HEADS, C_DK).astype(f32))
    s_hg, oc = rec_fn(hgrn_chunk, s_hg0.astype(f32), cqf, ck, logf,
                      ci.reshape(Bn, T, C_HEADS, C_DV).astype(f32))
    mu = jnp.mean(ob, axis=-1, keepdims=True)
    var = jnp.mean(jnp.square(ob - mu), axis=-1, keepdims=True)
    ob = (ob - mu) * lax.rsqrt(var + EPS) * ret_gn_g.astype(f32) + ret_gn_b.astype(f32)
    ob = ob * jax.nn.silu(bg.reshape(Bn, T, B_HEADS, B_DV).astype(f32))
    oc = oc * lax.rsqrt(jnp.mean(oc * oc, axis=-1, keepdims=True) + EPS) * hgrn_ng.astype(f32)
    oc = oc * jax.nn.silu(cg.reshape(Bn, T, C_HEADS, C_DV).astype(f32))
    cat = jnp.concatenate([oa.astype(x.dtype),
                           ob.reshape(Bn, T, B_HEADS * B_DV).astype(x.dtype),
                           oc.reshape(Bn, T, C_HEADS * C_DV).astype(x.dtype)], axis=-1)
    x = x + g1 * (cat @ w_out)
    h2 = rms_norm(x, ln2_g) * (1 + sc2) + sh2
    x = x + g2 * hier_moe(h2, w_rg, b_rg, w_re, b_re, w1, w3, w2)
    return x, new_k, new_v, s_ret, s_hg


def setup_inputs(seed: int = 0) -> dict:
    key = jax.random.key(seed)
    keys = iter(jax.random.split(key, 40))

    def nrm(shape, scale=1.0):
        return jax.random.normal(next(keys), shape, jnp.float32) * scale

    def gain(shape):
        return 1.0 + nrm(shape, 0.02)

    win_buf = min(WINDOW, PAST_LEN)
    return {
        'x_prompt': nrm((BATCH, SEQ, D_MODEL)),
        'x_sample': nrm((DEC_BATCH, DEC_SEQ, D_MODEL)),
        'c_prompt': nrm((BATCH, D_MODEL)),
        'c_sample': nrm((DEC_BATCH, D_MODEL)),
        'cache_swa_k': nrm((DEPTH, DEC_BATCH, win_buf, A_KV, A_HD)),
        'cache_swa_v': nrm((DEPTH, DEC_BATCH, win_buf, A_KV, A_HD)),
        'state_ret': nrm((DEPTH, DEC_BATCH, B_HEADS, B_DK, B_DV), 0.5),
        'state_hgrn': nrm((DEPTH, DEC_BATCH, C_HEADS, C_DK, C_DV), 0.5),
        'w_ada': nrm((DEPTH, D_MODEL, 6 * D_MODEL), 0.5 * D_MODEL ** -0.5),
        'b_ada': nrm((DEPTH, 6 * D_MODEL), 0.02),
        'ln1_g': gain((DEPTH, D_MODEL)),
        'w_in': nrm((DEPTH, D_MODEL, IN_WIDTH), D_MODEL ** -0.5),
        'qn_g': gain((DEPTH, A_HD)),
        'kn_g': gain((DEPTH, A_HD)),
        'attn_sinks': nrm((DEPTH, A_HEADS)),
        'ret_gn_g': gain((DEPTH, B_HEADS, B_DV)),
        'ret_gn_b': nrm((DEPTH, B_HEADS, B_DV), 0.02),
        'hgrn_lb': nrm((DEPTH, C_HEADS * C_DK), 0.5),
        'hgrn_ng': gain((DEPTH, C_DV)),
        'w_out': nrm((DEPTH, MIX, D_MODEL), MIX ** -0.5),
        'ln2_g': gain((DEPTH, D_MODEL)),
        'w_rg': nrm((DEPTH, D_MODEL, N_GROUPS), D_MODEL ** -0.5),
        'b_rg': nrm((DEPTH, N_GROUPS), 0.01),
        'w_re': nrm((DEPTH, D_MODEL, N_EXPERTS), D_MODEL ** -0.5),
        'b_re': nrm((DEPTH, N_EXPERTS), 0.01),
        'w1': nrm((DEPTH, N_EXPERTS, D_MODEL, D_EXPERT), D_MODEL ** -0.5),
        'w3': nrm((DEPTH, N_EXPERTS, D_MODEL, D_EXPERT), D_MODEL ** -0.5),
        'w2': nrm((DEPTH, N_EXPERTS, D_EXPERT, D_MODEL), D_EXPERT ** -0.5),
    }


def reference(x_prompt, x_sample, c_prompt, c_sample, cache_swa_k, cache_swa_v, state_ret, state_hgrn,
              w_ada, b_ada, ln1_g, w_in, qn_g, kn_g, attn_sinks, ret_gn_g, ret_gn_b, hgrn_lb, hgrn_ng,
              w_out, ln2_g, w_rg, b_rg, w_re, b_re, w1, w3, w2):
    f32 = jnp.float32
    lbp = jax.nn.softmax(hgrn_lb.astype(f32), axis=0)
    lb_all = jnp.cumsum(lbp, axis=0) - lbp[0]
    win_buf = cache_swa_k.shape[2]
    Bp, T = x_prompt.shape[:2]
    Ts = x_sample.shape[1]
    pos_p = jnp.arange(T)
    pos_s = PAST_LEN + jnp.arange(Ts)
    swa_p = functools.partial(swa_prompt, win_buf)
    xp, xs = x_prompt, x_sample
    kp_l, vp_l, rp_l, hp_l = [], [], [], []
    ks_l, vs_l, rs_l, hs_l = [], [], [], []
    for l in range(DEPTH):
        w = (w_ada[l], b_ada[l], ln1_g[l], w_in[l], qn_g[l], kn_g[l], attn_sinks[l],
             ret_gn_g[l], ret_gn_b[l], hgrn_ng[l], w_out[l], ln2_g[l],
             w_rg[l], b_rg[l], w_re[l], b_re[l], w1[l], w3[l], w2[l])
        xp, kp, vp, rp, hp = layer_forward(
            xp, c_prompt, pos_p, swa_p, chunked_scan,
            jnp.zeros((Bp, B_HEADS, B_DK, B_DV), f32), jnp.zeros((Bp, C_HEADS, C_DK, C_DV), f32),
            lb_all[l], *w)
        swa_s = functools.partial(swa_sample, cache_swa_k[l], cache_swa_v[l])
        xs, ks, vs, rs, hs = layer_forward(
            xs, c_sample, pos_s, swa_s, single_chunk, state_ret[l], state_hgrn[l], lb_all[l], *w)
        kp_l.append(kp); vp_l.append(vp); rp_l.append(rp); hp_l.append(hp)
        ks_l.append(ks); vs_l.append(vs); rs_l.append(rs); hs_l.append(hs)
    return (xp, xs,
            jnp.stack(kp_l), jnp.stack(vp_l), jnp.stack(rp_l), jnp.stack(hp_l),
            jnp.stack(ks_l), jnp.stack(vs_l), jnp.stack(rs_l), jnp.stack(hs_l))
```

```python
import functools

import numpy as np
import jax
import jax.numpy as jnp
from jax import lax
from jax.experimental import pallas as pl
from jax.experimental.pallas import tpu as pltpu

F32 = jnp.float32
BF16 = jnp.bfloat16

D_MODEL = 2048
BATCH = 2
SEQ = 4096
DEPTH = 4
DEC_BATCH = 32
PAST_LEN = 16384
A_HD = 128
A_HEADS = 8
A_KV = 2
A_GROUP = 4
WINDOW = 128
B_DV = 128
B_DK = 64
B_HEADS = 4
C_DK = 128
C_DV = 128
C_HEADS = 4
ROPE_BASE = 10000.0
N_GROUPS = 4
EXP_PER_GROUP = 4
N_EXPERTS = 16
D_EXPERT = 512
EPS = 1e-6
NEG_BIG = -1e30
IN_WIDTH = 5120

SROWS = 16
NP_ROWS = BATCH * SEQ
NS_ROWS = DEC_BATCH * SROWS
N_ROWS = NP_ROWS + NS_ROWS
TM = 512
N_TILES = N_ROWS // TM
MOD_ROWS = 16
N_TOK = NP_ROWS + DEC_BATCH
MOE_TILES = (N_TOK + N_GROUPS * (TM - 1)) // TM
MOE_ROWS = MOE_TILES * TM
HID_WORK = MOE_TILES * EXP_PER_GROUP
DOWN_TN = 1024
DOWN_NCOL = D_MODEL // DOWN_TN
DOWN_WORK = MOE_TILES * DOWN_NCOL

RET_CHUNK = 256
HG_TILE = 256
HG_BLK = 16

VMEM_LIMIT = 48 * 1024 * 1024


def _cparams(sem):
    return pltpu.CompilerParams(dimension_semantics=sem, vmem_limit_bytes=VMEM_LIMIT)


def _sigmoid(x):
    return 1.0 / (1.0 + jnp.exp(-x))


def _silu(x):
    return x * _sigmoid(x)


def _expand_rows(m, rows):
    g, n = m.shape
    return jnp.broadcast_to(m[:, None, :], (g, MOD_ROWS, n)).reshape(rows, n)


def _ada_kernel(c_ref, w_ref, b_ref, o_ref):
    c = _silu(c_ref[...])
    o_ref[...] = jnp.dot(c.astype(BF16), w_ref[...].astype(BF16),
                         preferred_element_type=F32) + b_ref[...]


def _ada(c_all, w_ada, b_ada):
    rows = c_all.shape[0]
    tn = 1024
    ncol = w_ada.shape[2] // tn
    return pl.pallas_call(
        _ada_kernel,
        grid=(DEPTH, ncol),
        in_specs=[
            pl.BlockSpec((rows, D_MODEL), lambda l, j: (0, 0)),
            pl.BlockSpec((None, D_MODEL, tn), lambda l, j: (l, 0, j)),
            pl.BlockSpec((None, 1, tn), lambda l, j: (l, 0, j)),
        ],
        out_specs=pl.BlockSpec((None, rows, tn), lambda l, j: (l, 0, j)),
        out_shape=jax.ShapeDtypeStruct((DEPTH, rows, w_ada.shape[2]), F32),
        compiler_params=_cparams(("arbitrary", "arbitrary")),
        name="ada",
    )(c_all, w_ada, b_ada.reshape(DEPTH, 1, -1))


def _norm_kernel(x_ref, g_ref, sc_ref, sh_ref, o_ref):
    x = x_ref[...]
    y = x * lax.rsqrt(jnp.mean(x * x, axis=-1, keepdims=True) + EPS) * g_ref[...]
    rows = x.shape[0]
    h = y * (1.0 + _expand_rows(sc_ref[...], rows)) + _expand_rows(sh_ref[...], rows)
    o_ref[...] = h.astype(o_ref.dtype)


def _norm_mod(x, g, scg, shg):
    mg = TM // MOD_ROWS
    return pl.pallas_call(
        _norm_kernel,
        grid=(N_TILES,),
        in_specs=[
            pl.BlockSpec((TM, D_MODEL), lambda i: (i, 0)),
            pl.BlockSpec((1, D_MODEL), lambda i: (0, 0)),
            pl.BlockSpec((mg, D_MODEL), lambda i: (i, 0)),
            pl.BlockSpec((mg, D_MODEL), lambda i: (i, 0)),
        ],
        out_specs=pl.BlockSpec((TM, D_MODEL), lambda i: (i, 0)),
        out_shape=jax.ShapeDtypeStruct((N_ROWS, D_MODEL), BF16),
        compiler_params=_cparams(("arbitrary",)),
        name="norm_mod",
    )(x, g.reshape(1, D_MODEL), scg, shg)


def _mm_in_kernel(x_ref, w_ref, o_ref, wb_ref):
    @pl.when(pl.program_id(1) == 0)
    def _():
        wb_ref[...] = w_ref[...].astype(BF16)

    o_ref[...] = jnp.dot(x_ref[...], wb_ref[...], preferred_element_type=F32)


def _mm_in(h, w_in, l):
    tn = 1024
    ncol = IN_WIDTH // tn
    return pl.pallas_call(
        _mm_in_kernel,
        grid=(ncol, N_TILES),
        in_specs=[
            pl.BlockSpec((TM, D_MODEL), lambda j, i: (i, 0)),
            pl.BlockSpec((None, D_MODEL, tn), lambda j, i: (l, 0, j)),
        ],
        out_specs=pl.BlockSpec((TM, tn), lambda j, i: (i, j)),
        out_shape=jax.ShapeDtypeStruct((N_ROWS, IN_WIDTH), F32),
        scratch_shapes=[pltpu.VMEM((D_MODEL, tn), BF16)],
        compiler_params=_cparams(("arbitrary", "arbitrary")),
        name="mm_in",
    )(h, w_in)


def _rms(x, g):
    return x * lax.rsqrt(jnp.mean(x * x, axis=-1, keepdims=True) + EPS) * g


def _swa_kernel(sink_ref, q_ref, k_ref, v_ref, pk_ref, pv_ref, qg_ref, kg_ref,
                *rest, tq, prev_raw, aliased):
    o_ref, kn_ref = rest[-2], rest[-1]
    if prev_raw:
        has_prev = pl.program_id(1) > 0
    else:
        has_prev = True
    qg = qg_ref[...]
    kg = kg_ref[...]
    nk = WINDOW + tq
    row = lax.broadcasted_iota(jnp.int32, (tq, nk), 0)
    col = lax.broadcasted_iota(jnp.int32, (tq, nk), 1)
    vis_prev = (col < WINDOW) & (col >= row) & has_prev
    vis_own = (col >= WINDOW) & ((col - WINDOW) <= row)
    mask = vis_prev | vis_own
    for kv in range(A_KV):
        ks = slice(kv * A_HD, (kv + 1) * A_HD)
        kn = _rms(k_ref[:, ks], kg)
        kn_ref[:, ks] = kn
        pk = pk_ref[:, ks]
        if prev_raw:
            pk = _rms(pk, kg)
        keys = jnp.concatenate([pk, kn], axis=0).astype(BF16)
        vals = jnp.concatenate([pv_ref[:, ks], v_ref[:, ks]], axis=0).astype(BF16)
        for g in range(A_GROUP):
            h = kv * A_GROUP + g
            hs = slice(h * A_HD, (h + 1) * A_HD)
            qn = _rms(q_ref[:, hs], qg).astype(BF16)
            s = lax.dot_general(qn, keys, (((1,), (1,)), ((), ())),
                                preferred_element_type=F32) * (A_HD ** -0.5)
            s = jnp.where(mask, s, NEG_BIG)
            snk = sink_ref[h]
            m = jnp.maximum(jnp.max(s, axis=-1, keepdims=True), snk)
            p = jnp.exp(s - m)
            den = jnp.sum(p, axis=-1, keepdims=True) + jnp.exp(snk - m)
            o = jnp.dot(p.astype(BF16), vals, preferred_element_type=F32) / den
            o_ref[:, hs] = o.astype(o_ref.dtype)


def _swa(proj, prev_k, prev_v, sinks, qn_g, kn_g, *, tq, n_seq, blocks_per_seq, row_block0,
         prev_raw, alias_bufs=None):
    qcol, kcol, vcol = 0, 1024 // 256, 1280 // 256

    def rb(s, n):
        return row_block0 + s * blocks_per_seq + n

    if prev_raw:
        def prev_map_k(s, n):
            return (rb(s, jnp.maximum(n - 1, 0)), kcol)

        def prev_map_v(s, n):
            return (rb(s, jnp.maximum(n - 1, 0)), vcol)
    else:
        def prev_map_k(s, n):
            return (s, 0)
        prev_map_v = prev_map_k

    in_specs = [
        pl.BlockSpec(memory_space=pltpu.SMEM),
        pl.BlockSpec((tq, 1024), lambda s, n: (rb(s, n), qcol)),
        pl.BlockSpec((tq, 256), lambda s, n: (rb(s, n), kcol)),
        pl.BlockSpec((tq, 256), lambda s, n: (rb(s, n), vcol)),
        pl.BlockSpec((WINDOW, 256), prev_map_k),
        pl.BlockSpec((WINDOW, 256), prev_map_v),
        pl.BlockSpec((1, A_HD), lambda s, n: (0, 0)),
        pl.BlockSpec((1, A_HD), lambda s, n: (0, 0)),
    ]
    args = [sinks, proj, proj, proj, prev_k, prev_v, qn_g.reshape(1, A_HD), kn_g.reshape(1, A_HD)]
    aliases = {}
    if alias_bufs is not None:
        for j, buf in enumerate(alias_bufs):
            in_specs.append(pl.BlockSpec(memory_space=pl.ANY))
            aliases[len(args)] = j
            args.append(buf)
    return pl.pallas_call(
        functools.partial(_swa_kernel, tq=tq, prev_raw=prev_raw, aliased=alias_bufs is not None),
        grid=(n_seq, blocks_per_seq),
        in_specs=in_specs,
        out_specs=[
            pl.BlockSpec((tq, 1024), lambda s, n: (rb(s, n), 0)),
            pl.BlockSpec((tq, 256), lambda s, n: (rb(s, n), 0)),
        ],
        out_shape=[
            jax.ShapeDtypeStruct((N_ROWS, 1024), BF16),
            jax.ShapeDtypeStruct((N_ROWS, 256), F32),
        ],
        input_output_aliases=aliases,
        compiler_params=_cparams(("arbitrary", "arbitrary")),
        name="swa_prompt" if prev_raw else "swa_sample",
    )(*args)


def _ret_tables(chunk, tv, pos0, t_len):
    lg = np.log1p(-(2.0 ** (-5.0 - np.arange(B_HEADS, dtype=np.float64))))
    idx = np.arange(chunk, dtype=np.float64)
    valid = idx < tv
    diff = idx[:, None] - idx[None, :]
    dmat = np.where((diff >= 0) & valid[:, None] & valid[None, :],
                    np.exp(np.maximum(diff, 0.0)[None] * lg[:, None, None]), 0.0)
    din = np.exp((idx + 1.0)[:, None] * lg[None])
    din = np.repeat(din, B_DV, axis=1)
    kfac = np.where(valid[:, None], np.exp((tv - 1.0 - idx)[:, None] * lg[None]), 0.0)
    kfac = np.repeat(kfac, B_DK, axis=1)
    gc = np.repeat(np.exp(tv * lg), B_DK)[:, None] * np.ones((1, B_DV))
    half = B_DK // 2
    inv = ROPE_BASE ** (-np.arange(half, dtype=np.float64) / half)
    ang = (pos0 + np.arange(t_len, dtype=np.float64))[:, None] * inv[None]
    cos = np.tile(np.cos(ang), (1, 2 * B_HEADS))
    sin = np.tile(np.concatenate([-np.sin(ang), np.sin(ang)], axis=1), (1, B_HEADS))
    f = lambda a: jnp.asarray(a, dtype=F32)
    return f(dmat), f(din), f(kfac), f(gc), f(cos), f(sin)


def _ret_kernel(q_ref, k_ref, v_ref, g_ref, cos_ref, sin_ref, dmat_ref, din_ref, kfac_ref,
                gc_ref, gng_ref, gnb_ref, s0_ref, *rest):
    o_ref, s_ref = rest[-2], rest[-1]

    @pl.when(pl.program_id(1) == 0)
    def _():
        s_ref[...] = s0_ref[...]

    cos = cos_ref[...]
    sin = sin_ref[...]
    width = B_HEADS * B_DK
    lane = lax.broadcasted_iota(jnp.int32, (1, width), 1)
    first_half = (lane % B_DK) < (B_DK // 2)
    lane_head = lane // B_DK

    def rope(x):
        partner = jnp.where(first_half, pltpu.roll(x, width - B_DK // 2, 1),
                            pltpu.roll(x, B_DK // 2, 1))
        return x * cos + partner * sin

    qr = rope(q_ref[...])
    kr = rope(k_ref[...]) * (B_DK ** -0.5)
    kb = kr.astype(BF16)
    kdec = (kr * kfac_ref[...]).astype(BF16)
    s_old = s_ref[...]
    s_b = s_old.astype(BF16)
    gc = gc_ref[...]
    for h in range(B_HEADS):
        vs = slice(h * B_DV, (h + 1) * B_DV)
        rs = slice(h * B_DK, (h + 1) * B_DK)
        qm = jnp.where(lane_head == h, qr, 0.0).astype(BF16)
        a = lax.dot_general(qm, kb, (((1,), (1,)), ((), ())),
                            preferred_element_type=F32) * dmat_ref[h]
        vh = v_ref[:, vs].astype(BF16)
        o = jnp.dot(a.astype(BF16), vh, preferred_element_type=F32)
        o = o + jnp.dot(qm, s_b, preferred_element_type=F32) * din_ref[:, vs]
        u = lax.dot_general(kdec, vh, (((0,), (0,)), ((), ())), preferred_element_type=F32)
        s_ref[rs, :] = gc[rs, :] * s_old[rs, :] + u[rs, :]
        mu = jnp.mean(o, axis=-1, keepdims=True)
        oc = o - mu
        var = jnp.mean(oc * oc, axis=-1, keepdims=True)
        y = oc * lax.rsqrt(var + EPS) * gng_ref[:, vs] + gnb_ref[:, vs]
        y = y * _silu(g_ref[:, vs])
        o_ref[:, vs] = y.astype(o_ref.dtype)


def _ret(proj, s0, gn_g, gn_b, tables, *, chunk, n_seq, chunks_per_seq, row_block0, table_per_chunk,
         alias_buf=None):
    dmat, din, kfac, gc, cos, sin = tables
    qcol, kcol, vcol, gcol = 1536 // 256, 1792 // 256, 2048 // 512, 2560 // 512

    def rb(s, c):
        return row_block0 + s * chunks_per_seq + c

    tmap = (lambda s, c: (c, 0)) if table_per_chunk else (lambda s, c: (0, 0))
    const2 = lambda s, c: (0, 0)
    in_specs = [
        pl.BlockSpec((chunk, 256), lambda s, c: (rb(s, c), qcol)),
        pl.BlockSpec((chunk, 256), lambda s, c: (rb(s, c), kcol)),
        pl.BlockSpec((chunk, 512), lambda s, c: (rb(s, c), vcol)),
        pl.BlockSpec((chunk, 512), lambda s, c: (rb(s, c), gcol)),
        pl.BlockSpec((chunk, 256), tmap),
        pl.BlockSpec((chunk, 256), tmap),
        pl.BlockSpec((B_HEADS, chunk, chunk), lambda s, c: (0, 0, 0)),
        pl.BlockSpec((chunk, 512), const2),
        pl.BlockSpec((chunk, 256), const2),
        pl.BlockSpec((256, B_DV), const2),
        pl.BlockSpec((1, 512), const2),
        pl.BlockSpec((1, 512), const2),
        pl.BlockSpec((None, 256, B_DV), lambda s, c: (s, 0, 0)),
    ]
    args = [proj, proj, proj, proj, cos, sin, dmat, din, kfac, gc,
            gn_g.reshape(1, 512), gn_b.reshape(1, 512), s0.reshape(n_seq, 256, B_DV)]
    aliases = {}
    if alias_buf is not None:
        in_specs.append(pl.BlockSpec(memory_space=pl.ANY))
        aliases[len(args)] = 0
        args.append(alias_buf)
    o, s_new = pl.pallas_call(
        _ret_kernel,
        grid=(n_seq, chunks_per_seq),
        in_specs=in_specs,
        out_specs=[
            pl.BlockSpec((chunk, 512), lambda s, c: (rb(s, c), 0)),
            pl.BlockSpec((None, 256, B_DV), lambda s, c: (s, 0, 0)),
        ],
        out_shape=[
            jax.ShapeDtypeStruct((N_ROWS, 512), BF16),
            jax.ShapeDtypeStruct((n_seq, 256, B_DV), F32),
        ],
        input_output_aliases=aliases,
        compiler_params=_cparams(("arbitrary", "arbitrary")),
        name="ret_prompt" if alias_buf is None else "ret_sample",
    )(*args)
    return o, s_new.reshape(n_seq, B_HEADS, B_DK, B_DV)


def _hgrn_kernel(q_ref, f_ref, i_ref, g_ref, lb_ref, ng_ref, st0_ref, *rest, rows, tv):
    o_ref, st_ref, q_s, k_s, bc_s, qd_s, kd_s, eb_s = rest[-8:]
    nblk = rows // HG_BLK
    width = C_HEADS * C_DK

    @pl.when(pl.program_id(1) == 0)
    def _():
        st_ref[...] = st0_ref[...]

    z = f_ref[...]
    lb = lb_ref[...]
    sig = _sigmoid(z)
    lf = jnp.log(lb + (1.0 - lb) * sig)
    k = (1.0 - lb) * _sigmoid(-z)
    if tv < rows:
        rvalid = lax.broadcasted_iota(jnp.int32, (rows, 1), 0) < tv
        lf = jnp.where(rvalid, lf, 0.0)
        k = jnp.where(rvalid, k, 0.0)
    ri = lax.broadcasted_iota(jnp.int32, (rows, rows), 0)
    ci = lax.broadcasted_iota(jnp.int32, (rows, rows), 1)
    same_blk = (ri // HG_BLK) == (ci // HG_BLK)
    tri = (same_blk & (ci <= ri)).astype(F32)
    upp = (same_blk & (ci > ri)).astype(F32)
    bc = jnp.dot(tri, lf, precision=lax.Precision.HIGHEST, preferred_element_type=F32)
    rest_dec = jnp.dot(upp, lf, precision=lax.Precision.HIGHEST, preferred_element_type=F32)
    blast_rows = bc + rest_dec
    q = _silu(q_ref[...])
    q_s[...] = q
    k_s[...] = k
    bc_s[...] = bc
    qd_s[...] = q * jnp.exp(bc)
    kd_s[...] = k * jnp.exp(rest_dec)
    eb_s[...] = jnp.exp(blast_rows)
    ng = ng_ref[...]
    rowi = lax.broadcasted_iota(jnp.int32, (HG_BLK, 1), 0)

    def blk_body(blk, carry):
        r0 = pl.multiple_of(blk * HG_BLK, HG_BLK)
        rsl = pl.ds(r0, HG_BLK)
        for h in range(C_HEADS):
            hs = slice(h * C_DK, (h + 1) * C_DK)
            vsl = slice(h * C_DV, (h + 1) * C_DV)
            qb = q_s[rsl, hs]
            kb = k_s[rsl, hs]
            bcb = bc_s[rsl, hs]
            vb = i_ref[rsl, hs]
            st = st_ref[vsl, :]
            o = lax.dot_general(qd_s[rsl, hs].astype(BF16), st.astype(BF16),
                                (((1,), (1,)), ((), ())), preferred_element_type=F32)
            for j in range(HG_BLK):
                dec = jnp.exp(jnp.minimum(bcb - bcb[j:j + 1, :], 0.0))
                a = jnp.sum(qb * kb[j:j + 1, :] * dec, axis=-1, keepdims=True)
                a = jnp.where(rowi >= j, a, 0.0)
                o = o + a * vb[j:j + 1, :]
            u = lax.dot_general(vb.astype(BF16), kd_s[rsl, hs].astype(BF16),
                                (((0,), (0,)), ((), ())), preferred_element_type=F32)
            st_ref[vsl, :] = st * eb_s[pl.ds(r0, 1), hs] + u
            y = o * lax.rsqrt(jnp.mean(o * o, axis=-1, keepdims=True) + EPS) * ng
            y = y * _silu(g_ref[rsl, hs])
            o_ref[rsl, vsl] = y.astype(o_ref.dtype)
        return carry

    lax.fori_loop(0, nblk, blk_body, 0)


def _hgrn(proj, st0, lb, ng, *, rows, tv, n_seq, tiles_per_seq, row_block0, alias_buf=None):
    qcol, fcol, icol, gcol = 3072 // 512, 3584 // 512, 4096 // 512, 4608 // 512

    def rb(s, c):
        return row_block0 + s * tiles_per_seq + c

    const2 = lambda s, c: (0, 0)
    in_specs = [
        pl.BlockSpec((rows, 512), lambda s, c: (rb(s, c), qcol)),
        pl.BlockSpec((rows, 512), lambda s, c: (rb(s, c), fcol)),
        pl.BlockSpec((rows, 512), lambda s, c: (rb(s, c), icol)),
        pl.BlockSpec((rows, 512), lambda s, c: (rb(s, c), gcol)),
        pl.BlockSpec((1, 512), const2),
        pl.BlockSpec((1, C_DV), const2),
        pl.BlockSpec((None, 512, C_DK), lambda s, c: (s, 0, 0)),
    ]
    args = [proj, proj, proj, proj, lb.reshape(1, 512), ng.reshape(1, C_DV), st0]
    aliases = {}
    if alias_buf is not None:
        in_specs.append(pl.BlockSpec(memory_space=pl.ANY))
        aliases[len(args)] = 0
        args.append(alias_buf)
    return pl.pallas_call(
        functools.partial(_hgrn_kernel, rows=rows, tv=tv),
        grid=(n_seq, tiles_per_seq),
        in_specs=in_specs,
        out_specs=[
            pl.BlockSpec((rows, 512), lambda s, c: (rb(s, c), 0)),
            pl.BlockSpec((None, 512, C_DK), lambda s, c: (s, 0, 0)),
        ],
        out_shape=[
            jax.ShapeDtypeStruct((N_ROWS, 512), BF16),
            jax.ShapeDtypeStruct((n_seq, 512, C_DK), F32),
        ],
        scratch_shapes=[pltpu.VMEM((rows, 512), F32) for _ in range(6)],
        input_output_aliases=aliases,
        compiler_params=_cparams(("arbitrary", "arbitrary")),
        name="hgrn_prompt" if alias_buf is None else "hgrn_sample",
    )(*args)


def _mm_out_kernel(oa_ref, ob_ref, oc_ref, w_ref, x_ref, g_ref, o_ref, wb_ref):
    @pl.when(pl.program_id(1) == 0)
    def _():
        wb_ref[...] = w_ref[...].astype(BF16)

    acc = jnp.dot(oa_ref[...], wb_ref[0:1024, :], preferred_element_type=F32)
    acc = acc + jnp.dot(ob_ref[...], wb_ref[1024:1536, :], preferred_element_type=F32)
    acc = acc + jnp.dot(oc_ref[...], wb_ref[1536:2048, :], preferred_element_type=F32)
    o_ref[...] = x_ref[...] + _expand_rows(g_ref[...], TM) * acc


def _mm_out(oa, ob, oc, w_out, x, g1g, l):
    tn = 1024
    ncol = D_MODEL // tn
    mg = TM // MOD_ROWS
    return pl.pallas_call(
        _mm_out_kernel,
        grid=(ncol, N_TILES),
        in_specs=[
            pl.BlockSpec((TM, 1024), lambda j, i: (i, 0)),
            pl.BlockSpec((TM, 512), lambda j, i: (i, 0)),
            pl.BlockSpec((TM, 512), lambda j, i: (i, 0)),
            pl.BlockSpec((None, D_MODEL, tn), lambda j, i: (l, 0, j)),
            pl.BlockSpec((TM, tn), lambda j, i: (i, j)),
            pl.BlockSpec((mg, tn), lambda j, i: (i, j)),
        ],
        out_specs=pl.BlockSpec((TM, tn), lambda j, i: (i, j)),
        out_shape=jax.ShapeDtypeStruct((N_ROWS, D_MODEL), F32),
        scratch_shapes=[pltpu.VMEM((D_MODEL, tn), BF16)],
        compiler_params=_cparams(("arbitrary", "arbitrary")),
        name="mm_out",
    )(oa, ob, oc, w_out, x, g1g)


META_GSEL = 0
META_GATE = 20


def _route_kernel(x_ref, g_ref, sc_ref, sh_ref, wr_ref, br_ref, h_ref, meta_ref):
    x = x_ref[...]
    rows = x.shape[0]
    y = x * lax.rsqrt(jnp.mean(x * x, axis=-1, keepdims=True) + EPS) * g_ref[...]
    h = y * (1.0 + _expand_rows(sc_ref[...], rows)) + _expand_rows(sh_ref[...], rows)
    h_ref[...] = h.astype(h_ref.dtype)
    logits = jnp.dot(h, wr_ref[...], precision=lax.Precision.HIGHEST,
                     preferred_element_type=F32) + br_ref[...]
    col = lax.broadcasted_iota(jnp.int32, logits.shape, 1).astype(F32)
    big = 1e9
    is_g = col < N_GROUPS
    gl = jnp.where(is_g, logits, -jnp.inf)
    gmax = jnp.max(gl, axis=-1, keepdims=True)
    gsel = jnp.min(jnp.where(gl == gmax, col, big), axis=-1, keepdims=True)
    gden = jnp.sum(jnp.where(is_g, jnp.exp(logits - gmax), 0.0), axis=-1, keepdims=True)
    g_w = 1.0 / gden
    e0 = N_GROUPS + EXP_PER_GROUP * gsel
    in_grp = (col >= e0) & (col < e0 + EXP_PER_GROUP)
    el = jnp.where(in_grp, logits, -jnp.inf)
    v1 = jnp.max(el, axis=-1, keepdims=True)
    i1 = jnp.min(jnp.where(el == v1, col, big), axis=-1, keepdims=True)
    el2 = jnp.where(col == i1, -jnp.inf, el)
    v2 = jnp.max(el2, axis=-1, keepdims=True)
    i2 = jnp.min(jnp.where(el2 == v2, col, big), axis=-1, keepdims=True)
    t = jnp.exp(v2 - v1)
    p1 = 1.0 / (1.0 + t)
    a1 = p1 * g_w
    a2 = t * p1 * g_w
    meta = jnp.where(col == META_GATE + (i1 - e0), a1, 0.0)
    meta = meta + jnp.where(col == META_GATE + (i2 - e0), a2, 0.0)
    meta = meta + jnp.where(col == META_GSEL, gsel, 0.0)
    meta_ref[...] = meta


def _route(x, g, scg, shg, wr, br):
    mg = TM // MOD_ROWS
    return pl.pallas_call(
        _route_kernel,
        grid=(N_TILES,),
        in_specs=[
            pl.BlockSpec((TM, D_MODEL), lambda i: (i, 0)),
            pl.BlockSpec((1, D_MODEL), lambda i: (0, 0)),
            pl.BlockSpec((mg, D_MODEL), lambda i: (i, 0)),
            pl.BlockSpec((mg, D_MODEL), lambda i: (i, 0)),
            pl.BlockSpec((D_MODEL, 128), lambda i: (0, 0)),
            pl.BlockSpec((1, 128), lambda i: (0, 0)),
        ],
        out_specs=[
            pl.BlockSpec((TM, D_MODEL), lambda i: (i, 0)),
            pl.BlockSpec((TM, 128), lambda i: (i, 0)),
        ],
        out_shape=[
            jax.ShapeDtypeStruct((N_ROWS, D_MODEL), BF16),
            jax.ShapeDtypeStruct((N_ROWS, 128), F32),
        ],
        compiler_params=_cparams(("arbitrary",)),
        name="route",
    )(x, g.reshape(1, D_MODEL), scg, shg, wr, br)


def _hidden_kernel(wt_ref, we_ref, nw_ref, x_ref, m_ref, w1_ref, w3_ref, o_ref, w1b, w3b):
    w = pl.program_id(0)
    e = we_ref[w]
    first = (w == 0) | (e != we_ref[jnp.maximum(w - 1, 0)])
    valid = w < nw_ref[0]

    @pl.when(valid & first)
    def _():
        w1b[...] = w1_ref[...].astype(BF16)
        w3b[...] = w3_ref[...].astype(BF16)

    @pl.when(valid)
    def _():
        x = x_ref[...]
        a = jnp.dot(x, w1b[...], preferred_element_type=F32)
        b = jnp.dot(x, w3b[...], preferred_element_type=F32)
        m = m_ref[...]
        col = lax.broadcasted_iota(jnp.int32, m.shape, 1)
        gate = jnp.sum(jnp.where(col == META_GATE + e % EXP_PER_GROUP, m, 0.0),
                       axis=-1, keepdims=True)
        o_ref[...] = (_silu(a) * b * gate).astype(o_ref.dtype)


def _moe_hidden(xs, metas, w1, w3, wt, we, nw, l):
    grid_spec = pltpu.PrefetchScalarGridSpec(
        num_scalar_prefetch=3,
        grid=(HID_WORK,),
        in_specs=[
            pl.BlockSpec((TM, D_MODEL), lambda w, wt, we, nw: (wt[w], 0)),
            pl.BlockSpec((TM, 128), lambda w, wt, we, nw: (wt[w], 0)),
            pl.BlockSpec((None, None, D_MODEL, D_EXPERT), lambda w, wt, we, nw: (l, we[w], 0, 0)),
            pl.BlockSpec((None, None, D_MODEL, D_EXPERT), lambda w, wt, we, nw: (l, we[w], 0, 0)),
        ],
        out_specs=pl.BlockSpec((TM, D_EXPERT),
                               lambda w, wt, we, nw: (wt[w], we[w] % EXP_PER_GROUP)),
        scratch_shapes=[pltpu.VMEM((D_MODEL, D_EXPERT), BF16),
                        pltpu.VMEM((D_MODEL, D_EXPERT), BF16)],
    )
    return pl.pallas_call(
        _hidden_kernel,
        grid_spec=grid_spec,
        out_shape=jax.ShapeDtypeStruct((MOE_ROWS, EXP_PER_GROUP * D_EXPERT), BF16),
        compiler_params=_cparams(("arbitrary",)),
        name="moe_hidden",
    )(wt, we, nw, xs, metas, w1, w3)


def _down_kernel(dt_ref, dg_ref, dn_ref, nw_ref, h_ref, w_ref, o_ref, wb):
    w = pl.program_id(0)
    prev = jnp.maximum(w - 1, 0)
    first = (w == 0) | (dg_ref[w] != dg_ref[prev]) | (dn_ref[w] != dn_ref[prev])
    valid = w < nw_ref[0]

    @pl.when(valid & first)
    def _():
        wb[...] = w_ref[...].astype(BF16)

    @pl.when(valid)
    def _():
        o_ref[...] = jnp.dot(h_ref[...], wb[...], preferred_element_type=F32)


def _moe_down(hid, w2g, dt, dg, dn, nw, l):
    kdim = EXP_PER_GROUP * D_EXPERT
    grid_spec = pltpu.PrefetchScalarGridSpec(
        num_scalar_prefetch=4,
        grid=(DOWN_WORK,),
        in_specs=[
            pl.BlockSpec((TM, kdim), lambda w, dt, dg, dn, nw: (dt[w], 0)),
            pl.BlockSpec((None, None, kdim, DOWN_TN), lambda w, dt, dg, dn, nw: (l, dg[w], 0, dn[w])),
        ],
        out_specs=pl.BlockSpec((TM, DOWN_TN), lambda w, dt, dg, dn, nw: (dt[w], dn[w])),
        scratch_shapes=[pltpu.VMEM((kdim, DOWN_TN), BF16)],
    )
    return pl.pallas_call(
        _down_kernel,
        grid_spec=grid_spec,
        out_shape=jax.ShapeDtypeStruct((MOE_ROWS, D_MODEL), F32),
        compiler_params=_cparams(("arbitrary",)),
        name="moe_down",
    )(dt, dg, dn, nw, hid, w2g)


def _moe_plan(gsel_tok):
    i32 = jnp.int32
    groups = jnp.arange(N_GROUPS, dtype=i32)
    onehot = (gsel_tok[:, None] == groups[None, :]).astype(i32)
    csum = jnp.cumsum(onehot, axis=0)
    counts = csum[-1]
    rank = jnp.take_along_axis(csum, gsel_tok[:, None], axis=1)[:, 0] - 1
    ntile = (counts + TM - 1) // TM
    tend = jnp.cumsum(ntile)
    tstart = tend - ntile
    pos = tstart[gsel_tok] * TM + rank
    src_tok = jnp.zeros((MOE_ROWS,), i32).at[pos].set(jnp.arange(N_TOK, dtype=i32))

    def work(per_tile, n_items):
        cnt = per_tile * ntile
        wend = jnp.cumsum(cnt)
        wstart = wend - cnt
        nwork = wend[-1]
        w = jnp.minimum(jnp.arange(n_items, dtype=i32), nwork - 1)
        g = jnp.sum((w[:, None] >= wend[None, :]).astype(i32), axis=1)
        r = w - wstart[g]
        ntg = jnp.maximum(ntile[g], 1)
        return g, r // ntg, tstart[g] + r % ntg, nwork.reshape(1)

    hg, he, ht, hn = work(EXP_PER_GROUP, HID_WORK)
    dg, dn, dt, dnw = work(DOWN_NCOL, DOWN_WORK)
    return pos, src_tok, (ht, hg * EXP_PER_GROUP + he, hn), (dt, dg, dn, dnw)


def _res_kernel(x_ref, y_ref, g_ref, o_ref):
    o_ref[...] = x_ref[...] + _expand_rows(g_ref[...], TM) * y_ref[...]


def _residual(x, y, g2g):
    mg = TM // MOD_ROWS
    return pl.pallas_call(
        _res_kernel,
        grid=(N_TILES,),
        in_specs=[
            pl.BlockSpec((TM, D_MODEL), lambda i: (i, 0)),
            pl.BlockSpec((TM, D_MODEL), lambda i: (i, 0)),
            pl.BlockSpec((mg, D_MODEL), lambda i: (i, 0)),
        ],
        out_specs=pl.BlockSpec((TM, D_MODEL), lambda i: (i, 0)),
        out_shape=jax.ShapeDtypeStruct((N_ROWS, D_MODEL), F32),
        compiler_params=_cparams(("arbitrary",)),
        name="residual",
    )(x, y, g2g)


def kernel(x_prompt, x_sample, c_prompt, c_sample, cache_swa_k, cache_swa_v, state_ret, state_hgrn,
           w_ada, b_ada, ln1_g, w_in, qn_g, kn_g, attn_sinks, ret_gn_g, ret_gn_b, hgrn_lb, hgrn_ng,
           w_out, ln2_g, w_rg, b_rg, w_re, b_re, w1, w3, w2):
    i32 = jnp.int32
    lbp = jax.nn.softmax(hgrn_lb.astype(F32), axis=0)
    lb_all = jnp.cumsum(lbp, axis=0) - lbp[0]

    xs_pad = jnp.pad(x_sample.reshape(DEC_BATCH, 1, D_MODEL), ((0, 0), (0, SROWS - 1), (0, 0)))
    x = jnp.concatenate([x_prompt.reshape(NP_ROWS, D_MODEL), xs_pad.reshape(NS_ROWS, D_MODEL)], axis=0)

    n_c = BATCH + DEC_BATCH
    c_rows = 40
    c_all = jnp.concatenate([c_prompt, c_sample, jnp.zeros((c_rows - n_c, D_MODEL), F32)], axis=0)
    mod = _ada(c_all, w_ada, b_ada)

    def mod_groups(m):
        return jnp.concatenate([jnp.repeat(m[:BATCH], SEQ // MOD_ROWS, axis=0), m[BATCH:n_c]], axis=0)

    tok_rows = jnp.concatenate([jnp.arange(NP_ROWS, dtype=i32),
                                NP_ROWS + SROWS * jnp.arange(DEC_BATCH, dtype=i32)])
    row_tok = jnp.concatenate([jnp.arange(NP_ROWS, dtype=i32),
                               NP_ROWS + jnp.arange(NS_ROWS, dtype=i32) // SROWS])

    ret_tab_p = _ret_tables(RET_CHUNK, RET_CHUNK, 0.0, SEQ)
    ret_tab_s = _ret_tables(SROWS, 1, float(PAST_LEN), SROWS)
    w2g = w2.reshape(DEPTH, N_GROUPS, EXP_PER_GROUP * D_EXPERT, D_MODEL)

    outs = {k: [] for k in ("kp", "vp", "rp", "hp", "ks", "vs", "rs", "hs")}
    for l in range(DEPTH):
        ml = mod[l]
        sh1, sc1, g1, sh2, sc2, g2 = [mod_groups(ml[:, i * D_MODEL:(i + 1) * D_MODEL]) for i in range(6)]
        h = _norm_mod(x, ln1_g[l], sc1, sh1)
        proj = _mm_in(h, w_in, l)

        oa, kn = _swa(proj, proj, proj, attn_sinks[l], qn_g[l], kn_g[l], tq=WINDOW, n_seq=BATCH,
                      blocks_per_seq=SEQ // WINDOW, row_block0=0, prev_raw=True)
        ck = cache_swa_k[l].reshape(DEC_BATCH * WINDOW, A_KV * A_HD)
        cv = cache_swa_v[l].reshape(DEC_BATCH * WINDOW, A_KV * A_HD)
        oa, kn = _swa(proj, ck, cv, attn_sinks[l], qn_g[l], kn_g[l], tq=SROWS, n_seq=DEC_BATCH,
                      blocks_per_seq=1, row_block0=NP_ROWS // SROWS, prev_raw=False,
                      alias_bufs=(oa, kn))
        zero_ret = jnp.zeros((BATCH, B_HEADS, B_DK, B_DV), F32)
        ob, rp = _ret(proj, zero_ret, ret_gn_g[l], ret_gn_b[l], ret_tab_p, chunk=RET_CHUNK,
                      n_seq=BATCH, chunks_per_seq=SEQ // RET_CHUNK, row_block0=0, table_per_chunk=True)
        ob, rs = _ret(proj, state_ret[l], ret_gn_g[l], ret_gn_b[l], ret_tab_s, chunk=SROWS,
                      n_seq=DEC_BATCH, chunks_per_seq=1, row_block0=NP_ROWS // SROWS,
                      table_per_chunk=False, alias_buf=ob)
        zero_hg = jnp.zeros((BATCH, C_HEADS * C_DV, C_DK), F32)
        oc, hp_t = _hgrn(proj, zero_hg, lb_all[l], hgrn_ng[l], rows=HG_TILE, tv=HG_TILE, n_seq=BATCH,
                         tiles_per_seq=SEQ // HG_TILE, row_block0=0)
        st_s = jnp.swapaxes(state_hgrn[l], -1, -2).reshape(DEC_BATCH, C_HEADS * C_DV, C_DK)
        oc, hs_t = _hgrn(proj, st_s, lb_all[l], hgrn_ng[l], rows=SROWS, tv=1, n_seq=DEC_BATCH,
                         tiles_per_seq=1, row_block0=NP_ROWS // SROWS, alias_buf=oc)

        x1 = _mm_out(oa, ob, oc, w_out, x, g1, l)

        wr = jnp.concatenate([w_rg[l], w_re[l], jnp.zeros((D_MODEL, 128 - N_GROUPS - N_EXPERTS), F32)], axis=1)
        br = jnp.concatenate([b_rg[l], b_re[l], jnp.zeros((128 - N_GROUPS - N_EXPERTS,), F32)]).reshape(1, 128)
        h2, meta = _route(x1, ln2_g[l], sc2, sh2, wr, br)
        gsel_tok = meta[tok_rows, META_GSEL].astype(i32)
        pos, src_tok, hid_work, down_work = _moe_plan(gsel_tok)
        src_rows = tok_rows[src_tok]
        xs_sorted = jnp.take(h2, src_rows, axis=0)
        meta_sorted = jnp.take(meta, src_rows, axis=0)
        hid = _moe_hidden(xs_sorted, meta_sorted, w1, w3, *hid_work, l)
        y_sorted = _moe_down(hid, w2g, *down_work, l)
        y_rows = jnp.take(y_sorted, pos[row_tok], axis=0)
        x = _residual(x1, y_rows, g2)

        knp = kn[:NP_ROWS].reshape(BATCH, SEQ, A_KV, A_HD)
        vpp = proj[:NP_ROWS, 1280:1536].reshape(BATCH, SEQ, A_KV, A_HD)
        outs["kp"].append(knp[:, SEQ - WINDOW:])
        outs["vp"].append(vpp[:, SEQ - WINDOW:])
        outs["rp"].append(rp)
        outs["hp"].append(jnp.swapaxes(hp_t.reshape(BATCH, C_HEADS, C_DV, C_DK), -1, -2))
        k_new = kn[NP_ROWS::SROWS].reshape(DEC_BATCH, 1, A_KV, A_HD)
        v_new = proj[NP_ROWS::SROWS, 1280:1536].reshape(DEC_BATCH, 1, A_KV, A_HD)
        outs["ks"].append(jnp.concatenate([cache_swa_k[l][:, 1:], k_new], axis=1))
        outs["vs"].append(jnp.concatenate([cache_swa_v[l][:, 1:], v_new], axis=1))
        outs["rs"].append(rs)
        outs["hs"].append(jnp.swapaxes(hs_t.reshape(DEC_BATCH, C_HEADS, C_DV, C_DK), -1, -2))

    y_prompt = x[:NP_ROWS].reshape(BATCH, SEQ, D_MODEL)
    y_sample = x[NP_ROWS::SROWS].reshape(DEC_BATCH, 1, D_MODEL)
    st = lambda k: jnp.stack(outs[k])
    return (y_prompt, y_sample, st("kp"), st("vp"), st("rp"), st("hp"),
            st("ks"), st("vs"), st("rs"), st("hs"))
```

```python
import functools

import numpy as np
import jax
import jax.numpy as jnp
from jax import lax
from jax.experimental import pallas as pl
from jax.experimental.pallas import tpu as pltpu

F32 = jnp.float32
BF16 = jnp.bfloat16

D_MODEL = 2048
BATCH = 2
SEQ = 4096
DEPTH = 4
DEC_BATCH = 32
PAST_LEN = 16384
A_HD = 128
A_HEADS = 8
A_KV = 2
A_GROUP = 4
WINDOW = 128
B_DV = 128
B_DK = 64
B_HEADS = 4
C_DK = 128
C_DV = 128
C_HEADS = 4
ROPE_BASE = 10000.0
N_GROUPS = 4
EXP_PER_GROUP = 4
N_EXPERTS = 16
D_EXPERT = 512
EPS = 1e-6
NEG_BIG = -1e30
IN_WIDTH = 5120

SROWS = 16
NP_ROWS = BATCH * SEQ
NS_ROWS = DEC_BATCH * SROWS
N_ROWS = NP_ROWS + NS_ROWS
TM = 512
N_TILES = N_ROWS // TM
MOD_ROWS = 16
N_TOK = NP_ROWS + DEC_BATCH
MOE_TILES = (N_TOK + N_GROUPS * (TM - 1)) // TM
MOE_ROWS = MOE_TILES * TM
HID_WORK = MOE_TILES * EXP_PER_GROUP
DOWN_TN = 1024
DOWN_NCOL = D_MODEL // DOWN_TN
DOWN_WORK = MOE_TILES * DOWN_NCOL

RET_CHUNK = 256
HG_TILE = 256
HG_BLK = 16

VMEM_LIMIT = 48 * 1024 * 1024


def _cparams(sem):
    return pltpu.CompilerParams(dimension_semantics=sem, vmem_limit_bytes=VMEM_LIMIT)


def _sigmoid(x):
    return 1.0 / (1.0 + jnp.exp(-x))


def _silu(x):
    return x * _sigmoid(x)


C_ROWS = 40
C_SAMPLE0 = 8
PROMPT_TILES = NP_ROWS // TM
MOD_COL = dict(sh1=0, sc1=1, g1=2, sh2=3, sc2=4, g2=5)


def _tile_mod(m_ref, i):
    n = m_ref.shape[-1]
    seq = jnp.minimum(i // (SEQ // TM), BATCH - 1)
    prow = jnp.broadcast_to(m_ref[pl.ds(seq, 1), :], (DEC_BATCH, n))
    srows = m_ref[C_SAMPLE0:C_SAMPLE0 + DEC_BATCH, :]
    m = jnp.where(i < PROMPT_TILES, prow, srows)
    return jnp.broadcast_to(m[:, None, :], (DEC_BATCH, SROWS, n)).reshape(TM, n)


def _mod_spec(l, name, tn=D_MODEL, colmap=None):
    base = MOD_COL[name] * (D_MODEL // tn)
    if colmap is None:
        return pl.BlockSpec((None, C_ROWS, tn), lambda *ids: (l, 0, base))
    return pl.BlockSpec((None, C_ROWS, tn), lambda *ids: (l, 0, base + colmap(*ids)))


def _ada_kernel(c_ref, w_ref, b_ref, o_ref):
    c = _silu(c_ref[...])
    o_ref[...] = jnp.dot(c.astype(BF16), w_ref[...].astype(BF16),
                         preferred_element_type=F32) + b_ref[...]


def _ada(c_all, w_ada, b_ada):
    rows = c_all.shape[0]
    tn = 1024
    ncol = w_ada.shape[2] // tn
    return pl.pallas_call(
        _ada_kernel,
        grid=(DEPTH, ncol),
        in_specs=[
            pl.BlockSpec((rows, D_MODEL), lambda l, j: (0, 0)),
            pl.BlockSpec((None, D_MODEL, tn), lambda l, j: (l, 0, j)),
            pl.BlockSpec((None, 1, tn), lambda l, j: (l, 0, j)),
        ],
        out_specs=pl.BlockSpec((None, rows, tn), lambda l, j: (l, 0, j)),
        out_shape=jax.ShapeDtypeStruct((DEPTH, rows, w_ada.shape[2]), F32),
        compiler_params=_cparams(("arbitrary", "arbitrary")),
        name="ada",
    )(c_all, w_ada, b_ada.reshape(DEPTH, 1, -1))


def _norm_mod_rows(x, g, sc_ref, sh_ref, i):
    y = x * lax.rsqrt(jnp.mean(x * x, axis=-1, keepdims=True) + EPS) * g
    return y * (1.0 + _tile_mod(sc_ref, i)) + _tile_mod(sh_ref, i)


def _norm_kernel(xp_ref, xs_ref, g_ref, sc_ref, sh_ref, x_ref, o_ref):
    i = pl.program_id(0)
    xs = xs_ref[...]
    rid = lax.broadcasted_iota(jnp.int32, (DEC_BATCH, SROWS, 1), 1)
    blk = jnp.where(rid == 0, jnp.broadcast_to(xs[:, None, :], (DEC_BATCH, SROWS, D_MODEL)), 0.0)
    x = jnp.where(i < PROMPT_TILES, xp_ref[...], blk.reshape(TM, D_MODEL))
    x_ref[...] = x
    o_ref[...] = _norm_mod_rows(x, g_ref[...], sc_ref, sh_ref, i).astype(o_ref.dtype)


def _norm_mod_first(x_prompt, x_sample, g, mod, l):
    return pl.pallas_call(
        _norm_kernel,
        grid=(N_TILES,),
        in_specs=[
            pl.BlockSpec((TM, D_MODEL), lambda i: (jnp.minimum(i, PROMPT_TILES - 1), 0)),
            pl.BlockSpec((DEC_BATCH, D_MODEL), lambda i: (0, 0)),
            pl.BlockSpec((1, D_MODEL), lambda i: (0, 0)),
            _mod_spec(l, "sc1"),
            _mod_spec(l, "sh1"),
        ],
        out_specs=[
            pl.BlockSpec((TM, D_MODEL), lambda i: (i, 0)),
            pl.BlockSpec((TM, D_MODEL), lambda i: (i, 0)),
        ],
        out_shape=[
            jax.ShapeDtypeStruct((N_ROWS, D_MODEL), F32),
            jax.ShapeDtypeStruct((N_ROWS, D_MODEL), BF16),
        ],
        compiler_params=_cparams(("arbitrary",)),
        name="norm_first",
    )(x_prompt, x_sample, g.reshape(1, D_MODEL), mod, mod)


def _mm_in_kernel(x_ref, w_ref, o_ref, wb_ref):
    @pl.when(pl.program_id(1) == 0)
    def _():
        wb_ref[...] = w_ref[...].astype(BF16)

    o_ref[...] = jnp.dot(x_ref[...], wb_ref[...], preferred_element_type=F32)


def _mm_in(h, w_in, l):
    tn = 1024
    ncol = IN_WIDTH // tn
    return pl.pallas_call(
        _mm_in_kernel,
        grid=(ncol, N_TILES),
        in_specs=[
            pl.BlockSpec((TM, D_MODEL), lambda j, i: (i, 0)),
            pl.BlockSpec((None, D_MODEL, tn), lambda j, i: (l, 0, j)),
        ],
        out_specs=pl.BlockSpec((TM, tn), lambda j, i: (i, j)),
        out_shape=jax.ShapeDtypeStruct((N_ROWS, IN_WIDTH), F32),
        scratch_shapes=[pltpu.VMEM((D_MODEL, tn), BF16)],
        compiler_params=_cparams(("arbitrary", "arbitrary")),
        name="mm_in",
    )(h, w_in)


def _rms(x, g):
    return x * lax.rsqrt(jnp.mean(x * x, axis=-1, keepdims=True) + EPS) * g


def _swa_kernel(sink_ref, q_ref, k_ref, v_ref, pk_ref, pv_ref, qg_ref, kg_ref,
                *rest, tq, prev_raw, aliased):
    o_ref, kn_ref = rest[-2], rest[-1]
    if prev_raw:
        has_prev = pl.program_id(1) > 0
    else:
        has_prev = True
    qg = qg_ref[...]
    kg = kg_ref[...]
    nk = WINDOW + tq
    row = lax.broadcasted_iota(jnp.int32, (tq, nk), 0)
    col = lax.broadcasted_iota(jnp.int32, (tq, nk), 1)
    vis_prev = (col < WINDOW) & (col >= row) & has_prev
    vis_own = (col >= WINDOW) & ((col - WINDOW) <= row)
    mask = vis_prev | vis_own
    for kv in range(A_KV):
        ks = slice(kv * A_HD, (kv + 1) * A_HD)
        kn = _rms(k_ref[:, ks], kg)
        kn_ref[:, ks] = kn
        pk = pk_ref[:, ks]
        if prev_raw:
            pk = _rms(pk, kg)
        keys = jnp.concatenate([pk, kn], axis=0).astype(BF16)
        vals = jnp.concatenate([pv_ref[:, ks], v_ref[:, ks]], axis=0).astype(BF16)
        for g in range(A_GROUP):
            h = kv * A_GROUP + g
            hs = slice(h * A_HD, (h + 1) * A_HD)
            qn = _rms(q_ref[:, hs], qg).astype(BF16)
            s = lax.dot_general(qn, keys, (((1,), (1,)), ((), ())),
                                preferred_element_type=F32) * (A_HD ** -0.5)
            s = jnp.where(mask, s, NEG_BIG)
            snk = sink_ref[h]
            m = jnp.maximum(jnp.max(s, axis=-1, keepdims=True), snk)
            p = jnp.exp(s - m)
            den = jnp.sum(p, axis=-1, keepdims=True) + jnp.exp(snk - m)
            o = jnp.dot(p.astype(BF16), vals, preferred_element_type=F32) / den
            o_ref[:, hs] = o.astype(o_ref.dtype)


def _swa(proj, prev_k, prev_v, sinks, qn_g, kn_g, *, tq, n_seq, blocks_per_seq, row_block0,
         prev_raw, alias_bufs=None):
    qcol, kcol, vcol = 0, 1024 // 256, 1280 // 256

    def rb(s, n):
        return row_block0 + s * blocks_per_seq + n

    if prev_raw:
        def prev_map_k(s, n):
            return (rb(s, jnp.maximum(n - 1, 0)), kcol)

        def prev_map_v(s, n):
            return (rb(s, jnp.maximum(n - 1, 0)), vcol)
    else:
        def prev_map_k(s, n):
            return (s, 0)
        prev_map_v = prev_map_k

    in_specs = [
        pl.BlockSpec(memory_space=pltpu.SMEM),
        pl.BlockSpec((tq, 1024), lambda s, n: (rb(s, n), qcol)),
        pl.BlockSpec((tq, 256), lambda s, n: (rb(s, n), kcol)),
        pl.BlockSpec((tq, 256), lambda s, n: (rb(s, n), vcol)),
        pl.BlockSpec((WINDOW, 256), prev_map_k),
        pl.BlockSpec((WINDOW, 256), prev_map_v),
        pl.BlockSpec((1, A_HD), lambda s, n: (0, 0)),
        pl.BlockSpec((1, A_HD), lambda s, n: (0, 0)),
    ]
    args = [sinks, proj, proj, proj, prev_k, prev_v, qn_g.reshape(1, A_HD), kn_g.reshape(1, A_HD)]
    aliases = {}
    if alias_bufs is not None:
        for j, buf in enumerate(alias_bufs):
            in_specs.append(pl.BlockSpec(memory_space=pl.ANY))
            aliases[len(args)] = j
            args.append(buf)
    return pl.pallas_call(
        functools.partial(_swa_kernel, tq=tq, prev_raw=prev_raw, aliased=alias_bufs is not None),
        grid=(n_seq, blocks_per_seq),
        in_specs=in_specs,
        out_specs=[
            pl.BlockSpec((tq, 1024), lambda s, n: (rb(s, n), 0)),
            pl.BlockSpec((tq, 256), lambda s, n: (rb(s, n), 0)),
        ],
        out_shape=[
            jax.ShapeDtypeStruct((N_ROWS, 1024), BF16),
            jax.ShapeDtypeStruct((N_ROWS, 256), F32),
        ],
        input_output_aliases=aliases,
        compiler_params=_cparams(("arbitrary", "arbitrary")),
        name="swa_prompt" if prev_raw else "swa_sample",
    )(*args)


def _ret_tables(chunk, tv, pos0, t_len):
    lg = np.log1p(-(2.0 ** (-5.0 - np.arange(B_HEADS, dtype=np.float64))))
    idx = np.arange(chunk, dtype=np.float64)
    valid = idx < tv
    diff = idx[:, None] - idx[None, :]
    dmat = np.where((diff >= 0) & valid[:, None] & valid[None, :],
                    np.exp(np.maximum(diff, 0.0)[None] * lg[:, None, None]), 0.0)
    din = np.exp((idx + 1.0)[:, None] * lg[None])
    din = np.repeat(din, B_DV, axis=1)
    kfac = np.where(valid[:, None], np.exp((tv - 1.0 - idx)[:, None] * lg[None]), 0.0)
    kfac = np.repeat(kfac, B_DK, axis=1)
    gc = np.repeat(np.exp(tv * lg), B_DK)[:, None] * np.ones((1, B_DV))
    half = B_DK // 2
    inv = ROPE_BASE ** (-np.arange(half, dtype=np.float64) / half)
    ang = (pos0 + np.arange(t_len, dtype=np.float64))[:, None] * inv[None]
    cos = np.tile(np.cos(ang), (1, 2 * B_HEADS))
    sin = np.tile(np.concatenate([-np.sin(ang), np.sin(ang)], axis=1), (1, B_HEADS))
    f = lambda a: jnp.asarray(a, dtype=F32)
    return f(dmat), f(din), f(kfac), f(gc), f(cos), f(sin)


def _ret_kernel(q_ref, k_ref, v_ref, g_ref, cos_ref, sin_ref, dmat_ref, din_ref, kfac_ref,
                gc_ref, gng_ref, gnb_ref, s0_ref, *rest):
    o_ref, s_ref = rest[-2], rest[-1]

    @pl.when(pl.program_id(1) == 0)
    def _():
        s_ref[...] = s0_ref[...]

    cos = cos_ref[...]
    sin = sin_ref[...]
    width = B_HEADS * B_DK
    lane = lax.broadcasted_iota(jnp.int32, (1, width), 1)
    first_half = (lane % B_DK) < (B_DK // 2)
    lane_head = lane // B_DK

    def rope(x):
        partner = jnp.where(first_half, pltpu.roll(x, width - B_DK // 2, 1),
                            pltpu.roll(x, B_DK // 2, 1))
        return x * cos + partner * sin

    qr = rope(q_ref[...])
    kr = rope(k_ref[...]) * (B_DK ** -0.5)
    kb = kr.astype(BF16)
    kdec = (kr * kfac_ref[...]).astype(BF16)
    s_old = s_ref[...]
    s_b = s_old.astype(BF16)
    gc = gc_ref[...]
    for h in range(B_HEADS):
        vs = slice(h * B_DV, (h + 1) * B_DV)
        rs = slice(h * B_DK, (h + 1) * B_DK)
        qm = jnp.where(lane_head == h, qr, 0.0).astype(BF16)
        a = lax.dot_general(qm, kb, (((1,), (1,)), ((), ())),
                            preferred_element_type=F32) * dmat_ref[h]
        vh = v_ref[:, vs].astype(BF16)
        o = jnp.dot(a.astype(BF16), vh, preferred_element_type=F32)
        o = o + jnp.dot(qm, s_b, preferred_element_type=F32) * din_ref[:, vs]
        u = lax.dot_general(kdec, vh, (((0,), (0,)), ((), ())), preferred_element_type=F32)
        s_ref[rs, :] = gc[rs, :] * s_old[rs, :] + u[rs, :]
        mu = jnp.mean(o, axis=-1, keepdims=True)
        oc = o - mu
        var = jnp.mean(oc * oc, axis=-1, keepdims=True)
        y = oc * lax.rsqrt(var + EPS) * gng_ref[:, vs] + gnb_ref[:, vs]
        y = y * _silu(g_ref[:, vs])
        o_ref[:, vs] = y.astype(o_ref.dtype)


def _ret(proj, s0, gn_g, gn_b, tables, *, chunk, n_seq, chunks_per_seq, row_block0, table_per_chunk,
         alias_buf=None):
    dmat, din, kfac, gc, cos, sin = tables
    qcol, kcol, vcol, gcol = 1536 // 256, 1792 // 256, 2048 // 512, 2560 // 512

    def rb(s, c):
        return row_block0 + s * chunks_per_seq + c

    tmap = (lambda s, c: (c, 0)) if table_per_chunk else (lambda s, c: (0, 0))
    const2 = lambda s, c: (0, 0)
    in_specs = [
        pl.BlockSpec((chunk, 256), lambda s, c: (rb(s, c), qcol)),
        pl.BlockSpec((chunk, 256), lambda s, c: (rb(s, c), kcol)),
        pl.BlockSpec((chunk, 512), lambda s, c: (rb(s, c), vcol)),
        pl.BlockSpec((chunk, 512), lambda s, c: (rb(s, c), gcol)),
        pl.BlockSpec((chunk, 256), tmap),
        pl.BlockSpec((chunk, 256), tmap),
        pl.BlockSpec((B_HEADS, chunk, chunk), lambda s, c: (0, 0, 0)),
        pl.BlockSpec((chunk, 512), const2),
        pl.BlockSpec((chunk, 256), const2),
        pl.BlockSpec((256, B_DV), const2),
        pl.BlockSpec((1, 512), const2),
        pl.BlockSpec((1, 512), const2),
        pl.BlockSpec((None, 256, B_DV), lambda s, c: (s, 0, 0)),
    ]
    args = [proj, proj, proj, proj, cos, sin, dmat, din, kfac, gc,
            gn_g.reshape(1, 512), gn_b.reshape(1, 512), s0.reshape(n_seq, 256, B_DV)]
    aliases = {}
    if alias_buf is not None:
        in_specs.append(pl.BlockSpec(memory_space=pl.ANY))
        aliases[len(args)] = 0
        args.append(alias_buf)
    o, s_new = pl.pallas_call(
        _ret_kernel,
        grid=(n_seq, chunks_per_seq),
        in_specs=in_specs,
        out_specs=[
            pl.BlockSpec((chunk, 512), lambda s, c: (rb(s, c), 0)),
            pl.BlockSpec((None, 256, B_DV), lambda s, c: (s, 0, 0)),
        ],
        out_shape=[
            jax.ShapeDtypeStruct((N_ROWS, 512), BF16),
            jax.ShapeDtypeStruct((n_seq, 256, B_DV), F32),
        ],
        input_output_aliases=aliases,
        compiler_params=_cparams(("arbitrary", "arbitrary")),
        name="ret_prompt" if alias_buf is None else "ret_sample",
    )(*args)
    return o, s_new.reshape(n_seq, B_HEADS, B_DK, B_DV)


def _hgrn_kernel(q_ref, f_ref, i_ref, g_ref, lb_ref, ng_ref, st0_ref, *rest, rows, tv):
    o_ref, st_ref, q_s, k_s, bc_s, qd_s, kd_s, eb_s = rest[-8:]
    nblk = rows // HG_BLK
    width = C_HEADS * C_DK

    @pl.when(pl.program_id(1) == 0)
    def _():
        st_ref[...] = st0_ref[...]

    z = f_ref[...]
    lb = lb_ref[...]
    sig = _sigmoid(z)
    lf = jnp.log(lb + (1.0 - lb) * sig)
    k = (1.0 - lb) * _sigmoid(-z)
    if tv < rows:
        rvalid = lax.broadcasted_iota(jnp.int32, (rows, 1), 0) < tv
        lf = jnp.where(rvalid, lf, 0.0)
        k = jnp.where(rvalid, k, 0.0)
    ri = lax.broadcasted_iota(jnp.int32, (rows, rows), 0)
    ci = lax.broadcasted_iota(jnp.int32, (rows, rows), 1)
    same_blk = (ri // HG_BLK) == (ci // HG_BLK)
    tri = (same_blk & (ci <= ri)).astype(F32)
    upp = (same_blk & (ci > ri)).astype(F32)
    bc = jnp.dot(tri, lf, precision=lax.Precision.HIGHEST, preferred_element_type=F32)
    rest_dec = jnp.dot(upp, lf, precision=lax.Precision.HIGHEST, preferred_element_type=F32)
    blast_rows = bc + rest_dec
    q = _silu(q_ref[...])
    q_s[...] = q
    k_s[...] = k
    bc_s[...] = bc
    qd_s[...] = q * jnp.exp(bc)
    kd_s[...] = k * jnp.exp(rest_dec)
    eb_s[...] = jnp.exp(blast_rows)
    ng = ng_ref[...]
    rowi = lax.broadcasted_iota(jnp.int32, (HG_BLK, 1), 0)

    def blk_body(blk, carry):
        r0 = pl.multiple_of(blk * HG_BLK, HG_BLK)
        rsl = pl.ds(r0, HG_BLK)
        for h in range(C_HEADS):
            hs = slice(h * C_DK, (h + 1) * C_DK)
            vsl = slice(h * C_DV, (h + 1) * C_DV)
            qb = q_s[rsl, hs]
            kb = k_s[rsl, hs]
            bcb = bc_s[rsl, hs]
            vb = i_ref[rsl, hs]
            st = st_ref[vsl, :]
            o = lax.dot_general(qd_s[rsl, hs].astype(BF16), st.astype(BF16),
                                (((1,), (1,)), ((), ())), preferred_element_type=F32)
            for j in range(HG_BLK):
                dec = jnp.exp(jnp.minimum(bcb - bcb[j:j + 1, :], 0.0))
                a = jnp.sum(qb * kb[j:j + 1, :] * dec, axis=-1, keepdims=True)
                a = jnp.where(rowi >= j, a, 0.0)
                o = o + a * vb[j:j + 1, :]
            u = lax.dot_general(vb.astype(BF16), kd_s[rsl, hs].astype(BF16),
                                (((0,), (0,)), ((), ())), preferred_element_type=F32)
            st_ref[vsl, :] = st * eb_s[pl.ds(r0, 1), hs] + u
            y = o * lax.rsqrt(jnp.mean(o * o, axis=-1, keepdims=True) + EPS) * ng
            y = y * _silu(g_ref[rsl, hs])
            o_ref[rsl, vsl] = y.astype(o_ref.dtype)
        return carry

    lax.fori_loop(0, nblk, blk_body, 0)


def _hgrn(proj, st0, lb, ng, *, rows, tv, n_seq, tiles_per_seq, row_block0, alias_buf=None):
    qcol, fcol, icol, gcol = 3072 // 512, 3584 // 512, 4096 // 512, 4608 // 512

    def rb(s, c):
        return row_block0 + s * tiles_per_seq + c

    const2 = lambda s, c: (0, 0)
    in_specs = [
        pl.BlockSpec((rows, 512), lambda s, c: (rb(s, c), qcol)),
        pl.BlockSpec((rows, 512), lambda s, c: (rb(s, c), fcol)),
        pl.BlockSpec((rows, 512), lambda s, c: (rb(s, c), icol)),
        pl.BlockSpec((rows, 512), lambda s, c: (rb(s, c), gcol)),
        pl.BlockSpec((1, 512), const2),
        pl.BlockSpec((1, C_DV), const2),
        pl.BlockSpec((None, 512, C_DK), lambda s, c: (s, 0, 0)),
    ]
    args = [proj, proj, proj, proj, lb.reshape(1, 512), ng.reshape(1, C_DV), st0]
    aliases = {}
    if alias_buf is not None:
        in_specs.append(pl.BlockSpec(memory_space=pl.ANY))
        aliases[len(args)] = 0
        args.append(alias_buf)
    return pl.pallas_call(
        functools.partial(_hgrn_kernel, rows=rows, tv=tv),
        grid=(n_seq, tiles_per_seq),
        in_specs=in_specs,
        out_specs=[
            pl.BlockSpec((rows, 512), lambda s, c: (rb(s, c), 0)),
            pl.BlockSpec((None, 512, C_DK), lambda s, c: (s, 0, 0)),
        ],
        out_shape=[
            jax.ShapeDtypeStruct((N_ROWS, 512), BF16),
            jax.ShapeDtypeStruct((n_seq, 512, C_DK), F32),
        ],
        scratch_shapes=[pltpu.VMEM((rows, 512), F32) for _ in range(6)],
        input_output_aliases=aliases,
        compiler_params=_cparams(("arbitrary", "arbitrary")),
        name="hgrn_prompt" if alias_buf is None else "hgrn_sample",
    )(*args)


def _mm_out_kernel(oa_ref, ob_ref, oc_ref, w_ref, x_ref, g_ref, o_ref, wb_ref):
    @pl.when(pl.program_id(1) == 0)
    def _():
        wb_ref[...] = w_ref[...].astype(BF16)

    acc = jnp.dot(oa_ref[...], wb_ref[0:1024, :], preferred_element_type=F32)
    acc = acc + jnp.dot(ob_ref[...], wb_ref[1024:1536, :], preferred_element_type=F32)
    acc = acc + jnp.dot(oc_ref[...], wb_ref[1536:2048, :], preferred_element_type=F32)
    o_ref[...] = x_ref[...] + _tile_mod(g_ref, pl.program_id(1)) * acc


def _mm_out(oa, ob, oc, w_out, x, mod, l):
    tn = 1024
    ncol = D_MODEL // tn
    return pl.pallas_call(
        _mm_out_kernel,
        grid=(ncol, N_TILES),
        in_specs=[
            pl.BlockSpec((TM, 1024), lambda j, i: (i, 0)),
            pl.BlockSpec((TM, 512), lambda j, i: (i, 0)),
            pl.BlockSpec((TM, 512), lambda j, i: (i, 0)),
            pl.BlockSpec((None, D_MODEL, tn), lambda j, i: (l, 0, j)),
            pl.BlockSpec((TM, tn), lambda j, i: (i, j)),
            _mod_spec(l, "g1", tn, lambda j, i: j),
        ],
        out_specs=pl.BlockSpec((TM, tn), lambda j, i: (i, j)),
        out_shape=jax.ShapeDtypeStruct((N_ROWS, D_MODEL), F32),
        scratch_shapes=[pltpu.VMEM((D_MODEL, tn), BF16)],
        compiler_params=_cparams(("arbitrary", "arbitrary")),
        name="mm_out",
    )(oa, ob, oc, w_out, x, mod)


META_GSEL = 0
META_RANK = 1
META_GATE = 20
XW = D_MODEL + 128


def _route_kernel(x_ref, g_ref, sc_ref, sh_ref, wr_ref, br_ref, hx_ref, meta_ref, cnt_ref, run_ref):
    i = pl.program_id(0)

    @pl.when(i == 0)
    def _():
        run_ref[...] = jnp.zeros_like(run_ref)

    h = _norm_mod_rows(x_ref[...], g_ref[...], sc_ref, sh_ref, i)
    hx_ref[:, 0:D_MODEL] = h
    logits = jnp.dot(h, wr_ref[...], precision=lax.Precision.HIGHEST,
                     preferred_element_type=F32) + br_ref[...]
    col = lax.broadcasted_iota(jnp.int32, logits.shape, 1).astype(F32)
    big = 1e9
    is_g = col < N_GROUPS
    gl = jnp.where(is_g, logits, -jnp.inf)
    gmax = jnp.max(gl, axis=-1, keepdims=True)
    gsel = jnp.min(jnp.where(gl == gmax, col, big), axis=-1, keepdims=True)
    gden = jnp.sum(jnp.where(is_g, jnp.exp(logits - gmax), 0.0), axis=-1, keepdims=True)
    g_w = 1.0 / gden
    e0 = N_GROUPS + EXP_PER_GROUP * gsel
    in_grp = (col >= e0) & (col < e0 + EXP_PER_GROUP)
    el = jnp.where(in_grp, logits, -jnp.inf)
    v1 = jnp.max(el, axis=-1, keepdims=True)
    i1 = jnp.min(jnp.where(el == v1, col, big), axis=-1, keepdims=True)
    el2 = jnp.where(col == i1, -jnp.inf, el)
    v2 = jnp.max(el2, axis=-1, keepdims=True)
    i2 = jnp.min(jnp.where(el2 == v2, col, big), axis=-1, keepdims=True)
    t = jnp.exp(v2 - v1)
    p1 = 1.0 / (1.0 + t)
    a1 = p1 * g_w
    a2 = t * p1 * g_w
    meta = jnp.where(col == META_GATE + (i1 - e0), a1, 0.0)
    meta = meta + jnp.where(col == META_GATE + (i2 - e0), a2, 0.0)
    meta = meta + jnp.where(col == META_GSEL, gsel, 0.0)
    rows = logits.shape[0]
    rowi = lax.broadcasted_iota(jnp.int32, (rows, 1), 0)
    is_tok = (i < PROMPT_TILES) | (rowi % SROWS == 0)
    onehot = jnp.where((col == gsel) & is_tok, 1.0, 0.0)
    ri = lax.broadcasted_iota(jnp.int32, (rows, rows), 0)
    ci = lax.broadcasted_iota(jnp.int32, (rows, rows), 1)
    tri = jnp.where(ci <= ri, 1.0, 0.0).astype(BF16)
    incl = jnp.dot(tri, onehot.astype(BF16), preferred_element_type=F32)
    run = run_ref[...]
    rank = jnp.sum(jnp.where(col == gsel, incl + run - 1.0, 0.0), axis=-1, keepdims=True)
    meta = meta + jnp.where(col == META_RANK, rank, 0.0)
    meta_ref[...] = meta
    hx_ref[:, D_MODEL:XW] = meta
    run = run + incl[rows - 1:rows, :]
    run_ref[...] = run
    cnt_ref[...] = run


def _route(x, g, mod, wr, br, l):
    return pl.pallas_call(
        _route_kernel,
        grid=(N_TILES,),
        in_specs=[
            pl.BlockSpec((TM, D_MODEL), lambda i: (i, 0)),
            pl.BlockSpec((1, D_MODEL), lambda i: (0, 0)),
            _mod_spec(l, "sc2"),
            _mod_spec(l, "sh2"),
            pl.BlockSpec((D_MODEL, 128), lambda i: (0, 0)),
            pl.BlockSpec((1, 128), lambda i: (0, 0)),
        ],
        out_specs=[
            pl.BlockSpec((TM, XW), lambda i: (i, 0)),
            pl.BlockSpec((TM, 128), lambda i: (i, 0)),
            pl.BlockSpec((1, 128), lambda i: (0, 0)),
        ],
        out_shape=[
            jax.ShapeDtypeStruct((N_ROWS, XW), F32),
            jax.ShapeDtypeStruct((N_ROWS, 128), F32),
            jax.ShapeDtypeStruct((1, 128), F32),
        ],
        scratch_shapes=[pltpu.VMEM((1, 128), F32)],
        compiler_params=_cparams(("arbitrary",)),
        name="route",
    )(x, g.reshape(1, D_MODEL), mod, mod, wr, br)


def _moe_positions(meta, cnt):
    i32 = jnp.int32
    counts = cnt[0, :N_GROUPS].astype(i32)
    ntile = (counts + TM - 1) // TM
    tstart = jnp.cumsum(ntile) - ntile
    gsel = meta[:, META_GSEL].astype(i32)
    rank = meta[:, META_RANK].astype(i32)
    first_row = jnp.zeros_like(gsel)
    for g in range(N_GROUPS):
        first_row = jnp.where(gsel == g, tstart[g] * TM, first_row)
    pos = first_row + rank
    pos_all = jnp.concatenate([pos[:NP_ROWS], jnp.repeat(pos[NP_ROWS::SROWS], SROWS)])
    plan = jnp.concatenate([tstart, ntile]).astype(i32)
    return pos_all, plan


def _scatter_kernel(pos_ref, src_hbm, buf_hbm, out_hbm, sem):
    del buf_hbm
    i = pl.program_id(0)
    base = i * TM

    def row_copy(row):
        return pltpu.make_async_copy(src_hbm.at[pl.ds(row, 1), :],
                                     out_hbm.at[pl.ds(pos_ref[row], 1), :], sem)

    def run(n, stride):
        def issue(k, c):
            row_copy(base + k * stride).start()
            return c

        def drain(k, c):
            row_copy(base + k * stride).wait()
            return c

        lax.fori_loop(0, n, issue, 0, unroll=8)
        lax.fori_loop(0, n, drain, 0, unroll=8)

    @pl.when(i < PROMPT_TILES)
    def _():
        run(TM, 1)

    @pl.when(i == PROMPT_TILES)
    def _():
        run(DEC_BATCH, SROWS)


def _moe_scatter(hx, buf, pos_all):
    grid_spec = pltpu.PrefetchScalarGridSpec(
        num_scalar_prefetch=1,
        grid=(N_TILES,),
        in_specs=[pl.BlockSpec(memory_space=pl.ANY), pl.BlockSpec(memory_space=pl.ANY)],
        out_specs=pl.BlockSpec(memory_space=pl.ANY),
        scratch_shapes=[pltpu.SemaphoreType.DMA(())],
    )
    return pl.pallas_call(
        _scatter_kernel,
        grid_spec=grid_spec,
        out_shape=jax.ShapeDtypeStruct((MOE_ROWS, XW), F32),
        input_output_aliases={2: 0},
        compiler_params=_cparams(("arbitrary",)),
        name="moe_scatter",
    )(pos_all, hx, buf)


def _work_item(w, plan_ref, per_tile):
    i32 = jnp.int32
    nt = [plan_ref[N_GROUPS + g] for g in range(N_GROUPS)]
    ts = [plan_ref[g] for g in range(N_GROUPS)]
    ends = []
    acc = 0
    for g in range(N_GROUPS):
        acc = acc + per_tile * nt[g]
        ends.append(acc)
    valid = w < ends[-1]
    wc = jnp.minimum(w, ends[-1] - 1)
    g = sum((wc >= ends[k]).astype(i32) for k in range(N_GROUPS - 1))

    def pick(vals):
        out = vals[N_GROUPS - 1]
        for k in range(N_GROUPS - 2, -1, -1):
            out = jnp.where(g == k, vals[k], out)
        return out

    r = wc - pick([0] + ends[:-1])
    ntg = pick(nt)
    c = sum((r >= m * ntg).astype(i32) for m in range(1, per_tile))
    t_in = r - c * ntg
    return g, c, pick(ts) + t_in, t_in == 0, valid


def _hidden_kernel(plan_ref, x_ref, w1_ref, w3_ref, o_ref, w1b, w3b):
    _, e, _, first, valid = _work_item(pl.program_id(0), plan_ref, EXP_PER_GROUP)

    @pl.when(valid & first)
    def _():
        w1b[...] = w1_ref[...].astype(BF16)
        w3b[...] = w3_ref[...].astype(BF16)

    @pl.when(valid)
    def _():
        x = x_ref[:, 0:D_MODEL].astype(BF16)
        a = jnp.dot(x, w1b[...], preferred_element_type=F32)
        b = jnp.dot(x, w3b[...], preferred_element_type=F32)
        m = x_ref[:, D_MODEL:XW]
        col = lax.broadcasted_iota(jnp.int32, m.shape, 1)
        gate = jnp.sum(jnp.where(col == META_GATE + e, m, 0.0), axis=-1, keepdims=True)
        o_ref[...] = (_silu(a) * b * gate).astype(o_ref.dtype)


def _moe_hidden(xs, w1, w3, plan, l):
    def item(w, plan):
        return _work_item(w, plan, EXP_PER_GROUP)

    def w_map(w, plan):
        g, e, _, _, _ = item(w, plan)
        return (l, g * EXP_PER_GROUP + e, 0, 0)

    grid_spec = pltpu.PrefetchScalarGridSpec(
        num_scalar_prefetch=1,
        grid=(HID_WORK,),
        in_specs=[
            pl.BlockSpec((TM, XW), lambda w, plan: (item(w, plan)[2], 0)),
            pl.BlockSpec((None, None, D_MODEL, D_EXPERT), w_map),
            pl.BlockSpec((None, None, D_MODEL, D_EXPERT), w_map),
        ],
        out_specs=pl.BlockSpec((TM, D_EXPERT), lambda w, plan: (item(w, plan)[2], item(w, plan)[1])),
        scratch_shapes=[pltpu.VMEM((D_MODEL, D_EXPERT), BF16),
                        pltpu.VMEM((D_MODEL, D_EXPERT), BF16)],
    )
    return pl.pallas_call(
        _hidden_kernel,
        grid_spec=grid_spec,
        out_shape=jax.ShapeDtypeStruct((MOE_ROWS, EXP_PER_GROUP * D_EXPERT), BF16),
        compiler_params=_cparams(("arbitrary",)),
        name="moe_hidden",
    )(plan, xs, w1, w3)


def _down_kernel(plan_ref, h_ref, w_ref, o_ref, wb):
    _, _, _, first, valid = _work_item(pl.program_id(0), plan_ref, DOWN_NCOL)

    @pl.when(valid & first)
    def _():
        wb[...] = w_ref[...].astype(BF16)

    @pl.when(valid)
    def _():
        o_ref[...] = jnp.dot(h_ref[...], wb[...], preferred_element_type=F32)


def _moe_down(hid, w2g, plan, l):
    kdim = EXP_PER_GROUP * D_EXPERT

    def item(w, plan):
        return _work_item(w, plan, DOWN_NCOL)

    grid_spec = pltpu.PrefetchScalarGridSpec(
        num_scalar_prefetch=1,
        grid=(DOWN_WORK,),
        in_specs=[
            pl.BlockSpec((TM, kdim), lambda w, plan: (item(w, plan)[2], 0)),
            pl.BlockSpec((None, None, kdim, DOWN_TN),
                         lambda w, plan: (l, item(w, plan)[0], 0, item(w, plan)[1])),
        ],
        out_specs=pl.BlockSpec((TM, DOWN_TN), lambda w, plan: (item(w, plan)[2], item(w, plan)[1])),
        scratch_shapes=[pltpu.VMEM((kdim, DOWN_TN), BF16)],
    )
    return pl.pallas_call(
        _down_kernel,
        grid_spec=grid_spec,
        out_shape=jax.ShapeDtypeStruct((MOE_ROWS, D_MODEL), F32),
        compiler_params=_cparams(("arbitrary",)),
        name="moe_down",
    )(plan, hid, w2g)


def _res_kernel(pos_ref, x_ref, y_hbm, g2_ref, *rest, with_norm):
    if with_norm:
        lg_ref, sc_ref, sh_ref, xo_ref, ho_ref, ybuf, sem = rest
    else:
        xo_ref, ybuf, sem = rest
    i = pl.program_id(0)
    base = i * TM

    def row_copy(k):
        return pltpu.make_async_copy(y_hbm.at[pl.ds(pos_ref[base + k], 1), :],
                                     ybuf.at[pl.ds(k, 1), :], sem)

    def issue(k, c):
        row_copy(k).start()
        return c

    def drain(k, c):
        row_copy(k).wait()
        return c

    lax.fori_loop(0, TM, issue, 0, unroll=8)
    lax.fori_loop(0, TM, drain, 0, unroll=8)
    xn = x_ref[...] + _tile_mod(g2_ref, i) * ybuf[...]
    xo_ref[...] = xn
    if with_norm:
        ho_ref[...] = _norm_mod_rows(xn, lg_ref[...], sc_ref, sh_ref, i).astype(ho_ref.dtype)


def _residual(x, y_sorted, pos_all, mod, l, next_g=None):
    with_norm = next_g is not None
    row_spec = pl.BlockSpec((TM, D_MODEL), lambda i, pos: (i, 0))
    in_specs = [row_spec, pl.BlockSpec(memory_space=pl.ANY), _mod_spec(l, "g2")]
    args = [pos_all, x, y_sorted, mod]
    out_specs = [row_spec]
    out_shape = [jax.ShapeDtypeStruct((N_ROWS, D_MODEL), F32)]
    if with_norm:
        in_specs += [pl.BlockSpec((1, D_MODEL), lambda i, pos: (0, 0)),
                     _mod_spec(l + 1, "sc1"), _mod_spec(l + 1, "sh1")]
        args += [next_g.reshape(1, D_MODEL), mod, mod]
        out_specs.append(row_spec)
        out_shape.append(jax.ShapeDtypeStruct((N_ROWS, D_MODEL), BF16))
    grid_spec = pltpu.PrefetchScalarGridSpec(
        num_scalar_prefetch=1,
        grid=(N_TILES,),
        in_specs=in_specs,
        out_specs=out_specs,
        scratch_shapes=[pltpu.VMEM((TM, D_MODEL), F32), pltpu.SemaphoreType.DMA(())],
    )
    return pl.pallas_call(
        functools.partial(_res_kernel, with_norm=with_norm),
        grid_spec=grid_spec,
        out_shape=out_shape,
        compiler_params=_cparams(("arbitrary",)),
        name="residual_norm" if with_norm else "residual",
    )(*args)


def kernel(x_prompt, x_sample, c_prompt, c_sample, cache_swa_k, cache_swa_v, state_ret, state_hgrn,
           w_ada, b_ada, ln1_g, w_in, qn_g, kn_g, attn_sinks, ret_gn_g, ret_gn_b, hgrn_lb, hgrn_ng,
           w_out, ln2_g, w_rg, b_rg, w_re, b_re, w1, w3, w2):
    i32 = jnp.int32
    lbp = jax.nn.softmax(hgrn_lb.astype(F32), axis=0)
    lb_all = jnp.cumsum(lbp, axis=0) - lbp[0]

    assert TM // SROWS == DEC_BATCH and NS_ROWS == TM
    c_all = jnp.concatenate([c_prompt, jnp.zeros((C_SAMPLE0 - BATCH, D_MODEL), F32), c_sample], axis=0)
    mod = _ada(c_all, w_ada, b_ada)

    ret_tab_p = _ret_tables(RET_CHUNK, RET_CHUNK, 0.0, SEQ)
    ret_tab_s = _ret_tables(SROWS, 1, float(PAST_LEN), SROWS)
    w2g = w2.reshape(DEPTH, N_GROUPS, EXP_PER_GROUP * D_EXPERT, D_MODEL)
    xs_buf = jnp.zeros((MOE_ROWS, XW), F32)

    x, h = _norm_mod_first(x_prompt.reshape(NP_ROWS, D_MODEL), x_sample.reshape(DEC_BATCH, D_MODEL),
                           ln1_g[0], mod, 0)

    outs = {k: [] for k in ("kp", "vp", "rp", "hp", "ks", "vs", "rs", "hs")}
    for l in range(DEPTH):
        proj = _mm_in(h, w_in, l)

        oa, kn = _swa(proj, proj, proj, attn_sinks[l], qn_g[l], kn_g[l], tq=WINDOW, n_seq=BATCH,
                      blocks_per_seq=SEQ // WINDOW, row_block0=0, prev_raw=True)
        ck = cache_swa_k[l].reshape(DEC_BATCH * WINDOW, A_KV * A_HD)
        cv = cache_swa_v[l].reshape(DEC_BATCH * WINDOW, A_KV * A_HD)
        oa, kn = _swa(proj, ck, cv, attn_sinks[l], qn_g[l], kn_g[l], tq=SROWS, n_seq=DEC_BATCH,
                      blocks_per_seq=1, row_block0=NP_ROWS // SROWS, prev_raw=False,
                      alias_bufs=(oa, kn))
        zero_ret = jnp.zeros((BATCH, B_HEADS, B_DK, B_DV), F32)
        ob, rp = _ret(proj, zero_ret, ret_gn_g[l], ret_gn_b[l], ret_tab_p, chunk=RET_CHUNK,
                      n_seq=BATCH, chunks_per_seq=SEQ // RET_CHUNK, row_block0=0, table_per_chunk=True)
        ob, rs = _ret(proj, state_ret[l], ret_gn_g[l], ret_gn_b[l], ret_tab_s, chunk=SROWS,
                      n_seq=DEC_BATCH, chunks_per_seq=1, row_block0=NP_ROWS // SROWS,
                      table_per_chunk=False, alias_buf=ob)
        zero_hg = jnp.zeros((BATCH, C_HEADS * C_DV, C_DK), F32)
        oc, hp_t = _hgrn(proj, zero_hg, lb_all[l], hgrn_ng[l], rows=HG_TILE, tv=HG_TILE, n_seq=BATCH,
                         tiles_per_seq=SEQ // HG_TILE, row_block0=0)
        st_s = jnp.swapaxes(state_hgrn[l], -1, -2).reshape(DEC_BATCH, C_HEADS * C_DV, C_DK)
        oc, hs_t = _hgrn(proj, st_s, lb_all[l], hgrn_ng[l], rows=SROWS, tv=1, n_seq=DEC_BATCH,
                         tiles_per_seq=1, row_block0=NP_ROWS // SROWS, alias_buf=oc)

        x1 = _mm_out(oa, ob, oc, w_out, x, mod, l)

        wr = jnp.concatenate([w_rg[l], w_re[l], jnp.zeros((D_MODEL, 128 - N_GROUPS - N_EXPERTS), F32)], axis=1)
        br = jnp.concatenate([b_rg[l], b_re[l], jnp.zeros((128 - N_GROUPS - N_EXPERTS,), F32)]).reshape(1, 128)
        hx, meta, cnt = _route(x1, ln2_g[l], mod, wr, br, l)
        pos_all, plan = _moe_positions(meta, cnt)
        xs_buf = _moe_scatter(hx, xs_buf, pos_all)
        hid = _moe_hidden(xs_buf, w1, w3, plan, l)
        y_sorted = _moe_down(hid, w2g, plan, l)
        if l + 1 < DEPTH:
            x, h = _residual(x1, y_sorted, pos_all, mod, l, ln1_g[l + 1])
        else:
            (x,) = _residual(x1, y_sorted, pos_all, mod, l)

        knp = kn[:NP_ROWS].reshape(BATCH, SEQ, A_KV, A_HD)
        vpp = proj[:NP_ROWS, 1280:1536].reshape(BATCH, SEQ, A_KV, A_HD)
        outs["kp"].append(knp[:, SEQ - WINDOW:])
        outs["vp"].append(vpp[:, SEQ - WINDOW:])
        outs["rp"].append(rp)
        outs["hp"].append(jnp.swapaxes(hp_t.reshape(BATCH, C_HEADS, C_DV, C_DK), -1, -2))
        k_new = kn[NP_ROWS::SROWS].reshape(DEC_BATCH, 1, A_KV, A_HD)
        v_new = proj[NP_ROWS::SROWS, 1280:1536].reshape(DEC_BATCH, 1, A_KV, A_HD)
        outs["ks"].append(jnp.concatenate([cache_swa_k[l][:, 1:], k_new], axis=1))
        outs["vs"].append(jnp.concatenate([cache_swa_v[l][:, 1:], v_new], axis=1))
        outs["rs"].append(rs)
        outs["hs"].append(jnp.swapaxes(hs_t.reshape(DEC_BATCH, C_HEADS, C_DV, C_DK), -1, -2))

    y_prompt = x[:NP_ROWS].reshape(BATCH, SEQ, D_MODEL)
    y_sample = x[NP_ROWS::SROWS].reshape(DEC_BATCH, 1, D_MODEL)
    st = lambda k: jnp.stack(outs[k])
    return (y_prompt, y_sample, st("kp"), st("vp"), st("rp"), st("hp"),
            st("ks"), st("vs"), st("rs"), st("hs"))
```

```python
import functools

import numpy as np
import jax
import jax.numpy as jnp
from jax import lax
from jax.experimental import pallas as pl
from jax.experimental.pallas import tpu as pltpu

F32 = jnp.float32
BF16 = jnp.bfloat16

D_MODEL = 2048
BATCH = 2
SEQ = 4096
DEPTH = 4
DEC_BATCH = 32
PAST_LEN = 16384
A_HD = 128
A_HEADS = 8
A_KV = 2
A_GROUP = 4
WINDOW = 128
B_DV = 128
B_DK = 64
B_HEADS = 4
C_DK = 128
C_DV = 128
C_HEADS = 4
ROPE_BASE = 10000.0
N_GROUPS = 4
EXP_PER_GROUP = 4
N_EXPERTS = 16
D_EXPERT = 512
EPS = 1e-6
NEG_BIG = -1e30
IN_WIDTH = 5120

SROWS = 16
NP_ROWS = BATCH * SEQ
NS_ROWS = DEC_BATCH * SROWS
N_ROWS = NP_ROWS + NS_ROWS
TM = 512
N_TILES = N_ROWS // TM
MOD_ROWS = 16
N_TOK = NP_ROWS + DEC_BATCH
MOE_TILES = (N_TOK + N_GROUPS * (TM - 1)) // TM
MOE_ROWS = MOE_TILES * TM
HID_WORK = MOE_TILES * EXP_PER_GROUP
DOWN_TN = 1024
DOWN_NCOL = D_MODEL // DOWN_TN
DOWN_WORK = MOE_TILES * DOWN_NCOL

RET_CHUNK = 256
HG_TILE = 256
HG_BLK = 16

VMEM_LIMIT = 48 * 1024 * 1024


def _cparams(sem):
    return pltpu.CompilerParams(dimension_semantics=sem, vmem_limit_bytes=VMEM_LIMIT)


def _sigmoid(x):
    return 1.0 / (1.0 + jnp.exp(-x))


def _silu(x):
    return x * _sigmoid(x)


C_ROWS = 40
C_SAMPLE0 = 8
PROMPT_TILES = NP_ROWS // TM
MOD_COL = dict(sh1=0, sc1=1, g1=2, sh2=3, sc2=4, g2=5)


def _tile_mod(m_ref, i):
    n = m_ref.shape[-1]
    seq = jnp.minimum(i // (SEQ // TM), BATCH - 1)
    prow = jnp.broadcast_to(m_ref[pl.ds(seq, 1), :], (DEC_BATCH, n))
    srows = m_ref[C_SAMPLE0:C_SAMPLE0 + DEC_BATCH, :]
    m = jnp.where(i < PROMPT_TILES, prow, srows)
    return jnp.broadcast_to(m[:, None, :], (DEC_BATCH, SROWS, n)).reshape(TM, n)


def _mod_spec(l, name, tn=D_MODEL, colmap=None):
    base = MOD_COL[name] * (D_MODEL // tn)
    if colmap is None:
        return pl.BlockSpec((None, C_ROWS, tn), lambda *ids: (l, 0, base))
    return pl.BlockSpec((None, C_ROWS, tn), lambda *ids: (l, 0, base + colmap(*ids)))


def _ada_kernel(c_ref, w_ref, b_ref, o_ref):
    c = _silu(c_ref[...])
    o_ref[...] = jnp.dot(c.astype(BF16), w_ref[...].astype(BF16),
                         preferred_element_type=F32) + b_ref[...]


def _ada(c_all, w_ada, b_ada):
    rows = c_all.shape[0]
    tn = 1024
    ncol = w_ada.shape[2] // tn
    return pl.pallas_call(
        _ada_kernel,
        grid=(DEPTH, ncol),
        in_specs=[
            pl.BlockSpec((rows, D_MODEL), lambda l, j: (0, 0)),
            pl.BlockSpec((None, D_MODEL, tn), lambda l, j: (l, 0, j)),
            pl.BlockSpec((None, 1, tn), lambda l, j: (l, 0, j)),
        ],
        out_specs=pl.BlockSpec((None, rows, tn), lambda l, j: (l, 0, j)),
        out_shape=jax.ShapeDtypeStruct((DEPTH, rows, w_ada.shape[2]), F32),
        compiler_params=_cparams(("arbitrary", "arbitrary")),
        name="ada",
    )(c_all, w_ada, b_ada.reshape(DEPTH, 1, -1))


def _norm_mod_rows(x, g, sc_ref, sh_ref, i):
    y = x * lax.rsqrt(jnp.mean(x * x, axis=-1, keepdims=True) + EPS) * g
    return y * (1.0 + _tile_mod(sc_ref, i)) + _tile_mod(sh_ref, i)


def _norm_kernel(xp_ref, xs_ref, g_ref, sc_ref, sh_ref, x_ref, o_ref):
    i = pl.program_id(0)
    xs = xs_ref[...]
    rid = lax.broadcasted_iota(jnp.int32, (DEC_BATCH, SROWS, 1), 1)
    blk = jnp.where(rid == 0, jnp.broadcast_to(xs[:, None, :], (DEC_BATCH, SROWS, D_MODEL)), 0.0)
    x = jnp.where(i < PROMPT_TILES, xp_ref[...], blk.reshape(TM, D_MODEL))
    x_ref[...] = x
    o_ref[...] = _norm_mod_rows(x, g_ref[...], sc_ref, sh_ref, i).astype(o_ref.dtype)


def _norm_mod_first(x_prompt, x_sample, g, mod, l):
    return pl.pallas_call(
        _norm_kernel,
        grid=(N_TILES,),
        in_specs=[
            pl.BlockSpec((TM, D_MODEL), lambda i: (jnp.minimum(i, PROMPT_TILES - 1), 0)),
            pl.BlockSpec((DEC_BATCH, D_MODEL), lambda i: (0, 0)),
            pl.BlockSpec((1, D_MODEL), lambda i: (0, 0)),
            _mod_spec(l, "sc1"),
            _mod_spec(l, "sh1"),
        ],
        out_specs=[
            pl.BlockSpec((TM, D_MODEL), lambda i: (i, 0)),
            pl.BlockSpec((TM, D_MODEL), lambda i: (i, 0)),
        ],
        out_shape=[
            jax.ShapeDtypeStruct((N_ROWS, D_MODEL), F32),
            jax.ShapeDtypeStruct((N_ROWS, D_MODEL), BF16),
        ],
        compiler_params=_cparams(("arbitrary",)),
        name="norm_first",
    )(x_prompt, x_sample, g.reshape(1, D_MODEL), mod, mod)


def _mm_in_kernel(x_ref, w_ref, o_ref, wb_ref):
    @pl.when(pl.program_id(1) == 0)
    def _():
        wb_ref[...] = w_ref[...].astype(BF16)

    o_ref[...] = jnp.dot(x_ref[...], wb_ref[...], preferred_element_type=F32)


def _mm_in(h, w_in, l):
    tn = 1024
    ncol = IN_WIDTH // tn
    return pl.pallas_call(
        _mm_in_kernel,
        grid=(ncol, N_TILES),
        in_specs=[
            pl.BlockSpec((TM, D_MODEL), lambda j, i: (i, 0)),
            pl.BlockSpec((None, D_MODEL, tn), lambda j, i: (l, 0, j)),
        ],
        out_specs=pl.BlockSpec((TM, tn), lambda j, i: (i, j)),
        out_shape=jax.ShapeDtypeStruct((N_ROWS, IN_WIDTH), F32),
        scratch_shapes=[pltpu.VMEM((D_MODEL, tn), BF16)],
        compiler_params=_cparams(("arbitrary", "arbitrary")),
        name="mm_in",
    )(h, w_in)


def _rms(x, g):
    return x * lax.rsqrt(jnp.mean(x * x, axis=-1, keepdims=True) + EPS) * g


def _swa_kernel(sink_ref, q_ref, k_ref, v_ref, pk_ref, pv_ref, qg_ref, kg_ref,
                *rest, tq, prev_raw, aliased):
    o_ref, kn_ref = rest[-2], rest[-1]
    if prev_raw:
        has_prev = pl.program_id(1) > 0
    else:
        has_prev = True
    qg = qg_ref[...]
    kg = kg_ref[...]
    nk = WINDOW + tq
    row = lax.broadcasted_iota(jnp.int32, (tq, nk), 0)
    col = lax.broadcasted_iota(jnp.int32, (tq, nk), 1)
    vis_prev = (col < WINDOW) & (col >= row) & has_prev
    vis_own = (col >= WINDOW) & ((col - WINDOW) <= row)
    mask = vis_prev | vis_own
    for kv in range(A_KV):
        ks = slice(kv * A_HD, (kv + 1) * A_HD)
        kn = _rms(k_ref[:, ks], kg)
        kn_ref[:, ks] = kn
        pk = pk_ref[:, ks]
        if prev_raw:
            pk = _rms(pk, kg)
        keys = jnp.concatenate([pk, kn], axis=0).astype(BF16)
        vals = jnp.concatenate([pv_ref[:, ks], v_ref[:, ks]], axis=0).astype(BF16)
        for g in range(A_GROUP):
            h = kv * A_GROUP + g
            hs = slice(h * A_HD, (h + 1) * A_HD)
            qn = _rms(q_ref[:, hs], qg).astype(BF16)
            s = lax.dot_general(qn, keys, (((1,), (1,)), ((), ())),
                                preferred_element_type=F32) * (A_HD ** -0.5)
            s = jnp.where(mask, s, NEG_BIG)
            snk = sink_ref[h]
            m = jnp.maximum(jnp.max(s, axis=-1, keepdims=True), snk)
            p = jnp.exp(s - m)
            den = jnp.sum(p, axis=-1, keepdims=True) + jnp.exp(snk - m)
            o = jnp.dot(p.astype(BF16), vals, preferred_element_type=F32) / den
            o_ref[:, hs] = o.astype(o_ref.dtype)


def _swa(proj, prev_k, prev_v, sinks, qn_g, kn_g, *, tq, n_seq, blocks_per_seq, row_block0,
         prev_raw, alias_bufs=None):
    qcol, kcol, vcol = 0, 1024 // 256, 1280 // 256

    def rb(s, n):
        return row_block0 + s * blocks_per_seq + n

    if prev_raw:
        def prev_map_k(s, n):
            return (rb(s, jnp.maximum(n - 1, 0)), kcol)

        def prev_map_v(s, n):
            return (rb(s, jnp.maximum(n - 1, 0)), vcol)
    else:
        def prev_map_k(s, n):
            return (s, 0)
        prev_map_v = prev_map_k

    in_specs = [
        pl.BlockSpec(memory_space=pltpu.SMEM),
        pl.BlockSpec((tq, 1024), lambda s, n: (rb(s, n), qcol)),
        pl.BlockSpec((tq, 256), lambda s, n: (rb(s, n), kcol)),
        pl.BlockSpec((tq, 256), lambda s, n: (rb(s, n), vcol)),
        pl.BlockSpec((WINDOW, 256), prev_map_k),
        pl.BlockSpec((WINDOW, 256), prev_map_v),
        pl.BlockSpec((1, A_HD), lambda s, n: (0, 0)),
        pl.BlockSpec((1, A_HD), lambda s, n: (0, 0)),
    ]
    args = [sinks, proj, proj, proj, prev_k, prev_v, qn_g.reshape(1, A_HD), kn_g.reshape(1, A_HD)]
    aliases = {}
    if alias_bufs is not None:
        for j, buf in enumerate(alias_bufs):
            in_specs.append(pl.BlockSpec(memory_space=pl.ANY))
            aliases[len(args)] = j
            args.append(buf)
    return pl.pallas_call(
        functools.partial(_swa_kernel, tq=tq, prev_raw=prev_raw, aliased=alias_bufs is not None),
        grid=(n_seq, blocks_per_seq),
        in_specs=in_specs,
        out_specs=[
            pl.BlockSpec((tq, 1024), lambda s, n: (rb(s, n), 0)),
            pl.BlockSpec((tq, 256), lambda s, n: (rb(s, n), 0)),
        ],
        out_shape=[
            jax.ShapeDtypeStruct((N_ROWS, 1024), BF16),
            jax.ShapeDtypeStruct((N_ROWS, 256), F32),
        ],
        input_output_aliases=aliases,
        compiler_params=_cparams(("arbitrary", "arbitrary")),
        name="swa_prompt" if prev_raw else "swa_sample",
    )(*args)


def _ret_tables(chunk, tv, pos0, t_len):
    lg = np.log1p(-(2.0 ** (-5.0 - np.arange(B_HEADS, dtype=np.float64))))
    idx = np.arange(chunk, dtype=np.float64)
    valid = idx < tv
    diff = idx[:, None] - idx[None, :]
    dmat = np.where((diff >= 0) & valid[:, None] & valid[None, :],
                    np.exp(np.maximum(diff, 0.0)[None] * lg[:, None, None]), 0.0)
    din = np.exp((idx + 1.0)[:, None] * lg[None])
    din = np.repeat(din, B_DV, axis=1)
    kfac = np.where(valid[:, None], np.exp((tv - 1.0 - idx)[:, None] * lg[None]), 0.0)
    kfac = np.repeat(kfac, B_DK, axis=1)
    gc = np.repeat(np.exp(tv * lg), B_DK)[:, None] * np.ones((1, B_DV))
    half = B_DK // 2
    inv = ROPE_BASE ** (-np.arange(half, dtype=np.float64) / half)
    ang = (pos0 + np.arange(t_len, dtype=np.float64))[:, None] * inv[None]
    cos = np.tile(np.cos(ang), (1, 2 * B_HEADS))
    sin = np.tile(np.concatenate([-np.sin(ang), np.sin(ang)], axis=1), (1, B_HEADS))
    f = lambda a: jnp.asarray(a, dtype=F32)
    return f(dmat), f(din), f(kfac), f(gc), f(cos), f(sin)


def _ret_kernel(q_ref, k_ref, v_ref, g_ref, cos_ref, sin_ref, dmat_ref, din_ref, kfac_ref,
                gc_ref, gng_ref, gnb_ref, s0_ref, *rest):
    o_ref, s_ref = rest[-2], rest[-1]

    @pl.when(pl.program_id(1) == 0)
    def _():
        s_ref[...] = s0_ref[...]

    cos = cos_ref[...]
    sin = sin_ref[...]
    width = B_HEADS * B_DK
    lane = lax.broadcasted_iota(jnp.int32, (1, width), 1)
    first_half = (lane % B_DK) < (B_DK // 2)
    lane_head = lane // B_DK

    def rope(x):
        partner = jnp.where(first_half, pltpu.roll(x, width - B_DK // 2, 1),
                            pltpu.roll(x, B_DK // 2, 1))
        return x * cos + partner * sin

    qr = rope(q_ref[...])
    kr = rope(k_ref[...]) * (B_DK ** -0.5)
    kb = kr.astype(BF16)
    kdec = (kr * kfac_ref[...]).astype(BF16)
    s_old = s_ref[...]
    s_b = s_old.astype(BF16)
    gc = gc_ref[...]
    for h in range(B_HEADS):
        vs = slice(h * B_DV, (h + 1) * B_DV)
        rs = slice(h * B_DK, (h + 1) * B_DK)
        qm = jnp.where(lane_head == h, qr, 0.0).astype(BF16)
        a = lax.dot_general(qm, kb, (((1,), (1,)), ((), ())),
                            preferred_element_type=F32) * dmat_ref[h]
        vh = v_ref[:, vs].astype(BF16)
        o = jnp.dot(a.astype(BF16), vh, preferred_element_type=F32)
        o = o + jnp.dot(qm, s_b, preferred_element_type=F32) * din_ref[:, vs]
        u = lax.dot_general(kdec, vh, (((0,), (0,)), ((), ())), preferred_element_type=F32)
        s_ref[rs, :] = gc[rs, :] * s_old[rs, :] + u[rs, :]
        mu = jnp.mean(o, axis=-1, keepdims=True)
        oc = o - mu
        var = jnp.mean(oc * oc, axis=-1, keepdims=True)
        y = oc * lax.rsqrt(var + EPS) * gng_ref[:, vs] + gnb_ref[:, vs]
        y = y * _silu(g_ref[:, vs])
        o_ref[:, vs] = y.astype(o_ref.dtype)


def _ret(proj, s0, gn_g, gn_b, tables, *, chunk, n_seq, chunks_per_seq, row_block0, table_per_chunk,
         alias_buf=None):
    dmat, din, kfac, gc, cos, sin = tables
    qcol, kcol, vcol, gcol = 1536 // 256, 1792 // 256, 2048 // 512, 2560 // 512

    def rb(s, c):
        return row_block0 + s * chunks_per_seq + c

    tmap = (lambda s, c: (c, 0)) if table_per_chunk else (lambda s, c: (0, 0))
    const2 = lambda s, c: (0, 0)
    in_specs = [
        pl.BlockSpec((chunk, 256), lambda s, c: (rb(s, c), qcol)),
        pl.BlockSpec((chunk, 256), lambda s, c: (rb(s, c), kcol)),
        pl.BlockSpec((chunk, 512), lambda s, c: (rb(s, c), vcol)),
        pl.BlockSpec((chunk, 512), lambda s, c: (rb(s, c), gcol)),
        pl.BlockSpec((chunk, 256), tmap),
        pl.BlockSpec((chunk, 256), tmap),
        pl.BlockSpec((B_HEADS, chunk, chunk), lambda s, c: (0, 0, 0)),
        pl.BlockSpec((chunk, 512), const2),
        pl.BlockSpec((chunk, 256), const2),
        pl.BlockSpec((256, B_DV), const2),
        pl.BlockSpec((1, 512), const2),
        pl.BlockSpec((1, 512), const2),
        pl.BlockSpec((None, 256, B_DV), lambda s, c: (s, 0, 0)),
    ]
    args = [proj, proj, proj, proj, cos, sin, dmat, din, kfac, gc,
            gn_g.reshape(1, 512), gn_b.reshape(1, 512), s0.reshape(n_seq, 256, B_DV)]
    aliases = {}
    if alias_buf is not None:
        in_specs.append(pl.BlockSpec(memory_space=pl.ANY))
        aliases[len(args)] = 0
        args.append(alias_buf)
    o, s_new = pl.pallas_call(
        _ret_kernel,
        grid=(n_seq, chunks_per_seq),
        in_specs=in_specs,
        out_specs=[
            pl.BlockSpec((chunk, 512), lambda s, c: (rb(s, c), 0)),
            pl.BlockSpec((None, 256, B_DV), lambda s, c: (s, 0, 0)),
        ],
        out_shape=[
            jax.ShapeDtypeStruct((N_ROWS, 512), BF16),
            jax.ShapeDtypeStruct((n_seq, 256, B_DV), F32),
        ],
        input_output_aliases=aliases,
        compiler_params=_cparams(("arbitrary", "arbitrary")),
        name="ret_prompt" if alias_buf is None else "ret_sample",
    )(*args)
    return o, s_new.reshape(n_seq, B_HEADS, B_DK, B_DV)


def _hgrn_kernel(q_ref, f_ref, i_ref, g_ref, lb_ref, ng_ref, st0_ref, *rest, rows, tv):
    o_ref, st_ref, q_s, k_s, bc_s, qd_s, kd_s, eb_s = rest[-8:]
    nblk = rows // HG_BLK
    width = C_HEADS * C_DK

    @pl.when(pl.program_id(1) == 0)
    def _():
        st_ref[...] = st0_ref[...]

    z = f_ref[...]
    lb = lb_ref[...]
    sig = _sigmoid(z)
    lf = jnp.log(lb + (1.0 - lb) * sig)
    k = (1.0 - lb) * _sigmoid(-z)
    if tv < rows:
        rvalid = lax.broadcasted_iota(jnp.int32, (rows, 1), 0) < tv
        lf = jnp.where(rvalid, lf, 0.0)
        k = jnp.where(rvalid, k, 0.0)
    ri = lax.broadcasted_iota(jnp.int32, (rows, rows), 0)
    ci = lax.broadcasted_iota(jnp.int32, (rows, rows), 1)
    same_blk = (ri // HG_BLK) == (ci // HG_BLK)
    tri = (same_blk & (ci <= ri)).astype(F32)
    upp = (same_blk & (ci > ri)).astype(F32)
    bc = jnp.dot(tri, lf, precision=lax.Precision.HIGHEST, preferred_element_type=F32)
    rest_dec = jnp.dot(upp, lf, precision=lax.Precision.HIGHEST, preferred_element_type=F32)
    blast_rows = bc + rest_dec
    q = _silu(q_ref[...])
    q_s[...] = q
    k_s[...] = k
    bc_s[...] = bc
    qd_s[...] = q * jnp.exp(bc)
    kd_s[...] = k * jnp.exp(rest_dec)
    eb_s[...] = jnp.exp(blast_rows)
    ng = ng_ref[...]
    rowi = lax.broadcasted_iota(jnp.int32, (HG_BLK, 1), 0)

    def blk_body(blk, carry):
        r0 = pl.multiple_of(blk * HG_BLK, HG_BLK)
        rsl = pl.ds(r0, HG_BLK)
        for h in range(C_HEADS):
            hs = slice(h * C_DK, (h + 1) * C_DK)
            vsl = slice(h * C_DV, (h + 1) * C_DV)
            qb = q_s[rsl, hs]
            kb = k_s[rsl, hs]
            bcb = bc_s[rsl, hs]
            vb = i_ref[rsl, hs]
            st = st_ref[vsl, :]
            o = lax.dot_general(qd_s[rsl, hs].astype(BF16), st.astype(BF16),
                                (((1,), (1,)), ((), ())), preferred_element_type=F32)
            for j in range(HG_BLK):
                dec = jnp.exp(jnp.minimum(bcb - bcb[j:j + 1, :], 0.0))
                a = jnp.sum(qb * kb[j:j + 1, :] * dec, axis=-1, keepdims=True)
                a = jnp.where(rowi >= j, a, 0.0)
                o = o + a * vb[j:j + 1, :]
            u = lax.dot_general(vb.astype(BF16), kd_s[rsl, hs].astype(BF16),
                                (((0,), (0,)), ((), ())), preferred_element_type=F32)
            st_ref[vsl, :] = st * eb_s[pl.ds(r0, 1), hs] + u
            y = o * lax.rsqrt(jnp.mean(o * o, axis=-1, keepdims=True) + EPS) * ng
            y = y * _silu(g_ref[rsl, hs])
            o_ref[rsl, vsl] = y.astype(o_ref.dtype)
        return carry

    lax.fori_loop(0, nblk, blk_body, 0)


def _hgrn(proj, st0, lb, ng, *, rows, tv, n_seq, tiles_per_seq, row_block0, alias_buf=None):
    qcol, fcol, icol, gcol = 3072 // 512, 3584 // 512, 4096 // 512, 4608 // 512

    def rb(s, c):
        return row_block0 + s * tiles_per_seq + c

    const2 = lambda s, c: (0, 0)
    in_specs = [
        pl.BlockSpec((rows, 512), lambda s, c: (rb(s, c), qcol)),
        pl.BlockSpec((rows, 512), lambda s, c: (rb(s, c), fcol)),
        pl.BlockSpec((rows, 512), lambda s, c: (rb(s, c), icol)),
        pl.BlockSpec((rows, 512), lambda s, c: (rb(s, c), gcol)),
        pl.BlockSpec((1, 512), const2),
        pl.BlockSpec((1, C_DV), const2),
        pl.BlockSpec((None, 512, C_DK), lambda s, c: (s, 0, 0)),
    ]
    args = [proj, proj, proj, proj, lb.reshape(1, 512), ng.reshape(1, C_DV), st0]
    aliases = {}
    if alias_buf is not None:
        in_specs.append(pl.BlockSpec(memory_space=pl.ANY))
        aliases[len(args)] = 0
        args.append(alias_buf)
    return pl.pallas_call(
        functools.partial(_hgrn_kernel, rows=rows, tv=tv),
        grid=(n_seq, tiles_per_seq),
        in_specs=in_specs,
        out_specs=[
            pl.BlockSpec((rows, 512), lambda s, c: (rb(s, c), 0)),
            pl.BlockSpec((None, 512, C_DK), lambda s, c: (s, 0, 0)),
        ],
        out_shape=[
            jax.ShapeDtypeStruct((N_ROWS, 512), BF16),
            jax.ShapeDtypeStruct((n_seq, 512, C_DK), F32),
        ],
        scratch_shapes=[pltpu.VMEM((rows, 512), F32) for _ in range(6)],
        input_output_aliases=aliases,
        compiler_params=_cparams(("arbitrary", "arbitrary")),
        name="hgrn_prompt" if alias_buf is None else "hgrn_sample",
    )(*args)


def _mm_out_kernel(oa_ref, ob_ref, oc_ref, w_ref, x_ref, g_ref, o_ref, wb_ref):
    @pl.when(pl.program_id(1) == 0)
    def _():
        wb_ref[...] = w_ref[...].astype(BF16)

    acc = jnp.dot(oa_ref[...], wb_ref[0:1024, :], preferred_element_type=F32)
    acc = acc + jnp.dot(ob_ref[...], wb_ref[1024:1536, :], preferred_element_type=F32)
    acc = acc + jnp.dot(oc_ref[...], wb_ref[1536:2048, :], preferred_element_type=F32)
    o_ref[...] = x_ref[...] + _tile_mod(g_ref, pl.program_id(1)) * acc


def _mm_out(oa, ob, oc, w_out, x, mod, l):
    tn = 1024
    ncol = D_MODEL // tn
    return pl.pallas_call(
        _mm_out_kernel,
        grid=(ncol, N_TILES),
        in_specs=[
            pl.BlockSpec((TM, 1024), lambda j, i: (i, 0)),
            pl.BlockSpec((TM, 512), lambda j, i: (i, 0)),
            pl.BlockSpec((TM, 512), lambda j, i: (i, 0)),
            pl.BlockSpec((None, D_MODEL, tn), lambda j, i: (l, 0, j)),
            pl.BlockSpec((TM, tn), lambda j, i: (i, j)),
            _mod_spec(l, "g1", tn, lambda j, i: j),
        ],
        out_specs=pl.BlockSpec((TM, tn), lambda j, i: (i, j)),
        out_shape=jax.ShapeDtypeStruct((N_ROWS, D_MODEL), F32),
        scratch_shapes=[pltpu.VMEM((D_MODEL, tn), BF16)],
        compiler_params=_cparams(("arbitrary", "arbitrary")),
        name="mm_out",
    )(oa, ob, oc, w_out, x, mod)


META_GSEL = 0
META_RANK = 1
META_GATE = 20
XW = D_MODEL + 128


def _route_kernel(x_ref, g_ref, sc_ref, sh_ref, wr_ref, br_ref, hx_ref, meta_ref, cnt_ref, run_ref):
    i = pl.program_id(0)

    @pl.when(i == 0)
    def _():
        run_ref[...] = jnp.zeros_like(run_ref)

    h = _norm_mod_rows(x_ref[...], g_ref[...], sc_ref, sh_ref, i)
    hx_ref[:, 0:D_MODEL] = h
    logits = jnp.dot(h, wr_ref[...], precision=lax.Precision.HIGHEST,
                     preferred_element_type=F32) + br_ref[...]
    col = lax.broadcasted_iota(jnp.int32, logits.shape, 1).astype(F32)
    big = 1e9
    is_g = col < N_GROUPS
    gl = jnp.where(is_g, logits, -jnp.inf)
    gmax = jnp.max(gl, axis=-1, keepdims=True)
    gsel = jnp.min(jnp.where(gl == gmax, col, big), axis=-1, keepdims=True)
    gden = jnp.sum(jnp.where(is_g, jnp.exp(logits - gmax), 0.0), axis=-1, keepdims=True)
    g_w = 1.0 / gden
    e0 = N_GROUPS + EXP_PER_GROUP * gsel
    in_grp = (col >= e0) & (col < e0 + EXP_PER_GROUP)
    el = jnp.where(in_grp, logits, -jnp.inf)
    v1 = jnp.max(el, axis=-1, keepdims=True)
    i1 = jnp.min(jnp.where(el == v1, col, big), axis=-1, keepdims=True)
    el2 = jnp.where(col == i1, -jnp.inf, el)
    v2 = jnp.max(el2, axis=-1, keepdims=True)
    i2 = jnp.min(jnp.where(el2 == v2, col, big), axis=-1, keepdims=True)
    t = jnp.exp(v2 - v1)
    p1 = 1.0 / (1.0 + t)
    a1 = p1 * g_w
    a2 = t * p1 * g_w
    meta = jnp.where(col == META_GATE + (i1 - e0), a1, 0.0)
    meta = meta + jnp.where(col == META_GATE + (i2 - e0), a2, 0.0)
    meta = meta + jnp.where(col == META_GSEL, gsel, 0.0)
    rows = logits.shape[0]
    rowi = lax.broadcasted_iota(jnp.int32, (rows, 1), 0)
    is_tok = (i < PROMPT_TILES) | (rowi % SROWS == 0)
    onehot = jnp.where((col == gsel) & is_tok, 1.0, 0.0)
    ri = lax.broadcasted_iota(jnp.int32, (rows, rows), 0)
    ci = lax.broadcasted_iota(jnp.int32, (rows, rows), 1)
    tri = jnp.where(ci <= ri, 1.0, 0.0).astype(BF16)
    incl = jnp.dot(tri, onehot.astype(BF16), preferred_element_type=F32)
    run = run_ref[...]
    rank = jnp.sum(jnp.where(col == gsel, incl + run - 1.0, 0.0), axis=-1, keepdims=True)
    meta = meta + jnp.where(col == META_RANK, rank, 0.0)
    meta_ref[...] = meta
    hx_ref[:, D_MODEL:XW] = meta
    run = run + incl[rows - 1:rows, :]
    run_ref[...] = run
    cnt_ref[...] = run


def _route(x, g, mod, wr, br, l):
    return pl.pallas_call(
        _route_kernel,
        grid=(N_TILES,),
        in_specs=[
            pl.BlockSpec((TM, D_MODEL), lambda i: (i, 0)),
            pl.BlockSpec((1, D_MODEL), lambda i: (0, 0)),
            _mod_spec(l, "sc2"),
            _mod_spec(l, "sh2"),
            pl.BlockSpec((D_MODEL, 128), lambda i: (0, 0)),
            pl.BlockSpec((1, 128), lambda i: (0, 0)),
        ],
        out_specs=[
            pl.BlockSpec((TM, XW), lambda i: (i, 0)),
            pl.BlockSpec((TM, 128), lambda i: (i, 0)),
            pl.BlockSpec((1, 128), lambda i: (0, 0)),
        ],
        out_shape=[
            jax.ShapeDtypeStruct((N_ROWS, XW), F32),
            jax.ShapeDtypeStruct((N_ROWS, 128), F32),
            jax.ShapeDtypeStruct((1, 128), F32),
        ],
        scratch_shapes=[pltpu.VMEM((1, 128), F32)],
        compiler_params=_cparams(("arbitrary",)),
        name="route",
    )(x, g.reshape(1, D_MODEL), mod, mod, wr, br)


def _moe_positions(meta, cnt):
    i32 = jnp.int32
    counts = cnt[0, :N_GROUPS].astype(i32)
    ntile = (counts + TM - 1) // TM
    tstart = jnp.cumsum(ntile) - ntile
    gsel = meta[:, META_GSEL].astype(i32)
    rank = meta[:, META_RANK].astype(i32)
    first_row = jnp.zeros_like(gsel)
    for g in range(N_GROUPS):
        first_row = jnp.where(gsel == g, tstart[g] * TM, first_row)
    pos = first_row + rank
    pos_all = jnp.concatenate([pos[:NP_ROWS], jnp.repeat(pos[NP_ROWS::SROWS], SROWS)])
    plan = jnp.concatenate([tstart, ntile]).astype(i32)
    return pos_all, plan


def _scatter_kernel(pos_ref, src_ref, buf_hbm, out_hbm, sem):
    del buf_hbm
    i = pl.program_id(0)
    base = i * TM

    def row_copy(r):
        return pltpu.make_async_copy(src_ref.at[pl.ds(r, 1), :],
                                     out_hbm.at[pl.ds(pos_ref[base + r], 1), :], sem)

    def run(n, stride):
        def issue(k, c):
            row_copy(k * stride).start()
            return c

        def drain(k, c):
            row_copy(k * stride).wait()
            return c

        lax.fori_loop(0, n, issue, 0, unroll=8)
        lax.fori_loop(0, n, drain, 0, unroll=8)

    @pl.when(i < PROMPT_TILES)
    def _():
        run(TM, 1)

    @pl.when(i == PROMPT_TILES)
    def _():
        run(DEC_BATCH, SROWS)


def _moe_scatter(hx, buf, pos_all):
    grid_spec = pltpu.PrefetchScalarGridSpec(
        num_scalar_prefetch=1,
        grid=(N_TILES,),
        in_specs=[pl.BlockSpec((TM, XW), lambda i, pos: (i, 0)), pl.BlockSpec(memory_space=pl.ANY)],
        out_specs=pl.BlockSpec(memory_space=pl.ANY),
        scratch_shapes=[pltpu.SemaphoreType.DMA(())],
    )
    return pl.pallas_call(
        _scatter_kernel,
        grid_spec=grid_spec,
        out_shape=jax.ShapeDtypeStruct((MOE_ROWS, XW), F32),
        input_output_aliases={2: 0},
        compiler_params=_cparams(("arbitrary",)),
        name="moe_scatter",
    )(pos_all, hx, buf)


def _work_item(w, plan_ref, per_tile):
    i32 = jnp.int32
    nt = [plan_ref[N_GROUPS + g] for g in range(N_GROUPS)]
    ts = [plan_ref[g] for g in range(N_GROUPS)]
    ends = []
    acc = 0
    for g in range(N_GROUPS):
        acc = acc + per_tile * nt[g]
        ends.append(acc)
    valid = w < ends[-1]
    wc = jnp.maximum(jnp.minimum(w, ends[-1] - 1), 0)
    g = sum((wc >= ends[k]).astype(i32) for k in range(N_GROUPS - 1))

    def pick(vals):
        out = vals[N_GROUPS - 1]
        for k in range(N_GROUPS - 2, -1, -1):
            out = jnp.where(g == k, vals[k], out)
        return out

    r = wc - pick([0] + ends[:-1])
    ntg = pick(nt)
    c = sum((r >= m * ntg).astype(i32) for m in range(1, per_tile))
    t_in = r - c * ntg
    t = pick(ts) + t_in
    extra = jnp.maximum(w - ends[-1], 0)
    used = ts[N_GROUPS - 1] + nt[N_GROUPS - 1]
    t_out = jnp.where(valid, t, used + extra // per_tile)
    c_out = jnp.where(valid, c, extra % per_tile)
    return g, c, t, t_in == 0, valid, t_out, c_out


def _hidden_kernel(plan_ref, x_ref, w1_ref, w3_ref, o_ref, w1b, w3b):
    _, e, _, first, valid, _, _ = _work_item(pl.program_id(0), plan_ref, EXP_PER_GROUP)

    @pl.when(jnp.logical_not(valid))
    def _():
        o_ref[...] = jnp.zeros_like(o_ref)

    @pl.when(valid & first)
    def _():
        w1b[...] = w1_ref[...].astype(BF16)
        w3b[...] = w3_ref[...].astype(BF16)

    @pl.when(valid)
    def _():
        x = x_ref[:, 0:D_MODEL].astype(BF16)
        a = jnp.dot(x, w1b[...], preferred_element_type=F32)
        b = jnp.dot(x, w3b[...], preferred_element_type=F32)
        m = x_ref[:, D_MODEL:XW]
        col = lax.broadcasted_iota(jnp.int32, m.shape, 1)
        gate = jnp.sum(jnp.where(col == META_GATE + e, m, 0.0), axis=-1, keepdims=True)
        o_ref[...] = (_silu(a) * b * gate).astype(o_ref.dtype)


def _moe_hidden(xs, w1, w3, plan, l):
    def item(w, plan):
        return _work_item(w, plan, EXP_PER_GROUP)

    def w_map(w, plan):
        it = item(w, plan)
        return (l, it[0] * EXP_PER_GROUP + it[1], 0, 0)

    grid_spec = pltpu.PrefetchScalarGridSpec(
        num_scalar_prefetch=1,
        grid=(HID_WORK,),
        in_specs=[
            pl.BlockSpec((TM, XW), lambda w, plan: (item(w, plan)[2], 0)),
            pl.BlockSpec((None, None, D_MODEL, D_EXPERT), w_map),
            pl.BlockSpec((None, None, D_MODEL, D_EXPERT), w_map),
        ],
        out_specs=pl.BlockSpec((TM, D_EXPERT), lambda w, plan: item(w, plan)[5:7]),
        scratch_shapes=[pltpu.VMEM((D_MODEL, D_EXPERT), BF16),
                        pltpu.VMEM((D_MODEL, D_EXPERT), BF16)],
    )
    return pl.pallas_call(
        _hidden_kernel,
        grid_spec=grid_spec,
        out_shape=jax.ShapeDtypeStruct((MOE_ROWS, EXP_PER_GROUP * D_EXPERT), BF16),
        compiler_params=_cparams(("arbitrary",)),
        name="moe_hidden",
    )(plan, xs, w1, w3)


def _down_kernel(plan_ref, h_ref, w_ref, o_ref, wb):
    _, _, _, first, valid, _, _ = _work_item(pl.program_id(0), plan_ref, DOWN_NCOL)

    @pl.when(jnp.logical_not(valid))
    def _():
        o_ref[...] = jnp.zeros_like(o_ref)

    @pl.when(valid & first)
    def _():
        wb[...] = w_ref[...].astype(BF16)

    @pl.when(valid)
    def _():
        o_ref[...] = jnp.dot(h_ref[...], wb[...], preferred_element_type=F32)


def _moe_down(hid, w2g, plan, l):
    kdim = EXP_PER_GROUP * D_EXPERT

    def item(w, plan):
        return _work_item(w, plan, DOWN_NCOL)

    grid_spec = pltpu.PrefetchScalarGridSpec(
        num_scalar_prefetch=1,
        grid=(DOWN_WORK,),
        in_specs=[
            pl.BlockSpec((TM, kdim), lambda w, plan: (item(w, plan)[2], 0)),
            pl.BlockSpec((None, None, kdim, DOWN_TN),
                         lambda w, plan: (l, item(w, plan)[0], 0, item(w, plan)[1])),
        ],
        out_specs=pl.BlockSpec((TM, DOWN_TN), lambda w, plan: item(w, plan)[5:7]),
        scratch_shapes=[pltpu.VMEM((kdim, DOWN_TN), BF16)],
    )
    return pl.pallas_call(
        _down_kernel,
        grid_spec=grid_spec,
        out_shape=jax.ShapeDtypeStruct((MOE_ROWS, D_MODEL), F32),
        compiler_params=_cparams(("arbitrary",)),
        name="moe_down",
    )(plan, hid, w2g)


def _res_kernel(pos_ref, x_ref, y_hbm, g2_ref, *rest, with_norm):
    if with_norm:
        lg_ref, sc_ref, sh_ref, xo_ref, ho_ref, ybuf, sem = rest
    else:
        xo_ref, ybuf, sem = rest
    i = pl.program_id(0)
    base = i * TM

    def row_copy(k):
        return pltpu.make_async_copy(y_hbm.at[pl.ds(pos_ref[base + k], 1), :],
                                     ybuf.at[pl.ds(k, 1), :], sem)

    def issue(k, c):
        row_copy(k).start()
        return c

    def drain(k, c):
        row_copy(k).wait()
        return c

    lax.fori_loop(0, TM, issue, 0, unroll=8)
    lax.fori_loop(0, TM, drain, 0, unroll=8)
    xn = x_ref[...] + _tile_mod(g2_ref, i) * ybuf[...]
    xo_ref[...] = xn
    if with_norm:
        ho_ref[...] = _norm_mod_rows(xn, lg_ref[...], sc_ref, sh_ref, i).astype(ho_ref.dtype)


def _residual(x, y_sorted, pos_all, mod, l, next_g=None):
    with_norm = next_g is not None
    row_spec = pl.BlockSpec((TM, D_MODEL), lambda i, pos: (i, 0))
    in_specs = [row_spec, pl.BlockSpec(memory_space=pl.ANY), _mod_spec(l, "g2")]
    args = [pos_all, x, y_sorted, mod]
    out_specs = [row_spec]
    out_shape = [jax.ShapeDtypeStruct((N_ROWS, D_MODEL), F32)]
    if with_norm:
        in_specs += [pl.BlockSpec((1, D_MODEL), lambda i, pos: (0, 0)),
                     _mod_spec(l + 1, "sc1"), _mod_spec(l + 1, "sh1")]
        args += [next_g.reshape(1, D_MODEL), mod, mod]
        out_specs.append(row_spec)
        out_shape.append(jax.ShapeDtypeStruct((N_ROWS, D_MODEL), BF16))
    grid_spec = pltpu.PrefetchScalarGridSpec(
        num_scalar_prefetch=1,
        grid=(N_TILES,),
        in_specs=in_specs,
        out_specs=out_specs,
        scratch_shapes=[pltpu.VMEM((TM, D_MODEL), F32), pltpu.SemaphoreType.DMA(())],
    )
    return pl.pallas_call(
        functools.partial(_res_kernel, with_norm=with_norm),
        grid_spec=grid_spec,
        out_shape=out_shape,
        compiler_params=_cparams(("arbitrary",)),
        name="residual_norm" if with_norm else "residual",
    )(*args)


def kernel(x_prompt, x_sample, c_prompt, c_sample, cache_swa_k, cache_swa_v, state_ret, state_hgrn,
           w_ada, b_ada, ln1_g, w_in, qn_g, kn_g, attn_sinks, ret_gn_g, ret_gn_b, hgrn_lb, hgrn_ng,
           w_out, ln2_g, w_rg, b_rg, w_re, b_re, w1, w3, w2):
    i32 = jnp.int32
    lbp = jax.nn.softmax(hgrn_lb.astype(F32), axis=0)
    lb_all = jnp.cumsum(lbp, axis=0) - lbp[0]

    assert TM // SROWS == DEC_BATCH and NS_ROWS == TM
    c_all = jnp.concatenate([c_prompt, jnp.zeros((C_SAMPLE0 - BATCH, D_MODEL), F32), c_sample], axis=0)
    mod = _ada(c_all, w_ada, b_ada)

    ret_tab_p = _ret_tables(RET_CHUNK, RET_CHUNK, 0.0, SEQ)
    ret_tab_s = _ret_tables(SROWS, 1, float(PAST_LEN), SROWS)
    w2g = w2.reshape(DEPTH, N_GROUPS, EXP_PER_GROUP * D_EXPERT, D_MODEL)
    xs_buf = jnp.zeros((MOE_ROWS, XW), F32)

    x, h = _norm_mod_first(x_prompt.reshape(NP_ROWS, D_MODEL), x_sample.reshape(DEC_BATCH, D_MODEL),
                           ln1_g[0], mod, 0)

    outs = {k: [] for k in ("kp", "vp", "rp", "hp", "ks", "vs", "rs", "hs")}
    for l in range(DEPTH):
        proj = _mm_in(h, w_in, l)

        oa, kn = _swa(proj, proj, proj, attn_sinks[l], qn_g[l], kn_g[l], tq=WINDOW, n_seq=BATCH,
                      blocks_per_seq=SEQ // WINDOW, row_block0=0, prev_raw=True)
        ck = cache_swa_k[l].reshape(DEC_BATCH * WINDOW, A_KV * A_HD)
        cv = cache_swa_v[l].reshape(DEC_BATCH * WINDOW, A_KV * A_HD)
        oa, kn = _swa(proj, ck, cv, attn_sinks[l], qn_g[l], kn_g[l], tq=SROWS, n_seq=DEC_BATCH,
                      blocks_per_seq=1, row_block0=NP_ROWS // SROWS, prev_raw=False,
                      alias_bufs=(oa, kn))
        zero_ret = jnp.zeros((BATCH, B_HEADS, B_DK, B_DV), F32)
        ob, rp = _ret(proj, zero_ret, ret_gn_g[l], ret_gn_b[l], ret_tab_p, chunk=RET_CHUNK,
                      n_seq=BATCH, chunks_per_seq=SEQ // RET_CHUNK, row_block0=0, table_per_chunk=True)
        ob, rs = _ret(proj, state_ret[l], ret_gn_g[l], ret_gn_b[l], ret_tab_s, chunk=SROWS,
                      n_seq=DEC_BATCH, chunks_per_seq=1, row_block0=NP_ROWS // SROWS,
                      table_per_chunk=False, alias_buf=ob)
        zero_hg = jnp.zeros((BATCH, C_HEADS * C_DV, C_DK), F32)
        oc, hp_t = _hgrn(proj, zero_hg, lb_all[l], hgrn_ng[l], rows=HG_TILE, tv=HG_TILE, n_seq=BATCH,
                         tiles_per_seq=SEQ // HG_TILE, row_block0=0)
        st_s = jnp.swapaxes(state_hgrn[l], -1, -2).reshape(DEC_BATCH, C_HEADS * C_DV, C_DK)
        oc, hs_t = _hgrn(proj, st_s, lb_all[l], hgrn_ng[l], rows=SROWS, tv=1, n_seq=DEC_BATCH,
                         tiles_per_seq=1, row_block0=NP_ROWS // SROWS, alias_buf=oc)

        x1 = _mm_out(oa, ob, oc, w_out, x, mod, l)

        wr = jnp.concatenate([w_rg[l], w_re[l], jnp.zeros((D_MODEL, 128 - N_GROUPS - N_EXPERTS), F32)], axis=1)
        br = jnp.concatenate([b_rg[l], b_re[l], jnp.zeros((128 - N_GROUPS - N_EXPERTS,), F32)]).reshape(1, 128)
        hx, meta, cnt = _route(x1, ln2_g[l], mod, wr, br, l)
        pos_all, plan = _moe_positions(meta, cnt)
        xs_buf = _moe_scatter(hx, xs_buf, pos_all)
        hid = _moe_hidden(xs_buf, w1, w3, plan, l)
        y_sorted = _moe_down(hid, w2g, plan, l)
        if l + 1 < DEPTH:
            x, h = _residual(x1, y_sorted, pos_all, mod, l, ln1_g[l + 1])
        else:
            (x,) = _residual(x1, y_sorted, pos_all, mod, l)

        knp = kn[:NP_ROWS].reshape(BATCH, SEQ, A_KV, A_HD)
        vpp = proj[:NP_ROWS, 1280:1536].reshape(BATCH, SEQ, A_KV, A_HD)
        outs["kp"].append(knp[:, SEQ - WINDOW:])
        outs["vp"].append(vpp[:, SEQ - WINDOW:])
        outs["rp"].append(rp)
        outs["hp"].append(jnp.swapaxes(hp_t.reshape(BATCH, C_HEADS, C_DV, C_DK), -1, -2))
        k_new = kn[NP_ROWS::SROWS].reshape(DEC_BATCH, 1, A_KV, A_HD)
        v_new = proj[NP_ROWS::SROWS, 1280:1536].reshape(DEC_BATCH, 1, A_KV, A_HD)
        outs["ks"].append(jnp.concatenate([cache_swa_k[l][:, 1:], k_new], axis=1))
        outs["vs"].append(jnp.concatenate([cache_swa_v[l][:, 1:], v_new], axis=1))
        outs["rs"].append(rs)
        outs["hs"].append(jnp.swapaxes(hs_t.reshape(DEC_BATCH, C_HEADS, C_DV, C_DK), -1, -2))

    y_prompt = x[:NP_ROWS].reshape(BATCH, SEQ, D_MODEL)
    y_sample = x[NP_ROWS::SROWS].reshape(DEC_BATCH, 1, D_MODEL)
    st = lambda k: jnp.stack(outs[k])
    return (y_prompt, y_sample, st("kp"), st("vp"), st("rp"), st("hp"),
            st("ks"), st("vs"), st("rs"), st("hs"))
```

```python
import functools

import numpy as np
import jax
import jax.numpy as jnp
from jax import lax
from jax.experimental import pallas as pl
from jax.experimental.pallas import tpu as pltpu

F32 = jnp.float32
BF16 = jnp.bfloat16

D_MODEL = 2048
BATCH = 2
SEQ = 4096
DEPTH = 4
DEC_BATCH = 32
PAST_LEN = 16384
A_HD = 128
A_HEADS = 8
A_KV = 2
A_GROUP = 4
WINDOW = 128
B_DV = 128
B_DK = 64
B_HEADS = 4
C_DK = 128
C_DV = 128
C_HEADS = 4
ROPE_BASE = 10000.0
N_GROUPS = 4
EXP_PER_GROUP = 4
N_EXPERTS = 16
D_EXPERT = 512
EPS = 1e-6
NEG_BIG = -1e30
IN_WIDTH = 5120

SROWS = 16
NP_ROWS = BATCH * SEQ
NS_ROWS = DEC_BATCH * SROWS
N_ROWS = NP_ROWS + NS_ROWS
TM = 512
N_TILES = N_ROWS // TM
MOD_ROWS = 16
N_TOK = NP_ROWS + DEC_BATCH
MOE_TILES = (N_TOK + N_GROUPS * (TM - 1)) // TM
MOE_ROWS = MOE_TILES * TM
HID_WORK = MOE_TILES * EXP_PER_GROUP
DOWN_TN = 1024
DOWN_NCOL = D_MODEL // DOWN_TN
DOWN_WORK = MOE_TILES * DOWN_NCOL

RET_CHUNK = 256
HG_TILE = 256
HG_BLK = 16
SAMPLE_NB = 8

VMEM_LIMIT = 48 * 1024 * 1024


def _cparams(sem):
    return pltpu.CompilerParams(dimension_semantics=sem, vmem_limit_bytes=VMEM_LIMIT)


def _sigmoid(x):
    return 1.0 / (1.0 + jnp.exp(-x))


def _silu(x):
    return x * _sigmoid(x)


C_ROWS = 40
C_SAMPLE0 = 8
PROMPT_TILES = NP_ROWS // TM
MOD_COL = dict(sh1=0, sc1=1, g1=2, sh2=3, sc2=4, g2=5)


def _tile_mod(m_ref, i):
    n = m_ref.shape[-1]
    seq = jnp.minimum(i // (SEQ // TM), BATCH - 1)
    prow = jnp.broadcast_to(m_ref[pl.ds(seq, 1), :], (DEC_BATCH, n))
    srows = m_ref[C_SAMPLE0:C_SAMPLE0 + DEC_BATCH, :]
    m = jnp.where(i < PROMPT_TILES, prow, srows)
    return jnp.broadcast_to(m[:, None, :], (DEC_BATCH, SROWS, n)).reshape(TM, n)


def _mod_spec(l, name, tn=D_MODEL, colmap=None):
    base = MOD_COL[name] * (D_MODEL // tn)
    if colmap is None:
        return pl.BlockSpec((None, C_ROWS, tn), lambda *ids: (l, 0, base))
    return pl.BlockSpec((None, C_ROWS, tn), lambda *ids: (l, 0, base + colmap(*ids)))


def _ada_kernel(c_ref, w_ref, b_ref, o_ref):
    c = _silu(c_ref[...])
    o_ref[...] = jnp.dot(c.astype(BF16), w_ref[...].astype(BF16),
                         preferred_element_type=F32) + b_ref[...]


def _ada(c_all, w_ada, b_ada):
    rows = c_all.shape[0]
    tn = 1024
    ncol = w_ada.shape[2] // tn
    return pl.pallas_call(
        _ada_kernel,
        grid=(DEPTH, ncol),
        in_specs=[
            pl.BlockSpec((rows, D_MODEL), lambda l, j: (0, 0)),
            pl.BlockSpec((None, D_MODEL, tn), lambda l, j: (l, 0, j)),
            pl.BlockSpec((None, 1, tn), lambda l, j: (l, 0, j)),
        ],
        out_specs=pl.BlockSpec((None, rows, tn), lambda l, j: (l, 0, j)),
        out_shape=jax.ShapeDtypeStruct((DEPTH, rows, w_ada.shape[2]), F32),
        compiler_params=_cparams(("arbitrary", "arbitrary")),
        name="ada",
    )(c_all, w_ada, b_ada.reshape(DEPTH, 1, -1))


def _norm_mod_rows(x, g, sc_ref, sh_ref, i):
    y = x * lax.rsqrt(jnp.mean(x * x, axis=-1, keepdims=True) + EPS) * g
    return y * (1.0 + _tile_mod(sc_ref, i)) + _tile_mod(sh_ref, i)


def _norm_kernel(xp_ref, xs_ref, g_ref, sc_ref, sh_ref, x_ref, o_ref):
    i = pl.program_id(0)
    xs = xs_ref[...]
    rid = lax.broadcasted_iota(jnp.int32, (DEC_BATCH, SROWS, 1), 1)
    blk = jnp.where(rid == 0, jnp.broadcast_to(xs[:, None, :], (DEC_BATCH, SROWS, D_MODEL)), 0.0)
    x = jnp.where(i < PROMPT_TILES, xp_ref[...], blk.reshape(TM, D_MODEL))
    x_ref[...] = x
    o_ref[...] = _norm_mod_rows(x, g_ref[...], sc_ref, sh_ref, i).astype(o_ref.dtype)


def _norm_mod_first(x_prompt, x_sample, g, mod, l):
    return pl.pallas_call(
        _norm_kernel,
        grid=(N_TILES,),
        in_specs=[
            pl.BlockSpec((TM, D_MODEL), lambda i: (jnp.minimum(i, PROMPT_TILES - 1), 0)),
            pl.BlockSpec((DEC_BATCH, D_MODEL), lambda i: (0, 0)),
            pl.BlockSpec((1, D_MODEL), lambda i: (0, 0)),
            _mod_spec(l, "sc1"),
            _mod_spec(l, "sh1"),
        ],
        out_specs=[
            pl.BlockSpec((TM, D_MODEL), lambda i: (i, 0)),
            pl.BlockSpec((TM, D_MODEL), lambda i: (i, 0)),
        ],
        out_shape=[
            jax.ShapeDtypeStruct((N_ROWS, D_MODEL), F32),
            jax.ShapeDtypeStruct((N_ROWS, D_MODEL), BF16),
        ],
        compiler_params=_cparams(("arbitrary",)),
        name="norm_first",
    )(x_prompt, x_sample, g.reshape(1, D_MODEL), mod, mod)


def _mm_in_kernel(x_ref, w_ref, o_ref, wb_ref):
    @pl.when(pl.program_id(1) == 0)
    def _():
        wb_ref[...] = w_ref[...].astype(BF16)

    o_ref[...] = jnp.dot(x_ref[...], wb_ref[...], preferred_element_type=F32)


def _mm_in(h, w_in, l):
    tn = 1024
    ncol = IN_WIDTH // tn
    return pl.pallas_call(
        _mm_in_kernel,
        grid=(ncol, N_TILES),
        in_specs=[
            pl.BlockSpec((TM, D_MODEL), lambda j, i: (i, 0)),
            pl.BlockSpec((None, D_MODEL, tn), lambda j, i: (l, 0, j)),
        ],
        out_specs=pl.BlockSpec((TM, tn), lambda j, i: (i, j)),
        out_shape=jax.ShapeDtypeStruct((N_ROWS, IN_WIDTH), F32),
        scratch_shapes=[pltpu.VMEM((D_MODEL, tn), BF16)],
        compiler_params=_cparams(("arbitrary", "arbitrary")),
        name="mm_in",
    )(h, w_in)


def _rms(x, g):
    return x * lax.rsqrt(jnp.mean(x * x, axis=-1, keepdims=True) + EPS) * g


def _swa_kernel(sink_ref, q_ref, k_ref, v_ref, pk_ref, pv_ref, qg_ref, kg_ref,
                *rest, tq, nb, prev_raw):
    o_ref, kn_ref = rest[-2], rest[-1]
    if prev_raw:
        has_prev = pl.program_id(1) > 0
    else:
        has_prev = True
    qg = qg_ref[...]
    kg = kg_ref[...]
    nk = WINDOW + tq
    nq = A_GROUP * tq
    row = lax.broadcasted_iota(jnp.int32, (nq, nk), 0) & (tq - 1)
    col = lax.broadcasted_iota(jnp.int32, (nq, nk), 1)
    vis_prev = (col < WINDOW) & (col >= row) & has_prev
    vis_own = (col >= WINDOW) & ((col - WINDOW) <= row)
    mask = vis_prev | vis_own
    rgrp = lax.broadcasted_iota(jnp.int32, (nq, 1), 0) // tq
    for sb in range(nb):
        rs = slice(sb * tq, (sb + 1) * tq)
        ps = slice(sb * WINDOW, (sb + 1) * WINDOW)
        for kv in range(A_KV):
            ks = slice(kv * A_HD, (kv + 1) * A_HD)
            kn = _rms(k_ref[rs, ks], kg)
            kn_ref[rs, ks] = kn
            pk = pk_ref[ps, ks]
            if prev_raw:
                pk = _rms(pk, kg)
            keys = jnp.concatenate([pk, kn], axis=0).astype(BF16)
            vals = jnp.concatenate([pv_ref[ps, ks], v_ref[rs, ks]], axis=0).astype(BF16)
            h0 = kv * A_GROUP
            qs = jnp.concatenate(
                [_rms(q_ref[rs, (h0 + g) * A_HD:(h0 + g + 1) * A_HD], qg) for g in range(A_GROUP)],
                axis=0).astype(BF16)
            s = lax.dot_general(qs, keys, (((1,), (1,)), ((), ())),
                                preferred_element_type=F32) * (A_HD ** -0.5)
            s = jnp.where(mask, s, NEG_BIG)
            snk = jnp.full((nq, 1), sink_ref[h0 + A_GROUP - 1], F32)
            for g in range(A_GROUP - 2, -1, -1):
                snk = jnp.where(rgrp == g, sink_ref[h0 + g], snk)
            m = jnp.maximum(jnp.max(s, axis=-1, keepdims=True), snk)
            p = jnp.exp(s - m)
            den = jnp.sum(p, axis=-1, keepdims=True) + jnp.exp(snk - m)
            o = jnp.dot(p.astype(BF16), vals, preferred_element_type=F32) / den
            for g in range(A_GROUP):
                o_ref[rs, (h0 + g) * A_HD:(h0 + g + 1) * A_HD] = o[g * tq:(g + 1) * tq].astype(o_ref.dtype)


def _swa(proj, prev_k, prev_v, sinks, qn_g, kn_g, *, tq, nb, n_seq, blocks_per_seq, row_block0,
         prev_raw, alias_bufs=None):
    qcol, kcol, vcol = 0, 1024 // 256, 1280 // 256
    assert nb == 1 or blocks_per_seq == 1

    def rb(s, n):
        return row_block0 + s * blocks_per_seq + n

    if prev_raw:
        def prev_map_k(s, n):
            return (rb(s, jnp.maximum(n - 1, 0)), kcol)

        def prev_map_v(s, n):
            return (rb(s, jnp.maximum(n - 1, 0)), vcol)
    else:
        def prev_map_k(s, n):
            return (s, 0)
        prev_map_v = prev_map_k

    rows = tq * nb
    in_specs = [
        pl.BlockSpec(memory_space=pltpu.SMEM),
        pl.BlockSpec((rows, 1024), lambda s, n: (rb(s, n), qcol)),
        pl.BlockSpec((rows, 256), lambda s, n: (rb(s, n), kcol)),
        pl.BlockSpec((rows, 256), lambda s, n: (rb(s, n), vcol)),
        pl.BlockSpec((WINDOW * nb, 256), prev_map_k),
        pl.BlockSpec((WINDOW * nb, 256), prev_map_v),
        pl.BlockSpec((1, A_HD), lambda s, n: (0, 0)),
        pl.BlockSpec((1, A_HD), lambda s, n: (0, 0)),
    ]
    args = [sinks, proj, proj, proj, prev_k, prev_v, qn_g.reshape(1, A_HD), kn_g.reshape(1, A_HD)]
    aliases = {}
    if alias_bufs is not None:
        for j, buf in enumerate(alias_bufs):
            in_specs.append(pl.BlockSpec(memory_space=pl.ANY))
            aliases[len(args)] = j
            args.append(buf)
    return pl.pallas_call(
        functools.partial(_swa_kernel, tq=tq, nb=nb, prev_raw=prev_raw),
        grid=(n_seq // nb, blocks_per_seq),
        in_specs=in_specs,
        out_specs=[
            pl.BlockSpec((rows, 1024), lambda s, n: (rb(s, n), 0)),
            pl.BlockSpec((rows, 256), lambda s, n: (rb(s, n), 0)),
        ],
        out_shape=[
            jax.ShapeDtypeStruct((N_ROWS, 1024), BF16),
            jax.ShapeDtypeStruct((N_ROWS, 256), F32),
        ],
        input_output_aliases=aliases,
        compiler_params=_cparams(("arbitrary", "arbitrary")),
        name="swa_prompt" if prev_raw else "swa_sample",
    )(*args)


def _ret_tables(chunk, tv, pos0, t_len):
    lg = np.log1p(-(2.0 ** (-5.0 - np.arange(B_HEADS, dtype=np.float64))))
    idx = np.arange(chunk, dtype=np.float64)
    valid = idx < tv
    diff = idx[:, None] - idx[None, :]
    dmat = np.where((diff >= 0) & valid[:, None] & valid[None, :],
                    np.exp(np.maximum(diff, 0.0)[None] * lg[:, None, None]), 0.0)
    din = np.exp((idx + 1.0)[:, None] * lg[None])
    din = np.repeat(din, B_DV, axis=1)
    kfac = np.where(valid[:, None], np.exp((tv - 1.0 - idx)[:, None] * lg[None]), 0.0)
    kfac = np.repeat(kfac, B_DK, axis=1)
    gc = np.repeat(np.exp(tv * lg), B_DK)[:, None] * np.ones((1, B_DV))
    half = B_DK // 2
    inv = ROPE_BASE ** (-np.arange(half, dtype=np.float64) / half)
    ang = (pos0 + np.arange(t_len, dtype=np.float64))[:, None] * inv[None]
    cos = np.tile(np.cos(ang), (1, 2 * B_HEADS))
    sin = np.tile(np.concatenate([-np.sin(ang), np.sin(ang)], axis=1), (1, B_HEADS))
    f = lambda a: jnp.asarray(a, dtype=F32)
    return f(dmat), f(din), f(kfac), f(gc), f(cos), f(sin)


def _ret_kernel(q_ref, k_ref, v_ref, g_ref, cos_ref, sin_ref, dmat_ref, din_ref, kfac_ref,
                gc_ref, gng_ref, gnb_ref, s0_ref, *rest, nb, chunk):
    o_ref, s_ref = rest[-2], rest[-1]

    @pl.when(pl.program_id(1) == 0)
    def _():
        s_ref[...] = s0_ref[...]

    cos = cos_ref[...]
    sin = sin_ref[...]
    width = B_HEADS * B_DK
    lane = lax.broadcasted_iota(jnp.int32, (1, width), 1)
    first_half = (lane % B_DK) < (B_DK // 2)
    lane_head = lane // B_DK

    def rope(x):
        partner = jnp.where(first_half, pltpu.roll(x, width - B_DK // 2, 1),
                            pltpu.roll(x, B_DK // 2, 1))
        return x * cos + partner * sin

    gc = gc_ref[...]
    for sb in range(nb):
        rws = slice(sb * chunk, (sb + 1) * chunk)
        qr = rope(q_ref[rws, :])
        kr = rope(k_ref[rws, :]) * (B_DK ** -0.5)
        kb = kr.astype(BF16)
        kdec = (kr * kfac_ref[...]).astype(BF16)
        s_old = s_ref[sb]
        s_b = s_old.astype(BF16)
        for h in range(B_HEADS):
            vs = slice(h * B_DV, (h + 1) * B_DV)
            rs = slice(h * B_DK, (h + 1) * B_DK)
            qm = jnp.where(lane_head == h, qr, 0.0).astype(BF16)
            a = lax.dot_general(qm, kb, (((1,), (1,)), ((), ())),
                                preferred_element_type=F32) * dmat_ref[h]
            vh = v_ref[rws, vs].astype(BF16)
            o = jnp.dot(a.astype(BF16), vh, preferred_element_type=F32)
            o = o + jnp.dot(qm, s_b, preferred_element_type=F32) * din_ref[:, vs]
            u = lax.dot_general(kdec, vh, (((0,), (0,)), ((), ())), preferred_element_type=F32)
            s_ref[sb, rs, :] = gc[rs, :] * s_old[rs, :] + u[rs, :]
            mu = jnp.mean(o, axis=-1, keepdims=True)
            oc = o - mu
            var = jnp.mean(oc * oc, axis=-1, keepdims=True)
            y = oc * lax.rsqrt(var + EPS) * gng_ref[:, vs] + gnb_ref[:, vs]
            y = y * _silu(g_ref[rws, vs])
            o_ref[rws, vs] = y.astype(o_ref.dtype)


def _ret(proj, s0, gn_g, gn_b, tables, *, chunk, nb, n_seq, chunks_per_seq, row_block0,
         table_per_chunk, alias_buf=None):
    dmat, din, kfac, gc, cos, sin = tables
    qcol, kcol, vcol, gcol = 1536 // 256, 1792 // 256, 2048 // 512, 2560 // 512
    assert nb == 1 or chunks_per_seq == 1
    rows = nb * chunk

    def rb(s, c):
        return row_block0 + s * chunks_per_seq + c

    tmap = (lambda s, c: (c, 0)) if table_per_chunk else (lambda s, c: (0, 0))
    const2 = lambda s, c: (0, 0)
    in_specs = [
        pl.BlockSpec((rows, 256), lambda s, c: (rb(s, c), qcol)),
        pl.BlockSpec((rows, 256), lambda s, c: (rb(s, c), kcol)),
        pl.BlockSpec((rows, 512), lambda s, c: (rb(s, c), vcol)),
        pl.BlockSpec((rows, 512), lambda s, c: (rb(s, c), gcol)),
        pl.BlockSpec((chunk, 256), tmap),
        pl.BlockSpec((chunk, 256), tmap),
        pl.BlockSpec((B_HEADS, chunk, chunk), lambda s, c: (0, 0, 0)),
        pl.BlockSpec((chunk, 512), const2),
        pl.BlockSpec((chunk, 256), const2),
        pl.BlockSpec((256, B_DV), const2),
        pl.BlockSpec((1, 512), const2),
        pl.BlockSpec((1, 512), const2),
        pl.BlockSpec((nb, 256, B_DV), lambda s, c: (s, 0, 0)),
    ]
    args = [proj, proj, proj, proj, cos, sin, dmat, din, kfac, gc,
            gn_g.reshape(1, 512), gn_b.reshape(1, 512), s0.reshape(n_seq, 256, B_DV)]
    aliases = {}
    if alias_buf is not None:
        in_specs.append(pl.BlockSpec(memory_space=pl.ANY))
        aliases[len(args)] = 0
        args.append(alias_buf)
    o, s_new = pl.pallas_call(
        functools.partial(_ret_kernel, nb=nb, chunk=chunk),
        grid=(n_seq // nb, chunks_per_seq),
        in_specs=in_specs,
        out_specs=[
            pl.BlockSpec((rows, 512), lambda s, c: (rb(s, c), 0)),
            pl.BlockSpec((nb, 256, B_DV), lambda s, c: (s, 0, 0)),
        ],
        out_shape=[
            jax.ShapeDtypeStruct((N_ROWS, 512), BF16),
            jax.ShapeDtypeStruct((n_seq, 256, B_DV), F32),
        ],
        input_output_aliases=aliases,
        compiler_params=_cparams(("arbitrary", "arbitrary")),
        name="ret_prompt" if alias_buf is None else "ret_sample",
    )(*args)
    return o, s_new.reshape(n_seq, B_HEADS, B_DK, B_DV)


HG_FAST = 32
HG_LIM = 60.0
HG_HDR = 32


def _hgrn_kernel(q_ref, f_ref, i_ref, g_ref, lb_ref, ng_ref, st0_ref, *rest, rows, seq_rows, tv):
    o_ref, st_ref, q_s, k_s, cs_s = rest[-5:]
    nseq = rows // seq_rows

    @pl.when(pl.program_id(1) == 0)
    def _():
        st_ref[...] = st0_ref[...]

    z = f_ref[...]
    lb = lb_ref[...]
    sig = _sigmoid(z)
    lf = jnp.log(lb + (1.0 - lb) * sig)
    k = (1.0 - lb) * _sigmoid(-z)
    if tv < seq_rows:
        rvalid = (lax.broadcasted_iota(jnp.int32, (rows, 1), 0) % seq_rows) < tv
        lf = jnp.where(rvalid, lf, 0.0)
        k = jnp.where(rvalid, k, 0.0)
    ri = lax.broadcasted_iota(jnp.int32, (rows, rows), 0)
    ci = lax.broadcasted_iota(jnp.int32, (rows, rows), 1)
    tri = (ci <= ri).astype(F32)
    cs = jnp.dot(tri, lf, precision=lax.Precision.HIGHEST, preferred_element_type=F32)
    q_s[...] = _silu(q_ref[...])
    k_s[...] = k
    cs_s[0:HG_HDR, :] = jnp.zeros((HG_HDR, C_HEADS * C_DK), F32)
    cs_s[HG_HDR:HG_HDR + rows, :] = cs
    ng = ng_ref[...]

    def block(blk, size, pairwise):
        r0 = pl.multiple_of(blk * size, size)
        rsl = pl.ds(r0, size)
        sidx = 0 if nseq == 1 else blk // (seq_rows // size)
        rowi = lax.broadcasted_iota(jnp.int32, (size, 1), 0)
        for h in range(C_HEADS):
            hs = slice(h * C_DK, (h + 1) * C_DK)
            vsl = slice(h * C_DV, (h + 1) * C_DV)
            qb = q_s[rsl, hs]
            kb = k_s[rsl, hs]
            vb = i_ref[rsl, hs]
            prev = cs_s[pl.ds(HG_HDR + r0 - 8, 8), hs][7:8, :]
            bc = cs_s[pl.ds(HG_HDR + r0, size), hs] - prev
            blast = bc[size - 1:size, :]
            qd = (qb * jnp.exp(bc)).astype(BF16)
            kd = (kb * jnp.exp(blast - bc)).astype(BF16)
            st = st_ref[sidx, vsl, :]
            o = lax.dot_general(qd, st.astype(BF16), (((1,), (1,)), ((), ())),
                                preferred_element_type=F32)
            if pairwise:
                for j in range(size):
                    dec = jnp.exp(jnp.minimum(bc - bc[j:j + 1, :], 0.0))
                    a = jnp.sum(qb * kb[j:j + 1, :] * dec, axis=-1, keepdims=True)
                    a = jnp.where(rowi >= j, a, 0.0)
                    o = o + a * vb[j:j + 1, :]
            else:
                kinv = (kb * jnp.exp(-bc)).astype(BF16)
                a = lax.dot_general(qd, kinv, (((1,), (1,)), ((), ())), preferred_element_type=F32)
                coli = lax.broadcasted_iota(jnp.int32, (size, size), 1)
                a = jnp.where(rowi >= coli, a, 0.0)
                o = o + jnp.dot(a.astype(BF16), vb.astype(BF16), preferred_element_type=F32)
            u = lax.dot_general(vb.astype(BF16), kd, (((0,), (0,)), ((), ())),
                                preferred_element_type=F32)
            st_ref[sidx, vsl, :] = st * jnp.exp(blast) + u
            y = o * lax.rsqrt(jnp.mean(o * o, axis=-1, keepdims=True) + EPS) * ng
            y = y * _silu(g_ref[rsl, hs])
            o_ref[rsl, vsl] = y.astype(o_ref.dtype)

    def run(size, pairwise):
        def body(blk, carry):
            block(blk, size, pairwise)
            return carry
        lax.fori_loop(0, rows // size, body, 0, unroll=not pairwise)

    if seq_rows % HG_FAST == 0:
        worst = jnp.max(cs_s[HG_HDR - HG_FAST:HG_HDR - HG_FAST + rows, :] - cs)
        bounded = worst < HG_LIM

        @pl.when(bounded)
        def _():
            run(HG_FAST, False)

        @pl.when(jnp.logical_not(bounded))
        def _():
            run(HG_BLK, True)
    else:
        run(HG_BLK, True)


def _hgrn(proj, st0, lb, ng, *, rows, seq_rows, tv, n_seq, tiles_per_seq, row_block0, alias_buf=None):
    qcol, fcol, icol, gcol = 3072 // 512, 3584 // 512, 4096 // 512, 4608 // 512
    nst = max(rows // seq_rows, 1)
    assert nst == 1 or tiles_per_seq == 1

    def rb(s, c):
        return row_block0 + s * tiles_per_seq + c

    const2 = lambda s, c: (0, 0)
    in_specs = [
        pl.BlockSpec((rows, 512), lambda s, c: (rb(s, c), qcol)),
        pl.BlockSpec((rows, 512), lambda s, c: (rb(s, c), fcol)),
        pl.BlockSpec((rows, 512), lambda s, c: (rb(s, c), icol)),
        pl.BlockSpec((rows, 512), lambda s, c: (rb(s, c), gcol)),
        pl.BlockSpec((1, 512), const2),
        pl.BlockSpec((1, C_DV), const2),
        pl.BlockSpec((nst, 512, C_DK), lambda s, c: (s, 0, 0)),
    ]
    args = [proj, proj, proj, proj, lb.reshape(1, 512), ng.reshape(1, C_DV), st0]
    aliases = {}
    if alias_buf is not None:
        in_specs.append(pl.BlockSpec(memory_space=pl.ANY))
        aliases[len(args)] = 0
        args.append(alias_buf)
    return pl.pallas_call(
        functools.partial(_hgrn_kernel, rows=rows, seq_rows=seq_rows, tv=tv),
        grid=(n_seq // nst, tiles_per_seq),
        in_specs=in_specs,
        out_specs=[
            pl.BlockSpec((rows, 512), lambda s, c: (rb(s, c), 0)),
            pl.BlockSpec((nst, 512, C_DK), lambda s, c: (s, 0, 0)),
        ],
        out_shape=[
            jax.ShapeDtypeStruct((N_ROWS, 512), BF16),
            jax.ShapeDtypeStruct((n_seq, 512, C_DK), F32),
        ],
        scratch_shapes=[pltpu.VMEM((rows, 512), F32), pltpu.VMEM((rows, 512), F32),
                        pltpu.VMEM((HG_HDR + rows, 512), F32)],
        input_output_aliases=aliases,
        compiler_params=_cparams(("arbitrary", "arbitrary")),
        name="hgrn_prompt" if alias_buf is None else "hgrn_sample",
    )(*args)


def _mm_out_kernel(oa_ref, ob_ref, oc_ref, w_ref, x_ref, g_ref, o_ref, wb_ref):
    @pl.when(pl.program_id(1) == 0)
    def _():
        wb_ref[...] = w_ref[...].astype(BF16)

    acc = jnp.dot(oa_ref[...], wb_ref[0:1024, :], preferred_element_type=F32)
    acc = acc + jnp.dot(ob_ref[...], wb_ref[1024:1536, :], preferred_element_type=F32)
    acc = acc + jnp.dot(oc_ref[...], wb_ref[1536:2048, :], preferred_element_type=F32)
    o_ref[...] = x_ref[...] + _tile_mod(g_ref, pl.program_id(1)) * acc


def _mm_out(oa, ob, oc, w_out, x, mod, l):
    tn = 1024
    ncol = D_MODEL // tn
    return pl.pallas_call(
        _mm_out_kernel,
        grid=(ncol, N_TILES),
        in_specs=[
            pl.BlockSpec((TM, 1024), lambda j, i: (i, 0)),
            pl.BlockSpec((TM, 512), lambda j, i: (i, 0)),
            pl.BlockSpec((TM, 512), lambda j, i: (i, 0)),
            pl.BlockSpec((None, D_MODEL, tn), lambda j, i: (l, 0, j)),
            pl.BlockSpec((TM, tn), lambda j, i: (i, j)),
            _mod_spec(l, "g1", tn, lambda j, i: j),
        ],
        out_specs=pl.BlockSpec((TM, tn), lambda j, i: (i, j)),
        out_shape=jax.ShapeDtypeStruct((N_ROWS, D_MODEL), F32),
        scratch_shapes=[pltpu.VMEM((D_MODEL, tn), BF16)],
        compiler_params=_cparams(("arbitrary", "arbitrary")),
        name="mm_out",
    )(oa, ob, oc, w_out, x, mod)


META_GSEL = 0
META_RANK = 1
META_GATE = 20
XW = D_MODEL + 128


def _route_kernel(x_ref, g_ref, sc_ref, sh_ref, wr_ref, br_ref, hx_ref, meta_ref, cnt_ref, run_ref):
    i = pl.program_id(0)

    @pl.when(i == 0)
    def _():
        run_ref[...] = jnp.zeros_like(run_ref)

    h = _norm_mod_rows(x_ref[...], g_ref[...], sc_ref, sh_ref, i)
    hx_ref[:, 0:D_MODEL] = h
    logits = jnp.dot(h, wr_ref[...], precision=lax.Precision.HIGHEST,
                     preferred_element_type=F32) + br_ref[...]
    col = lax.broadcasted_iota(jnp.int32, logits.shape, 1).astype(F32)
    big = 1e9
    is_g = col < N_GROUPS
    gl = jnp.where(is_g, logits, -jnp.inf)
    gmax = jnp.max(gl, axis=-1, keepdims=True)
    gsel = jnp.min(jnp.where(gl == gmax, col, big), axis=-1, keepdims=True)
    gden = jnp.sum(jnp.where(is_g, jnp.exp(logits - gmax), 0.0), axis=-1, keepdims=True)
    g_w = 1.0 / gden
    e0 = N_GROUPS + EXP_PER_GROUP * gsel
    in_grp = (col >= e0) & (col < e0 + EXP_PER_GROUP)
    el = jnp.where(in_grp, logits, -jnp.inf)
    v1 = jnp.max(el, axis=-1, keepdims=True)
    i1 = jnp.min(jnp.where(el == v1, col, big), axis=-1, keepdims=True)
    el2 = jnp.where(col == i1, -jnp.inf, el)
    v2 = jnp.max(el2, axis=-1, keepdims=True)
    i2 = jnp.min(jnp.where(el2 == v2, col, big), axis=-1, keepdims=True)
    t = jnp.exp(v2 - v1)
    p1 = 1.0 / (1.0 + t)
    a1 = p1 * g_w
    a2 = t * p1 * g_w
    meta = jnp.where(col == META_GATE + (i1 - e0), a1, 0.0)
    meta = meta + jnp.where(col == META_GATE + (i2 - e0), a2, 0.0)
    meta = meta + jnp.where(col == META_GSEL, gsel, 0.0)
    rows = logits.shape[0]
    rowi = lax.broadcasted_iota(jnp.int32, (rows, 1), 0)
    is_tok = (i < PROMPT_TILES) | (rowi % SROWS == 0)
    onehot = jnp.where((col == gsel) & is_tok, 1.0, 0.0)
    ri = lax.broadcasted_iota(jnp.int32, (rows, rows), 0)
    ci = lax.broadcasted_iota(jnp.int32, (rows, rows), 1)
    tri = jnp.where(ci <= ri, 1.0, 0.0).astype(BF16)
    incl = jnp.dot(tri, onehot.astype(BF16), preferred_element_type=F32)
    run = run_ref[...]
    rank = jnp.sum(jnp.where(col == gsel, incl + run - 1.0, 0.0), axis=-1, keepdims=True)
    meta = meta + jnp.where(col == META_RANK, rank, 0.0)
    meta_ref[...] = meta
    hx_ref[:, D_MODEL:XW] = meta
    run = run + incl[rows - 1:rows, :]
    run_ref[...] = run
    cnt_ref[...] = run


def _route(x, g, mod, wr, br, l):
    return pl.pallas_call(
        _route_kernel,
        grid=(N_TILES,),
        in_specs=[
            pl.BlockSpec((TM, D_MODEL), lambda i: (i, 0)),
            pl.BlockSpec((1, D_MODEL), lambda i: (0, 0)),
            _mod_spec(l, "sc2"),
            _mod_spec(l, "sh2"),
            pl.BlockSpec((D_MODEL, 128), lambda i: (0, 0)),
            pl.BlockSpec((1, 128), lambda i: (0, 0)),
        ],
        out_specs=[
            pl.BlockSpec((TM, XW), lambda i: (i, 0)),
            pl.BlockSpec((TM, 128), lambda i: (i, 0)),
            pl.BlockSpec((1, 128), lambda i: (0, 0)),
        ],
        out_shape=[
            jax.ShapeDtypeStruct((N_ROWS, XW), F32),
            jax.ShapeDtypeStruct((N_ROWS, 128), F32),
            jax.ShapeDtypeStruct((1, 128), F32),
        ],
        scratch_shapes=[pltpu.VMEM((1, 128), F32)],
        compiler_params=_cparams(("arbitrary",)),
        name="route",
    )(x, g.reshape(1, D_MODEL), mod, mod, wr, br)


def _moe_positions(meta, cnt):
    i32 = jnp.int32
    counts = cnt[0, :N_GROUPS].astype(i32)
    ntile = (counts + TM - 1) // TM
    tstart = jnp.cumsum(ntile) - ntile
    gsel = meta[:, META_GSEL].astype(i32)
    rank = meta[:, META_RANK].astype(i32)
    first_row = jnp.zeros_like(gsel)
    for g in range(N_GROUPS):
        first_row = jnp.where(gsel == g, tstart[g] * TM, first_row)
    pos = first_row + rank
    pos_all = jnp.concatenate([pos[:NP_ROWS], jnp.repeat(pos[NP_ROWS::SROWS], SROWS)])
    plan = jnp.concatenate([tstart, ntile]).astype(i32)
    return pos_all, plan


def _scatter_kernel(pos_ref, src_ref, buf_hbm, out_hbm, sem):
    del buf_hbm
    i = pl.program_id(0)
    base = i * TM

    def row_copy(r):
        return pltpu.make_async_copy(src_ref.at[pl.ds(r, 1), :],
                                     out_hbm.at[pl.ds(pos_ref[base + r], 1), :], sem)

    def run(n, stride):
        def issue(k, c):
            row_copy(k * stride).start()
            return c

        def drain(k, c):
            row_copy(k * stride).wait()
            return c

        lax.fori_loop(0, n, issue, 0, unroll=8)
        lax.fori_loop(0, n, drain, 0, unroll=8)

    @pl.when(i < PROMPT_TILES)
    def _():
        run(TM, 1)

    @pl.when(i == PROMPT_TILES)
    def _():
        run(DEC_BATCH, SROWS)


def _moe_scatter(hx, buf, pos_all):
    grid_spec = pltpu.PrefetchScalarGridSpec(
        num_scalar_prefetch=1,
        grid=(N_TILES,),
        in_specs=[pl.BlockSpec((TM, XW), lambda i, pos: (i, 0)), pl.BlockSpec(memory_space=pl.ANY)],
        out_specs=pl.BlockSpec(memory_space=pl.ANY),
        scratch_shapes=[pltpu.SemaphoreType.DMA(())],
    )
    return pl.pallas_call(
        _scatter_kernel,
        grid_spec=grid_spec,
        out_shape=jax.ShapeDtypeStruct((MOE_ROWS, XW), F32),
        input_output_aliases={2: 0},
        compiler_params=_cparams(("arbitrary",)),
        name="moe_scatter",
    )(pos_all, hx, buf)


def _work_item(w, plan_ref, per_tile):
    i32 = jnp.int32
    nt = [plan_ref[N_GROUPS + g] for g in range(N_GROUPS)]
    ts = [plan_ref[g] for g in range(N_GROUPS)]
    ends = []
    acc = 0
    for g in range(N_GROUPS):
        acc = acc + per_tile * nt[g]
        ends.append(acc)
    valid = w < ends[-1]
    wc = jnp.maximum(jnp.minimum(w, ends[-1] - 1), 0)
    g = sum((wc >= ends[k]).astype(i32) for k in range(N_GROUPS - 1))

    def pick(vals):
        out = vals[N_GROUPS - 1]
        for k in range(N_GROUPS - 2, -1, -1):
            out = jnp.where(g == k, vals[k], out)
        return out

    r = wc - pick([0] + ends[:-1])
    ntg = pick(nt)
    c = sum((r >= m * ntg).astype(i32) for m in range(1, per_tile))
    t_in = r - c * ntg
    t = pick(ts) + t_in
    extra = jnp.maximum(w - ends[-1], 0)
    used = ts[N_GROUPS - 1] + nt[N_GROUPS - 1]
    t_out = jnp.where(valid, t, used + extra // per_tile)
    c_out = jnp.where(valid, c, extra % per_tile)
    return g, c, t, t_in == 0, valid, t_out, c_out


def _hidden_kernel(plan_ref, x_ref, w1_ref, w3_ref, o_ref, w1b, w3b):
    _, e, _, first, valid, _, _ = _work_item(pl.program_id(0), plan_ref, EXP_PER_GROUP)

    @pl.when(jnp.logical_not(valid))
    def _():
        o_ref[...] = jnp.zeros_like(o_ref)

    @pl.when(valid & first)
    def _():
        w1b[...] = w1_ref[...].astype(BF16)
        w3b[...] = w3_ref[...].astype(BF16)

    @pl.when(valid)
    def _():
        x = x_ref[:, 0:D_MODEL].astype(BF16)
        a = jnp.dot(x, w1b[...], preferred_element_type=F32)
        b = jnp.dot(x, w3b[...], preferred_element_type=F32)
        m = x_ref[:, D_MODEL:XW]
        col = lax.broadcasted_iota(jnp.int32, m.shape, 1)
        gate = jnp.sum(jnp.where(col == META_GATE + e, m, 0.0), axis=-1, keepdims=True)
        o_ref[...] = (_silu(a) * b * gate).astype(o_ref.dtype)


def _moe_hidden(xs, w1, w3, plan, l):
    def item(w, plan):
        return _work_item(w, plan, EXP_PER_GROUP)

    def w_map(w, plan):
        it = item(w, plan)
        return (l, it[0] * EXP_PER_GROUP + it[1], 0, 0)

    grid_spec = pltpu.PrefetchScalarGridSpec(
        num_scalar_prefetch=1,
        grid=(HID_WORK,),
        in_specs=[
            pl.BlockSpec((TM, XW), lambda w, plan: (item(w, plan)[2], 0)),
            pl.BlockSpec((None, None, D_MODEL, D_EXPERT), w_map),
            pl.BlockSpec((None, None, D_MODEL, D_EXPERT), w_map),
        ],
        out_specs=pl.BlockSpec((TM, D_EXPERT), lambda w, plan: item(w, plan)[5:7]),
        scratch_shapes=[pltpu.VMEM((D_MODEL, D_EXPERT), BF16),
                        pltpu.VMEM((D_MODEL, D_EXPERT), BF16)],
    )
    return pl.pallas_call(
        _hidden_kernel,
        grid_spec=grid_spec,
        out_shape=jax.ShapeDtypeStruct((MOE_ROWS, EXP_PER_GROUP * D_EXPERT), BF16),
        compiler_params=_cparams(("arbitrary",)),
        name="moe_hidden",
    )(plan, xs, w1, w3)


def _down_kernel(plan_ref, h_ref, w_ref, o_ref, wb):
    _, _, _, first, valid, _, _ = _work_item(pl.program_id(0), plan_ref, DOWN_NCOL)

    @pl.when(jnp.logical_not(valid))
    def _():
        o_ref[...] = jnp.zeros_like(o_ref)

    @pl.when(valid & first)
    def _():
        wb[...] = w_ref[...].astype(BF16)

    @pl.when(valid)
    def _():
        o_ref[...] = jnp.dot(h_ref[...], wb[...], preferred_element_type=F32)


def _moe_down(hid, w2g, plan, l):
    kdim = EXP_PER_GROUP * D_EXPERT

    def item(w, plan):
        return _work_item(w, plan, DOWN_NCOL)

    grid_spec = pltpu.PrefetchScalarGridSpec(
        num_scalar_prefetch=1,
        grid=(DOWN_WORK,),
        in_specs=[
            pl.BlockSpec((TM, kdim), lambda w, plan: (item(w, plan)[2], 0)),
            pl.BlockSpec((None, None, kdim, DOWN_TN),
                         lambda w, plan: (l, item(w, plan)[0], 0, item(w, plan)[1])),
        ],
        out_specs=pl.BlockSpec((TM, DOWN_TN), lambda w, plan: item(w, plan)[5:7]),
        scratch_shapes=[pltpu.VMEM((kdim, DOWN_TN), BF16)],
    )
    return pl.pallas_call(
        _down_kernel,
        grid_spec=grid_spec,
        out_shape=jax.ShapeDtypeStruct((MOE_ROWS, D_MODEL), F32),
        compiler_params=_cparams(("arbitrary",)),
        name="moe_down",
    )(plan, hid, w2g)


def _res_kernel(pos_ref, x_ref, y_hbm, g2_ref, *rest, with_norm):
    if with_norm:
        lg_ref, sc_ref, sh_ref, xo_ref, ho_ref, ybuf, sem = rest
    else:
        xo_ref, ybuf, sem = rest
    i = pl.program_id(0)
    base = i * TM

    def row_copy(k):
        return pltpu.make_async_copy(y_hbm.at[pl.ds(pos_ref[base + k], 1), :],
                                     ybuf.at[pl.ds(k, 1), :], sem)

    def issue(k, c):
        row_copy(k).start()
        return c

    def drain(k, c):
        row_copy(k).wait()
        return c

    lax.fori_loop(0, TM, issue, 0, unroll=8)
    lax.fori_loop(0, TM, drain, 0, unroll=8)
    xn = x_ref[...] + _tile_mod(g2_ref, i) * ybuf[...]
    xo_ref[...] = xn
    if with_norm:
        ho_ref[...] = _norm_mod_rows(xn, lg_ref[...], sc_ref, sh_ref, i).astype(ho_ref.dtype)


def _residual(x, y_sorted, pos_all, mod, l, next_g=None):
    with_norm = next_g is not None
    row_spec = pl.BlockSpec((TM, D_MODEL), lambda i, pos: (i, 0))
    in_specs = [row_spec, pl.BlockSpec(memory_space=pl.ANY), _mod_spec(l, "g2")]
    args = [pos_all, x, y_sorted, mod]
    out_specs = [row_spec]
    out_shape = [jax.ShapeDtypeStruct((N_ROWS, D_MODEL), F32)]
    if with_norm:
        in_specs += [pl.BlockSpec((1, D_MODEL), lambda i, pos: (0, 0)),
                     _mod_spec(l + 1, "sc1"), _mod_spec(l + 1, "sh1")]
        args += [next_g.reshape(1, D_MODEL), mod, mod]
        out_specs.append(row_spec)
        out_shape.append(jax.ShapeDtypeStruct((N_ROWS, D_MODEL), BF16))
    grid_spec = pltpu.PrefetchScalarGridSpec(
        num_scalar_prefetch=1,
        grid=(N_TILES,),
        in_specs=in_specs,
        out_specs=out_specs,
        scratch_shapes=[pltpu.VMEM((TM, D_MODEL), F32), pltpu.SemaphoreType.DMA(())],
    )
    return pl.pallas_call(
        functools.partial(_res_kernel, with_norm=with_norm),
        grid_spec=grid_spec,
        out_shape=out_shape,
        compiler_params=_cparams(("arbitrary",)),
        name="residual_norm" if with_norm else "residual",
    )(*args)


def kernel(x_prompt, x_sample, c_prompt, c_sample, cache_swa_k, cache_swa_v, state_ret, state_hgrn,
           w_ada, b_ada, ln1_g, w_in, qn_g, kn_g, attn_sinks, ret_gn_g, ret_gn_b, hgrn_lb, hgrn_ng,
           w_out, ln2_g, w_rg, b_rg, w_re, b_re, w1, w3, w2):
    i32 = jnp.int32
    lbp = jax.nn.softmax(hgrn_lb.astype(F32), axis=0)
    lb_all = jnp.cumsum(lbp, axis=0) - lbp[0]

    assert TM // SROWS == DEC_BATCH and NS_ROWS == TM
    c_all = jnp.concatenate([c_prompt, jnp.zeros((C_SAMPLE0 - BATCH, D_MODEL), F32), c_sample], axis=0)
    mod = _ada(c_all, w_ada, b_ada)

    ret_tab_p = _ret_tables(RET_CHUNK, RET_CHUNK, 0.0, SEQ)
    ret_tab_s = _ret_tables(SROWS, 1, float(PAST_LEN), SROWS)
    w2g = w2.reshape(DEPTH, N_GROUPS, EXP_PER_GROUP * D_EXPERT, D_MODEL)
    xs_buf = jnp.zeros((MOE_ROWS, XW), F32)

    x, h = _norm_mod_first(x_prompt.reshape(NP_ROWS, D_MODEL), x_sample.reshape(DEC_BATCH, D_MODEL),
                           ln1_g[0], mod, 0)

    outs = {k: [] for k in ("kp", "vp", "rp", "hp", "ks", "vs", "rs", "hs")}
    for l in range(DEPTH):
        proj = _mm_in(h, w_in, l)

        oa, kn = _swa(proj, proj, proj, attn_sinks[l], qn_g[l], kn_g[l], tq=WINDOW, nb=1, n_seq=BATCH,
                      blocks_per_seq=SEQ // WINDOW, row_block0=0, prev_raw=True)
        ck = cache_swa_k[l].reshape(DEC_BATCH * WINDOW, A_KV * A_HD)
        cv = cache_swa_v[l].reshape(DEC_BATCH * WINDOW, A_KV * A_HD)
        srows = SAMPLE_NB * SROWS
        oa, kn = _swa(proj, ck, cv, attn_sinks[l], qn_g[l], kn_g[l], tq=SROWS, nb=SAMPLE_NB,
                      n_seq=DEC_BATCH, blocks_per_seq=1, row_block0=NP_ROWS // srows, prev_raw=False,
                      alias_bufs=(oa, kn))
        zero_ret = jnp.zeros((BATCH, B_HEADS, B_DK, B_DV), F32)
        ob, rp = _ret(proj, zero_ret, ret_gn_g[l], ret_gn_b[l], ret_tab_p, chunk=RET_CHUNK, nb=1,
                      n_seq=BATCH, chunks_per_seq=SEQ // RET_CHUNK, row_block0=0, table_per_chunk=True)
        ob, rs = _ret(proj, state_ret[l], ret_gn_g[l], ret_gn_b[l], ret_tab_s, chunk=SROWS, nb=SAMPLE_NB,
                      n_seq=DEC_BATCH, chunks_per_seq=1, row_block0=NP_ROWS // srows,
                      table_per_chunk=False, alias_buf=ob)
        zero_hg = jnp.zeros((BATCH, C_HEADS * C_DV, C_DK), F32)
        oc, hp_t = _hgrn(proj, zero_hg, lb_all[l], hgrn_ng[l], rows=HG_TILE, seq_rows=HG_TILE,
                         tv=HG_TILE, n_seq=BATCH, tiles_per_seq=SEQ // HG_TILE, row_block0=0)
        st_s = jnp.swapaxes(state_hgrn[l], -1, -2).reshape(DEC_BATCH, C_HEADS * C_DV, C_DK)
        oc, hs_t = _hgrn(proj, st_s, lb_all[l], hgrn_ng[l], rows=srows, seq_rows=SROWS, tv=1,
                         n_seq=DEC_BATCH, tiles_per_seq=1, row_block0=NP_ROWS // srows, alias_buf=oc)

        x1 = _mm_out(oa, ob, oc, w_out, x, mod, l)

        wr = jnp.concatenate([w_rg[l], w_re[l], jnp.zeros((D_MODEL, 128 - N_GROUPS - N_EXPERTS), F32)], axis=1)
        br = jnp.concatenate([b_rg[l], b_re[l], jnp.zeros((128 - N_GROUPS - N_EXPERTS,), F32)]).reshape(1, 128)
        hx, meta, cnt = _route(x1, ln2_g[l], mod, wr, br, l)
        pos_all, plan = _moe_positions(meta, cnt)
        xs_buf = _moe_scatter(hx, xs_buf, pos_all)
        hid = _moe_hidden(xs_buf, w1, w3, plan, l)
        y_sorted = _moe_down(hid, w2g, plan, l)
        if l + 1 < DEPTH:
            x, h = _residual(x1, y_sorted, pos_all, mod, l, ln1_g[l + 1])
        else:
            (x,) = _residual(x1, y_sorted, pos_all, mod, l)

        knp = kn[:NP_ROWS].reshape(BATCH, SEQ, A_KV, A_HD)
        vpp = proj[:NP_ROWS, 1280:1536].reshape(BATCH, SEQ, A_KV, A_HD)
        outs["kp"].append(knp[:, SEQ - WINDOW:])
        outs["vp"].append(vpp[:, SEQ - WINDOW:])
        outs["rp"].append(rp)
        outs["hp"].append(jnp.swapaxes(hp_t.reshape(BATCH, C_HEADS, C_DV, C_DK), -1, -2))
        k_new = kn[NP_ROWS::SROWS].reshape(DEC_BATCH, 1, A_KV, A_HD)
        v_new = proj[NP_ROWS::SROWS, 1280:1536].reshape(DEC_BATCH, 1, A_KV, A_HD)
        outs["ks"].append(jnp.concatenate([cache_swa_k[l][:, 1:], k_new], axis=1))
        outs["vs"].append(jnp.concatenate([cache_swa_v[l][:, 1:], v_new], axis=1))
        outs["rs"].append(rs)
        outs["hs"].append(jnp.swapaxes(hs_t.reshape(DEC_BATCH, C_HEADS, C_DV, C_DK), -1, -2))

    y_prompt = x[:NP_ROWS].reshape(BATCH, SEQ, D_MODEL)
    y_sample = x[NP_ROWS::SROWS].reshape(DEC_BATCH, 1, D_MODEL)
    st = lambda k: jnp.stack(outs[k])
    return (y_prompt, y_sample, st("kp"), st("vp"), st("rp"), st("hp"),
            st("ks"), st("vs"), st("rs"), st("hs"))
```

```python
import functools

import numpy as np
import jax
import jax.numpy as jnp
from jax import lax
from jax.experimental import pallas as pl
from jax.experimental.pallas import tpu as pltpu

F32 = jnp.float32
BF16 = jnp.bfloat16

D_MODEL = 2048
BATCH = 2
SEQ = 4096
DEPTH = 4
DEC_BATCH = 32
PAST_LEN = 16384
A_HD = 128
A_HEADS = 8
A_KV = 2
A_GROUP = 4
WINDOW = 128
B_DV = 128
B_DK = 64
B_HEADS = 4
C_DK = 128
C_DV = 128
C_HEADS = 4
ROPE_BASE = 10000.0
N_GROUPS = 4
EXP_PER_GROUP = 4
N_EXPERTS = 16
D_EXPERT = 512
EPS = 1e-6
NEG_BIG = -1e30
IN_WIDTH = 5120

SROWS = 16
NP_ROWS = BATCH * SEQ
NS_ROWS = DEC_BATCH * SROWS
N_ROWS = NP_ROWS + NS_ROWS
TM = 512
N_TILES = N_ROWS // TM
MOD_ROWS = 16
N_TOK = NP_ROWS + DEC_BATCH
MOE_TILES = (N_TOK + N_GROUPS * (TM - 1)) // TM
MOE_ROWS = MOE_TILES * TM
HID_WORK = MOE_TILES * EXP_PER_GROUP
DOWN_TN = 1024
DOWN_NCOL = D_MODEL // DOWN_TN
DOWN_WORK = MOE_TILES * DOWN_NCOL

RET_CHUNK = 256
HG_TILE = 256
HG_BLK = 16
SAMPLE_NB = 8

VMEM_LIMIT = 48 * 1024 * 1024


def _cparams(sem):
    return pltpu.CompilerParams(dimension_semantics=sem, vmem_limit_bytes=VMEM_LIMIT)


def _sigmoid(x):
    return 1.0 / (1.0 + jnp.exp(-x))


def _silu(x):
    return x * _sigmoid(x)


C_ROWS = 40
C_SAMPLE0 = 8
PROMPT_TILES = NP_ROWS // TM
MOD_COL = dict(sh1=0, sc1=1, g1=2, sh2=3, sc2=4, g2=5)


def _tile_mod(m_ref, i):
    n = m_ref.shape[-1]
    seq = jnp.minimum(i // (SEQ // TM), BATCH - 1)
    prow = jnp.broadcast_to(m_ref[pl.ds(seq, 1), :], (DEC_BATCH, n))
    srows = m_ref[C_SAMPLE0:C_SAMPLE0 + DEC_BATCH, :]
    m = jnp.where(i < PROMPT_TILES, prow, srows)
    return jnp.broadcast_to(m[:, None, :], (DEC_BATCH, SROWS, n)).reshape(TM, n)


N_SLABS = TM // SROWS


def _stage_mod(m_ref, stage_ref, i):
    n = m_ref.shape[-1]
    seq = jnp.minimum(i // (SEQ // TM), BATCH - 1)
    prow = jnp.broadcast_to(m_ref[pl.ds(seq, 1), :], (N_SLABS, n))
    srows = m_ref[C_SAMPLE0:C_SAMPLE0 + DEC_BATCH, :]
    stage_ref[...] = jnp.where(i < PROMPT_TILES, prow, srows)


def _slab(s):
    return slice(s * SROWS, (s + 1) * SROWS)


def _mod_spec(l, name, tn=D_MODEL, colmap=None):
    base = MOD_COL[name] * (D_MODEL // tn)
    if colmap is None:
        return pl.BlockSpec((None, C_ROWS, tn), lambda *ids: (l, 0, base))
    return pl.BlockSpec((None, C_ROWS, tn), lambda *ids: (l, 0, base + colmap(*ids)))


def _ada_kernel(c_ref, w_ref, b_ref, o_ref):
    c = _silu(c_ref[...])
    o_ref[...] = jnp.dot(c.astype(BF16), w_ref[...].astype(BF16),
                         preferred_element_type=F32) + b_ref[...]


def _ada(c_all, w_ada, b_ada):
    rows = c_all.shape[0]
    tn = 1024
    ncol = w_ada.shape[2] // tn
    return pl.pallas_call(
        _ada_kernel,
        grid=(DEPTH, ncol),
        in_specs=[
            pl.BlockSpec((rows, D_MODEL), lambda l, j: (0, 0)),
            pl.BlockSpec((None, D_MODEL, tn), lambda l, j: (l, 0, j)),
            pl.BlockSpec((None, 1, tn), lambda l, j: (l, 0, j)),
        ],
        out_specs=pl.BlockSpec((None, rows, tn), lambda l, j: (l, 0, j)),
        out_shape=jax.ShapeDtypeStruct((DEPTH, rows, w_ada.shape[2]), F32),
        compiler_params=_cparams(("arbitrary", "arbitrary")),
        name="ada",
    )(c_all, w_ada, b_ada.reshape(DEPTH, 1, -1))


def _norm_mod_rows(x, g, sc_ref, sh_ref, i):
    y = x * lax.rsqrt(jnp.mean(x * x, axis=-1, keepdims=True) + EPS) * g
    return y * (1.0 + _tile_mod(sc_ref, i)) + _tile_mod(sh_ref, i)


def _norm_mod_slab(x, g, sc_row, sh_row):
    y = x * lax.rsqrt(jnp.mean(x * x, axis=-1, keepdims=True) + EPS) * g
    return y * (1.0 + sc_row) + sh_row


def _norm_kernel(xp_ref, xs_ref, g_ref, sc_ref, sh_ref, x_ref, o_ref):
    i = pl.program_id(0)
    xs = xs_ref[...]
    rid = lax.broadcasted_iota(jnp.int32, (DEC_BATCH, SROWS, 1), 1)
    blk = jnp.where(rid == 0, jnp.broadcast_to(xs[:, None, :], (DEC_BATCH, SROWS, D_MODEL)), 0.0)
    x = jnp.where(i < PROMPT_TILES, xp_ref[...], blk.reshape(TM, D_MODEL))
    x_ref[...] = x
    o_ref[...] = _norm_mod_rows(x, g_ref[...], sc_ref, sh_ref, i).astype(o_ref.dtype)


def _norm_mod_first(x_prompt, x_sample, g, mod, l):
    return pl.pallas_call(
        _norm_kernel,
        grid=(N_TILES,),
        in_specs=[
            pl.BlockSpec((TM, D_MODEL), lambda i: (jnp.minimum(i, PROMPT_TILES - 1), 0)),
            pl.BlockSpec((DEC_BATCH, D_MODEL), lambda i: (0, 0)),
            pl.BlockSpec((1, D_MODEL), lambda i: (0, 0)),
            _mod_spec(l, "sc1"),
            _mod_spec(l, "sh1"),
        ],
        out_specs=[
            pl.BlockSpec((TM, D_MODEL), lambda i: (i, 0)),
            pl.BlockSpec((TM, D_MODEL), lambda i: (i, 0)),
        ],
        out_shape=[
            jax.ShapeDtypeStruct((N_ROWS, D_MODEL), F32),
            jax.ShapeDtypeStruct((N_ROWS, D_MODEL), BF16),
        ],
        compiler_params=_cparams(("arbitrary",)),
        name="norm_first",
    )(x_prompt, x_sample, g.reshape(1, D_MODEL), mod, mod)


def _mm_in_kernel(x_ref, w_ref, o_ref, wb_ref):
    @pl.when(pl.program_id(1) == 0)
    def _():
        wb_ref[...] = w_ref[...].astype(BF16)

    o_ref[...] = jnp.dot(x_ref[...], wb_ref[...], preferred_element_type=F32)


def _mm_in(h, w_in, l):
    tn = 1024
    ncol = IN_WIDTH // tn
    return pl.pallas_call(
        _mm_in_kernel,
        grid=(ncol, N_TILES),
        in_specs=[
            pl.BlockSpec((TM, D_MODEL), lambda j, i: (i, 0)),
            pl.BlockSpec((None, D_MODEL, tn), lambda j, i: (l, 0, j)),
        ],
        out_specs=pl.BlockSpec((TM, tn), lambda j, i: (i, j)),
        out_shape=jax.ShapeDtypeStruct((N_ROWS, IN_WIDTH), F32),
        scratch_shapes=[pltpu.VMEM((D_MODEL, tn), BF16)],
        compiler_params=_cparams(("arbitrary", "arbitrary")),
        name="mm_in",
    )(h, w_in)


def _rms(x, g):
    return x * lax.rsqrt(jnp.mean(x * x, axis=-1, keepdims=True) + EPS) * g


def _swa_kernel(sink_ref, q_ref, k_ref, v_ref, pk_ref, pv_ref, qg_ref, kg_ref,
                *rest, tq, nb, prev_raw):
    o_ref, kn_ref = rest[-2], rest[-1]
    if prev_raw:
        has_prev = pl.program_id(1) > 0
    else:
        has_prev = True
    qg = qg_ref[...]
    kg = kg_ref[...]
    nk = WINDOW + tq
    nq = A_GROUP * tq
    row = lax.broadcasted_iota(jnp.int32, (nq, nk), 0) & (tq - 1)
    col = lax.broadcasted_iota(jnp.int32, (nq, nk), 1)
    vis_prev = (col < WINDOW) & (col >= row) & has_prev
    vis_own = (col >= WINDOW) & ((col - WINDOW) <= row)
    mask = vis_prev | vis_own
    rgrp = lax.broadcasted_iota(jnp.int32, (nq, 1), 0) // tq
    for sb in range(nb):
        rs = slice(sb * tq, (sb + 1) * tq)
        ps = slice(sb * WINDOW, (sb + 1) * WINDOW)
        for kv in range(A_KV):
            ks = slice(kv * A_HD, (kv + 1) * A_HD)
            kn = _rms(k_ref[rs, ks], kg)
            kn_ref[rs, ks] = kn
            pk = pk_ref[ps, ks]
            if prev_raw:
                pk = _rms(pk, kg)
            keys = jnp.concatenate([pk, kn], axis=0).astype(BF16)
            vals = jnp.concatenate([pv_ref[ps, ks], v_ref[rs, ks]], axis=0).astype(BF16)
            h0 = kv * A_GROUP
            qs = jnp.concatenate(
                [_rms(q_ref[rs, (h0 + g) * A_HD:(h0 + g + 1) * A_HD], qg) for g in range(A_GROUP)],
                axis=0).astype(BF16)
            s = lax.dot_general(qs, keys, (((1,), (1,)), ((), ())),
                                preferred_element_type=F32) * (A_HD ** -0.5)
            s = jnp.where(mask, s, NEG_BIG)
            snk = jnp.full((nq, 1), sink_ref[h0 + A_GROUP - 1], F32)
            for g in range(A_GROUP - 2, -1, -1):
                snk = jnp.where(rgrp == g, sink_ref[h0 + g], snk)
            m = jnp.maximum(jnp.max(s, axis=-1, keepdims=True), snk)
            p = jnp.exp(s - m)
            den = jnp.sum(p, axis=-1, keepdims=True) + jnp.exp(snk - m)
            o = jnp.dot(p.astype(BF16), vals, preferred_element_type=F32) / den
            for g in range(A_GROUP):
                o_ref[rs, (h0 + g) * A_HD:(h0 + g + 1) * A_HD] = o[g * tq:(g + 1) * tq].astype(o_ref.dtype)


def _swa(proj, prev_k, prev_v, sinks, qn_g, kn_g, *, tq, nb, n_seq, blocks_per_seq, row_block0,
         prev_raw, alias_bufs=None):
    qcol, kcol, vcol = 0, 1024 // 256, 1280 // 256
    assert nb == 1 or blocks_per_seq == 1

    def rb(s, n):
        return row_block0 + s * blocks_per_seq + n

    if prev_raw:
        def prev_map_k(s, n):
            return (rb(s, jnp.maximum(n - 1, 0)), kcol)

        def prev_map_v(s, n):
            return (rb(s, jnp.maximum(n - 1, 0)), vcol)
    else:
        def prev_map_k(s, n):
            return (s, 0)
        prev_map_v = prev_map_k

    rows = tq * nb
    in_specs = [
        pl.BlockSpec(memory_space=pltpu.SMEM),
        pl.BlockSpec((rows, 1024), lambda s, n: (rb(s, n), qcol)),
        pl.BlockSpec((rows, 256), lambda s, n: (rb(s, n), kcol)),
        pl.BlockSpec((rows, 256), lambda s, n: (rb(s, n), vcol)),
        pl.BlockSpec((WINDOW * nb, 256), prev_map_k),
        pl.BlockSpec((WINDOW * nb, 256), prev_map_v),
        pl.BlockSpec((1, A_HD), lambda s, n: (0, 0)),
        pl.BlockSpec((1, A_HD), lambda s, n: (0, 0)),
    ]
    args = [sinks, proj, proj, proj, prev_k, prev_v, qn_g.reshape(1, A_HD), kn_g.reshape(1, A_HD)]
    aliases = {}
    if alias_bufs is not None:
        for j, buf in enumerate(alias_bufs):
            in_specs.append(pl.BlockSpec(memory_space=pl.ANY))
            aliases[len(args)] = j
            args.append(buf)
    return pl.pallas_call(
        functools.partial(_swa_kernel, tq=tq, nb=nb, prev_raw=prev_raw),
        grid=(n_seq // nb, blocks_per_seq),
        in_specs=in_specs,
        out_specs=[
            pl.BlockSpec((rows, 1024), lambda s, n: (rb(s, n), 0)),
            pl.BlockSpec((rows, 256), lambda s, n: (rb(s, n), 0)),
        ],
        out_shape=[
            jax.ShapeDtypeStruct((N_ROWS, 1024), BF16),
            jax.ShapeDtypeStruct((N_ROWS, 256), F32),
        ],
        input_output_aliases=aliases,
        compiler_params=_cparams(("arbitrary", "arbitrary")),
        name="swa_prompt" if prev_raw else "swa_sample",
    )(*args)


def _ret_tables(chunk, tv, pos0, t_len):
    lg = np.log1p(-(2.0 ** (-5.0 - np.arange(B_HEADS, dtype=np.float64))))
    idx = np.arange(chunk, dtype=np.float64)
    valid = idx < tv
    diff = idx[:, None] - idx[None, :]
    dmat = np.where((diff >= 0) & valid[:, None] & valid[None, :],
                    np.exp(np.maximum(diff, 0.0)[None] * lg[:, None, None]), 0.0)
    din = np.exp((idx + 1.0)[:, None] * lg[None])
    din = np.repeat(din, B_DV, axis=1)
    kfac = np.where(valid[:, None], np.exp((tv - 1.0 - idx)[:, None] * lg[None]), 0.0)
    kfac = np.repeat(kfac, B_DK, axis=1)
    gc = np.repeat(np.exp(tv * lg), B_DK)[:, None] * np.ones((1, B_DV))
    half = B_DK // 2
    inv = ROPE_BASE ** (-np.arange(half, dtype=np.float64) / half)
    ang = (pos0 + np.arange(t_len, dtype=np.float64))[:, None] * inv[None]
    cos = np.tile(np.cos(ang), (1, 2 * B_HEADS))
    sin = np.tile(np.concatenate([-np.sin(ang), np.sin(ang)], axis=1), (1, B_HEADS))
    f = lambda a: jnp.asarray(a, dtype=F32)
    return f(dmat), f(din), f(kfac), f(gc), f(cos), f(sin)


def _ret_kernel(q_ref, k_ref, v_ref, g_ref, cos_ref, sin_ref, dmat_ref, din_ref, kfac_ref,
                gc_ref, gng_ref, gnb_ref, s0_ref, *rest, nb, chunk):
    o_ref, s_ref = rest[-2], rest[-1]

    @pl.when(pl.program_id(1) == 0)
    def _():
        s_ref[...] = s0_ref[...]

    cos = cos_ref[...]
    sin = sin_ref[...]
    width = B_HEADS * B_DK
    lane = lax.broadcasted_iota(jnp.int32, (1, width), 1)
    first_half = (lane % B_DK) < (B_DK // 2)
    lane_head = lane // B_DK

    def rope(x):
        partner = jnp.where(first_half, pltpu.roll(x, width - B_DK // 2, 1),
                            pltpu.roll(x, B_DK // 2, 1))
        return x * cos + partner * sin

    gc = gc_ref[...]
    for sb in range(nb):
        rws = slice(sb * chunk, (sb + 1) * chunk)
        qr = rope(q_ref[rws, :])
        kr = rope(k_ref[rws, :]) * (B_DK ** -0.5)
        kb = kr.astype(BF16)
        kdec = (kr * kfac_ref[...]).astype(BF16)
        s_old = s_ref[sb]
        s_b = s_old.astype(BF16)
        for h in range(B_HEADS):
            vs = slice(h * B_DV, (h + 1) * B_DV)
            rs = slice(h * B_DK, (h + 1) * B_DK)
            qm = jnp.where(lane_head == h, qr, 0.0).astype(BF16)
            a = lax.dot_general(qm, kb, (((1,), (1,)), ((), ())),
                                preferred_element_type=F32) * dmat_ref[h]
            vh = v_ref[rws, vs].astype(BF16)
            o = jnp.dot(a.astype(BF16), vh, preferred_element_type=F32)
            o = o + jnp.dot(qm, s_b, preferred_element_type=F32) * din_ref[:, vs]
            u = lax.dot_general(kdec, vh, (((0,), (0,)), ((), ())), preferred_element_type=F32)
            s_ref[sb, rs, :] = gc[rs, :] * s_old[rs, :] + u[rs, :]
            mu = jnp.mean(o, axis=-1, keepdims=True)
            oc = o - mu
            var = jnp.mean(oc * oc, axis=-1, keepdims=True)
            y = oc * lax.rsqrt(var + EPS) * gng_ref[:, vs] + gnb_ref[:, vs]
            y = y * _silu(g_ref[rws, vs])
            o_ref[rws, vs] = y.astype(o_ref.dtype)


def _ret(proj, s0, gn_g, gn_b, tables, *, chunk, nb, n_seq, chunks_per_seq, row_block0,
         table_per_chunk, alias_buf=None):
    dmat, din, kfac, gc, cos, sin = tables
    qcol, kcol, vcol, gcol = 1536 // 256, 1792 // 256, 2048 // 512, 2560 // 512
    assert nb == 1 or chunks_per_seq == 1
    rows = nb * chunk

    def rb(s, c):
        return row_block0 + s * chunks_per_seq + c

    tmap = (lambda s, c: (c, 0)) if table_per_chunk else (lambda s, c: (0, 0))
    const2 = lambda s, c: (0, 0)
    in_specs = [
        pl.BlockSpec((rows, 256), lambda s, c: (rb(s, c), qcol)),
        pl.BlockSpec((rows, 256), lambda s, c: (rb(s, c), kcol)),
        pl.BlockSpec((rows, 512), lambda s, c: (rb(s, c), vcol)),
        pl.BlockSpec((rows, 512), lambda s, c: (rb(s, c), gcol)),
        pl.BlockSpec((chunk, 256), tmap),
        pl.BlockSpec((chunk, 256), tmap),
        pl.BlockSpec((B_HEADS, chunk, chunk), lambda s, c: (0, 0, 0)),
        pl.BlockSpec((chunk, 512), const2),
        pl.BlockSpec((chunk, 256), const2),
        pl.BlockSpec((256, B_DV), const2),
        pl.BlockSpec((1, 512), const2),
        pl.BlockSpec((1, 512), const2),
        pl.BlockSpec((nb, 256, B_DV), lambda s, c: (s, 0, 0)),
    ]
    args = [proj, proj, proj, proj, cos, sin, dmat, din, kfac, gc,
            gn_g.reshape(1, 512), gn_b.reshape(1, 512), s0.reshape(n_seq, 256, B_DV)]
    aliases = {}
    if alias_buf is not None:
        in_specs.append(pl.BlockSpec(memory_space=pl.ANY))
        aliases[len(args)] = 0
        args.append(alias_buf)
    o, s_new = pl.pallas_call(
        functools.partial(_ret_kernel, nb=nb, chunk=chunk),
        grid=(n_seq // nb, chunks_per_seq),
        in_specs=in_specs,
        out_specs=[
            pl.BlockSpec((rows, 512), lambda s, c: (rb(s, c), 0)),
            pl.BlockSpec((nb, 256, B_DV), lambda s, c: (s, 0, 0)),
        ],
        out_shape=[
            jax.ShapeDtypeStruct((N_ROWS, 512), BF16),
            jax.ShapeDtypeStruct((n_seq, 256, B_DV), F32),
        ],
        input_output_aliases=aliases,
        compiler_params=_cparams(("arbitrary", "arbitrary")),
        name="ret_prompt" if alias_buf is None else "ret_sample",
    )(*args)
    return o, s_new.reshape(n_seq, B_HEADS, B_DK, B_DV)


HG_FAST = 32
HG_LIM = 60.0
HG_HDR = 32


def _hgrn_kernel(q_ref, f_ref, i_ref, g_ref, lb_ref, ng_ref, st0_ref, *rest, rows, seq_rows, tv):
    use_fast = seq_rows % HG_FAST == 0
    if use_fast:
        o_ref, st_ref, q_s, k_s, cs_s, qd_s, ki_s, kd_s, eb_s, stb_s = rest[-10:]
    else:
        o_ref, st_ref, q_s, k_s, cs_s = rest[-5:]
    nseq = rows // seq_rows

    @pl.when(pl.program_id(1) == 0)
    def _():
        st_ref[...] = st0_ref[...]

    z = f_ref[...]
    lb = lb_ref[...]
    sig = _sigmoid(z)
    lf = jnp.log(lb + (1.0 - lb) * sig)
    k = (1.0 - lb) * _sigmoid(-z)
    if tv < seq_rows:
        rvalid = (lax.broadcasted_iota(jnp.int32, (rows, 1), 0) % seq_rows) < tv
        lf = jnp.where(rvalid, lf, 0.0)
        k = jnp.where(rvalid, k, 0.0)
    ri = lax.broadcasted_iota(jnp.int32, (rows, rows), 0)
    ci = lax.broadcasted_iota(jnp.int32, (rows, rows), 1)
    tri = (ci <= ri).astype(F32)
    cs = jnp.dot(tri, lf, precision=lax.Precision.HIGHEST, preferred_element_type=F32)
    q_s[...] = _silu(q_ref[...])
    k_s[...] = k
    cs_s[0:HG_HDR, :] = jnp.zeros((HG_HDR, C_HEADS * C_DK), F32)
    cs_s[HG_HDR:HG_HDR + rows, :] = cs
    ng = ng_ref[...]

    nt_dims = (((1,), (1,)), ((), ()))
    tn_dims = (((0,), (0,)), ((), ()))

    def finish(o, rsl, hs):
        y = o * lax.rsqrt(jnp.mean(o * o, axis=-1, keepdims=True) + EPS) * ng
        return (y * _silu(g_ref[rsl, hs])).astype(o_ref.dtype)

    def pairwise_block(blk, carry):
        size = HG_BLK
        r0 = pl.multiple_of(blk * size, size)
        rsl = pl.ds(r0, size)
        sidx = 0 if nseq == 1 else blk // (seq_rows // size)
        rowi = lax.broadcasted_iota(jnp.int32, (size, 1), 0)
        for h in range(C_HEADS):
            hs = slice(h * C_DK, (h + 1) * C_DK)
            vsl = slice(h * C_DV, (h + 1) * C_DV)
            qb = q_s[rsl, hs]
            kb = k_s[rsl, hs]
            vb = i_ref[rsl, hs]
            prev = cs_s[pl.ds(HG_HDR + r0 - 8, 8), hs][7:8, :]
            bc = cs_s[pl.ds(HG_HDR + r0, size), hs] - prev
            blast = bc[size - 1:size, :]
            qd = (qb * jnp.exp(bc)).astype(BF16)
            kd = (kb * jnp.exp(blast - bc)).astype(BF16)
            st = st_ref[sidx, vsl, :]
            o = lax.dot_general(qd, st.astype(BF16), nt_dims, preferred_element_type=F32)
            for j in range(size):
                dec = jnp.exp(jnp.minimum(bc - bc[j:j + 1, :], 0.0))
                a = jnp.sum(qb * kb[j:j + 1, :] * dec, axis=-1, keepdims=True)
                a = jnp.where(rowi >= j, a, 0.0)
                o = o + a * vb[j:j + 1, :]
            u = lax.dot_general(vb.astype(BF16), kd, tn_dims, preferred_element_type=F32)
            st_ref[sidx, vsl, :] = st * jnp.exp(blast) + u
            o_ref[rsl, vsl] = finish(o, rsl, hs)
        return carry

    def factorised_tile():
        nblk = rows // HG_FAST
        for b in range(nblk):
            r = slice(b * HG_FAST, (b + 1) * HG_FAST)
            lo = HG_HDR + b * HG_FAST
            prev = cs_s[lo - 8:lo, :][7:8, :]
            bc = cs_s[lo:lo + HG_FAST, :] - prev
            blast = bc[HG_FAST - 1:HG_FAST, :]
            qb = q_s[r, :]
            kb = k_s[r, :]
            qd_s[r, :] = (qb * jnp.exp(bc)).astype(BF16)
            ki_s[r, :] = (kb * jnp.exp(-bc)).astype(BF16)
            kd_s[r, :] = (kb * jnp.exp(blast - bc)).astype(BF16)
            eb_s[b:b + 1, :] = jnp.exp(blast)
        same = (ri // HG_FAST) == (ci // HG_FAST)
        keep = same & (ci <= ri)
        for h in range(C_HEADS):
            hs = slice(h * C_DK, (h + 1) * C_DK)
            vsl = slice(h * C_DV, (h + 1) * C_DV)
            vh = i_ref[:, hs].astype(BF16)
            st = st_ref[0, vsl, :]
            for b in range(nblk):
                r = slice(b * HG_FAST, (b + 1) * HG_FAST)
                stb_s[h, b] = st.astype(BF16)
                u = lax.dot_general(vh[r, :], kd_s[r, hs], tn_dims, preferred_element_type=F32)
                st = st * eb_s[b:b + 1, hs] + u
            st_ref[0, vsl, :] = st
            a = lax.dot_general(qd_s[:, hs], ki_s[:, hs], nt_dims, preferred_element_type=F32)
            a = jnp.where(keep, a, 0.0)
            o = jnp.dot(a.astype(BF16), vh, preferred_element_type=F32)
            inter = [lax.dot_general(qd_s[b * HG_FAST:(b + 1) * HG_FAST, hs], stb_s[h, b], nt_dims,
                                     preferred_element_type=F32) for b in range(nblk)]
            o = o + jnp.concatenate(inter, axis=0)
            o_ref[:, vsl] = finish(o, slice(None), hs)

    def pairwise_tile():
        lax.fori_loop(0, rows // HG_BLK, pairwise_block, 0)

    if use_fast:
        worst = jnp.max(cs_s[HG_HDR - HG_FAST:HG_HDR - HG_FAST + rows, :] - cs)
        bounded = worst < HG_LIM
        pl.when(bounded)(factorised_tile)
        pl.when(jnp.logical_not(bounded))(pairwise_tile)
    else:
        pairwise_tile()


def _hgrn(proj, st0, lb, ng, *, rows, seq_rows, tv, n_seq, tiles_per_seq, row_block0, alias_buf=None):
    qcol, fcol, icol, gcol = 3072 // 512, 3584 // 512, 4096 // 512, 4608 // 512
    nst = max(rows // seq_rows, 1)
    assert nst == 1 or tiles_per_seq == 1
    width = C_HEADS * C_DK
    scratch = [pltpu.VMEM((rows, width), F32), pltpu.VMEM((rows, width), F32),
               pltpu.VMEM((HG_HDR + rows, width), F32)]
    if seq_rows % HG_FAST == 0:
        scratch += [pltpu.VMEM((rows, width), BF16) for _ in range(3)]
        scratch += [pltpu.VMEM((rows // HG_FAST, width), F32),
                    pltpu.VMEM((C_HEADS, rows // HG_FAST, C_DV, C_DK), BF16)]

    def rb(s, c):
        return row_block0 + s * tiles_per_seq + c

    const2 = lambda s, c: (0, 0)
    in_specs = [
        pl.BlockSpec((rows, 512), lambda s, c: (rb(s, c), qcol)),
        pl.BlockSpec((rows, 512), lambda s, c: (rb(s, c), fcol)),
        pl.BlockSpec((rows, 512), lambda s, c: (rb(s, c), icol)),
        pl.BlockSpec((rows, 512), lambda s, c: (rb(s, c), gcol)),
        pl.BlockSpec((1, 512), const2),
        pl.BlockSpec((1, C_DV), const2),
        pl.BlockSpec((nst, 512, C_DK), lambda s, c: (s, 0, 0)),
    ]
    args = [proj, proj, proj, proj, lb.reshape(1, 512), ng.reshape(1, C_DV), st0]
    aliases = {}
    if alias_buf is not None:
        in_specs.append(pl.BlockSpec(memory_space=pl.ANY))
        aliases[len(args)] = 0
        args.append(alias_buf)
    return pl.pallas_call(
        functools.partial(_hgrn_kernel, rows=rows, seq_rows=seq_rows, tv=tv),
        grid=(n_seq // nst, tiles_per_seq),
        in_specs=in_specs,
        out_specs=[
            pl.BlockSpec((rows, 512), lambda s, c: (rb(s, c), 0)),
            pl.BlockSpec((nst, 512, C_DK), lambda s, c: (s, 0, 0)),
        ],
        out_shape=[
            jax.ShapeDtypeStruct((N_ROWS, 512), BF16),
            jax.ShapeDtypeStruct((n_seq, 512, C_DK), F32),
        ],
        scratch_shapes=scratch,
        input_output_aliases=aliases,
        compiler_params=_cparams(("arbitrary", "arbitrary")),
        name="hgrn_prompt" if alias_buf is None else "hgrn_sample",
    )(*args)


def _mm_out_kernel(oa_ref, ob_ref, oc_ref, w_ref, x_ref, g_ref, o_ref, wb_ref):
    @pl.when(pl.program_id(1) == 0)
    def _():
        wb_ref[...] = w_ref[...].astype(BF16)

    acc = jnp.dot(oa_ref[...], wb_ref[0:1024, :], preferred_element_type=F32)
    acc = acc + jnp.dot(ob_ref[...], wb_ref[1024:1536, :], preferred_element_type=F32)
    acc = acc + jnp.dot(oc_ref[...], wb_ref[1536:2048, :], preferred_element_type=F32)
    o_ref[...] = x_ref[...] + _tile_mod(g_ref, pl.program_id(1)) * acc


def _mm_out(oa, ob, oc, w_out, x, mod, l):
    tn = 1024
    ncol = D_MODEL // tn
    return pl.pallas_call(
        _mm_out_kernel,
        grid=(ncol, N_TILES),
        in_specs=[
            pl.BlockSpec((TM, 1024), lambda j, i: (i, 0)),
            pl.BlockSpec((TM, 512), lambda j, i: (i, 0)),
            pl.BlockSpec((TM, 512), lambda j, i: (i, 0)),
            pl.BlockSpec((None, D_MODEL, tn), lambda j, i: (l, 0, j)),
            pl.BlockSpec((TM, tn), lambda j, i: (i, j)),
            _mod_spec(l, "g1", tn, lambda j, i: j),
        ],
        out_specs=pl.BlockSpec((TM, tn), lambda j, i: (i, j)),
        out_shape=jax.ShapeDtypeStruct((N_ROWS, D_MODEL), F32),
        scratch_shapes=[pltpu.VMEM((D_MODEL, tn), BF16)],
        compiler_params=_cparams(("arbitrary", "arbitrary")),
        name="mm_out",
    )(oa, ob, oc, w_out, x, mod)


META_GSEL = 0
META_RANK = 1
META_GATE = 20
XW = D_MODEL + 128


def _route_kernel(x_ref, g_ref, sc_ref, sh_ref, whl_ref, wh_ref, br_ref, hx_ref, meta_ref, cnt_ref,
                  run_ref, sc_st, sh_st, hi_s, lo_s):
    i = pl.program_id(0)

    @pl.when(i == 0)
    def _():
        run_ref[...] = jnp.zeros_like(run_ref)

    _stage_mod(sc_ref, sc_st, i)
    _stage_mod(sh_ref, sh_st, i)
    lg = g_ref[...]
    for s in range(N_SLABS):
        r = _slab(s)
        h = _norm_mod_slab(x_ref[r, :], lg, sc_st[s:s + 1, :], sh_st[s:s + 1, :])
        hx_ref[r, 0:D_MODEL] = h
        hi = h.astype(BF16)
        hi_s[r, :] = hi
        lo_s[r, :] = (h - hi.astype(F32)).astype(BF16)
    l1 = jnp.dot(hi_s[...], whl_ref[...], preferred_element_type=F32)
    l2 = jnp.dot(lo_s[...], wh_ref[...], preferred_element_type=F32)
    logits = l1[:, 0:128] + l1[:, 128:256] + l2 + br_ref[...]
    col = lax.broadcasted_iota(jnp.int32, logits.shape, 1).astype(F32)
    big = 1e9
    is_g = col < N_GROUPS
    gl = jnp.where(is_g, logits, -jnp.inf)
    gmax = jnp.max(gl, axis=-1, keepdims=True)
    gsel = jnp.min(jnp.where(gl == gmax, col, big), axis=-1, keepdims=True)
    gden = jnp.sum(jnp.where(is_g, jnp.exp(logits - gmax), 0.0), axis=-1, keepdims=True)
    g_w = 1.0 / gden
    e0 = N_GROUPS + EXP_PER_GROUP * gsel
    in_grp = (col >= e0) & (col < e0 + EXP_PER_GROUP)
    el = jnp.where(in_grp, logits, -jnp.inf)
    v1 = jnp.max(el, axis=-1, keepdims=True)
    i1 = jnp.min(jnp.where(el == v1, col, big), axis=-1, keepdims=True)
    el2 = jnp.where(col == i1, -jnp.inf, el)
    v2 = jnp.max(el2, axis=-1, keepdims=True)
    i2 = jnp.min(jnp.where(el2 == v2, col, big), axis=-1, keepdims=True)
    t = jnp.exp(v2 - v1)
    p1 = 1.0 / (1.0 + t)
    a1 = p1 * g_w
    a2 = t * p1 * g_w
    meta = jnp.where(col == META_GATE + (i1 - e0), a1, 0.0)
    meta = meta + jnp.where(col == META_GATE + (i2 - e0), a2, 0.0)
    meta = meta + jnp.where(col == META_GSEL, gsel, 0.0)
    rows = logits.shape[0]
    rowi = lax.broadcasted_iota(jnp.int32, (rows, 1), 0)
    is_tok = (i < PROMPT_TILES) | (rowi % SROWS == 0)
    onehot = jnp.where((col == gsel) & is_tok, 1.0, 0.0)
    ri = lax.broadcasted_iota(jnp.int32, (rows, rows), 0)
    ci = lax.broadcasted_iota(jnp.int32, (rows, rows), 1)
    tri = jnp.where(ci <= ri, 1.0, 0.0).astype(BF16)
    incl = jnp.dot(tri, onehot.astype(BF16), preferred_element_type=F32)
    run = run_ref[...]
    rank = jnp.sum(jnp.where(col == gsel, incl + run - 1.0, 0.0), axis=-1, keepdims=True)
    meta = meta + jnp.where(col == META_RANK, rank, 0.0)
    meta_ref[...] = meta
    hx_ref[:, D_MODEL:XW] = meta
    run = run + incl[rows - 1:rows, :]
    run_ref[...] = run
    cnt_ref[...] = run


def _route(x, g, mod, wr, br, l):
    w_hi = wr.astype(BF16)
    w_lo = (wr - w_hi.astype(F32)).astype(BF16)
    whl = jnp.concatenate([w_hi, w_lo], axis=1)
    return pl.pallas_call(
        _route_kernel,
        grid=(N_TILES,),
        in_specs=[
            pl.BlockSpec((TM, D_MODEL), lambda i: (i, 0)),
            pl.BlockSpec((1, D_MODEL), lambda i: (0, 0)),
            _mod_spec(l, "sc2"),
            _mod_spec(l, "sh2"),
            pl.BlockSpec((D_MODEL, 256), lambda i: (0, 0)),
            pl.BlockSpec((D_MODEL, 128), lambda i: (0, 0)),
            pl.BlockSpec((1, 128), lambda i: (0, 0)),
        ],
        out_specs=[
            pl.BlockSpec((TM, XW), lambda i: (i, 0)),
            pl.BlockSpec((TM, 128), lambda i: (i, 0)),
            pl.BlockSpec((1, 128), lambda i: (0, 0)),
        ],
        out_shape=[
            jax.ShapeDtypeStruct((N_ROWS, XW), F32),
            jax.ShapeDtypeStruct((N_ROWS, 128), F32),
            jax.ShapeDtypeStruct((1, 128), F32),
        ],
        scratch_shapes=[pltpu.VMEM((1, 128), F32),
                        pltpu.VMEM((N_SLABS, D_MODEL), F32), pltpu.VMEM((N_SLABS, D_MODEL), F32),
                        pltpu.VMEM((TM, D_MODEL), BF16), pltpu.VMEM((TM, D_MODEL), BF16)],
        compiler_params=_cparams(("arbitrary",)),
        name="route",
    )(x, g.reshape(1, D_MODEL), mod, mod, whl, w_hi, br)


def _moe_positions(meta, cnt):
    i32 = jnp.int32
    counts = cnt[0, :N_GROUPS].astype(i32)
    ntile = (counts + TM - 1) // TM
    tstart = jnp.cumsum(ntile) - ntile
    gsel = meta[:, META_GSEL].astype(i32)
    rank = meta[:, META_RANK].astype(i32)
    first_row = jnp.zeros_like(gsel)
    for g in range(N_GROUPS):
        first_row = jnp.where(gsel == g, tstart[g] * TM, first_row)
    pos = first_row + rank
    pos_all = jnp.concatenate([pos[:NP_ROWS], jnp.repeat(pos[NP_ROWS::SROWS], SROWS)])
    plan = jnp.concatenate([tstart, ntile]).astype(i32)
    return pos_all, plan


def _scatter_kernel(pos_ref, src_ref, buf_hbm, out_hbm, sem):
    del buf_hbm
    i = pl.program_id(0)
    base = i * TM

    def row_copy(r):
        return pltpu.make_async_copy(src_ref.at[pl.ds(r, 1), :],
                                     out_hbm.at[pl.ds(pos_ref[base + r], 1), :], sem)

    def run(n, stride):
        def issue(k, c):
            row_copy(k * stride).start()
            return c

        def drain(k, c):
            row_copy(k * stride).wait()
            return c

        lax.fori_loop(0, n, issue, 0, unroll=8)
        lax.fori_loop(0, n, drain, 0, unroll=8)

    @pl.when(i < PROMPT_TILES)
    def _():
        run(TM, 1)

    @pl.when(i == PROMPT_TILES)
    def _():
        run(DEC_BATCH, SROWS)


def _moe_scatter(hx, buf, pos_all):
    grid_spec = pltpu.PrefetchScalarGridSpec(
        num_scalar_prefetch=1,
        grid=(N_TILES,),
        in_specs=[pl.BlockSpec((TM, XW), lambda i, pos: (i, 0)), pl.BlockSpec(memory_space=pl.ANY)],
        out_specs=pl.BlockSpec(memory_space=pl.ANY),
        scratch_shapes=[pltpu.SemaphoreType.DMA(())],
    )
    return pl.pallas_call(
        _scatter_kernel,
        grid_spec=grid_spec,
        out_shape=jax.ShapeDtypeStruct((MOE_ROWS, XW), F32),
        input_output_aliases={2: 0},
        compiler_params=_cparams(("arbitrary",)),
        name="moe_scatter",
    )(pos_all, hx, buf)


def _work_item(w, plan_ref, per_tile):
    i32 = jnp.int32
    nt = [plan_ref[N_GROUPS + g] for g in range(N_GROUPS)]
    ts = [plan_ref[g] for g in range(N_GROUPS)]
    ends = []
    acc = 0
    for g in range(N_GROUPS):
        acc = acc + per_tile * nt[g]
        ends.append(acc)
    valid = w < ends[-1]
    wc = jnp.maximum(jnp.minimum(w, ends[-1] - 1), 0)
    g = sum((wc >= ends[k]).astype(i32) for k in range(N_GROUPS - 1))

    def pick(vals):
        out = vals[N_GROUPS - 1]
        for k in range(N_GROUPS - 2, -1, -1):
            out = jnp.where(g == k, vals[k], out)
        return out

    r = wc - pick([0] + ends[:-1])
    ntg = pick(nt)
    c = sum((r >= m * ntg).astype(i32) for m in range(1, per_tile))
    t_in = r - c * ntg
    t = pick(ts) + t_in
    extra = jnp.maximum(w - ends[-1], 0)
    used = ts[N_GROUPS - 1] + nt[N_GROUPS - 1]
    t_out = jnp.where(valid, t, used + extra // per_tile)
    c_out = jnp.where(valid, c, extra % per_tile)
    return g, c, t, t_in == 0, valid, t_out, c_out


def _hidden_kernel(plan_ref, x_ref, w1_ref, w3_ref, o_ref, w1b, w3b):
    _, e, _, first, valid, _, _ = _work_item(pl.program_id(0), plan_ref, EXP_PER_GROUP)

    @pl.when(jnp.logical_not(valid))
    def _():
        o_ref[...] = jnp.zeros_like(o_ref)

    @pl.when(valid & first)
    def _():
        w1b[...] = w1_ref[...].astype(BF16)
        w3b[...] = w3_ref[...].astype(BF16)

    @pl.when(valid)
    def _():
        x = x_ref[:, 0:D_MODEL].astype(BF16)
        a = jnp.dot(x, w1b[...], preferred_element_type=F32)
        b = jnp.dot(x, w3b[...], preferred_element_type=F32)
        m = x_ref[:, D_MODEL:XW]
        col = lax.broadcasted_iota(jnp.int32, m.shape, 1)
        gate = jnp.sum(jnp.where(col == META_GATE + e, m, 0.0), axis=-1, keepdims=True)
        o_ref[...] = (_silu(a) * b * gate).astype(o_ref.dtype)


def _moe_hidden(xs, w1, w3, plan, l):
    def item(w, plan):
        return _work_item(w, plan, EXP_PER_GROUP)

    def w_map(w, plan):
        it = item(w, plan)
        return (l, it[0] * EXP_PER_GROUP + it[1], 0, 0)

    grid_spec = pltpu.PrefetchScalarGridSpec(
        num_scalar_prefetch=1,
        grid=(HID_WORK,),
        in_specs=[
            pl.BlockSpec((TM, XW), lambda w, plan: (item(w, plan)[2], 0)),
            pl.BlockSpec((None, None, D_MODEL, D_EXPERT), w_map),
            pl.BlockSpec((None, None, D_MODEL, D_EXPERT), w_map),
        ],
        out_specs=pl.BlockSpec((TM, D_EXPERT), lambda w, plan: item(w, plan)[5:7]),
        scratch_shapes=[pltpu.VMEM((D_MODEL, D_EXPERT), BF16),
                        pltpu.VMEM((D_MODEL, D_EXPERT), BF16)],
    )
    return pl.pallas_call(
        _hidden_kernel,
        grid_spec=grid_spec,
        out_shape=jax.ShapeDtypeStruct((MOE_ROWS, EXP_PER_GROUP * D_EXPERT), BF16),
        compiler_params=_cparams(("arbitrary",)),
        name="moe_hidden",
    )(plan, xs, w1, w3)


def _down_kernel(plan_ref, h_ref, w_ref, o_ref, wb):
    _, _, _, first, valid, _, _ = _work_item(pl.program_id(0), plan_ref, DOWN_NCOL)

    @pl.when(jnp.logical_not(valid))
    def _():
        o_ref[...] = jnp.zeros_like(o_ref)

    @pl.when(valid & first)
    def _():
        wb[...] = w_ref[...].astype(BF16)

    @pl.when(valid)
    def _():
        o_ref[...] = jnp.dot(h_ref[...], wb[...], preferred_element_type=F32)


def _moe_down(hid, w2g, plan, l):
    kdim = EXP_PER_GROUP * D_EXPERT

    def item(w, plan):
        return _work_item(w, plan, DOWN_NCOL)

    grid_spec = pltpu.PrefetchScalarGridSpec(
        num_scalar_prefetch=1,
        grid=(DOWN_WORK,),
        in_specs=[
            pl.BlockSpec((TM, kdim), lambda w, plan: (item(w, plan)[2], 0)),
            pl.BlockSpec((None, None, kdim, DOWN_TN),
                         lambda w, plan: (l, item(w, plan)[0], 0, item(w, plan)[1])),
        ],
        out_specs=pl.BlockSpec((TM, DOWN_TN), lambda w, plan: item(w, plan)[5:7]),
        scratch_shapes=[pltpu.VMEM((kdim, DOWN_TN), BF16)],
    )
    return pl.pallas_call(
        _down_kernel,
        grid_spec=grid_spec,
        out_shape=jax.ShapeDtypeStruct((MOE_ROWS, D_MODEL), F32),
        compiler_params=_cparams(("arbitrary",)),
        name="moe_down",
    )(plan, hid, w2g)


def _res_kernel(pos_ref, x_ref, y_hbm, g2_ref, *rest, with_norm):
    if with_norm:
        lg_ref, sc_ref, sh_ref, xo_ref, ho_ref, ybuf, sem, g2_st, sc_st, sh_st = rest
    else:
        yp_ref, ys_ref, ybuf, sem, g2_st = rest
    i = pl.program_id(0)
    base = i * TM

    def row_copy(k):
        return pltpu.make_async_copy(y_hbm.at[pl.ds(pos_ref[base + k], 1), :],
                                     ybuf.at[pl.ds(k, 1), :], sem)

    def issue(k, c):
        row_copy(k).start()
        return c

    def drain(k, c):
        row_copy(k).wait()
        return c

    lax.fori_loop(0, TM, issue, 0, unroll=8)
    _stage_mod(g2_ref, g2_st, i)
    if with_norm:
        _stage_mod(sc_ref, sc_st, i)
        _stage_mod(sh_ref, sh_st, i)
    lax.fori_loop(0, TM, drain, 0, unroll=8)

    def new_x(s):
        r = _slab(s)
        return x_ref[r, :] + g2_st[s:s + 1, :] * ybuf[r, :]

    if with_norm:
        lg = lg_ref[...]
        for s in range(N_SLABS):
            xn = new_x(s)
            xo_ref[_slab(s), :] = xn
            hn = _norm_mod_slab(xn, lg, sc_st[s:s + 1, :], sh_st[s:s + 1, :])
            ho_ref[_slab(s), :] = hn.astype(ho_ref.dtype)
    else:
        @pl.when(i < PROMPT_TILES)
        def _():
            for s in range(N_SLABS):
                yp_ref[_slab(s), :] = new_x(s)

        @pl.when(i == PROMPT_TILES)
        def _():
            for s in range(N_SLABS):
                ys_ref[s:s + 1, :] = new_x(s)[0:1, :]


def _residual(x, y_sorted, pos_all, mod, l, next_g=None):
    with_norm = next_g is not None
    row_spec = pl.BlockSpec((TM, D_MODEL), lambda i, pos: (i, 0))
    in_specs = [row_spec, pl.BlockSpec(memory_space=pl.ANY), _mod_spec(l, "g2")]
    args = [pos_all, x, y_sorted, mod]
    stage = pltpu.VMEM((N_SLABS, D_MODEL), F32)
    scratch = [pltpu.VMEM((TM, D_MODEL), F32), pltpu.SemaphoreType.DMA(()), stage]
    if with_norm:
        in_specs += [pl.BlockSpec((1, D_MODEL), lambda i, pos: (0, 0)),
                     _mod_spec(l + 1, "sc1"), _mod_spec(l + 1, "sh1")]
        args += [next_g.reshape(1, D_MODEL), mod, mod]
        out_specs = [row_spec, row_spec]
        out_shape = [jax.ShapeDtypeStruct((N_ROWS, D_MODEL), F32),
                     jax.ShapeDtypeStruct((N_ROWS, D_MODEL), BF16)]
        scratch += [stage, stage]
    else:
        out_specs = [pl.BlockSpec((TM, D_MODEL), lambda i, pos: (jnp.minimum(i, PROMPT_TILES - 1), 0)),
                     pl.BlockSpec((DEC_BATCH, D_MODEL), lambda i, pos: (0, 0))]
        out_shape = [jax.ShapeDtypeStruct((NP_ROWS, D_MODEL), F32),
                     jax.ShapeDtypeStruct((DEC_BATCH, D_MODEL), F32)]
    grid_spec = pltpu.PrefetchScalarGridSpec(
        num_scalar_prefetch=1,
        grid=(N_TILES,),
        in_specs=in_specs,
        out_specs=out_specs,
        scratch_shapes=scratch,
    )
    return pl.pallas_call(
        functools.partial(_res_kernel, with_norm=with_norm),
        grid_spec=grid_spec,
        out_shape=out_shape,
        compiler_params=_cparams(("arbitrary",)),
        name="residual_norm" if with_norm else "residual",
    )(*args)


def kernel(x_prompt, x_sample, c_prompt, c_sample, cache_swa_k, cache_swa_v, state_ret, state_hgrn,
           w_ada, b_ada, ln1_g, w_in, qn_g, kn_g, attn_sinks, ret_gn_g, ret_gn_b, hgrn_lb, hgrn_ng,
           w_out, ln2_g, w_rg, b_rg, w_re, b_re, w1, w3, w2):
    i32 = jnp.int32
    lbp = jax.nn.softmax(hgrn_lb.astype(F32), axis=0)
    lb_all = jnp.cumsum(lbp, axis=0) - lbp[0]

    assert TM // SROWS == DEC_BATCH and NS_ROWS == TM
    c_all = jnp.concatenate([c_prompt, jnp.zeros((C_SAMPLE0 - BATCH, D_MODEL), F32), c_sample], axis=0)
    mod = _ada(c_all, w_ada, b_ada)

    ret_tab_p = _ret_tables(RET_CHUNK, RET_CHUNK, 0.0, SEQ)
    ret_tab_s = _ret_tables(SROWS, 1, float(PAST_LEN), SROWS)
    w2g = w2.reshape(DEPTH, N_GROUPS, EXP_PER_GROUP * D_EXPERT, D_MODEL)
    xs_buf = jnp.zeros((MOE_ROWS, XW), F32)

    x, h = _norm_mod_first(x_prompt.reshape(NP_ROWS, D_MODEL), x_sample.reshape(DEC_BATCH, D_MODEL),
                           ln1_g[0], mod, 0)

    outs = {k: [] for k in ("kp", "vp", "rp", "hp", "ks", "vs", "rs", "hs")}
    for l in range(DEPTH):
        proj = _mm_in(h, w_in, l)

        oa, kn = _swa(proj, proj, proj, attn_sinks[l], qn_g[l], kn_g[l], tq=WINDOW, nb=1, n_seq=BATCH,
                      blocks_per_seq=SEQ // WINDOW, row_block0=0, prev_raw=True)
        ck = cache_swa_k[l].reshape(DEC_BATCH * WINDOW, A_KV * A_HD)
        cv = cache_swa_v[l].reshape(DEC_BATCH * WINDOW, A_KV * A_HD)
        srows = SAMPLE_NB * SROWS
        oa, kn = _swa(proj, ck, cv, attn_sinks[l], qn_g[l], kn_g[l], tq=SROWS, nb=SAMPLE_NB,
                      n_seq=DEC_BATCH, blocks_per_seq=1, row_block0=NP_ROWS // srows, prev_raw=False,
                      alias_bufs=(oa, kn))
        zero_ret = jnp.zeros((BATCH, B_HEADS, B_DK, B_DV), F32)
        ob, rp = _ret(proj, zero_ret, ret_gn_g[l], ret_gn_b[l], ret_tab_p, chunk=RET_CHUNK, nb=1,
                      n_seq=BATCH, chunks_per_seq=SEQ // RET_CHUNK, row_block0=0, table_per_chunk=True)
        ob, rs = _ret(proj, state_ret[l], ret_gn_g[l], ret_gn_b[l], ret_tab_s, chunk=SROWS, nb=SAMPLE_NB,
                      n_seq=DEC_BATCH, chunks_per_seq=1, row_block0=NP_ROWS // srows,
                      table_per_chunk=False, alias_buf=ob)
        zero_hg = jnp.zeros((BATCH, C_HEADS * C_DV, C_DK), F32)
        oc, hp_t = _hgrn(proj, zero_hg, lb_all[l], hgrn_ng[l], rows=HG_TILE, seq_rows=HG_TILE,
                         tv=HG_TILE, n_seq=BATCH, tiles_per_seq=SEQ // HG_TILE, row_block0=0)
        st_s = jnp.swapaxes(state_hgrn[l], -1, -2).reshape(DEC_BATCH, C_HEADS * C_DV, C_DK)
        oc, hs_t = _hgrn(proj, st_s, lb_all[l], hgrn_ng[l], rows=srows, seq_rows=SROWS, tv=1,
                         n_seq=DEC_BATCH, tiles_per_seq=1, row_block0=NP_ROWS // srows, alias_buf=oc)

        x1 = _mm_out(oa, ob, oc, w_out, x, mod, l)

        wr = jnp.concatenate([w_rg[l], w_re[l], jnp.zeros((D_MODEL, 128 - N_GROUPS - N_EXPERTS), F32)], axis=1)
        br = jnp.concatenate([b_rg[l], b_re[l], jnp.zeros((128 - N_GROUPS - N_EXPERTS,), F32)]).reshape(1, 128)
        hx, meta, cnt = _route(x1, ln2_g[l], mod, wr, br, l)
        pos_all, plan = _moe_positions(meta, cnt)
        xs_buf = _moe_scatter(hx, xs_buf, pos_all)
        hid = _moe_hidden(xs_buf, w1, w3, plan, l)
        y_sorted = _moe_down(hid, w2g, plan, l)
        if l + 1 < DEPTH:
            x, h = _residual(x1, y_sorted, pos_all, mod, l, ln1_g[l + 1])
        else:
            y_prompt, y_sample = _residual(x1, y_sorted, pos_all, mod, l)

        knp = kn[:NP_ROWS].reshape(BATCH, SEQ, A_KV, A_HD)
        vpp = proj[:NP_ROWS, 1280:1536].reshape(BATCH, SEQ, A_KV, A_HD)
        outs["kp"].append(knp[:, SEQ - WINDOW:])
        outs["vp"].append(vpp[:, SEQ - WINDOW:])
        outs["rp"].append(rp)
        outs["hp"].append(jnp.swapaxes(hp_t.reshape(BATCH, C_HEADS, C_DV, C_DK), -1, -2))
        k_new = kn[NP_ROWS::SROWS].reshape(DEC_BATCH, 1, A_KV, A_HD)
        v_new = proj[NP_ROWS::SROWS, 1280:1536].reshape(DEC_BATCH, 1, A_KV, A_HD)
        outs["ks"].append(jnp.concatenate([cache_swa_k[l][:, 1:], k_new], axis=1))
        outs["vs"].append(jnp.concatenate([cache_swa_v[l][:, 1:], v_new], axis=1))
        outs["rs"].append(rs)
        outs["hs"].append(jnp.swapaxes(hs_t.reshape(DEC_BATCH, C_HEADS, C_DV, C_DK), -1, -2))

    y_prompt = y_prompt.reshape(BATCH, SEQ, D_MODEL)
    y_sample = y_sample.reshape(DEC_BATCH, 1, D_MODEL)
    st = lambda k: jnp.stack(outs[k])
    return (y_prompt, y_sample, st("kp"), st("vp"), st("rp"), st("hp"),
            st("ks"), st("vs"), st("rs"), st("hs"))
```

```python
import functools

import numpy as np
import jax
import jax.numpy as jnp
from jax import lax
from jax.experimental import pallas as pl
from jax.experimental.pallas import tpu as pltpu

F32 = jnp.float32
BF16 = jnp.bfloat16

D_MODEL = 2048
BATCH = 2
SEQ = 4096
DEPTH = 4
DEC_BATCH = 32
PAST_LEN = 16384
A_HD = 128
A_HEADS = 8
A_KV = 2
A_GROUP = 4
WINDOW = 128
B_DV = 128
B_DK = 64
B_HEADS = 4
C_DK = 128
C_DV = 128
C_HEADS = 4
ROPE_BASE = 10000.0
N_GROUPS = 4
EXP_PER_GROUP = 4
N_EXPERTS = 16
D_EXPERT = 512
EPS = 1e-6
NEG_BIG = -1e30
IN_WIDTH = 5120

SROWS = 16
NP_ROWS = BATCH * SEQ
NS_ROWS = DEC_BATCH * SROWS
N_ROWS = NP_ROWS + NS_ROWS
TM = 512
N_TILES = N_ROWS // TM
MOD_ROWS = 16
N_TOK = NP_ROWS + DEC_BATCH
MOE_TILES = (N_TOK + N_GROUPS * (TM - 1)) // TM
MOE_ROWS = MOE_TILES * TM
HID_WORK = MOE_TILES * EXP_PER_GROUP
DOWN_TN = 1024
DOWN_NCOL = D_MODEL // DOWN_TN
DOWN_WORK = MOE_TILES * DOWN_NCOL

RET_CHUNK = 256
HG_TILE = 256
HG_BLK = 16
SAMPLE_NB = 8

VMEM_LIMIT = 48 * 1024 * 1024


def _cparams(sem):
    return pltpu.CompilerParams(dimension_semantics=sem, vmem_limit_bytes=VMEM_LIMIT)


def _sigmoid(x):
    return 1.0 / (1.0 + jnp.exp(-x))


def _silu(x):
    return x * _sigmoid(x)


C_ROWS = 40
C_SAMPLE0 = 8
PROMPT_TILES = NP_ROWS // TM
MOD_COL = dict(sh1=0, sc1=1, g1=2, sh2=3, sc2=4, g2=5)


def _tile_mod(m_ref, i):
    n = m_ref.shape[-1]
    seq = jnp.minimum(i // (SEQ // TM), BATCH - 1)
    prow = jnp.broadcast_to(m_ref[pl.ds(seq, 1), :], (DEC_BATCH, n))
    srows = m_ref[C_SAMPLE0:C_SAMPLE0 + DEC_BATCH, :]
    m = jnp.where(i < PROMPT_TILES, prow, srows)
    return jnp.broadcast_to(m[:, None, :], (DEC_BATCH, SROWS, n)).reshape(TM, n)


N_SLABS = TM // SROWS


def _stage_mod(m_ref, stage_ref, i):
    n = m_ref.shape[-1]
    seq = jnp.minimum(i // (SEQ // TM), BATCH - 1)
    prow = jnp.broadcast_to(m_ref[pl.ds(seq, 1), :], (N_SLABS, n))
    srows = m_ref[C_SAMPLE0:C_SAMPLE0 + DEC_BATCH, :]
    stage_ref[...] = jnp.where(i < PROMPT_TILES, prow, srows)


def _slab(s):
    return slice(s * SROWS, (s + 1) * SROWS)


def _mod_spec(l, name, tn=D_MODEL, colmap=None):
    base = MOD_COL[name] * (D_MODEL // tn)
    if colmap is None:
        return pl.BlockSpec((None, C_ROWS, tn), lambda *ids: (l, 0, base))
    return pl.BlockSpec((None, C_ROWS, tn), lambda *ids: (l, 0, base + colmap(*ids)))


def _ada_kernel(c_ref, w_ref, b_ref, o_ref):
    c = _silu(c_ref[...])
    o_ref[...] = jnp.dot(c.astype(BF16), w_ref[...].astype(BF16),
                         preferred_element_type=F32) + b_ref[...]


def _ada(c_all, w_ada, b_ada):
    rows = c_all.shape[0]
    tn = 1024
    ncol = w_ada.shape[2] // tn
    return pl.pallas_call(
        _ada_kernel,
        grid=(DEPTH, ncol),
        in_specs=[
            pl.BlockSpec((rows, D_MODEL), lambda l, j: (0, 0)),
            pl.BlockSpec((None, D_MODEL, tn), lambda l, j: (l, 0, j)),
            pl.BlockSpec((None, 1, tn), lambda l, j: (l, 0, j)),
        ],
        out_specs=pl.BlockSpec((None, rows, tn), lambda l, j: (l, 0, j)),
        out_shape=jax.ShapeDtypeStruct((DEPTH, rows, w_ada.shape[2]), F32),
        compiler_params=_cparams(("arbitrary", "arbitrary")),
        name="ada",
    )(c_all, w_ada, b_ada.reshape(DEPTH, 1, -1))


def _norm_mod_rows(x, g, sc_ref, sh_ref, i):
    y = x * lax.rsqrt(jnp.mean(x * x, axis=-1, keepdims=True) + EPS) * g
    return y * (1.0 + _tile_mod(sc_ref, i)) + _tile_mod(sh_ref, i)


def _norm_mod_slab(x, g, sc_row, sh_row):
    y = x * lax.rsqrt(jnp.mean(x * x, axis=-1, keepdims=True) + EPS) * g
    return y * (1.0 + sc_row) + sh_row


def _norm_kernel(xp_ref, xs_ref, g_ref, sc_ref, sh_ref, x_ref, o_ref):
    i = pl.program_id(0)
    xs = xs_ref[...]
    rid = lax.broadcasted_iota(jnp.int32, (DEC_BATCH, SROWS, 1), 1)
    blk = jnp.where(rid == 0, jnp.broadcast_to(xs[:, None, :], (DEC_BATCH, SROWS, D_MODEL)), 0.0)
    x = jnp.where(i < PROMPT_TILES, xp_ref[...], blk.reshape(TM, D_MODEL))
    x_ref[...] = x
    o_ref[...] = _norm_mod_rows(x, g_ref[...], sc_ref, sh_ref, i).astype(o_ref.dtype)


def _norm_mod_first(x_prompt, x_sample, g, mod, l):
    return pl.pallas_call(
        _norm_kernel,
        grid=(N_TILES,),
        in_specs=[
            pl.BlockSpec((TM, D_MODEL), lambda i: (jnp.minimum(i, PROMPT_TILES - 1), 0)),
            pl.BlockSpec((DEC_BATCH, D_MODEL), lambda i: (0, 0)),
            pl.BlockSpec((1, D_MODEL), lambda i: (0, 0)),
            _mod_spec(l, "sc1"),
            _mod_spec(l, "sh1"),
        ],
        out_specs=[
            pl.BlockSpec((TM, D_MODEL), lambda i: (i, 0)),
            pl.BlockSpec((TM, D_MODEL), lambda i: (i, 0)),
        ],
        out_shape=[
            jax.ShapeDtypeStruct((N_ROWS, D_MODEL), F32),
            jax.ShapeDtypeStruct((N_ROWS, D_MODEL), BF16),
        ],
        compiler_params=_cparams(("arbitrary",)),
        name="norm_first",
    )(x_prompt, x_sample, g.reshape(1, D_MODEL), mod, mod)


def _mm_in_kernel(x_ref, w_ref, o_ref, wb_ref):
    @pl.when(pl.program_id(1) == 0)
    def _():
        wb_ref[...] = w_ref[...].astype(BF16)

    o_ref[...] = jnp.dot(x_ref[...], wb_ref[...], preferred_element_type=F32)


MM_IN_TILES = 8


def _mm_in(h, w_in, l):
    tn = 1024
    ncol = IN_WIDTH // tn
    tm = N_ROWS // MM_IN_TILES
    return pl.pallas_call(
        _mm_in_kernel,
        grid=(ncol, MM_IN_TILES),
        in_specs=[
            pl.BlockSpec((tm, D_MODEL), lambda j, i: (i, 0)),
            pl.BlockSpec((None, D_MODEL, tn), lambda j, i: (l, 0, j)),
        ],
        out_specs=pl.BlockSpec((tm, tn), lambda j, i: (i, j)),
        out_shape=jax.ShapeDtypeStruct((N_ROWS, IN_WIDTH), F32),
        scratch_shapes=[pltpu.VMEM((D_MODEL, tn), BF16)],
        compiler_params=_cparams(("arbitrary", "arbitrary")),
        name="mm_in",
    )(h, w_in)


def _rms(x, g):
    return x * lax.rsqrt(jnp.mean(x * x, axis=-1, keepdims=True) + EPS) * g


def _swa_kernel(sink_ref, q_ref, k_ref, v_ref, pk_ref, pv_ref, qg_ref, kg_ref,
                *rest, tq, nb, prev_raw):
    o_ref, kn_ref = rest[-2], rest[-1]
    if prev_raw:
        has_prev = pl.program_id(1) > 0
    else:
        has_prev = True
    qg = qg_ref[...]
    kg = kg_ref[...]
    nk = WINDOW + tq
    nq = A_GROUP * tq
    row = lax.broadcasted_iota(jnp.int32, (nq, nk), 0) & (tq - 1)
    col = lax.broadcasted_iota(jnp.int32, (nq, nk), 1)
    vis_prev = (col < WINDOW) & (col >= row) & has_prev
    vis_own = (col >= WINDOW) & ((col - WINDOW) <= row)
    mask = vis_prev | vis_own
    rgrp = lax.broadcasted_iota(jnp.int32, (nq, 1), 0) // tq
    for sb in range(nb):
        rs = slice(sb * tq, (sb + 1) * tq)
        ps = slice(sb * WINDOW, (sb + 1) * WINDOW)
        for kv in range(A_KV):
            ks = slice(kv * A_HD, (kv + 1) * A_HD)
            kn = _rms(k_ref[rs, ks], kg)
            kn_ref[rs, ks] = kn
            pk = pk_ref[ps, ks]
            if prev_raw:
                pk = _rms(pk, kg)
            keys = jnp.concatenate([pk, kn], axis=0).astype(BF16)
            vals = jnp.concatenate([pv_ref[ps, ks], v_ref[rs, ks]], axis=0).astype(BF16)
            h0 = kv * A_GROUP
            qs = jnp.concatenate(
                [_rms(q_ref[rs, (h0 + g) * A_HD:(h0 + g + 1) * A_HD], qg) for g in range(A_GROUP)],
                axis=0).astype(BF16)
            s = lax.dot_general(qs, keys, (((1,), (1,)), ((), ())),
                                preferred_element_type=F32) * (A_HD ** -0.5)
            s = jnp.where(mask, s, NEG_BIG)
            snk = jnp.full((nq, 1), sink_ref[h0 + A_GROUP - 1], F32)
            for g in range(A_GROUP - 2, -1, -1):
                snk = jnp.where(rgrp == g, sink_ref[h0 + g], snk)
            m = jnp.maximum(jnp.max(s, axis=-1, keepdims=True), snk)
            p = jnp.exp(s - m)
            den = jnp.sum(p, axis=-1, keepdims=True) + jnp.exp(snk - m)
            o = jnp.dot(p.astype(BF16), vals, preferred_element_type=F32) / den
            for g in range(A_GROUP):
                o_ref[rs, (h0 + g) * A_HD:(h0 + g + 1) * A_HD] = o[g * tq:(g + 1) * tq].astype(o_ref.dtype)


def _swa(proj, prev_k, prev_v, sinks, qn_g, kn_g, *, tq, nb, n_seq, blocks_per_seq, row_block0,
         prev_raw, alias_bufs=None):
    qcol, kcol, vcol = 0, 1024 // 256, 1280 // 256
    assert nb == 1 or blocks_per_seq == 1

    def rb(s, n):
        return row_block0 + s * blocks_per_seq + n

    if prev_raw:
        def prev_map_k(s, n):
            return (rb(s, jnp.maximum(n - 1, 0)), kcol)

        def prev_map_v(s, n):
            return (rb(s, jnp.maximum(n - 1, 0)), vcol)
    else:
        def prev_map_k(s, n):
            return (s, 0)
        prev_map_v = prev_map_k

    rows = tq * nb
    in_specs = [
        pl.BlockSpec(memory_space=pltpu.SMEM),
        pl.BlockSpec((rows, 1024), lambda s, n: (rb(s, n), qcol)),
        pl.BlockSpec((rows, 256), lambda s, n: (rb(s, n), kcol)),
        pl.BlockSpec((rows, 256), lambda s, n: (rb(s, n), vcol)),
        pl.BlockSpec((WINDOW * nb, 256), prev_map_k),
        pl.BlockSpec((WINDOW * nb, 256), prev_map_v),
        pl.BlockSpec((1, A_HD), lambda s, n: (0, 0)),
        pl.BlockSpec((1, A_HD), lambda s, n: (0, 0)),
    ]
    args = [sinks, proj, proj, proj, prev_k, prev_v, qn_g.reshape(1, A_HD), kn_g.reshape(1, A_HD)]
    aliases = {}
    if alias_bufs is not None:
        for j, buf in enumerate(alias_bufs):
            in_specs.append(pl.BlockSpec(memory_space=pl.ANY))
            aliases[len(args)] = j
            args.append(buf)
    return pl.pallas_call(
        functools.partial(_swa_kernel, tq=tq, nb=nb, prev_raw=prev_raw),
        grid=(n_seq // nb, blocks_per_seq),
        in_specs=in_specs,
        out_specs=[
            pl.BlockSpec((rows, 1024), lambda s, n: (rb(s, n), 0)),
            pl.BlockSpec((rows, 256), lambda s, n: (rb(s, n), 0)),
        ],
        out_shape=[
            jax.ShapeDtypeStruct((N_ROWS, 1024), BF16),
            jax.ShapeDtypeStruct((N_ROWS, 256), F32),
        ],
        input_output_aliases=aliases,
        compiler_params=_cparams(("arbitrary", "arbitrary")),
        name="swa_prompt" if prev_raw else "swa_sample",
    )(*args)


def _ret_tables(chunk, tv, pos0, t_len):
    lg = np.log1p(-(2.0 ** (-5.0 - np.arange(B_HEADS, dtype=np.float64))))
    idx = np.arange(chunk, dtype=np.float64)
    valid = idx < tv
    diff = idx[:, None] - idx[None, :]
    dmat = np.where((diff >= 0) & valid[:, None] & valid[None, :],
                    np.exp(np.maximum(diff, 0.0)[None] * lg[:, None, None]), 0.0)
    din = np.exp((idx + 1.0)[:, None] * lg[None])
    din = np.repeat(din, B_DV, axis=1)
    kfac = np.where(valid[:, None], np.exp((tv - 1.0 - idx)[:, None] * lg[None]), 0.0)
    kfac = np.repeat(kfac, B_DK, axis=1)
    gc = np.repeat(np.exp(tv * lg), B_DK)[:, None] * np.ones((1, B_DV))
    half = B_DK // 2
    inv = ROPE_BASE ** (-np.arange(half, dtype=np.float64) / half)
    ang = (pos0 + np.arange(t_len, dtype=np.float64))[:, None] * inv[None]
    cos = np.tile(np.cos(ang), (1, 2 * B_HEADS))
    sin = np.tile(np.concatenate([-np.sin(ang), np.sin(ang)], axis=1), (1, B_HEADS))
    f = lambda a: jnp.asarray(a, dtype=F32)
    return f(dmat), f(din), f(kfac), f(gc), f(cos), f(sin)


def _ret_kernel(q_ref, k_ref, v_ref, g_ref, cos_ref, sin_ref, dmat_ref, din_ref, kfac_ref,
                gc_ref, gng_ref, gnb_ref, s0_ref, *rest, nb, chunk):
    o_ref, s_ref = rest[-2], rest[-1]

    @pl.when(pl.program_id(1) == 0)
    def _():
        s_ref[...] = s0_ref[...]

    cos = cos_ref[...]
    sin = sin_ref[...]
    width = B_HEADS * B_DK
    lane = lax.broadcasted_iota(jnp.int32, (1, width), 1)
    first_half = (lane % B_DK) < (B_DK // 2)
    lane_head = lane // B_DK

    def rope(x):
        partner = jnp.where(first_half, pltpu.roll(x, width - B_DK // 2, 1),
                            pltpu.roll(x, B_DK // 2, 1))
        return x * cos + partner * sin

    gc = gc_ref[...]
    for sb in range(nb):
        rws = slice(sb * chunk, (sb + 1) * chunk)
        qr = rope(q_ref[rws, :])
        kr = rope(k_ref[rws, :]) * (B_DK ** -0.5)
        kb = kr.astype(BF16)
        kdec = (kr * kfac_ref[...]).astype(BF16)
        s_old = s_ref[sb]
        s_b = s_old.astype(BF16)
        for h in range(B_HEADS):
            vs = slice(h * B_DV, (h + 1) * B_DV)
            rs = slice(h * B_DK, (h + 1) * B_DK)
            qm = jnp.where(lane_head == h, qr, 0.0).astype(BF16)
            a = lax.dot_general(qm, kb, (((1,), (1,)), ((), ())),
                                preferred_element_type=F32) * dmat_ref[h]
            vh = v_ref[rws, vs].astype(BF16)
            o = jnp.dot(a.astype(BF16), vh, preferred_element_type=F32)
            o = o + jnp.dot(qm, s_b, preferred_element_type=F32) * din_ref[:, vs]
            u = lax.dot_general(kdec, vh, (((0,), (0,)), ((), ())), preferred_element_type=F32)
            s_ref[sb, rs, :] = gc[rs, :] * s_old[rs, :] + u[rs, :]
            mu = jnp.mean(o, axis=-1, keepdims=True)
            oc = o - mu
            var = jnp.mean(oc * oc, axis=-1, keepdims=True)
            y = oc * lax.rsqrt(var + EPS) * gng_ref[:, vs] + gnb_ref[:, vs]
            y = y * _silu(g_ref[rws, vs])
            o_ref[rws, vs] = y.astype(o_ref.dtype)


def _ret(proj, s0, gn_g, gn_b, tables, *, chunk, nb, n_seq, chunks_per_seq, row_block0,
         table_per_chunk, alias_buf=None):
    dmat, din, kfac, gc, cos, sin = tables
    qcol, kcol, vcol, gcol = 1536 // 256, 1792 // 256, 2048 // 512, 2560 // 512
    assert nb == 1 or chunks_per_seq == 1
    rows = nb * chunk

    def rb(s, c):
        return row_block0 + s * chunks_per_seq + c

    tmap = (lambda s, c: (c, 0)) if table_per_chunk else (lambda s, c: (0, 0))
    const2 = lambda s, c: (0, 0)
    in_specs = [
        pl.BlockSpec((rows, 256), lambda s, c: (rb(s, c), qcol)),
        pl.BlockSpec((rows, 256), lambda s, c: (rb(s, c), kcol)),
        pl.BlockSpec((rows, 512), lambda s, c: (rb(s, c), vcol)),
        pl.BlockSpec((rows, 512), lambda s, c: (rb(s, c), gcol)),
        pl.BlockSpec((chunk, 256), tmap),
        pl.BlockSpec((chunk, 256), tmap),
        pl.BlockSpec((B_HEADS, chunk, chunk), lambda s, c: (0, 0, 0)),
        pl.BlockSpec((chunk, 512), const2),
        pl.BlockSpec((chunk, 256), const2),
        pl.BlockSpec((256, B_DV), const2),
        pl.BlockSpec((1, 512), const2),
        pl.BlockSpec((1, 512), const2),
        pl.BlockSpec((nb, 256, B_DV), lambda s, c: (s, 0, 0)),
    ]
    args = [proj, proj, proj, proj, cos, sin, dmat, din, kfac, gc,
            gn_g.reshape(1, 512), gn_b.reshape(1, 512), s0.reshape(n_seq, 256, B_DV)]
    aliases = {}
    if alias_buf is not None:
        in_specs.append(pl.BlockSpec(memory_space=pl.ANY))
        aliases[len(args)] = 0
        args.append(alias_buf)
    o, s_new = pl.pallas_call(
        functools.partial(_ret_kernel, nb=nb, chunk=chunk),
        grid=(n_seq // nb, chunks_per_seq),
        in_specs=in_specs,
        out_specs=[
            pl.BlockSpec((rows, 512), lambda s, c: (rb(s, c), 0)),
            pl.BlockSpec((nb, 256, B_DV), lambda s, c: (s, 0, 0)),
        ],
        out_shape=[
            jax.ShapeDtypeStruct((N_ROWS, 512), BF16),
            jax.ShapeDtypeStruct((n_seq, 256, B_DV), F32),
        ],
        input_output_aliases=aliases,
        compiler_params=_cparams(("arbitrary", "arbitrary")),
        name="ret_prompt" if alias_buf is None else "ret_sample",
    )(*args)
    return o, s_new.reshape(n_seq, B_HEADS, B_DK, B_DV)


HG_FAST = 32
HG_LIM = 60.0
HG_HDR = 32


def _hgrn_kernel(q_ref, f_ref, i_ref, g_ref, lb_ref, ng_ref, st0_ref, *rest, rows, seq_rows, tv):
    use_fast = seq_rows % HG_FAST == 0
    if use_fast:
        o_ref, st_ref, q_s, k_s, cs_s, qd_s, ki_s, kd_s, eb_s, stb_s = rest[-10:]
    else:
        o_ref, st_ref, q_s, k_s, cs_s = rest[-5:]
    nseq = rows // seq_rows

    @pl.when(pl.program_id(1) == 0)
    def _():
        st_ref[...] = st0_ref[...]

    z = f_ref[...]
    lb = lb_ref[...]
    sig = _sigmoid(z)
    lf = jnp.log(lb + (1.0 - lb) * sig)
    k = (1.0 - lb) * _sigmoid(-z)
    if tv < seq_rows:
        rvalid = (lax.broadcasted_iota(jnp.int32, (rows, 1), 0) % seq_rows) < tv
        lf = jnp.where(rvalid, lf, 0.0)
        k = jnp.where(rvalid, k, 0.0)
    ri = lax.broadcasted_iota(jnp.int32, (rows, rows), 0)
    ci = lax.broadcasted_iota(jnp.int32, (rows, rows), 1)
    tri = (ci <= ri).astype(F32)
    cs = jnp.dot(tri, lf, precision=lax.Precision.HIGHEST, preferred_element_type=F32)
    q_s[...] = _silu(q_ref[...])
    k_s[...] = k
    cs_s[0:HG_HDR, :] = jnp.zeros((HG_HDR, C_HEADS * C_DK), F32)
    cs_s[HG_HDR:HG_HDR + rows, :] = cs
    ng = ng_ref[...]

    nt_dims = (((1,), (1,)), ((), ()))
    tn_dims = (((0,), (0,)), ((), ()))

    def finish(o, rsl, hs):
        y = o * lax.rsqrt(jnp.mean(o * o, axis=-1, keepdims=True) + EPS) * ng
        return (y * _silu(g_ref[rsl, hs])).astype(o_ref.dtype)

    def pairwise_block(blk, carry):
        size = HG_BLK
        r0 = pl.multiple_of(blk * size, size)
        rsl = pl.ds(r0, size)
        sidx = 0 if nseq == 1 else blk // (seq_rows // size)
        rowi = lax.broadcasted_iota(jnp.int32, (size, 1), 0)
        for h in range(C_HEADS):
            hs = slice(h * C_DK, (h + 1) * C_DK)
            vsl = slice(h * C_DV, (h + 1) * C_DV)
            qb = q_s[rsl, hs]
            kb = k_s[rsl, hs]
            vb = i_ref[rsl, hs]
            prev = cs_s[pl.ds(HG_HDR + r0 - 8, 8), hs][7:8, :]
            bc = cs_s[pl.ds(HG_HDR + r0, size), hs] - prev
            blast = bc[size - 1:size, :]
            qd = (qb * jnp.exp(bc)).astype(BF16)
            kd = (kb * jnp.exp(blast - bc)).astype(BF16)
            st = st_ref[sidx, vsl, :]
            o = lax.dot_general(qd, st.astype(BF16), nt_dims, preferred_element_type=F32)
            for j in range(size):
                dec = jnp.exp(jnp.minimum(bc - bc[j:j + 1, :], 0.0))
                a = jnp.sum(qb * kb[j:j + 1, :] * dec, axis=-1, keepdims=True)
                a = jnp.where(rowi >= j, a, 0.0)
                o = o + a * vb[j:j + 1, :]
            u = lax.dot_general(vb.astype(BF16), kd, tn_dims, preferred_element_type=F32)
            st_ref[sidx, vsl, :] = st * jnp.exp(blast) + u
            o_ref[rsl, vsl] = finish(o, rsl, hs)
        return carry

    def factorised_tile():
        nblk = rows // HG_FAST
        for b in range(nblk):
            r = slice(b * HG_FAST, (b + 1) * HG_FAST)
            lo = HG_HDR + b * HG_FAST
            prev = cs_s[lo - 8:lo, :][7:8, :]
            bc = cs_s[lo:lo + HG_FAST, :] - prev
            blast = bc[HG_FAST - 1:HG_FAST, :]
            qb = q_s[r, :]
            kb = k_s[r, :]
            qd_s[r, :] = (qb * jnp.exp(bc)).astype(BF16)
            ki_s[r, :] = (kb * jnp.exp(-bc)).astype(BF16)
            kd_s[r, :] = (kb * jnp.exp(blast - bc)).astype(BF16)
            eb_s[b:b + 1, :] = jnp.exp(blast)
        same = (ri // HG_FAST) == (ci // HG_FAST)
        keep = same & (ci <= ri)
        for h in range(C_HEADS):
            hs = slice(h * C_DK, (h + 1) * C_DK)
            vsl = slice(h * C_DV, (h + 1) * C_DV)
            vh = i_ref[:, hs].astype(BF16)
            st = st_ref[0, vsl, :]
            for b in range(nblk):
                r = slice(b * HG_FAST, (b + 1) * HG_FAST)
                stb_s[h, b] = st.astype(BF16)
                u = lax.dot_general(vh[r, :], kd_s[r, hs], tn_dims, preferred_element_type=F32)
                st = st * eb_s[b:b + 1, hs] + u
            st_ref[0, vsl, :] = st
            a = lax.dot_general(qd_s[:, hs], ki_s[:, hs], nt_dims, preferred_element_type=F32)
            a = jnp.where(keep, a, 0.0)
            o = jnp.dot(a.astype(BF16), vh, preferred_element_type=F32)
            inter = [lax.dot_general(qd_s[b * HG_FAST:(b + 1) * HG_FAST, hs], stb_s[h, b], nt_dims,
                                     preferred_element_type=F32) for b in range(nblk)]
            o = o + jnp.concatenate(inter, axis=0)
            o_ref[:, vsl] = finish(o, slice(None), hs)

    def pairwise_tile():
        lax.fori_loop(0, rows // HG_BLK, pairwise_block, 0)

    if use_fast:
        worst = jnp.max(cs_s[HG_HDR - HG_FAST:HG_HDR - HG_FAST + rows, :] - cs)
        bounded = worst < HG_LIM
        pl.when(bounded)(factorised_tile)
        pl.when(jnp.logical_not(bounded))(pairwise_tile)
    else:
        pairwise_tile()


def _hgrn(proj, st0, lb, ng, *, rows, seq_rows, tv, n_seq, tiles_per_seq, row_block0, alias_buf=None):
    qcol, fcol, icol, gcol = 3072 // 512, 3584 // 512, 4096 // 512, 4608 // 512
    nst = max(rows // seq_rows, 1)
    assert nst == 1 or tiles_per_seq == 1
    width = C_HEADS * C_DK
    scratch = [pltpu.VMEM((rows, width), F32), pltpu.VMEM((rows, width), F32),
               pltpu.VMEM((HG_HDR + rows, width), F32)]
    if seq_rows % HG_FAST == 0:
        scratch += [pltpu.VMEM((rows, width), BF16) for _ in range(3)]
        scratch += [pltpu.VMEM((rows // HG_FAST, width), F32),
                    pltpu.VMEM((C_HEADS, rows // HG_FAST, C_DV, C_DK), BF16)]

    def rb(s, c):
        return row_block0 + s * tiles_per_seq + c

    const2 = lambda s, c: (0, 0)
    in_specs = [
        pl.BlockSpec((rows, 512), lambda s, c: (rb(s, c), qcol)),
        pl.BlockSpec((rows, 512), lambda s, c: (rb(s, c), fcol)),
        pl.BlockSpec((rows, 512), lambda s, c: (rb(s, c), icol)),
        pl.BlockSpec((rows, 512), lambda s, c: (rb(s, c), gcol)),
        pl.BlockSpec((1, 512), const2),
        pl.BlockSpec((1, C_DV), const2),
        pl.BlockSpec((nst, 512, C_DK), lambda s, c: (s, 0, 0)),
    ]
    args = [proj, proj, proj, proj, lb.reshape(1, 512), ng.reshape(1, C_DV), st0]
    aliases = {}
    if alias_buf is not None:
        in_specs.append(pl.BlockSpec(memory_space=pl.ANY))
        aliases[len(args)] = 0
        args.append(alias_buf)
    return pl.pallas_call(
        functools.partial(_hgrn_kernel, rows=rows, seq_rows=seq_rows, tv=tv),
        grid=(n_seq // nst, tiles_per_seq),
        in_specs=in_specs,
        out_specs=[
            pl.BlockSpec((rows, 512), lambda s, c: (rb(s, c), 0)),
            pl.BlockSpec((nst, 512, C_DK), lambda s, c: (s, 0, 0)),
        ],
        out_shape=[
            jax.ShapeDtypeStruct((N_ROWS, 512), BF16),
            jax.ShapeDtypeStruct((n_seq, 512, C_DK), F32),
        ],
        scratch_shapes=scratch,
        input_output_aliases=aliases,
        compiler_params=_cparams(("arbitrary", "arbitrary")),
        name="hgrn_prompt" if alias_buf is None else "hgrn_sample",
    )(*args)


def _mm_out_kernel(oa_ref, ob_ref, oc_ref, w_ref, x_ref, g_ref, o_ref, wb_ref):
    @pl.when(pl.program_id(1) == 0)
    def _():
        wb_ref[...] = w_ref[...].astype(BF16)

    acc = jnp.dot(oa_ref[...], wb_ref[0:1024, :], preferred_element_type=F32)
    acc = acc + jnp.dot(ob_ref[...], wb_ref[1024:1536, :], preferred_element_type=F32)
    acc = acc + jnp.dot(oc_ref[...], wb_ref[1536:2048, :], preferred_element_type=F32)
    o_ref[...] = x_ref[...] + _tile_mod(g_ref, pl.program_id(1)) * acc


def _mm_out(oa, ob, oc, w_out, x, mod, l):
    tn = 1024
    ncol = D_MODEL // tn
    return pl.pallas_call(
        _mm_out_kernel,
        grid=(ncol, N_TILES),
        in_specs=[
            pl.BlockSpec((TM, 1024), lambda j, i: (i, 0)),
            pl.BlockSpec((TM, 512), lambda j, i: (i, 0)),
            pl.BlockSpec((TM, 512), lambda j, i: (i, 0)),
            pl.BlockSpec((None, D_MODEL, tn), lambda j, i: (l, 0, j)),
            pl.BlockSpec((TM, tn), lambda j, i: (i, j)),
            _mod_spec(l, "g1", tn, lambda j, i: j),
        ],
        out_specs=pl.BlockSpec((TM, tn), lambda j, i: (i, j)),
        out_shape=jax.ShapeDtypeStruct((N_ROWS, D_MODEL), F32),
        scratch_shapes=[pltpu.VMEM((D_MODEL, tn), BF16)],
        compiler_params=_cparams(("arbitrary", "arbitrary")),
        name="mm_out",
    )(oa, ob, oc, w_out, x, mod)


META_GSEL = 0
META_RANK = 1
META_GATE = 20
XW = D_MODEL + 128


def _route_kernel(x_ref, g_ref, sc_ref, sh_ref, whl_ref, wh_ref, br_ref, hx_ref, meta_ref, cnt_ref,
                  run_ref, sc_st, sh_st, hi_s, lo_s):
    i = pl.program_id(0)

    @pl.when(i == 0)
    def _():
        run_ref[...] = jnp.zeros_like(run_ref)

    _stage_mod(sc_ref, sc_st, i)
    _stage_mod(sh_ref, sh_st, i)
    lg = g_ref[...]
    for s in range(N_SLABS):
        r = _slab(s)
        h = _norm_mod_slab(x_ref[r, :], lg, sc_st[s:s + 1, :], sh_st[s:s + 1, :])
        hx_ref[r, 0:D_MODEL] = h
        hi = h.astype(BF16)
        hi_s[r, :] = hi
        lo_s[r, :] = (h - hi.astype(F32)).astype(BF16)
    l1 = jnp.dot(hi_s[...], whl_ref[...], preferred_element_type=F32)
    l2 = jnp.dot(lo_s[...], wh_ref[...], preferred_element_type=F32)
    logits = l1[:, 0:128] + l1[:, 128:256] + l2 + br_ref[...]
    col = lax.broadcasted_iota(jnp.int32, logits.shape, 1).astype(F32)
    big = 1e9
    is_g = col < N_GROUPS
    gl = jnp.where(is_g, logits, -jnp.inf)
    gmax = jnp.max(gl, axis=-1, keepdims=True)
    gsel = jnp.min(jnp.where(gl == gmax, col, big), axis=-1, keepdims=True)
    gden = jnp.sum(jnp.where(is_g, jnp.exp(logits - gmax), 0.0), axis=-1, keepdims=True)
    g_w = 1.0 / gden
    e0 = N_GROUPS + EXP_PER_GROUP * gsel
    in_grp = (col >= e0) & (col < e0 + EXP_PER_GROUP)
    el = jnp.where(in_grp, logits, -jnp.inf)
    v1 = jnp.max(el, axis=-1, keepdims=True)
    i1 = jnp.min(jnp.where(el == v1, col, big), axis=-1, keepdims=True)
    el2 = jnp.where(col == i1, -jnp.inf, el)
    v2 = jnp.max(el2, axis=-1, keepdims=True)
    i2 = jnp.min(jnp.where(el2 == v2, col, big), axis=-1, keepdims=True)
    t = jnp.exp(v2 - v1)
    p1 = 1.0 / (1.0 + t)
    a1 = p1 * g_w
    a2 = t * p1 * g_w
    meta = jnp.where(col == META_GATE + (i1 - e0), a1, 0.0)
    meta = meta + jnp.where(col == META_GATE + (i2 - e0), a2, 0.0)
    meta = meta + jnp.where(col == META_GSEL, gsel, 0.0)
    rows = logits.shape[0]
    rowi = lax.broadcasted_iota(jnp.int32, (rows, 1), 0)
    is_tok = (i < PROMPT_TILES) | (rowi % SROWS == 0)
    onehot = jnp.where((col == gsel) & is_tok, 1.0, 0.0)
    ri = lax.broadcasted_iota(jnp.int32, (rows, rows), 0)
    ci = lax.broadcasted_iota(jnp.int32, (rows, rows), 1)
    tri = jnp.where(ci <= ri, 1.0, 0.0).astype(BF16)
    incl = jnp.dot(tri, onehot.astype(BF16), preferred_element_type=F32)
    run = run_ref[...]
    rank = jnp.sum(jnp.where(col == gsel, incl + run - 1.0, 0.0), axis=-1, keepdims=True)
    meta = meta + jnp.where(col == META_RANK, rank, 0.0)
    meta_ref[...] = meta
    hx_ref[:, D_MODEL:XW] = meta
    run = run + incl[rows - 1:rows, :]
    run_ref[...] = run
    cnt_ref[...] = run


def _router_weights(w_rg, b_rg, w_re, b_re):
    pad = 128 - N_GROUPS - N_EXPERTS
    wr = jnp.concatenate([w_rg, w_re, jnp.zeros((DEPTH, D_MODEL, pad), F32)], axis=2)
    w_hi = wr.astype(BF16)
    w_lo = (wr - w_hi.astype(F32)).astype(BF16)
    br = jnp.concatenate([b_rg, b_re, jnp.zeros((DEPTH, pad), F32)], axis=1).reshape(DEPTH, 1, 128)
    return jnp.concatenate([w_hi, w_lo], axis=2), w_hi, br


def _route(x, g, mod, whl, w_hi, br, l):
    return pl.pallas_call(
        _route_kernel,
        grid=(N_TILES,),
        in_specs=[
            pl.BlockSpec((TM, D_MODEL), lambda i: (i, 0)),
            pl.BlockSpec((1, D_MODEL), lambda i: (0, 0)),
            _mod_spec(l, "sc2"),
            _mod_spec(l, "sh2"),
            pl.BlockSpec((None, D_MODEL, 256), lambda i: (l, 0, 0)),
            pl.BlockSpec((None, D_MODEL, 128), lambda i: (l, 0, 0)),
            pl.BlockSpec((None, 1, 128), lambda i: (l, 0, 0)),
        ],
        out_specs=[
            pl.BlockSpec((TM, XW), lambda i: (i, 0)),
            pl.BlockSpec((TM, 128), lambda i: (i, 0)),
            pl.BlockSpec((1, 128), lambda i: (0, 0)),
        ],
        out_shape=[
            jax.ShapeDtypeStruct((N_ROWS, XW), F32),
            jax.ShapeDtypeStruct((N_ROWS, 128), F32),
            jax.ShapeDtypeStruct((1, 128), F32),
        ],
        scratch_shapes=[pltpu.VMEM((1, 128), F32),
                        pltpu.VMEM((N_SLABS, D_MODEL), F32), pltpu.VMEM((N_SLABS, D_MODEL), F32),
                        pltpu.VMEM((TM, D_MODEL), BF16), pltpu.VMEM((TM, D_MODEL), BF16)],
        compiler_params=_cparams(("arbitrary",)),
        name="route",
    )(x, g.reshape(1, D_MODEL), mod, mod, whl, w_hi, br)


def _moe_positions(meta, cnt):
    i32 = jnp.int32
    counts = cnt[0, :N_GROUPS].astype(i32)
    ntile = (counts + TM - 1) // TM
    tstart = jnp.cumsum(ntile) - ntile
    gsel = meta[:, META_GSEL].astype(i32)
    rank = meta[:, META_RANK].astype(i32)
    first_row = jnp.zeros_like(gsel)
    for g in range(N_GROUPS):
        first_row = jnp.where(gsel == g, tstart[g] * TM, first_row)
    pos = first_row + rank
    pos_all = jnp.concatenate([pos[:NP_ROWS], jnp.repeat(pos[NP_ROWS::SROWS], SROWS)])
    plan = jnp.concatenate([tstart, ntile]).astype(i32)
    return pos_all, plan


def _scatter_kernel(pos_ref, src_ref, buf_hbm, out_hbm, sem):
    del buf_hbm
    i = pl.program_id(0)
    base = i * TM

    def row_copy(r):
        return pltpu.make_async_copy(src_ref.at[pl.ds(r, 1), :],
                                     out_hbm.at[pl.ds(pos_ref[base + r], 1), :], sem)

    def run(n, stride):
        def issue(k, c):
            row_copy(k * stride).start()
            return c

        def drain(k, c):
            row_copy(k * stride).wait()
            return c

        lax.fori_loop(0, n, issue, 0, unroll=8)
        lax.fori_loop(0, n, drain, 0, unroll=8)

    @pl.when(i < PROMPT_TILES)
    def _():
        run(TM, 1)

    @pl.when(i == PROMPT_TILES)
    def _():
        run(DEC_BATCH, SROWS)


def _moe_scatter(hx, buf, pos_all):
    grid_spec = pltpu.PrefetchScalarGridSpec(
        num_scalar_prefetch=1,
        grid=(N_TILES,),
        in_specs=[pl.BlockSpec((TM, XW), lambda i, pos: (i, 0)), pl.BlockSpec(memory_space=pl.ANY)],
        out_specs=pl.BlockSpec(memory_space=pl.ANY),
        scratch_shapes=[pltpu.SemaphoreType.DMA(())],
    )
    return pl.pallas_call(
        _scatter_kernel,
        grid_spec=grid_spec,
        out_shape=jax.ShapeDtypeStruct((MOE_ROWS, XW), F32),
        input_output_aliases={2: 0},
        compiler_params=_cparams(("arbitrary",)),
        name="moe_scatter",
    )(pos_all, hx, buf)


def _work_item(w, plan_ref, per_tile):
    i32 = jnp.int32
    nt = [plan_ref[N_GROUPS + g] for g in range(N_GROUPS)]
    ts = [plan_ref[g] for g in range(N_GROUPS)]
    ends = []
    acc = 0
    for g in range(N_GROUPS):
        acc = acc + per_tile * nt[g]
        ends.append(acc)
    valid = w < ends[-1]
    wc = jnp.maximum(jnp.minimum(w, ends[-1] - 1), 0)
    g = sum((wc >= ends[k]).astype(i32) for k in range(N_GROUPS - 1))

    def pick(vals):
        out = vals[N_GROUPS - 1]
        for k in range(N_GROUPS - 2, -1, -1):
            out = jnp.where(g == k, vals[k], out)
        return out

    r = wc - pick([0] + ends[:-1])
    ntg = pick(nt)
    c = sum((r >= m * ntg).astype(i32) for m in range(1, per_tile))
    t_in = r - c * ntg
    t = pick(ts) + t_in
    extra = jnp.maximum(w - ends[-1], 0)
    used = ts[N_GROUPS - 1] + nt[N_GROUPS - 1]
    t_out = jnp.where(valid, t, used + extra // per_tile)
    c_out = jnp.where(valid, c, extra % per_tile)
    return g, c, t, t_in == 0, valid, t_out, c_out


def _hidden_kernel(plan_ref, x_ref, w1_ref, w3_ref, o_ref, w1b, w3b):
    _, e, _, first, valid, _, _ = _work_item(pl.program_id(0), plan_ref, EXP_PER_GROUP)

    @pl.when(jnp.logical_not(valid))
    def _():
        o_ref[...] = jnp.zeros_like(o_ref)

    @pl.when(valid & first)
    def _():
        w1b[...] = w1_ref[...].astype(BF16)
        w3b[...] = w3_ref[...].astype(BF16)

    @pl.when(valid)
    def _():
        x = x_ref[:, 0:D_MODEL].astype(BF16)
        a = jnp.dot(x, w1b[...], preferred_element_type=F32)
        b = jnp.dot(x, w3b[...], preferred_element_type=F32)
        m = x_ref[:, D_MODEL:XW]
        col = lax.broadcasted_iota(jnp.int32, m.shape, 1)
        gate = jnp.sum(jnp.where(col == META_GATE + e, m, 0.0), axis=-1, keepdims=True)
        o_ref[...] = (_silu(a) * b * gate).astype(o_ref.dtype)


def _moe_hidden(xs, w1, w3, plan, l):
    def item(w, plan):
        return _work_item(w, plan, EXP_PER_GROUP)

    def w_map(w, plan):
        it = item(w, plan)
        return (l, it[0] * EXP_PER_GROUP + it[1], 0, 0)

    grid_spec = pltpu.PrefetchScalarGridSpec(
        num_scalar_prefetch=1,
        grid=(HID_WORK,),
        in_specs=[
            pl.BlockSpec((TM, XW), lambda w, plan: (item(w, plan)[2], 0)),
            pl.BlockSpec((None, None, D_MODEL, D_EXPERT), w_map),
            pl.BlockSpec((None, None, D_MODEL, D_EXPERT), w_map),
        ],
        out_specs=pl.BlockSpec((TM, D_EXPERT), lambda w, plan: item(w, plan)[5:7]),
        scratch_shapes=[pltpu.VMEM((D_MODEL, D_EXPERT), BF16),
                        pltpu.VMEM((D_MODEL, D_EXPERT), BF16)],
    )
    return pl.pallas_call(
        _hidden_kernel,
        grid_spec=grid_spec,
        out_shape=jax.ShapeDtypeStruct((MOE_ROWS, EXP_PER_GROUP * D_EXPERT), BF16),
        compiler_params=_cparams(("arbitrary",)),
        name="moe_hidden",
    )(plan, xs, w1, w3)


def _down_kernel(plan_ref, h_ref, w_ref, o_ref, wb):
    _, _, _, first, valid, _, _ = _work_item(pl.program_id(0), plan_ref, DOWN_NCOL)

    @pl.when(jnp.logical_not(valid))
    def _():
        o_ref[...] = jnp.zeros_like(o_ref)

    @pl.when(valid & first)
    def _():
        wb[...] = w_ref[...].astype(BF16)

    @pl.when(valid)
    def _():
        o_ref[...] = jnp.dot(h_ref[...], wb[...], preferred_element_type=F32)


def _moe_down(hid, w2g, plan, l):
    kdim = EXP_PER_GROUP * D_EXPERT

    def item(w, plan):
        return _work_item(w, plan, DOWN_NCOL)

    grid_spec = pltpu.PrefetchScalarGridSpec(
        num_scalar_prefetch=1,
        grid=(DOWN_WORK,),
        in_specs=[
            pl.BlockSpec((TM, kdim), lambda w, plan: (item(w, plan)[2], 0)),
            pl.BlockSpec((None, None, kdim, DOWN_TN),
                         lambda w, plan: (l, item(w, plan)[0], 0, item(w, plan)[1])),
        ],
        out_specs=pl.BlockSpec((TM, DOWN_TN), lambda w, plan: item(w, plan)[5:7]),
        scratch_shapes=[pltpu.VMEM((kdim, DOWN_TN), BF16)],
    )
    return pl.pallas_call(
        _down_kernel,
        grid_spec=grid_spec,
        out_shape=jax.ShapeDtypeStruct((MOE_ROWS, D_MODEL), F32),
        compiler_params=_cparams(("arbitrary",)),
        name="moe_down",
    )(plan, hid, w2g)


def _res_kernel(pos_ref, x_ref, y_hbm, g2_ref, *rest, with_norm):
    if with_norm:
        lg_ref, sc_ref, sh_ref, xo_ref, ho_ref, ybuf, sem, g2_st, sc_st, sh_st = rest
    else:
        yp_ref, ys_ref, ybuf, sem, g2_st = rest
    i = pl.program_id(0)
    slot = i % 2

    def row_copy(tile, k, sl):
        return pltpu.make_async_copy(y_hbm.at[pl.ds(pos_ref[tile * TM + k], 1), :],
                                     ybuf.at[sl, pl.ds(k, 1), :], sem.at[sl])

    def issue_tile(tile, sl):
        def issue(k, c):
            row_copy(tile, k, sl).start()
            return c
        lax.fori_loop(0, TM, issue, 0, unroll=8)

    @pl.when(i == 0)
    def _():
        issue_tile(0, 0)

    @pl.when(i + 1 < N_TILES)
    def _():
        issue_tile(i + 1, 1 - slot)

    _stage_mod(g2_ref, g2_st, i)
    if with_norm:
        _stage_mod(sc_ref, sc_st, i)
        _stage_mod(sh_ref, sh_st, i)

    def drain(k, c):
        row_copy(i, k, slot).wait()
        return c

    lax.fori_loop(0, TM, drain, 0, unroll=8)

    def new_x(s):
        r = _slab(s)
        return x_ref[r, :] + g2_st[s:s + 1, :] * ybuf[slot, r, :]

    if with_norm:
        lg = lg_ref[...]
        for s in range(N_SLABS):
            xn = new_x(s)
            xo_ref[_slab(s), :] = xn
            hn = _norm_mod_slab(xn, lg, sc_st[s:s + 1, :], sh_st[s:s + 1, :])
            ho_ref[_slab(s), :] = hn.astype(ho_ref.dtype)
    else:
        @pl.when(i < PROMPT_TILES)
        def _():
            for s in range(N_SLABS):
                yp_ref[_slab(s), :] = new_x(s)

        @pl.when(i == PROMPT_TILES)
        def _():
            for s in range(N_SLABS):
                ys_ref[s:s + 1, :] = new_x(s)[0:1, :]


def _residual(x, y_sorted, pos_all, mod, l, next_g=None):
    with_norm = next_g is not None
    row_spec = pl.BlockSpec((TM, D_MODEL), lambda i, pos: (i, 0))
    in_specs = [row_spec, pl.BlockSpec(memory_space=pl.ANY), _mod_spec(l, "g2")]
    args = [pos_all, x, y_sorted, mod]
    stage = pltpu.VMEM((N_SLABS, D_MODEL), F32)
    scratch = [pltpu.VMEM((2, TM, D_MODEL), F32), pltpu.SemaphoreType.DMA((2,)), stage]
    if with_norm:
        in_specs += [pl.BlockSpec((1, D_MODEL), lambda i, pos: (0, 0)),
                     _mod_spec(l + 1, "sc1"), _mod_spec(l + 1, "sh1")]
        args += [next_g.reshape(1, D_MODEL), mod, mod]
        out_specs = [row_spec, row_spec]
        out_shape = [jax.ShapeDtypeStruct((N_ROWS, D_MODEL), F32),
                     jax.ShapeDtypeStruct((N_ROWS, D_MODEL), BF16)]
        scratch += [stage, stage]
    else:
        out_specs = [pl.BlockSpec((TM, D_MODEL), lambda i, pos: (jnp.minimum(i, PROMPT_TILES - 1), 0)),
                     pl.BlockSpec((DEC_BATCH, D_MODEL), lambda i, pos: (0, 0))]
        out_shape = [jax.ShapeDtypeStruct((NP_ROWS, D_MODEL), F32),
                     jax.ShapeDtypeStruct((DEC_BATCH, D_MODEL), F32)]
    grid_spec = pltpu.PrefetchScalarGridSpec(
        num_scalar_prefetch=1,
        grid=(N_TILES,),
        in_specs=in_specs,
        out_specs=out_specs,
        scratch_shapes=scratch,
    )
    return pl.pallas_call(
        functools.partial(_res_kernel, with_norm=with_norm),
        grid_spec=grid_spec,
        out_shape=out_shape,
        compiler_params=_cparams(("arbitrary",)),
        name="residual_norm" if with_norm else "residual",
    )(*args)


def kernel(x_prompt, x_sample, c_prompt, c_sample, cache_swa_k, cache_swa_v, state_ret, state_hgrn,
           w_ada, b_ada, ln1_g, w_in, qn_g, kn_g, attn_sinks, ret_gn_g, ret_gn_b, hgrn_lb, hgrn_ng,
           w_out, ln2_g, w_rg, b_rg, w_re, b_re, w1, w3, w2):
    i32 = jnp.int32
    lbp = jax.nn.softmax(hgrn_lb.astype(F32), axis=0)
    lb_all = jnp.cumsum(lbp, axis=0) - lbp[0]

    assert TM // SROWS == DEC_BATCH and NS_ROWS == TM
    c_all = jnp.concatenate([c_prompt, jnp.zeros((C_SAMPLE0 - BATCH, D_MODEL), F32), c_sample], axis=0)
    mod = _ada(c_all, w_ada, b_ada)

    ret_tab_p = _ret_tables(RET_CHUNK, RET_CHUNK, 0.0, SEQ)
    ret_tab_s = _ret_tables(SROWS, 1, float(PAST_LEN), SROWS)
    w2g = w2.reshape(DEPTH, N_GROUPS, EXP_PER_GROUP * D_EXPERT, D_MODEL)
    whl_all, whi_all, br_all = _router_weights(w_rg, b_rg, w_re, b_re)
    st_s_all = jnp.swapaxes(state_hgrn, -1, -2).reshape(DEPTH, DEC_BATCH, C_HEADS * C_DV, C_DK)
    xs_buf = jnp.zeros((MOE_ROWS, XW), F32)

    x, h = _norm_mod_first(x_prompt.reshape(NP_ROWS, D_MODEL), x_sample.reshape(DEC_BATCH, D_MODEL),
                           ln1_g[0], mod, 0)

    outs = {k: [] for k in ("kp", "vp", "rp", "hp", "ks", "vs", "rs", "hs")}
    for l in range(DEPTH):
        proj = _mm_in(h, w_in, l)

        oa, kn = _swa(proj, proj, proj, attn_sinks[l], qn_g[l], kn_g[l], tq=WINDOW, nb=1, n_seq=BATCH,
                      blocks_per_seq=SEQ // WINDOW, row_block0=0, prev_raw=True)
        ck = cache_swa_k[l].reshape(DEC_BATCH * WINDOW, A_KV * A_HD)
        cv = cache_swa_v[l].reshape(DEC_BATCH * WINDOW, A_KV * A_HD)
        srows = SAMPLE_NB * SROWS
        oa, kn = _swa(proj, ck, cv, attn_sinks[l], qn_g[l], kn_g[l], tq=SROWS, nb=SAMPLE_NB,
                      n_seq=DEC_BATCH, blocks_per_seq=1, row_block0=NP_ROWS // srows, prev_raw=False,
                      alias_bufs=(oa, kn))
        zero_ret = jnp.zeros((BATCH, B_HEADS, B_DK, B_DV), F32)
        ob, rp = _ret(proj, zero_ret, ret_gn_g[l], ret_gn_b[l], ret_tab_p, chunk=RET_CHUNK, nb=1,
                      n_seq=BATCH, chunks_per_seq=SEQ // RET_CHUNK, row_block0=0, table_per_chunk=True)
        ob, rs = _ret(proj, state_ret[l], ret_gn_g[l], ret_gn_b[l], ret_tab_s, chunk=SROWS, nb=SAMPLE_NB,
                      n_seq=DEC_BATCH, chunks_per_seq=1, row_block0=NP_ROWS // srows,
                      table_per_chunk=False, alias_buf=ob)
        zero_hg = jnp.zeros((BATCH, C_HEADS * C_DV, C_DK), F32)
        oc, hp_t = _hgrn(proj, zero_hg, lb_all[l], hgrn_ng[l], rows=HG_TILE, seq_rows=HG_TILE,
                         tv=HG_TILE, n_seq=BATCH, tiles_per_seq=SEQ // HG_TILE, row_block0=0)
        oc, hs_t = _hgrn(proj, st_s_all[l], lb_all[l], hgrn_ng[l], rows=srows, seq_rows=SROWS, tv=1,
                         n_seq=DEC_BATCH, tiles_per_seq=1, row_block0=NP_ROWS // srows, alias_buf=oc)

        x1 = _mm_out(oa, ob, oc, w_out, x, mod, l)

        hx, meta, cnt = _route(x1, ln2_g[l], mod, whl_all, whi_all, br_all, l)
        pos_all, plan = _moe_positions(meta, cnt)
        xs_buf = _moe_scatter(hx, xs_buf, pos_all)
        hid = _moe_hidden(xs_buf, w1, w3, plan, l)
        y_sorted = _moe_down(hid, w2g, plan, l)
        if l + 1 < DEPTH:
            x, h = _residual(x1, y_sorted, pos_all, mod, l, ln1_g[l + 1])
        else:
            y_prompt, y_sample = _residual(x1, y_sorted, pos_all, mod, l)

        knp = kn[:NP_ROWS].reshape(BATCH, SEQ, A_KV, A_HD)
        vpp = proj[:NP_ROWS, 1280:1536].reshape(BATCH, SEQ, A_KV, A_HD)
        outs["kp"].append(knp[:, SEQ - WINDOW:])
        outs["vp"].append(vpp[:, SEQ - WINDOW:])
        outs["rp"].append(rp)
        outs["hp"].append(hp_t.reshape(BATCH, C_HEADS, C_DV, C_DK))
        k_new = kn[NP_ROWS::SROWS].reshape(DEC_BATCH, 1, A_KV, A_HD)
        v_new = proj[NP_ROWS::SROWS, 1280:1536].reshape(DEC_BATCH, 1, A_KV, A_HD)
        outs["ks"].append(k_new)
        outs["vs"].append(v_new)
        outs["rs"].append(rs)
        outs["hs"].append(hs_t.reshape(DEC_BATCH, C_HEADS, C_DV, C_DK))

    y_prompt = y_prompt.reshape(BATCH, SEQ, D_MODEL)
    y_sample = y_sample.reshape(DEC_BATCH, 1, D_MODEL)
    st = lambda k: jnp.stack(outs[k])
    un_t = lambda k: jnp.swapaxes(st(k), -1, -2)
    roll_in = lambda cache, k: jnp.concatenate([cache[:, :, 1:], st(k)], axis=2)
    return (y_prompt, y_sample, st("kp"), st("vp"), st("rp"), un_t("hp"),
            roll_in(cache_swa_k, "ks"), roll_in(cache_swa_v, "vs"), st("rs"), un_t("hs"))
```

```python
import functools

import numpy as np
import jax
import jax.numpy as jnp
from jax import lax
from jax.experimental import pallas as pl
from jax.experimental.pallas import tpu as pltpu

F32 = jnp.float32
BF16 = jnp.bfloat16

D_MODEL = 2048
BATCH = 2
SEQ = 4096
DEPTH = 4
DEC_BATCH = 32
PAST_LEN = 16384
A_HD = 128
A_HEADS = 8
A_KV = 2
A_GROUP = 4
WINDOW = 128
B_DV = 128
B_DK = 64
B_HEADS = 4
C_DK = 128
C_DV = 128
C_HEADS = 4
ROPE_BASE = 10000.0
N_GROUPS = 4
EXP_PER_GROUP = 4
N_EXPERTS = 16
D_EXPERT = 512
EPS = 1e-6
NEG_BIG = -1e30
IN_WIDTH = 5120

SROWS = 16
NP_ROWS = BATCH * SEQ
NS_ROWS = DEC_BATCH * SROWS
N_ROWS = NP_ROWS + NS_ROWS
TM = 512
N_TILES = N_ROWS // TM
MOD_ROWS = 16
N_TOK = NP_ROWS + DEC_BATCH
MOE_TILES = (N_TOK + N_GROUPS * (TM - 1)) // TM
MOE_ROWS = MOE_TILES * TM
HID_WORK = MOE_TILES * EXP_PER_GROUP
DOWN_TN = 1024
DOWN_NCOL = D_MODEL // DOWN_TN
DOWN_WORK = MOE_TILES * DOWN_NCOL

RET_CHUNK = 256
HG_TILE = 256
HG_BLK = 16
SAMPLE_NB = 8
SWA_NB = 2

VMEM_LIMIT = 48 * 1024 * 1024


def _cparams(sem):
    return pltpu.CompilerParams(dimension_semantics=sem, vmem_limit_bytes=VMEM_LIMIT)


def _sigmoid(x):
    return 1.0 / (1.0 + jnp.exp(-x))


def _silu(x):
    return x * _sigmoid(x)


C_ROWS = 40
C_SAMPLE0 = 8
PROMPT_TILES = NP_ROWS // TM
MOD_COL = dict(sh1=0, sc1=1, g1=2, sh2=3, sc2=4, g2=5)


def _tile_mod(m_ref, i):
    n = m_ref.shape[-1]
    seq = jnp.minimum(i // (SEQ // TM), BATCH - 1)
    prow = jnp.broadcast_to(m_ref[pl.ds(seq, 1), :], (DEC_BATCH, n))
    srows = m_ref[C_SAMPLE0:C_SAMPLE0 + DEC_BATCH, :]
    m = jnp.where(i < PROMPT_TILES, prow, srows)
    return jnp.broadcast_to(m[:, None, :], (DEC_BATCH, SROWS, n)).reshape(TM, n)


N_SLABS = TM // SROWS


def _stage_mod(m_ref, stage_ref, i):
    n = m_ref.shape[-1]
    seq = jnp.minimum(i // (SEQ // TM), BATCH - 1)
    prow = jnp.broadcast_to(m_ref[pl.ds(seq, 1), :], (N_SLABS, n))
    srows = m_ref[C_SAMPLE0:C_SAMPLE0 + DEC_BATCH, :]
    stage_ref[...] = jnp.where(i < PROMPT_TILES, prow, srows)


def _slab(s):
    return slice(s * SROWS, (s + 1) * SROWS)


def _mod_spec(l, name, tn=D_MODEL, colmap=None):
    base = MOD_COL[name] * (D_MODEL // tn)
    if colmap is None:
        return pl.BlockSpec((None, C_ROWS, tn), lambda *ids: (l, 0, base))
    return pl.BlockSpec((None, C_ROWS, tn), lambda *ids: (l, 0, base + colmap(*ids)))


def _ada_kernel(c_ref, w_ref, b_ref, o_ref):
    c = _silu(c_ref[...])
    o_ref[...] = jnp.dot(c.astype(BF16), w_ref[...].astype(BF16),
                         preferred_element_type=F32) + b_ref[...]


def _ada(c_all, w_ada, b_ada):
    rows = c_all.shape[0]
    tn = 2048
    ncol = w_ada.shape[2] // tn
    return pl.pallas_call(
        _ada_kernel,
        grid=(DEPTH, ncol),
        in_specs=[
            pl.BlockSpec((rows, D_MODEL), lambda l, j: (0, 0)),
            pl.BlockSpec((None, D_MODEL, tn), lambda l, j: (l, 0, j)),
            pl.BlockSpec((None, 1, tn), lambda l, j: (l, 0, j)),
        ],
        out_specs=pl.BlockSpec((None, rows, tn), lambda l, j: (l, 0, j)),
        out_shape=jax.ShapeDtypeStruct((DEPTH, rows, w_ada.shape[2]), F32),
        compiler_params=_cparams(("arbitrary", "arbitrary")),
        name="ada",
    )(c_all, w_ada, b_ada.reshape(DEPTH, 1, -1))


def _norm_mod_rows(x, g, sc_ref, sh_ref, i):
    y = x * lax.rsqrt(jnp.mean(x * x, axis=-1, keepdims=True) + EPS) * g
    return y * (1.0 + _tile_mod(sc_ref, i)) + _tile_mod(sh_ref, i)


def _norm_mod_slab(x, g, sc_row, sh_row):
    y = x * lax.rsqrt(jnp.mean(x * x, axis=-1, keepdims=True) + EPS) * g
    return y * (1.0 + sc_row) + sh_row


def _norm_kernel(xp_ref, xs_ref, g_ref, sc_ref, sh_ref, x_ref, o_ref):
    i = pl.program_id(0)
    xs = xs_ref[...]
    rid = lax.broadcasted_iota(jnp.int32, (DEC_BATCH, SROWS, 1), 1)
    blk = jnp.where(rid == 0, jnp.broadcast_to(xs[:, None, :], (DEC_BATCH, SROWS, D_MODEL)), 0.0)
    x = jnp.where(i < PROMPT_TILES, xp_ref[...], blk.reshape(TM, D_MODEL))
    x_ref[...] = x
    o_ref[...] = _norm_mod_rows(x, g_ref[...], sc_ref, sh_ref, i).astype(o_ref.dtype)


def _norm_mod_first(x_prompt, x_sample, g, mod, l):
    return pl.pallas_call(
        _norm_kernel,
        grid=(N_TILES,),
        in_specs=[
            pl.BlockSpec((TM, D_MODEL), lambda i: (jnp.minimum(i, PROMPT_TILES - 1), 0)),
            pl.BlockSpec((DEC_BATCH, D_MODEL), lambda i: (0, 0)),
            pl.BlockSpec((1, D_MODEL), lambda i: (0, 0)),
            _mod_spec(l, "sc1"),
            _mod_spec(l, "sh1"),
        ],
        out_specs=[
            pl.BlockSpec((TM, D_MODEL), lambda i: (i, 0)),
            pl.BlockSpec((TM, D_MODEL), lambda i: (i, 0)),
        ],
        out_shape=[
            jax.ShapeDtypeStruct((N_ROWS, D_MODEL), F32),
            jax.ShapeDtypeStruct((N_ROWS, D_MODEL), BF16),
        ],
        compiler_params=_cparams(("arbitrary",)),
        name="norm_first",
    )(x_prompt, x_sample, g.reshape(1, D_MODEL), mod, mod)


def _mm_in_kernel(x_ref, w_ref, o_ref, wb_ref):
    @pl.when(pl.program_id(1) == 0)
    def _():
        wb_ref[...] = w_ref[...].astype(BF16)

    o_ref[...] = jnp.dot(x_ref[...], wb_ref[...], preferred_element_type=F32)


MM_IN_TILES = 8


def _mm_in(h, w_in, l):
    tn = 1024
    ncol = IN_WIDTH // tn
    tm = N_ROWS // MM_IN_TILES
    return pl.pallas_call(
        _mm_in_kernel,
        grid=(ncol, MM_IN_TILES),
        in_specs=[
            pl.BlockSpec((tm, D_MODEL), lambda j, i: (i, 0)),
            pl.BlockSpec((None, D_MODEL, tn), lambda j, i: (l, 0, j)),
        ],
        out_specs=pl.BlockSpec((tm, tn), lambda j, i: (i, j)),
        out_shape=jax.ShapeDtypeStruct((N_ROWS, IN_WIDTH), F32),
        scratch_shapes=[pltpu.VMEM((D_MODEL, tn), BF16)],
        compiler_params=_cparams(("arbitrary", "arbitrary")),
        name="mm_in",
    )(h, w_in)


def _rms(x, g):
    return x * lax.rsqrt(jnp.mean(x * x, axis=-1, keepdims=True) + EPS) * g


def _swa_kernel(sink_ref, q_ref, k_ref, v_ref, pk_ref, pv_ref, qg_ref, kg_ref,
                *rest, tq, nb, chain):
    o_ref, kc_ref, vc_ref = rest[-3:]
    qg = qg_ref[...]
    kg = kg_ref[...]
    nk = WINDOW + tq
    nq = A_GROUP * tq
    row = lax.broadcasted_iota(jnp.int32, (nq, nk), 0) & (tq - 1)
    col = lax.broadcasted_iota(jnp.int32, (nq, nk), 1)
    vis_prev = (col < WINDOW) & (col >= row)
    vis_own = (col >= WINDOW) & ((col - WINDOW) <= row)
    mask_all = vis_prev | vis_own
    if chain:
        mask_first = (vis_prev & (pl.program_id(1) > 0)) | vis_own
        last_step = pl.program_id(1) == pl.num_programs(1) - 1
    rgrp = lax.broadcasted_iota(jnp.int32, (nq, 1), 0) // tq
    last = [None] * A_KV
    for sb in range(nb):
        rs = slice(sb * tq, (sb + 1) * tq)
        mask = mask_first if (chain and sb == 0) else mask_all
        for kv in range(A_KV):
            ks = slice(kv * A_HD, (kv + 1) * A_HD)
            kn = _rms(k_ref[rs, ks], kg)
            vn = v_ref[rs, ks]
            if chain and sb > 0:
                pk, pv = last[kv]
            elif chain:
                pk = _rms(pk_ref[:, ks], kg)
                pv = pv_ref[:, ks]
            else:
                ps = slice(sb * WINDOW, (sb + 1) * WINDOW)
                pk = pk_ref[ps, ks]
                pv = pv_ref[ps, ks]
            if chain:
                if sb == nb - 1:
                    @pl.when(last_step)
                    def _():
                        kc_ref[:, ks] = kn
                        vc_ref[:, ks] = vn
            else:
                kc_ref[sb:sb + 1, ks] = kn[0:1, :]
                vc_ref[sb:sb + 1, ks] = vn[0:1, :]
            last[kv] = (kn, vn)
            keys = jnp.concatenate([pk, kn], axis=0).astype(BF16)
            vals = jnp.concatenate([pv, vn], axis=0).astype(BF16)
            h0 = kv * A_GROUP
            qs = jnp.concatenate(
                [_rms(q_ref[rs, (h0 + g) * A_HD:(h0 + g + 1) * A_HD], qg) for g in range(A_GROUP)],
                axis=0).astype(BF16)
            s = lax.dot_general(qs, keys, (((1,), (1,)), ((), ())),
                                preferred_element_type=F32) * (A_HD ** -0.5)
            s = jnp.where(mask, s, NEG_BIG)
            snk = jnp.full((nq, 1), sink_ref[h0 + A_GROUP - 1], F32)
            for g in range(A_GROUP - 2, -1, -1):
                snk = jnp.where(rgrp == g, sink_ref[h0 + g], snk)
            m = jnp.maximum(jnp.max(s, axis=-1, keepdims=True), snk)
            p = jnp.exp(s - m)
            den = jnp.sum(p, axis=-1, keepdims=True) + jnp.exp(snk - m)
            o = jnp.dot(p.astype(BF16), vals, preferred_element_type=F32) / den
            for g in range(A_GROUP):
                o_ref[rs, (h0 + g) * A_HD:(h0 + g + 1) * A_HD] = o[g * tq:(g + 1) * tq].astype(o_ref.dtype)


def _swa(proj, prev_k, prev_v, sinks, qn_g, kn_g, *, tq, nb, n_seq, steps_per_seq, row_block0,
         chain, layer=None, alias_buf=None):
    qcol, kcol, vcol = 0, 1024 // 256, 1280 // 256
    assert chain or steps_per_seq == 1
    rows = tq * nb

    def rb(s, n):
        return row_block0 + s * steps_per_seq + n

    if chain:
        def before(s, n):
            return jnp.maximum(rb(s, n) * nb - 1, rb(s, 0) * nb)
        prev_specs = [pl.BlockSpec((WINDOW, 256), lambda s, n: (before(s, n), kcol)),
                      pl.BlockSpec((WINDOW, 256), lambda s, n: (before(s, n), vcol))]
        cache_spec = pl.BlockSpec((None, WINDOW, 256), lambda s, n: (s, 0, 0))
        cache_shape = jax.ShapeDtypeStruct((n_seq, WINDOW, 256), F32)
    else:
        prev_specs = [pl.BlockSpec((None, WINDOW * nb, 256), lambda s, n: (layer, s, 0))] * 2
        cache_spec = pl.BlockSpec((nb, 256), lambda s, n: (s, 0))
        cache_shape = jax.ShapeDtypeStruct((n_seq, 256), F32)

    in_specs = [
        pl.BlockSpec(memory_space=pltpu.SMEM),
        pl.BlockSpec((rows, 1024), lambda s, n: (rb(s, n), qcol)),
        pl.BlockSpec((rows, 256), lambda s, n: (rb(s, n), kcol)),
        pl.BlockSpec((rows, 256), lambda s, n: (rb(s, n), vcol)),
        *prev_specs,
        pl.BlockSpec((1, A_HD), lambda s, n: (0, 0)),
        pl.BlockSpec((1, A_HD), lambda s, n: (0, 0)),
    ]
    args = [sinks, proj, proj, proj, prev_k, prev_v, qn_g.reshape(1, A_HD), kn_g.reshape(1, A_HD)]
    aliases = {}
    if alias_buf is not None:
        in_specs.append(pl.BlockSpec(memory_space=pl.ANY))
        aliases[len(args)] = 0
        args.append(alias_buf)
    return pl.pallas_call(
        functools.partial(_swa_kernel, tq=tq, nb=nb, chain=chain),
        grid=(n_seq if chain else n_seq // nb, steps_per_seq),
        in_specs=in_specs,
        out_specs=[pl.BlockSpec((rows, 1024), lambda s, n: (rb(s, n), 0)), cache_spec, cache_spec],
        out_shape=[jax.ShapeDtypeStruct((N_ROWS, 1024), BF16), cache_shape, cache_shape],
        input_output_aliases=aliases,
        compiler_params=_cparams(("arbitrary", "arbitrary")),
        name="swa_prompt" if chain else "swa_sample",
    )(*args)


def _ret_tables(chunk, tv, pos0, t_len):
    lg = np.log1p(-(2.0 ** (-5.0 - np.arange(B_HEADS, dtype=np.float64))))
    idx = np.arange(chunk, dtype=np.float64)
    valid = idx < tv
    diff = idx[:, None] - idx[None, :]
    dmat = np.where((diff >= 0) & valid[:, None] & valid[None, :],
                    np.exp(np.maximum(diff, 0.0)[None] * lg[:, None, None]), 0.0)
    din = np.exp((idx + 1.0)[:, None] * lg[None])
    din = np.repeat(din, B_DV, axis=1)
    kfac = np.where(valid[:, None], np.exp((tv - 1.0 - idx)[:, None] * lg[None]), 0.0)
    kfac = np.repeat(kfac, B_DK, axis=1)
    gc = np.repeat(np.exp(tv * lg), B_DK)[:, None] * np.ones((1, B_DV))
    half = B_DK // 2
    inv = ROPE_BASE ** (-np.arange(half, dtype=np.float64) / half)
    ang = (pos0 + np.arange(t_len, dtype=np.float64))[:, None] * inv[None]
    cos = np.tile(np.cos(ang), (1, 2 * B_HEADS))
    sin = np.tile(np.concatenate([-np.sin(ang), np.sin(ang)], axis=1), (1, B_HEADS))
    f = lambda a: jnp.asarray(a, dtype=F32)
    return f(dmat), f(din), f(kfac), f(gc), f(cos), f(sin)


def _ret_kernel(q_ref, k_ref, v_ref, g_ref, cos_ref, sin_ref, dmat_ref, din_ref, kfac_ref,
                gc_ref, gng_ref, gnb_ref, s0_ref, *rest, nb, chunk):
    o_ref, s_ref = rest[-2], rest[-1]

    @pl.when(pl.program_id(1) == 0)
    def _():
        s_ref[...] = s0_ref[...]

    cos = cos_ref[...]
    sin = sin_ref[...]
    width = B_HEADS * B_DK
    lane = lax.broadcasted_iota(jnp.int32, (1, width), 1)
    first_half = (lane % B_DK) < (B_DK // 2)
    lane_head = lane // B_DK

    def rope(x):
        partner = jnp.where(first_half, pltpu.roll(x, width - B_DK // 2, 1),
                            pltpu.roll(x, B_DK // 2, 1))
        return x * cos + partner * sin

    gc = gc_ref[...]
    for sb in range(nb):
        rws = slice(sb * chunk, (sb + 1) * chunk)
        qr = rope(q_ref[rws, :])
        kr = rope(k_ref[rws, :]) * (B_DK ** -0.5)
        kb = kr.astype(BF16)
        kdec = (kr * kfac_ref[...]).astype(BF16)
        s_old = s_ref[sb]
        s_b = s_old.astype(BF16)
        for h in range(B_HEADS):
            vs = slice(h * B_DV, (h + 1) * B_DV)
            rs = slice(h * B_DK, (h + 1) * B_DK)
            qm = jnp.where(lane_head == h, qr, 0.0).astype(BF16)
            a = lax.dot_general(qm, kb, (((1,), (1,)), ((), ())),
                                preferred_element_type=F32) * dmat_ref[h]
            vh = v_ref[rws, vs].astype(BF16)
            o = jnp.dot(a.astype(BF16), vh, preferred_element_type=F32)
            o = o + jnp.dot(qm, s_b, preferred_element_type=F32) * din_ref[:, vs]
            u = lax.dot_general(kdec, vh, (((0,), (0,)), ((), ())), preferred_element_type=F32)
            s_ref[sb, rs, :] = gc[rs, :] * s_old[rs, :] + u[rs, :]
            mu = jnp.mean(o, axis=-1, keepdims=True)
            oc = o - mu
            var = jnp.mean(oc * oc, axis=-1, keepdims=True)
            y = oc * lax.rsqrt(var + EPS) * gng_ref[:, vs] + gnb_ref[:, vs]
            y = y * _silu(g_ref[rws, vs])
            o_ref[rws, vs] = y.astype(o_ref.dtype)


def _ret(proj, s0, gn_g, gn_b, tables, *, chunk, nb, n_seq, chunks_per_seq, row_block0,
         table_per_chunk, alias_buf=None):
    dmat, din, kfac, gc, cos, sin = tables
    qcol, kcol, vcol, gcol = 1536 // 256, 1792 // 256, 2048 // 512, 2560 // 512
    assert nb == 1 or chunks_per_seq == 1
    rows = nb * chunk

    def rb(s, c):
        return row_block0 + s * chunks_per_seq + c

    tmap = (lambda s, c: (c, 0)) if table_per_chunk else (lambda s, c: (0, 0))
    const2 = lambda s, c: (0, 0)
    in_specs = [
        pl.BlockSpec((rows, 256), lambda s, c: (rb(s, c), qcol)),
        pl.BlockSpec((rows, 256), lambda s, c: (rb(s, c), kcol)),
        pl.BlockSpec((rows, 512), lambda s, c: (rb(s, c), vcol)),
        pl.BlockSpec((rows, 512), lambda s, c: (rb(s, c), gcol)),
        pl.BlockSpec((chunk, 256), tmap),
        pl.BlockSpec((chunk, 256), tmap),
        pl.BlockSpec((B_HEADS, chunk, chunk), lambda s, c: (0, 0, 0)),
        pl.BlockSpec((chunk, 512), const2),
        pl.BlockSpec((chunk, 256), const2),
        pl.BlockSpec((256, B_DV), const2),
        pl.BlockSpec((1, 512), const2),
        pl.BlockSpec((1, 512), const2),
        pl.BlockSpec((nb, 256, B_DV), lambda s, c: (s, 0, 0)),
    ]
    args = [proj, proj, proj, proj, cos, sin, dmat, din, kfac, gc,
            gn_g.reshape(1, 512), gn_b.reshape(1, 512), s0.reshape(n_seq, 256, B_DV)]
    aliases = {}
    if alias_buf is not None:
        in_specs.append(pl.BlockSpec(memory_space=pl.ANY))
        aliases[len(args)] = 0
        args.append(alias_buf)
    o, s_new = pl.pallas_call(
        functools.partial(_ret_kernel, nb=nb, chunk=chunk),
        grid=(n_seq // nb, chunks_per_seq),
        in_specs=in_specs,
        out_specs=[
            pl.BlockSpec((rows, 512), lambda s, c: (rb(s, c), 0)),
            pl.BlockSpec((nb, 256, B_DV), lambda s, c: (s, 0, 0)),
        ],
        out_shape=[
            jax.ShapeDtypeStruct((N_ROWS, 512), BF16),
            jax.ShapeDtypeStruct((n_seq, 256, B_DV), F32),
        ],
        input_output_aliases=aliases,
        compiler_params=_cparams(("arbitrary", "arbitrary")),
        name="ret_prompt" if alias_buf is None else "ret_sample",
    )(*args)
    return o, s_new.reshape(n_seq, B_HEADS, B_DK, B_DV)


HG_FAST = 32
HG_LIM = 60.0
HG_HDR = 32


def _hgrn_kernel(q_ref, f_ref, i_ref, g_ref, lb_ref, ng_ref, st0_ref, *rest, rows, seq_rows, tv):
    use_fast = seq_rows % HG_FAST == 0
    if use_fast:
        o_ref, st_ref, q_s, k_s, cs_s, qd_s, ki_s, kd_s, eb_s, stb_s = rest[-10:]
    else:
        o_ref, st_ref, q_s, k_s, cs_s = rest[-5:]
    nseq = rows // seq_rows

    @pl.when(pl.program_id(1) == 0)
    def _():
        st_ref[...] = st0_ref[...]

    z = f_ref[...]
    lb = lb_ref[...]
    sig = _sigmoid(z)
    lf = jnp.log(lb + (1.0 - lb) * sig)
    k = (1.0 - lb) * _sigmoid(-z)
    if tv < seq_rows:
        rvalid = (lax.broadcasted_iota(jnp.int32, (rows, 1), 0) % seq_rows) < tv
        lf = jnp.where(rvalid, lf, 0.0)
        k = jnp.where(rvalid, k, 0.0)
    ri = lax.broadcasted_iota(jnp.int32, (rows, rows), 0)
    ci = lax.broadcasted_iota(jnp.int32, (rows, rows), 1)
    tri = (ci <= ri).astype(F32)
    cs = jnp.dot(tri, lf, precision=lax.Precision.HIGHEST, preferred_element_type=F32)
    q_s[...] = _silu(q_ref[...])
    k_s[...] = k
    cs_s[0:HG_HDR, :] = jnp.zeros((HG_HDR, C_HEADS * C_DK), F32)
    cs_s[HG_HDR:HG_HDR + rows, :] = cs
    ng = ng_ref[...]

    nt_dims = (((1,), (1,)), ((), ()))
    tn_dims = (((0,), (0,)), ((), ()))

    def finish(o, rsl, hs):
        y = o * lax.rsqrt(jnp.mean(o * o, axis=-1, keepdims=True) + EPS) * ng
        return (y * _silu(g_ref[rsl, hs])).astype(o_ref.dtype)

    def pairwise_block(blk, carry):
        size = HG_BLK
        r0 = pl.multiple_of(blk * size, size)
        rsl = pl.ds(r0, size)
        sidx = 0 if nseq == 1 else blk // (seq_rows // size)
        rowi = lax.broadcasted_iota(jnp.int32, (size, 1), 0)
        for h in range(C_HEADS):
            hs = slice(h * C_DK, (h + 1) * C_DK)
            vsl = slice(h * C_DV, (h + 1) * C_DV)
            qb = q_s[rsl, hs]
            kb = k_s[rsl, hs]
            vb = i_ref[rsl, hs]
            prev = cs_s[pl.ds(HG_HDR + r0 - 8, 8), hs][7:8, :]
            bc = cs_s[pl.ds(HG_HDR + r0, size), hs] - prev
            blast = bc[size - 1:size, :]
            qd = (qb * jnp.exp(bc)).astype(BF16)
            kd = (kb * jnp.exp(blast - bc)).astype(BF16)
            st = st_ref[sidx, vsl, :]
            o = lax.dot_general(qd, st.astype(BF16), nt_dims, preferred_element_type=F32)
            for j in range(size):
                dec = jnp.exp(jnp.minimum(bc - bc[j:j + 1, :], 0.0))
                a = jnp.sum(qb * kb[j:j + 1, :] * dec, axis=-1, keepdims=True)
                a = jnp.where(rowi >= j, a, 0.0)
                o = o + a * vb[j:j + 1, :]
            u = lax.dot_general(vb.astype(BF16), kd, tn_dims, preferred_element_type=F32)
            st_ref[sidx, vsl, :] = st * jnp.exp(blast) + u
            o_ref[rsl, vsl] = finish(o, rsl, hs)
        return carry

    def factorised_tile():
        nblk = rows // HG_FAST
        for b in range(nblk):
            r = slice(b * HG_FAST, (b + 1) * HG_FAST)
            lo = HG_HDR + b * HG_FAST
            prev = cs_s[lo - 8:lo, :][7:8, :]
            bc = cs_s[lo:lo + HG_FAST, :] - prev
            blast = bc[HG_FAST - 1:HG_FAST, :]
            qb = q_s[r, :]
            kb = k_s[r, :]
            qd_s[r, :] = (qb * jnp.exp(bc)).astype(BF16)
            ki_s[r, :] = (kb * jnp.exp(-bc)).astype(BF16)
            kd_s[r, :] = (kb * jnp.exp(blast - bc)).astype(BF16)
            eb_s[b:b + 1, :] = jnp.exp(blast)
        same = (ri // HG_FAST) == (ci // HG_FAST)
        keep = same & (ci <= ri)
        for h in range(C_HEADS):
            hs = slice(h * C_DK, (h + 1) * C_DK)
            vsl = slice(h * C_DV, (h + 1) * C_DV)
            vh = i_ref[:, hs].astype(BF16)
            st = st_ref[0, vsl, :]
            for b in range(nblk):
                r = slice(b * HG_FAST, (b + 1) * HG_FAST)
                stb_s[h, b] = st.astype(BF16)
                u = lax.dot_general(vh[r, :], kd_s[r, hs], tn_dims, preferred_element_type=F32)
                st = st * eb_s[b:b + 1, hs] + u
            st_ref[0, vsl, :] = st
            a = lax.dot_general(qd_s[:, hs], ki_s[:, hs], nt_dims, preferred_element_type=F32)
            a = jnp.where(keep, a, 0.0)
            o = jnp.dot(a.astype(BF16), vh, preferred_element_type=F32)
            inter = [lax.dot_general(qd_s[b * HG_FAST:(b + 1) * HG_FAST, hs], stb_s[h, b], nt_dims,
                                     preferred_element_type=F32) for b in range(nblk)]
            o = o + jnp.concatenate(inter, axis=0)
            o_ref[:, vsl] = finish(o, slice(None), hs)

    def pairwise_tile():
        lax.fori_loop(0, rows // HG_BLK, pairwise_block, 0)

    if use_fast:
        worst = jnp.max(cs_s[HG_HDR - HG_FAST:HG_HDR - HG_FAST + rows, :] - cs)
        bounded = worst < HG_LIM
        pl.when(bounded)(factorised_tile)
        pl.when(jnp.logical_not(bounded))(pairwise_tile)
    else:
        pairwise_tile()


def _hgrn(proj, st0, lb, ng, *, rows, seq_rows, tv, n_seq, tiles_per_seq, row_block0, alias_buf=None):
    qcol, fcol, icol, gcol = 3072 // 512, 3584 // 512, 4096 // 512, 4608 // 512
    nst = max(rows // seq_rows, 1)
    assert nst == 1 or tiles_per_seq == 1
    width = C_HEADS * C_DK
    scratch = [pltpu.VMEM((rows, width), F32), pltpu.VMEM((rows, width), F32),
               pltpu.VMEM((HG_HDR + rows, width), F32)]
    if seq_rows % HG_FAST == 0:
        scratch += [pltpu.VMEM((rows, width), BF16) for _ in range(3)]
        scratch += [pltpu.VMEM((rows // HG_FAST, width), F32),
                    pltpu.VMEM((C_HEADS, rows // HG_FAST, C_DV, C_DK), BF16)]

    def rb(s, c):
        return row_block0 + s * tiles_per_seq + c

    const2 = lambda s, c: (0, 0)
    in_specs = [
        pl.BlockSpec((rows, 512), lambda s, c: (rb(s, c), qcol)),
        pl.BlockSpec((rows, 512), lambda s, c: (rb(s, c), fcol)),
        pl.BlockSpec((rows, 512), lambda s, c: (rb(s, c), icol)),
        pl.BlockSpec((rows, 512), lambda s, c: (rb(s, c), gcol)),
        pl.BlockSpec((1, 512), const2),
        pl.BlockSpec((1, C_DV), const2),
        pl.BlockSpec((nst, 512, C_DK), lambda s, c: (s, 0, 0)),
    ]
    args = [proj, proj, proj, proj, lb.reshape(1, 512), ng.reshape(1, C_DV), st0]
    aliases = {}
    if alias_buf is not None:
        in_specs.append(pl.BlockSpec(memory_space=pl.ANY))
        aliases[len(args)] = 0
        args.append(alias_buf)
    return pl.pallas_call(
        functools.partial(_hgrn_kernel, rows=rows, seq_rows=seq_rows, tv=tv),
        grid=(n_seq // nst, tiles_per_seq),
        in_specs=in_specs,
        out_specs=[
            pl.BlockSpec((rows, 512), lambda s, c: (rb(s, c), 0)),
            pl.BlockSpec((nst, 512, C_DK), lambda s, c: (s, 0, 0)),
        ],
        out_shape=[
            jax.ShapeDtypeStruct((N_ROWS, 512), BF16),
            jax.ShapeDtypeStruct((n_seq, 512, C_DK), F32),
        ],
        scratch_shapes=scratch,
        input_output_aliases=aliases,
        compiler_params=_cparams(("arbitrary", "arbitrary")),
        name="hgrn_prompt" if alias_buf is None else "hgrn_sample",
    )(*args)


def _mm_out_kernel(oa_ref, ob_ref, oc_ref, w_ref, x_ref, g_ref, o_ref, wb_ref):
    @pl.when(pl.program_id(1) == 0)
    def _():
        wb_ref[...] = w_ref[...].astype(BF16)

    acc = jnp.dot(oa_ref[...], wb_ref[0:1024, :], preferred_element_type=F32)
    acc = acc + jnp.dot(ob_ref[...], wb_ref[1024:1536, :], preferred_element_type=F32)
    acc = acc + jnp.dot(oc_ref[...], wb_ref[1536:2048, :], preferred_element_type=F32)
    o_ref[...] = x_ref[...] + _tile_mod(g_ref, pl.program_id(1)) * acc


def _mm_out(oa, ob, oc, w_out, x, mod, l):
    tn = 1024
    ncol = D_MODEL // tn
    return pl.pallas_call(
        _mm_out_kernel,
        grid=(ncol, N_TILES),
        in_specs=[
            pl.BlockSpec((TM, 1024), lambda j, i: (i, 0)),
            pl.BlockSpec((TM, 512), lambda j, i: (i, 0)),
            pl.BlockSpec((TM, 512), lambda j, i: (i, 0)),
            pl.BlockSpec((None, D_MODEL, tn), lambda j, i: (l, 0, j)),
            pl.BlockSpec((TM, tn), lambda j, i: (i, j)),
            _mod_spec(l, "g1", tn, lambda j, i: j),
        ],
        out_specs=pl.BlockSpec((TM, tn), lambda j, i: (i, j)),
        out_shape=jax.ShapeDtypeStruct((N_ROWS, D_MODEL), F32),
        scratch_shapes=[pltpu.VMEM((D_MODEL, tn), BF16)],
        compiler_params=_cparams(("arbitrary", "arbitrary")),
        name="mm_out",
    )(oa, ob, oc, w_out, x, mod)


META_GSEL = 0
META_RANK = 1
META_GATE = 20
XW = D_MODEL + 128


def _route_kernel(x_ref, g_ref, sc_ref, sh_ref, whl_ref, wh_ref, br_ref, hx_ref, meta_ref, cnt_ref,
                  run_ref, sc_st, sh_st, hi_s, lo_s):
    i = pl.program_id(0)

    @pl.when(i == 0)
    def _():
        run_ref[...] = jnp.zeros_like(run_ref)

    _stage_mod(sc_ref, sc_st, i)
    _stage_mod(sh_ref, sh_st, i)
    lg = g_ref[...]
    for s in range(N_SLABS):
        r = _slab(s)
        h = _norm_mod_slab(x_ref[r, :], lg, sc_st[s:s + 1, :], sh_st[s:s + 1, :])
        hx_ref[r, 0:D_MODEL] = h
        hi = h.astype(BF16)
        hi_s[r, :] = hi
        lo_s[r, :] = (h - hi.astype(F32)).astype(BF16)
    l1 = jnp.dot(hi_s[...], whl_ref[...], preferred_element_type=F32)
    l2 = jnp.dot(lo_s[...], wh_ref[...], preferred_element_type=F32)
    logits = l1[:, 0:128] + l1[:, 128:256] + l2 + br_ref[...]
    col = lax.broadcasted_iota(jnp.int32, logits.shape, 1).astype(F32)
    big = 1e9
    is_g = col < N_GROUPS
    gl = jnp.where(is_g, logits, -jnp.inf)
    gmax = jnp.max(gl, axis=-1, keepdims=True)
    gsel = jnp.min(jnp.where(gl == gmax, col, big), axis=-1, keepdims=True)
    gden = jnp.sum(jnp.where(is_g, jnp.exp(logits - gmax), 0.0), axis=-1, keepdims=True)
    g_w = 1.0 / gden
    e0 = N_GROUPS + EXP_PER_GROUP * gsel
    in_grp = (col >= e0) & (col < e0 + EXP_PER_GROUP)
    el = jnp.where(in_grp, logits, -jnp.inf)
    v1 = jnp.max(el, axis=-1, keepdims=True)
    i1 = jnp.min(jnp.where(el == v1, col, big), axis=-1, keepdims=True)
    el2 = jnp.where(col == i1, -jnp.inf, el)
    v2 = jnp.max(el2, axis=-1, keepdims=True)
    i2 = jnp.min(jnp.where(el2 == v2, col, big), axis=-1, keepdims=True)
    t = jnp.exp(v2 - v1)
    p1 = 1.0 / (1.0 + t)
    a1 = p1 * g_w
    a2 = t * p1 * g_w
    meta = jnp.where(col == META_GATE + (i1 - e0), a1, 0.0)
    meta = meta + jnp.where(col == META_GATE + (i2 - e0), a2, 0.0)
    meta = meta + jnp.where(col == META_GSEL, gsel, 0.0)
    rows = logits.shape[0]
    rowi = lax.broadcasted_iota(jnp.int32, (rows, 1), 0)
    is_tok = (i < PROMPT_TILES) | (rowi % SROWS == 0)
    onehot = jnp.where((col == gsel) & is_tok, 1.0, 0.0)
    ri = lax.broadcasted_iota(jnp.int32, (rows, rows), 0)
    ci = lax.broadcasted_iota(jnp.int32, (rows, rows), 1)
    tri = jnp.where(ci <= ri, 1.0, 0.0).astype(BF16)
    incl = jnp.dot(tri, onehot.astype(BF16), preferred_element_type=F32)
    run = run_ref[...]
    rank = jnp.sum(jnp.where(col == gsel, incl + run - 1.0, 0.0), axis=-1, keepdims=True)
    meta = meta + jnp.where(col == META_RANK, rank, 0.0)
    meta_ref[...] = meta
    hx_ref[:, D_MODEL:XW] = meta
    run = run + incl[rows - 1:rows, :]
    run_ref[...] = run
    cnt_ref[...] = run


def _router_weights(w_rg, b_rg, w_re, b_re):
    pad = 128 - N_GROUPS - N_EXPERTS
    wr = jnp.concatenate([w_rg, w_re, jnp.zeros((DEPTH, D_MODEL, pad), F32)], axis=2)
    w_hi = wr.astype(BF16)
    w_lo = (wr - w_hi.astype(F32)).astype(BF16)
    br = jnp.concatenate([b_rg, b_re, jnp.zeros((DEPTH, pad), F32)], axis=1).reshape(DEPTH, 1, 128)
    return jnp.concatenate([w_hi, w_lo], axis=2), w_hi, br


def _route(x, g, mod, whl, w_hi, br, l):
    return pl.pallas_call(
        _route_kernel,
        grid=(N_TILES,),
        in_specs=[
            pl.BlockSpec((TM, D_MODEL), lambda i: (i, 0)),
            pl.BlockSpec((1, D_MODEL), lambda i: (0, 0)),
            _mod_spec(l, "sc2"),
            _mod_spec(l, "sh2"),
            pl.BlockSpec((None, D_MODEL, 256), lambda i: (l, 0, 0)),
            pl.BlockSpec((None, D_MODEL, 128), lambda i: (l, 0, 0)),
            pl.BlockSpec((None, 1, 128), lambda i: (l, 0, 0)),
        ],
        out_specs=[
            pl.BlockSpec((TM, XW), lambda i: (i, 0)),
            pl.BlockSpec((TM, 128), lambda i: (i, 0)),
            pl.BlockSpec((1, 128), lambda i: (0, 0)),
        ],
        out_shape=[
            jax.ShapeDtypeStruct((N_ROWS, XW), F32),
            jax.ShapeDtypeStruct((N_ROWS, 128), F32),
            jax.ShapeDtypeStruct((1, 128), F32),
        ],
        scratch_shapes=[pltpu.VMEM((1, 128), F32),
                        pltpu.VMEM((N_SLABS, D_MODEL), F32), pltpu.VMEM((N_SLABS, D_MODEL), F32),
                        pltpu.VMEM((TM, D_MODEL), BF16), pltpu.VMEM((TM, D_MODEL), BF16)],
        compiler_params=_cparams(("arbitrary",)),
        name="route",
    )(x, g.reshape(1, D_MODEL), mod, mod, whl, w_hi, br)


def _moe_positions(meta, cnt):
    i32 = jnp.int32
    counts = cnt[0, :N_GROUPS].astype(i32)
    ntile = (counts + TM - 1) // TM
    tstart = jnp.cumsum(ntile) - ntile
    gsel = meta[:, META_GSEL].astype(i32)
    rank = meta[:, META_RANK].astype(i32)
    first_row = jnp.zeros_like(gsel)
    for g in range(N_GROUPS):
        first_row = jnp.where(gsel == g, tstart[g] * TM, first_row)
    pos = first_row + rank
    pos_all = jnp.concatenate([pos[:NP_ROWS], jnp.repeat(pos[NP_ROWS::SROWS], SROWS)])
    plan = jnp.concatenate([tstart, ntile]).astype(i32)
    return pos_all, plan


def _scatter_kernel(pos_ref, src_ref, buf_hbm, out_hbm, sem):
    del buf_hbm
    i = pl.program_id(0)
    base = i * TM

    def row_copy(r):
        return pltpu.make_async_copy(src_ref.at[pl.ds(r, 1), :],
                                     out_hbm.at[pl.ds(pos_ref[base + r], 1), :], sem)

    def run(n, stride):
        def issue(k, c):
            row_copy(k * stride).start()
            return c

        def drain(k, c):
            row_copy(k * stride).wait()
            return c

        lax.fori_loop(0, n, issue, 0, unroll=8)
        lax.fori_loop(0, n, drain, 0, unroll=8)

    @pl.when(i < PROMPT_TILES)
    def _():
        run(TM, 1)

    @pl.when(i == PROMPT_TILES)
    def _():
        run(DEC_BATCH, SROWS)


def _moe_scatter(hx, buf, pos_all):
    grid_spec = pltpu.PrefetchScalarGridSpec(
        num_scalar_prefetch=1,
        grid=(N_TILES,),
        in_specs=[pl.BlockSpec((TM, XW), lambda i, pos: (i, 0)), pl.BlockSpec(memory_space=pl.ANY)],
        out_specs=pl.BlockSpec(memory_space=pl.ANY),
        scratch_shapes=[pltpu.SemaphoreType.DMA(())],
    )
    return pl.pallas_call(
        _scatter_kernel,
        grid_spec=grid_spec,
        out_shape=jax.ShapeDtypeStruct((MOE_ROWS, XW), F32),
        input_output_aliases={2: 0},
        compiler_params=_cparams(("arbitrary",)),
        name="moe_scatter",
    )(pos_all, hx, buf)


def _work_item(w, plan_ref, per_tile):
    i32 = jnp.int32
    nt = [plan_ref[N_GROUPS + g] for g in range(N_GROUPS)]
    ts = [plan_ref[g] for g in range(N_GROUPS)]
    ends = []
    acc = 0
    for g in range(N_GROUPS):
        acc = acc + per_tile * nt[g]
        ends.append(acc)
    valid = w < ends[-1]
    wc = jnp.maximum(jnp.minimum(w, ends[-1] - 1), 0)
    g = sum((wc >= ends[k]).astype(i32) for k in range(N_GROUPS - 1))

    def pick(vals):
        out = vals[N_GROUPS - 1]
        for k in range(N_GROUPS - 2, -1, -1):
            out = jnp.where(g == k, vals[k], out)
        return out

    r = wc - pick([0] + ends[:-1])
    ntg = pick(nt)
    c = sum((r >= m * ntg).astype(i32) for m in range(1, per_tile))
    t_in = r - c * ntg
    t = pick(ts) + t_in
    extra = jnp.maximum(w - ends[-1], 0)
    used = ts[N_GROUPS - 1] + nt[N_GROUPS - 1]
    t_out = jnp.where(valid, t, used + extra // per_tile)
    c_out = jnp.where(valid, c, extra % per_tile)
    return g, c, t, t_in == 0, valid, t_out, c_out


def _hidden_kernel(plan_ref, x_ref, w1_ref, w3_ref, o_ref, w1b, w3b):
    _, e, _, first, valid, _, _ = _work_item(pl.program_id(0), plan_ref, EXP_PER_GROUP)

    @pl.when(jnp.logical_not(valid))
    def _():
        o_ref[...] = jnp.zeros_like(o_ref)

    @pl.when(valid & first)
    def _():
        w1b[...] = w1_ref[...].astype(BF16)
        w3b[...] = w3_ref[...].astype(BF16)

    @pl.when(valid)
    def _():
        x = x_ref[:, 0:D_MODEL].astype(BF16)
        a = jnp.dot(x, w1b[...], preferred_element_type=F32)
        b = jnp.dot(x, w3b[...], preferred_element_type=F32)
        m = x_ref[:, D_MODEL:XW]
        col = lax.broadcasted_iota(jnp.int32, m.shape, 1)
        gate = jnp.sum(jnp.where(col == META_GATE + e, m, 0.0), axis=-1, keepdims=True)
        o_ref[...] = (_silu(a) * b * gate).astype(o_ref.dtype)


def _moe_hidden(xs, w1, w3, plan, l):
    def item(w, plan):
        return _work_item(w, plan, EXP_PER_GROUP)

    def w_map(w, plan):
        it = item(w, plan)
        return (l, it[0] * EXP_PER_GROUP + it[1], 0, 0)

    grid_spec = pltpu.PrefetchScalarGridSpec(
        num_scalar_prefetch=1,
        grid=(HID_WORK,),
        in_specs=[
            pl.BlockSpec((TM, XW), lambda w, plan: (item(w, plan)[2], 0)),
            pl.BlockSpec((None, None, D_MODEL, D_EXPERT), w_map),
            pl.BlockSpec((None, None, D_MODEL, D_EXPERT), w_map),
        ],
        out_specs=pl.BlockSpec((TM, D_EXPERT), lambda w, plan: item(w, plan)[5:7]),
        scratch_shapes=[pltpu.VMEM((D_MODEL, D_EXPERT), BF16),
                        pltpu.VMEM((D_MODEL, D_EXPERT), BF16)],
    )
    return pl.pallas_call(
        _hidden_kernel,
        grid_spec=grid_spec,
        out_shape=jax.ShapeDtypeStruct((MOE_ROWS, EXP_PER_GROUP * D_EXPERT), BF16),
        compiler_params=_cparams(("arbitrary",)),
        name="moe_hidden",
    )(plan, xs, w1, w3)


def _down_kernel(plan_ref, h_ref, w_ref, o_ref, wb):
    _, _, _, first, valid, _, _ = _work_item(pl.program_id(0), plan_ref, DOWN_NCOL)

    @pl.when(jnp.logical_not(valid))
    def _():
        o_ref[...] = jnp.zeros_like(o_ref)

    @pl.when(valid & first)
    def _():
        wb[...] = w_ref[...].astype(BF16)

    @pl.when(valid)
    def _():
        o_ref[...] = jnp.dot(h_ref[...], wb[...], preferred_element_type=F32)


def _moe_down(hid, w2g, plan, l):
    kdim = EXP_PER_GROUP * D_EXPERT

    def item(w, plan):
        return _work_item(w, plan, DOWN_NCOL)

    grid_spec = pltpu.PrefetchScalarGridSpec(
        num_scalar_prefetch=1,
        grid=(DOWN_WORK,),
        in_specs=[
            pl.BlockSpec((TM, kdim), lambda w, plan: (item(w, plan)[2], 0)),
            pl.BlockSpec((None, None, kdim, DOWN_TN),
                         lambda w, plan: (l, item(w, plan)[0], 0, item(w, plan)[1])),
        ],
        out_specs=pl.BlockSpec((TM, DOWN_TN), lambda w, plan: item(w, plan)[5:7]),
        scratch_shapes=[pltpu.VMEM((kdim, DOWN_TN), BF16)],
    )
    return pl.pallas_call(
        _down_kernel,
        grid_spec=grid_spec,
        out_shape=jax.ShapeDtypeStruct((MOE_ROWS, D_MODEL), F32),
        compiler_params=_cparams(("arbitrary",)),
        name="moe_down",
    )(plan, hid, w2g)


def _res_kernel(pos_ref, x_ref, y_hbm, g2_ref, *rest, with_norm):
    if with_norm:
        lg_ref, sc_ref, sh_ref, xo_ref, ho_ref, ybuf, sem, g2_st, sc_st, sh_st = rest
    else:
        yp_ref, ys_ref, ybuf, sem, g2_st = rest
    i = pl.program_id(0)
    slot = i % 2

    def row_copy(tile, k, sl):
        return pltpu.make_async_copy(y_hbm.at[pl.ds(pos_ref[tile * TM + k], 1), :],
                                     ybuf.at[sl, pl.ds(k, 1), :], sem.at[sl])

    def issue_tile(tile, sl):
        def issue(k, c):
            row_copy(tile, k, sl).start()
            return c
        lax.fori_loop(0, TM, issue, 0, unroll=8)

    @pl.when(i == 0)
    def _():
        issue_tile(0, 0)

    @pl.when(i + 1 < N_TILES)
    def _():
        issue_tile(i + 1, 1 - slot)

    _stage_mod(g2_ref, g2_st, i)
    if with_norm:
        _stage_mod(sc_ref, sc_st, i)
        _stage_mod(sh_ref, sh_st, i)

    def drain(k, c):
        row_copy(i, k, slot).wait()
        return c

    lax.fori_loop(0, TM, drain, 0, unroll=8)

    def new_x(s):
        r = _slab(s)
        return x_ref[r, :] + g2_st[s:s + 1, :] * ybuf[slot, r, :]

    if with_norm:
        lg = lg_ref[...]
        for s in range(N_SLABS):
            xn = new_x(s)
            xo_ref[_slab(s), :] = xn
            hn = _norm_mod_slab(xn, lg, sc_st[s:s + 1, :], sh_st[s:s + 1, :])
            ho_ref[_slab(s), :] = hn.astype(ho_ref.dtype)
    else:
        @pl.when(i < PROMPT_TILES)
        def _():
            for s in range(N_SLABS):
                yp_ref[_slab(s), :] = new_x(s)

        @pl.when(i == PROMPT_TILES)
        def _():
            for s in range(N_SLABS):
                ys_ref[s:s + 1, :] = new_x(s)[0:1, :]


def _residual(x, y_sorted, pos_all, mod, l, next_g=None):
    with_norm = next_g is not None
    row_spec = pl.BlockSpec((TM, D_MODEL), lambda i, pos: (i, 0))
    in_specs = [row_spec, pl.BlockSpec(memory_space=pl.ANY), _mod_spec(l, "g2")]
    args = [pos_all, x, y_sorted, mod]
    stage = pltpu.VMEM((N_SLABS, D_MODEL), F32)
    scratch = [pltpu.VMEM((2, TM, D_MODEL), F32), pltpu.SemaphoreType.DMA((2,)), stage]
    if with_norm:
        in_specs += [pl.BlockSpec((1, D_MODEL), lambda i, pos: (0, 0)),
                     _mod_spec(l + 1, "sc1"), _mod_spec(l + 1, "sh1")]
        args += [next_g.reshape(1, D_MODEL), mod, mod]
        out_specs = [row_spec, row_spec]
        out_shape = [jax.ShapeDtypeStruct((N_ROWS, D_MODEL), F32),
                     jax.ShapeDtypeStruct((N_ROWS, D_MODEL), BF16)]
        scratch += [stage, stage]
    else:
        out_specs = [pl.BlockSpec((TM, D_MODEL), lambda i, pos: (jnp.minimum(i, PROMPT_TILES - 1), 0)),
                     pl.BlockSpec((DEC_BATCH, D_MODEL), lambda i, pos: (0, 0))]
        out_shape = [jax.ShapeDtypeStruct((NP_ROWS, D_MODEL), F32),
                     jax.ShapeDtypeStruct((DEC_BATCH, D_MODEL), F32)]
    grid_spec = pltpu.PrefetchScalarGridSpec(
        num_scalar_prefetch=1,
        grid=(N_TILES,),
        in_specs=in_specs,
        out_specs=out_specs,
        scratch_shapes=scratch,
    )
    return pl.pallas_call(
        functools.partial(_res_kernel, with_norm=with_norm),
        grid_spec=grid_spec,
        out_shape=out_shape,
        compiler_params=_cparams(("arbitrary",)),
        name="residual_norm" if with_norm else "residual",
    )(*args)


def kernel(x_prompt, x_sample, c_prompt, c_sample, cache_swa_k, cache_swa_v, state_ret, state_hgrn,
           w_ada, b_ada, ln1_g, w_in, qn_g, kn_g, attn_sinks, ret_gn_g, ret_gn_b, hgrn_lb, hgrn_ng,
           w_out, ln2_g, w_rg, b_rg, w_re, b_re, w1, w3, w2):
    i32 = jnp.int32
    lbp = jax.nn.softmax(hgrn_lb.astype(F32), axis=0)
    lb_all = jnp.cumsum(lbp, axis=0) - lbp[0]

    assert TM // SROWS == DEC_BATCH and NS_ROWS == TM
    c_all = jnp.concatenate([c_prompt, jnp.zeros((C_SAMPLE0 - BATCH, D_MODEL), F32), c_sample], axis=0)
    mod = _ada(c_all, w_ada, b_ada)

    ret_tab_p = _ret_tables(RET_CHUNK, RET_CHUNK, 0.0, SEQ)
    ret_tab_s = _ret_tables(SROWS, 1, float(PAST_LEN), SROWS)
    w2g = w2.reshape(DEPTH, N_GROUPS, EXP_PER_GROUP * D_EXPERT, D_MODEL)
    whl_all, whi_all, br_all = _router_weights(w_rg, b_rg, w_re, b_re)
    st_s_all = jnp.swapaxes(state_hgrn, -1, -2).reshape(DEPTH, DEC_BATCH, C_HEADS * C_DV, C_DK)
    ck_all = cache_swa_k.reshape(DEPTH, DEC_BATCH * WINDOW, A_KV * A_HD)
    cv_all = cache_swa_v.reshape(DEPTH, DEC_BATCH * WINDOW, A_KV * A_HD)
    xs_buf = jnp.zeros((MOE_ROWS, XW), F32)

    x, h = _norm_mod_first(x_prompt.reshape(NP_ROWS, D_MODEL), x_sample.reshape(DEC_BATCH, D_MODEL),
                           ln1_g[0], mod, 0)

    outs = {k: [] for k in ("kp", "vp", "rp", "hp", "ks", "vs", "rs", "hs")}
    for l in range(DEPTH):
        proj = _mm_in(h, w_in, l)

        oa, kp, vp = _swa(proj, proj, proj, attn_sinks[l], qn_g[l], kn_g[l], tq=WINDOW, nb=SWA_NB,
                          n_seq=BATCH, steps_per_seq=SEQ // (WINDOW * SWA_NB), row_block0=0, chain=True)
        srows = SAMPLE_NB * SROWS
        oa, ks, vs = _swa(proj, ck_all, cv_all, attn_sinks[l], qn_g[l], kn_g[l], tq=SROWS, nb=SAMPLE_NB,
                          n_seq=DEC_BATCH, steps_per_seq=1, row_block0=NP_ROWS // srows, chain=False,
                          layer=l, alias_buf=oa)
        zero_ret = jnp.zeros((BATCH, B_HEADS, B_DK, B_DV), F32)
        ob, rp = _ret(proj, zero_ret, ret_gn_g[l], ret_gn_b[l], ret_tab_p, chunk=RET_CHUNK, nb=1,
                      n_seq=BATCH, chunks_per_seq=SEQ // RET_CHUNK, row_block0=0, table_per_chunk=True)
        ob, rs = _ret(proj, state_ret[l], ret_gn_g[l], ret_gn_b[l], ret_tab_s, chunk=SROWS, nb=SAMPLE_NB,
                      n_seq=DEC_BATCH, chunks_per_seq=1, row_block0=NP_ROWS // srows,
                      table_per_chunk=False, alias_buf=ob)
        zero_hg = jnp.zeros((BATCH, C_HEADS * C_DV, C_DK), F32)
        oc, hp_t = _hgrn(proj, zero_hg, lb_all[l], hgrn_ng[l], rows=HG_TILE, seq_rows=HG_TILE,
                         tv=HG_TILE, n_seq=BATCH, tiles_per_seq=SEQ // HG_TILE, row_block0=0)
        oc, hs_t = _hgrn(proj, st_s_all[l], lb_all[l], hgrn_ng[l], rows=srows, seq_rows=SROWS, tv=1,
                         n_seq=DEC_BATCH, tiles_per_seq=1, row_block0=NP_ROWS // srows, alias_buf=oc)

        x1 = _mm_out(oa, ob, oc, w_out, x, mod, l)

        hx, meta, cnt = _route(x1, ln2_g[l], mod, whl_all, whi_all, br_all, l)
        pos_all, plan = _moe_positions(meta, cnt)
        xs_buf = _moe_scatter(hx, xs_buf, pos_all)
        hid = _moe_hidden(xs_buf, w1, w3, plan, l)
        y_sorted = _moe_down(hid, w2g, plan, l)
        if l + 1 < DEPTH:
            x, h = _residual(x1, y_sorted, pos_all, mod, l, ln1_g[l + 1])
        else:
            y_prompt, y_sample = _residual(x1, y_sorted, pos_all, mod, l)

        outs["kp"].append(kp.reshape(BATCH, WINDOW, A_KV, A_HD))
        outs["vp"].append(vp.reshape(BATCH, WINDOW, A_KV, A_HD))
        outs["rp"].append(rp)
        outs["hp"].append(hp_t.reshape(BATCH, C_HEADS, C_DV, C_DK))
        outs["ks"].append(ks.reshape(DEC_BATCH, 1, A_KV, A_HD))
        outs["vs"].append(vs.reshape(DEC_BATCH, 1, A_KV, A_HD))
        outs["rs"].append(rs)
        outs["hs"].append(hs_t.reshape(DEC_BATCH, C_HEADS, C_DV, C_DK))

    y_prompt = y_prompt.reshape(BATCH, SEQ, D_MODEL)
    y_sample = y_sample.reshape(DEC_BATCH, 1, D_MODEL)
    st = lambda k: jnp.stack(outs[k])
    un_t = lambda k: jnp.swapaxes(st(k), -1, -2)
    roll_in = lambda cache, k: jnp.concatenate([cache[:, :, 1:], st(k)], axis=2)
    return (y_prompt, y_sample, st("kp"), st("vp"), st("rp"), un_t("hp"),
            roll_in(cache_swa_k, "ks"), roll_in(cache_swa_v, "vs"), st("rs"), un_t("hs"))
```

```python
import functools

import numpy as np
import jax
import jax.numpy as jnp
from jax import lax
from jax.experimental import pallas as pl
from jax.experimental.pallas import tpu as pltpu

F32 = jnp.float32
BF16 = jnp.bfloat16

D_MODEL = 2048
BATCH = 2
SEQ = 4096
DEPTH = 4
DEC_BATCH = 32
PAST_LEN = 16384
A_HD = 128
A_HEADS = 8
A_KV = 2
A_GROUP = 4
WINDOW = 128
B_DV = 128
B_DK = 64
B_HEADS = 4
C_DK = 128
C_DV = 128
C_HEADS = 4
ROPE_BASE = 10000.0
N_GROUPS = 4
EXP_PER_GROUP = 4
N_EXPERTS = 16
D_EXPERT = 512
EPS = 1e-6
NEG_BIG = -1e30
IN_WIDTH = 5120

SROWS = 16
NP_ROWS = BATCH * SEQ
NS_ROWS = DEC_BATCH * SROWS
N_ROWS = NP_ROWS + NS_ROWS
TM = 512
N_TILES = N_ROWS // TM
MOD_ROWS = 16
N_TOK = NP_ROWS + DEC_BATCH
MOE_TILES = (N_TOK + N_GROUPS * (TM - 1)) // TM
MOE_ROWS = MOE_TILES * TM
DOWN_TN = 2048
DOWN_NCOL = D_MODEL // DOWN_TN
DOWN_WORK = MOE_TILES * DOWN_NCOL

RET_CHUNK = 256
HG_TILE = 256
HG_BLK = 16
SAMPLE_NB = 8
SWA_NB = 2

VMEM_LIMIT = 48 * 1024 * 1024


def _cparams(sem):
    return pltpu.CompilerParams(dimension_semantics=sem, vmem_limit_bytes=VMEM_LIMIT)


def _sigmoid(x):
    return 1.0 / (1.0 + jnp.exp(-x))


def _silu(x):
    return x * _sigmoid(x)


C_ROWS = 40
C_SAMPLE0 = 8
PROMPT_TILES = NP_ROWS // TM
MOD_COL = dict(sh1=0, sc1=1, g1=2, sh2=3, sc2=4, g2=5)


def _tile_mod(m_ref, i):
    n = m_ref.shape[-1]
    seq = jnp.minimum(i // (SEQ // TM), BATCH - 1)
    prow = jnp.broadcast_to(m_ref[pl.ds(seq, 1), :], (DEC_BATCH, n))
    srows = m_ref[C_SAMPLE0:C_SAMPLE0 + DEC_BATCH, :]
    m = jnp.where(i < PROMPT_TILES, prow, srows)
    return jnp.broadcast_to(m[:, None, :], (DEC_BATCH, SROWS, n)).reshape(TM, n)


N_SLABS = TM // SROWS


def _stage_mod(m_ref, stage_ref, i):
    n = m_ref.shape[-1]
    seq = jnp.minimum(i // (SEQ // TM), BATCH - 1)
    prow = jnp.broadcast_to(m_ref[pl.ds(seq, 1), :], (N_SLABS, n))
    srows = m_ref[C_SAMPLE0:C_SAMPLE0 + DEC_BATCH, :]
    stage_ref[...] = jnp.where(i < PROMPT_TILES, prow, srows)


def _slab(s):
    return slice(s * SROWS, (s + 1) * SROWS)


def _mod_spec(l, name, tn=D_MODEL, colmap=None):
    base = MOD_COL[name] * (D_MODEL // tn)
    if colmap is None:
        return pl.BlockSpec((None, C_ROWS, tn), lambda *ids: (l, 0, base))
    return pl.BlockSpec((None, C_ROWS, tn), lambda *ids: (l, 0, base + colmap(*ids)))


def _ada_kernel(c_ref, w_ref, b_ref, o_ref):
    c = _silu(c_ref[...])
    o_ref[...] = jnp.dot(c.astype(BF16), w_ref[...].astype(BF16),
                         preferred_element_type=F32) + b_ref[...]


def _ada(c_all, w_ada, b_ada):
    rows = c_all.shape[0]
    tn = 2048
    ncol = w_ada.shape[2] // tn
    return pl.pallas_call(
        _ada_kernel,
        grid=(DEPTH, ncol),
        in_specs=[
            pl.BlockSpec((rows, D_MODEL), lambda l, j: (0, 0)),
            pl.BlockSpec((None, D_MODEL, tn), lambda l, j: (l, 0, j)),
            pl.BlockSpec((None, 1, tn), lambda l, j: (l, 0, j)),
        ],
        out_specs=pl.BlockSpec((None, rows, tn), lambda l, j: (l, 0, j)),
        out_shape=jax.ShapeDtypeStruct((DEPTH, rows, w_ada.shape[2]), F32),
        compiler_params=_cparams(("arbitrary", "arbitrary")),
        name="ada",
    )(c_all, w_ada, b_ada.reshape(DEPTH, 1, -1))


def _norm_mod_rows(x, g, sc_ref, sh_ref, i):
    y = x * lax.rsqrt(jnp.mean(x * x, axis=-1, keepdims=True) + EPS) * g
    return y * (1.0 + _tile_mod(sc_ref, i)) + _tile_mod(sh_ref, i)


def _norm_mod_slab(x, g, sc_row, sh_row):
    y = x * lax.rsqrt(jnp.mean(x * x, axis=-1, keepdims=True) + EPS) * g
    return y * (1.0 + sc_row) + sh_row


def _norm_kernel(xp_ref, xs_ref, g_ref, sc_ref, sh_ref, x_ref, o_ref):
    i = pl.program_id(0)
    xs = xs_ref[...]
    rid = lax.broadcasted_iota(jnp.int32, (DEC_BATCH, SROWS, 1), 1)
    blk = jnp.where(rid == 0, jnp.broadcast_to(xs[:, None, :], (DEC_BATCH, SROWS, D_MODEL)), 0.0)
    x = jnp.where(i < PROMPT_TILES, xp_ref[...], blk.reshape(TM, D_MODEL))
    x_ref[...] = x
    o_ref[...] = _norm_mod_rows(x, g_ref[...], sc_ref, sh_ref, i).astype(o_ref.dtype)


def _norm_mod_first(x_prompt, x_sample, g, mod, l):
    return pl.pallas_call(
        _norm_kernel,
        grid=(N_TILES,),
        in_specs=[
            pl.BlockSpec((TM, D_MODEL), lambda i: (jnp.minimum(i, PROMPT_TILES - 1), 0)),
            pl.BlockSpec((DEC_BATCH, D_MODEL), lambda i: (0, 0)),
            pl.BlockSpec((1, D_MODEL), lambda i: (0, 0)),
            _mod_spec(l, "sc1"),
            _mod_spec(l, "sh1"),
        ],
        out_specs=[
            pl.BlockSpec((TM, D_MODEL), lambda i: (i, 0)),
            pl.BlockSpec((TM, D_MODEL), lambda i: (i, 0)),
        ],
        out_shape=[
            jax.ShapeDtypeStruct((N_ROWS, D_MODEL), F32),
            jax.ShapeDtypeStruct((N_ROWS, D_MODEL), BF16),
        ],
        compiler_params=_cparams(("arbitrary",)),
        name="norm_first",
    )(x_prompt, x_sample, g.reshape(1, D_MODEL), mod, mod)


def _mm_in_kernel(x_ref, w_ref, o_ref, wb_ref):
    @pl.when(pl.program_id(1) == 0)
    def _():
        wb_ref[...] = w_ref[...].astype(BF16)

    o_ref[...] = jnp.dot(x_ref[...], wb_ref[...], preferred_element_type=F32)


MM_IN_TILES = 8


def _mm_in(h, w_in, l):
    tn = 1024
    ncol = IN_WIDTH // tn
    tm = N_ROWS // MM_IN_TILES
    return pl.pallas_call(
        _mm_in_kernel,
        grid=(ncol, MM_IN_TILES),
        in_specs=[
            pl.BlockSpec((tm, D_MODEL), lambda j, i: (i, 0)),
            pl.BlockSpec((None, D_MODEL, tn), lambda j, i: (l, 0, j)),
        ],
        out_specs=pl.BlockSpec((tm, tn), lambda j, i: (i, j)),
        out_shape=jax.ShapeDtypeStruct((N_ROWS, IN_WIDTH), F32),
        scratch_shapes=[pltpu.VMEM((D_MODEL, tn), BF16)],
        compiler_params=_cparams(("arbitrary", "arbitrary")),
        name="mm_in",
    )(h, w_in)


def _rms(x, g):
    return x * lax.rsqrt(jnp.mean(x * x, axis=-1, keepdims=True) + EPS) * g


def _swa_kernel(sink_ref, q_ref, k_ref, v_ref, pk_ref, pv_ref, qg_ref, kg_ref,
                *rest, tq, nb, chain):
    o_ref, kc_ref, vc_ref = rest[-3:]
    qg = qg_ref[...]
    kg = kg_ref[...]
    nk = WINDOW + tq
    nq = A_GROUP * tq
    row = lax.broadcasted_iota(jnp.int32, (nq, nk), 0) & (tq - 1)
    col = lax.broadcasted_iota(jnp.int32, (nq, nk), 1)
    vis_prev = (col < WINDOW) & (col >= row)
    vis_own = (col >= WINDOW) & ((col - WINDOW) <= row)
    mask_all = vis_prev | vis_own
    if chain:
        mask_first = (vis_prev & (pl.program_id(1) > 0)) | vis_own
        last_step = pl.program_id(1) == pl.num_programs(1) - 1
    rgrp = lax.broadcasted_iota(jnp.int32, (nq, 1), 0) // tq
    last = [None] * A_KV
    for sb in range(nb):
        rs = slice(sb * tq, (sb + 1) * tq)
        mask = mask_first if (chain and sb == 0) else mask_all
        for kv in range(A_KV):
            ks = slice(kv * A_HD, (kv + 1) * A_HD)
            kn = _rms(k_ref[rs, ks], kg)
            vn = v_ref[rs, ks]
            if chain and sb > 0:
                pk, pv = last[kv]
            elif chain:
                pk = _rms(pk_ref[:, ks], kg)
                pv = pv_ref[:, ks]
            else:
                ps = slice(sb * WINDOW, (sb + 1) * WINDOW)
                pk = pk_ref[ps, ks]
                pv = pv_ref[ps, ks]
            if chain:
                if sb == nb - 1:
                    @pl.when(last_step)
                    def _():
                        kc_ref[:, ks] = kn
                        vc_ref[:, ks] = vn
            else:
                kc_ref[sb:sb + 1, ks] = kn[0:1, :]
                vc_ref[sb:sb + 1, ks] = vn[0:1, :]
            last[kv] = (kn, vn)
            keys = jnp.concatenate([pk, kn], axis=0).astype(BF16)
            vals = jnp.concatenate([pv, vn], axis=0).astype(BF16)
            h0 = kv * A_GROUP
            qs = jnp.concatenate(
                [_rms(q_ref[rs, (h0 + g) * A_HD:(h0 + g + 1) * A_HD], qg) for g in range(A_GROUP)],
                axis=0).astype(BF16)
            s = lax.dot_general(qs, keys, (((1,), (1,)), ((), ())),
                                preferred_element_type=F32) * (A_HD ** -0.5)
            s = jnp.where(mask, s, NEG_BIG)
            snk = jnp.full((nq, 1), sink_ref[h0 + A_GROUP - 1], F32)
            for g in range(A_GROUP - 2, -1, -1):
                snk = jnp.where(rgrp == g, sink_ref[h0 + g], snk)
            m = jnp.maximum(jnp.max(s, axis=-1, keepdims=True), snk)
            p = jnp.exp(s - m)
            den = jnp.sum(p, axis=-1, keepdims=True) + jnp.exp(snk - m)
            o = jnp.dot(p.astype(BF16), vals, preferred_element_type=F32) / den
            for g in range(A_GROUP):
                o_ref[rs, (h0 + g) * A_HD:(h0 + g + 1) * A_HD] = o[g * tq:(g + 1) * tq].astype(o_ref.dtype)


def _swa(proj, prev_k, prev_v, sinks, qn_g, kn_g, *, tq, nb, n_seq, steps_per_seq, row_block0,
         chain, layer=None, alias_buf=None):
    qcol, kcol, vcol = 0, 1024 // 256, 1280 // 256
    assert chain or steps_per_seq == 1
    rows = tq * nb

    def rb(s, n):
        return row_block0 + s * steps_per_seq + n

    if chain:
        def before(s, n):
            return jnp.maximum(rb(s, n) * nb - 1, rb(s, 0) * nb)
        prev_specs = [pl.BlockSpec((WINDOW, 256), lambda s, n: (before(s, n), kcol)),
                      pl.BlockSpec((WINDOW, 256), lambda s, n: (before(s, n), vcol))]
        cache_spec = pl.BlockSpec((None, WINDOW, 256), lambda s, n: (s, 0, 0))
        cache_shape = jax.ShapeDtypeStruct((n_seq, WINDOW, 256), F32)
    else:
        prev_specs = [pl.BlockSpec((None, WINDOW * nb, 256), lambda s, n: (layer, s, 0))] * 2
        cache_spec = pl.BlockSpec((nb, 256), lambda s, n: (s, 0))
        cache_shape = jax.ShapeDtypeStruct((n_seq, 256), F32)

    in_specs = [
        pl.BlockSpec(memory_space=pltpu.SMEM),
        pl.BlockSpec((rows, 1024), lambda s, n: (rb(s, n), qcol)),
        pl.BlockSpec((rows, 256), lambda s, n: (rb(s, n), kcol)),
        pl.BlockSpec((rows, 256), lambda s, n: (rb(s, n), vcol)),
        *prev_specs,
        pl.BlockSpec((1, A_HD), lambda s, n: (0, 0)),
        pl.BlockSpec((1, A_HD), lambda s, n: (0, 0)),
    ]
    args = [sinks, proj, proj, proj, prev_k, prev_v, qn_g.reshape(1, A_HD), kn_g.reshape(1, A_HD)]
    aliases = {}
    if alias_buf is not None:
        in_specs.append(pl.BlockSpec(memory_space=pl.ANY))
        aliases[len(args)] = 0
        args.append(alias_buf)
    return pl.pallas_call(
        functools.partial(_swa_kernel, tq=tq, nb=nb, chain=chain),
        grid=(n_seq if chain else n_seq // nb, steps_per_seq),
        in_specs=in_specs,
        out_specs=[pl.BlockSpec((rows, 1024), lambda s, n: (rb(s, n), 0)), cache_spec, cache_spec],
        out_shape=[jax.ShapeDtypeStruct((N_ROWS, 1024), BF16), cache_shape, cache_shape],
        input_output_aliases=aliases,
        compiler_params=_cparams(("arbitrary", "arbitrary")),
        name="swa_prompt" if chain else "swa_sample",
    )(*args)


def _ret_tables(chunk, tv, pos0, t_len):
    lg = np.log1p(-(2.0 ** (-5.0 - np.arange(B_HEADS, dtype=np.float64))))
    idx = np.arange(chunk, dtype=np.float64)
    valid = idx < tv
    diff = idx[:, None] - idx[None, :]
    dmat = np.where((diff >= 0) & valid[:, None] & valid[None, :],
                    np.exp(np.maximum(diff, 0.0)[None] * lg[:, None, None]), 0.0)
    din = np.exp((idx + 1.0)[:, None] * lg[None])
    din = np.repeat(din, B_DV, axis=1)
    kfac = np.where(valid[:, None], np.exp((tv - 1.0 - idx)[:, None] * lg[None]), 0.0)
    kfac = np.repeat(kfac, B_DK, axis=1)
    gc = np.repeat(np.exp(tv * lg), B_DK)[:, None] * np.ones((1, B_DV))
    half = B_DK // 2
    inv = ROPE_BASE ** (-np.arange(half, dtype=np.float64) / half)
    ang = (pos0 + np.arange(t_len, dtype=np.float64))[:, None] * inv[None]
    cos = np.tile(np.cos(ang), (1, 2 * B_HEADS))
    sin = np.tile(np.concatenate([-np.sin(ang), np.sin(ang)], axis=1), (1, B_HEADS))
    f = lambda a: jnp.asarray(a, dtype=F32)
    return f(dmat), f(din), f(kfac), f(gc), f(cos), f(sin)


def _ret_kernel(q_ref, k_ref, v_ref, g_ref, cos_ref, sin_ref, dmat_ref, din_ref, kfac_ref,
                gc_ref, gng_ref, gnb_ref, s0_ref, *rest, nb, chunk):
    o_ref, s_ref = rest[-2], rest[-1]

    @pl.when(pl.program_id(1) == 0)
    def _():
        s_ref[...] = s0_ref[...]

    cos = cos_ref[...]
    sin = sin_ref[...]
    width = B_HEADS * B_DK
    lane = lax.broadcasted_iota(jnp.int32, (1, width), 1)
    first_half = (lane % B_DK) < (B_DK // 2)
    lane_head = lane // B_DK

    def rope(x):
        partner = jnp.where(first_half, pltpu.roll(x, width - B_DK // 2, 1),
                            pltpu.roll(x, B_DK // 2, 1))
        return x * cos + partner * sin

    gc = gc_ref[...]
    for sb in range(nb):
        rws = slice(sb * chunk, (sb + 1) * chunk)
        qr = rope(q_ref[rws, :])
        kr = rope(k_ref[rws, :]) * (B_DK ** -0.5)
        kb = kr.astype(BF16)
        kdec = (kr * kfac_ref[...]).astype(BF16)
        s_old = s_ref[sb]
        s_b = s_old.astype(BF16)
        for h in range(B_HEADS):
            vs = slice(h * B_DV, (h + 1) * B_DV)
            rs = slice(h * B_DK, (h + 1) * B_DK)
            qm = jnp.where(lane_head == h, qr, 0.0).astype(BF16)
            a = lax.dot_general(qm, kb, (((1,), (1,)), ((), ())),
                                preferred_element_type=F32) * dmat_ref[h]
            vh = v_ref[rws, vs].astype(BF16)
            o = jnp.dot(a.astype(BF16), vh, preferred_element_type=F32)
            o = o + jnp.dot(qm, s_b, preferred_element_type=F32) * din_ref[:, vs]
            u = lax.dot_general(kdec, vh, (((0,), (0,)), ((), ())), preferred_element_type=F32)
            s_ref[sb, rs, :] = gc[rs, :] * s_old[rs, :] + u[rs, :]
            mu = jnp.mean(o, axis=-1, keepdims=True)
            oc = o - mu
            var = jnp.mean(oc * oc, axis=-1, keepdims=True)
            y = oc * lax.rsqrt(var + EPS) * gng_ref[:, vs] + gnb_ref[:, vs]
            y = y * _silu(g_ref[rws, vs])
            o_ref[rws, vs] = y.astype(o_ref.dtype)


def _ret(proj, s0, gn_g, gn_b, tables, *, chunk, nb, n_seq, chunks_per_seq, row_block0,
         table_per_chunk, alias_buf=None):
    dmat, din, kfac, gc, cos, sin = tables
    qcol, kcol, vcol, gcol = 1536 // 256, 1792 // 256, 2048 // 512, 2560 // 512
    assert nb == 1 or chunks_per_seq == 1
    rows = nb * chunk

    def rb(s, c):
        return row_block0 + s * chunks_per_seq + c

    tmap = (lambda s, c: (c, 0)) if table_per_chunk else (lambda s, c: (0, 0))
    const2 = lambda s, c: (0, 0)
    in_specs = [
        pl.BlockSpec((rows, 256), lambda s, c: (rb(s, c), qcol)),
        pl.BlockSpec((rows, 256), lambda s, c: (rb(s, c), kcol)),
        pl.BlockSpec((rows, 512), lambda s, c: (rb(s, c), vcol)),
        pl.BlockSpec((rows, 512), lambda s, c: (rb(s, c), gcol)),
        pl.BlockSpec((chunk, 256), tmap),
        pl.BlockSpec((chunk, 256), tmap),
        pl.BlockSpec((B_HEADS, chunk, chunk), lambda s, c: (0, 0, 0)),
        pl.BlockSpec((chunk, 512), const2),
        pl.BlockSpec((chunk, 256), const2),
        pl.BlockSpec((256, B_DV), const2),
        pl.BlockSpec((1, 512), const2),
        pl.BlockSpec((1, 512), const2),
        pl.BlockSpec((nb, 256, B_DV), lambda s, c: (s, 0, 0)),
    ]
    args = [proj, proj, proj, proj, cos, sin, dmat, din, kfac, gc,
            gn_g.reshape(1, 512), gn_b.reshape(1, 512), s0.reshape(n_seq, 256, B_DV)]
    aliases = {}
    if alias_buf is not None:
        in_specs.append(pl.BlockSpec(memory_space=pl.ANY))
        aliases[len(args)] = 0
        args.append(alias_buf)
    o, s_new = pl.pallas_call(
        functools.partial(_ret_kernel, nb=nb, chunk=chunk),
        grid=(n_seq // nb, chunks_per_seq),
        in_specs=in_specs,
        out_specs=[
            pl.BlockSpec((rows, 512), lambda s, c: (rb(s, c), 0)),
            pl.BlockSpec((nb, 256, B_DV), lambda s, c: (s, 0, 0)),
        ],
        out_shape=[
            jax.ShapeDtypeStruct((N_ROWS, 512), BF16),
            jax.ShapeDtypeStruct((n_seq, 256, B_DV), F32),
        ],
        input_output_aliases=aliases,
        compiler_params=_cparams(("arbitrary", "arbitrary")),
        name="ret_prompt" if alias_buf is None else "ret_sample",
    )(*args)
    return o, s_new.reshape(n_seq, B_HEADS, B_DK, B_DV)


HG_FAST = 32
HG_LIM = 60.0
HG_HDR = 32


def _hgrn_kernel(q_ref, f_ref, i_ref, g_ref, lb_ref, ng_ref, st0_ref, *rest, rows, seq_rows, tv):
    use_fast = seq_rows % HG_FAST == 0
    if use_fast:
        o_ref, st_ref, q_s, k_s, cs_s, qd_s, ki_s, kd_s, eb_s, stb_s = rest[-10:]
    else:
        o_ref, st_ref, q_s, k_s, cs_s = rest[-5:]
    nseq = rows // seq_rows

    @pl.when(pl.program_id(1) == 0)
    def _():
        st_ref[...] = st0_ref[...]

    z = f_ref[...]
    lb = lb_ref[...]
    sig = _sigmoid(z)
    lf = jnp.log(lb + (1.0 - lb) * sig)
    k = (1.0 - lb) * _sigmoid(-z)
    if tv < seq_rows:
        rvalid = (lax.broadcasted_iota(jnp.int32, (rows, 1), 0) % seq_rows) < tv
        lf = jnp.where(rvalid, lf, 0.0)
        k = jnp.where(rvalid, k, 0.0)
    ri = lax.broadcasted_iota(jnp.int32, (rows, rows), 0)
    ci = lax.broadcasted_iota(jnp.int32, (rows, rows), 1)
    tri = (ci <= ri).astype(F32)
    cs = jnp.dot(tri, lf, precision=lax.Precision.HIGHEST, preferred_element_type=F32)
    q_s[...] = _silu(q_ref[...])
    k_s[...] = k
    cs_s[0:HG_HDR, :] = jnp.zeros((HG_HDR, C_HEADS * C_DK), F32)
    cs_s[HG_HDR:HG_HDR + rows, :] = cs
    ng = ng_ref[...]

    nt_dims = (((1,), (1,)), ((), ()))
    tn_dims = (((0,), (0,)), ((), ()))

    def finish(o, rsl, hs):
        y = o * lax.rsqrt(jnp.mean(o * o, axis=-1, keepdims=True) + EPS) * ng
        return (y * _silu(g_ref[rsl, hs])).astype(o_ref.dtype)

    def pairwise_block(blk, carry):
        size = HG_BLK
        r0 = pl.multiple_of(blk * size, size)
        rsl = pl.ds(r0, size)
        sidx = 0 if nseq == 1 else blk // (seq_rows // size)
        rowi = lax.broadcasted_iota(jnp.int32, (size, 1), 0)
        for h in range(C_HEADS):
            hs = slice(h * C_DK, (h + 1) * C_DK)
            vsl = slice(h * C_DV, (h + 1) * C_DV)
            qb = q_s[rsl, hs]
            kb = k_s[rsl, hs]
            vb = i_ref[rsl, hs]
            prev = cs_s[pl.ds(HG_HDR + r0 - 8, 8), hs][7:8, :]
            bc = cs_s[pl.ds(HG_HDR + r0, size), hs] - prev
            blast = bc[size - 1:size, :]
            qd = (qb * jnp.exp(bc)).astype(BF16)
            kd = (kb * jnp.exp(blast - bc)).astype(BF16)
            st = st_ref[sidx, vsl, :]
            o = lax.dot_general(qd, st.astype(BF16), nt_dims, preferred_element_type=F32)
            for j in range(size):
                dec = jnp.exp(jnp.minimum(bc - bc[j:j + 1, :], 0.0))
                a = jnp.sum(qb * kb[j:j + 1, :] * dec, axis=-1, keepdims=True)
                a = jnp.where(rowi >= j, a, 0.0)
                o = o + a * vb[j:j + 1, :]
            u = lax.dot_general(vb.astype(BF16), kd, tn_dims, preferred_element_type=F32)
            st_ref[sidx, vsl, :] = st * jnp.exp(blast) + u
            o_ref[rsl, vsl] = finish(o, rsl, hs)
        return carry

    def factorised_tile():
        nblk = rows // HG_FAST
        for b in range(nblk):
            r = slice(b * HG_FAST, (b + 1) * HG_FAST)
            lo = HG_HDR + b * HG_FAST
            prev = cs_s[lo - 8:lo, :][7:8, :]
            bc = cs_s[lo:lo + HG_FAST, :] - prev
            blast = bc[HG_FAST - 1:HG_FAST, :]
            qb = q_s[r, :]
            kb = k_s[r, :]
            qd_s[r, :] = (qb * jnp.exp(bc)).astype(BF16)
            ki_s[r, :] = (kb * jnp.exp(-bc)).astype(BF16)
            kd_s[r, :] = (kb * jnp.exp(blast - bc)).astype(BF16)
            eb_s[b:b + 1, :] = jnp.exp(blast)
        same = (ri // HG_FAST) == (ci // HG_FAST)
        keep = same & (ci <= ri)
        for h in range(C_HEADS):
            hs = slice(h * C_DK, (h + 1) * C_DK)
            vsl = slice(h * C_DV, (h + 1) * C_DV)
            vh = i_ref[:, hs].astype(BF16)
            st = st_ref[0, vsl, :]
            for b in range(nblk):
                r = slice(b * HG_FAST, (b + 1) * HG_FAST)
                stb_s[h, b] = st.astype(BF16)
                u = lax.dot_general(vh[r, :], kd_s[r, hs], tn_dims, preferred_element_type=F32)
                st = st * eb_s[b:b + 1, hs] + u
            st_ref[0, vsl, :] = st
            a = lax.dot_general(qd_s[:, hs], ki_s[:, hs], nt_dims, preferred_element_type=F32)
            a = jnp.where(keep, a, 0.0)
            o = jnp.dot(a.astype(BF16), vh, preferred_element_type=F32)
            inter = [lax.dot_general(qd_s[b * HG_FAST:(b + 1) * HG_FAST, hs], stb_s[h, b], nt_dims,
                                     preferred_element_type=F32) for b in range(nblk)]
            o = o + jnp.concatenate(inter, axis=0)
            o_ref[:, vsl] = finish(o, slice(None), hs)

    def pairwise_tile():
        lax.fori_loop(0, rows // HG_BLK, pairwise_block, 0)

    if use_fast:
        worst = jnp.max(cs_s[HG_HDR - HG_FAST:HG_HDR - HG_FAST + rows, :] - cs)
        bounded = worst < HG_LIM
        pl.when(bounded)(factorised_tile)
        pl.when(jnp.logical_not(bounded))(pairwise_tile)
    else:
        pairwise_tile()


def _hgrn(proj, st0, lb, ng, *, rows, seq_rows, tv, n_seq, tiles_per_seq, row_block0, alias_buf=None):
    qcol, fcol, icol, gcol = 3072 // 512, 3584 // 512, 4096 // 512, 4608 // 512
    nst = max(rows // seq_rows, 1)
    assert nst == 1 or tiles_per_seq == 1
    width = C_HEADS * C_DK
    scratch = [pltpu.VMEM((rows, width), F32), pltpu.VMEM((rows, width), F32),
               pltpu.VMEM((HG_HDR + rows, width), F32)]
    if seq_rows % HG_FAST == 0:
        scratch += [pltpu.VMEM((rows, width), BF16) for _ in range(3)]
        scratch += [pltpu.VMEM((rows // HG_FAST, width), F32),
                    pltpu.VMEM((C_HEADS, rows // HG_FAST, C_DV, C_DK), BF16)]

    def rb(s, c):
        return row_block0 + s * tiles_per_seq + c

    const2 = lambda s, c: (0, 0)
    in_specs = [
        pl.BlockSpec((rows, 512), lambda s, c: (rb(s, c), qcol)),
        pl.BlockSpec((rows, 512), lambda s, c: (rb(s, c), fcol)),
        pl.BlockSpec((rows, 512), lambda s, c: (rb(s, c), icol)),
        pl.BlockSpec((rows, 512), lambda s, c: (rb(s, c), gcol)),
        pl.BlockSpec((1, 512), const2),
        pl.BlockSpec((1, C_DV), const2),
        pl.BlockSpec((nst, 512, C_DK), lambda s, c: (s, 0, 0)),
    ]
    args = [proj, proj, proj, proj, lb.reshape(1, 512), ng.reshape(1, C_DV), st0]
    aliases = {}
    if alias_buf is not None:
        in_specs.append(pl.BlockSpec(memory_space=pl.ANY))
        aliases[len(args)] = 0
        args.append(alias_buf)
    return pl.pallas_call(
        functools.partial(_hgrn_kernel, rows=rows, seq_rows=seq_rows, tv=tv),
        grid=(n_seq // nst, tiles_per_seq),
        in_specs=in_specs,
        out_specs=[
            pl.BlockSpec((rows, 512), lambda s, c: (rb(s, c), 0)),
            pl.BlockSpec((nst, 512, C_DK), lambda s, c: (s, 0, 0)),
        ],
        out_shape=[
            jax.ShapeDtypeStruct((N_ROWS, 512), BF16),
            jax.ShapeDtypeStruct((n_seq, 512, C_DK), F32),
        ],
        scratch_shapes=scratch,
        input_output_aliases=aliases,
        compiler_params=_cparams(("arbitrary", "arbitrary")),
        name="hgrn_prompt" if alias_buf is None else "hgrn_sample",
    )(*args)


def _mm_out_kernel(oa_ref, ob_ref, oc_ref, w_ref, x_ref, g_ref, o_ref, wb_ref):
    @pl.when(pl.program_id(1) == 0)
    def _():
        wb_ref[...] = w_ref[...].astype(BF16)

    acc = jnp.dot(oa_ref[...], wb_ref[0:1024, :], preferred_element_type=F32)
    acc = acc + jnp.dot(ob_ref[...], wb_ref[1024:1536, :], preferred_element_type=F32)
    acc = acc + jnp.dot(oc_ref[...], wb_ref[1536:2048, :], preferred_element_type=F32)
    o_ref[...] = x_ref[...] + _tile_mod(g_ref, pl.program_id(1)) * acc


def _mm_out(oa, ob, oc, w_out, x, mod, l):
    tn = 1024
    ncol = D_MODEL // tn
    return pl.pallas_call(
        _mm_out_kernel,
        grid=(ncol, N_TILES),
        in_specs=[
            pl.BlockSpec((TM, 1024), lambda j, i: (i, 0)),
            pl.BlockSpec((TM, 512), lambda j, i: (i, 0)),
            pl.BlockSpec((TM, 512), lambda j, i: (i, 0)),
            pl.BlockSpec((None, D_MODEL, tn), lambda j, i: (l, 0, j)),
            pl.BlockSpec((TM, tn), lambda j, i: (i, j)),
            _mod_spec(l, "g1", tn, lambda j, i: j),
        ],
        out_specs=pl.BlockSpec((TM, tn), lambda j, i: (i, j)),
        out_shape=jax.ShapeDtypeStruct((N_ROWS, D_MODEL), F32),
        scratch_shapes=[pltpu.VMEM((D_MODEL, tn), BF16)],
        compiler_params=_cparams(("arbitrary", "arbitrary")),
        name="mm_out",
    )(oa, ob, oc, w_out, x, mod)


META_GSEL = 0
META_RANK = 1
META_GATE = 20
XW = D_MODEL + 128


def _route_kernel(x_ref, g_ref, sc_ref, sh_ref, whl_ref, wh_ref, br_ref, hx_ref, meta_ref, cnt_ref,
                  run_ref, sc_st, sh_st, hi_s, lo_s):
    i = pl.program_id(0)

    @pl.when(i == 0)
    def _():
        run_ref[...] = jnp.zeros_like(run_ref)

    _stage_mod(sc_ref, sc_st, i)
    _stage_mod(sh_ref, sh_st, i)
    lg = g_ref[...]
    for s in range(N_SLABS):
        r = _slab(s)
        h = _norm_mod_slab(x_ref[r, :], lg, sc_st[s:s + 1, :], sh_st[s:s + 1, :])
        hx_ref[r, 0:D_MODEL] = h
        hi = h.astype(BF16)
        hi_s[r, :] = hi
        lo_s[r, :] = (h - hi.astype(F32)).astype(BF16)
    l1 = jnp.dot(hi_s[...], whl_ref[...], preferred_element_type=F32)
    l2 = jnp.dot(lo_s[...], wh_ref[...], preferred_element_type=F32)
    logits = l1[:, 0:128] + l1[:, 128:256] + l2 + br_ref[...]
    col = lax.broadcasted_iota(jnp.int32, logits.shape, 1).astype(F32)
    big = 1e9
    is_g = col < N_GROUPS
    gl = jnp.where(is_g, logits, -jnp.inf)
    gmax = jnp.max(gl, axis=-1, keepdims=True)
    gsel = jnp.min(jnp.where(gl == gmax, col, big), axis=-1, keepdims=True)
    gden = jnp.sum(jnp.where(is_g, jnp.exp(logits - gmax), 0.0), axis=-1, keepdims=True)
    g_w = 1.0 / gden
    e0 = N_GROUPS + EXP_PER_GROUP * gsel
    in_grp = (col >= e0) & (col < e0 + EXP_PER_GROUP)
    el = jnp.where(in_grp, logits, -jnp.inf)
    v1 = jnp.max(el, axis=-1, keepdims=True)
    i1 = jnp.min(jnp.where(el == v1, col, big), axis=-1, keepdims=True)
    el2 = jnp.where(col == i1, -jnp.inf, el)
    v2 = jnp.max(el2, axis=-1, keepdims=True)
    i2 = jnp.min(jnp.where(el2 == v2, col, big), axis=-1, keepdims=True)
    t = jnp.exp(v2 - v1)
    p1 = 1.0 / (1.0 + t)
    a1 = p1 * g_w
    a2 = t * p1 * g_w
    meta = jnp.where(col == META_GATE + (i1 - e0), a1, 0.0)
    meta = meta + jnp.where(col == META_GATE + (i2 - e0), a2, 0.0)
    meta = meta + jnp.where(col == META_GSEL, gsel, 0.0)
    rows = logits.shape[0]
    rowi = lax.broadcasted_iota(jnp.int32, (rows, 1), 0)
    is_tok = (i < PROMPT_TILES) | (rowi % SROWS == 0)
    onehot = jnp.where((col == gsel) & is_tok, 1.0, 0.0)
    ri = lax.broadcasted_iota(jnp.int32, (rows, rows), 0)
    ci = lax.broadcasted_iota(jnp.int32, (rows, rows), 1)
    tri = jnp.where(ci <= ri, 1.0, 0.0).astype(BF16)
    incl = jnp.dot(tri, onehot.astype(BF16), preferred_element_type=F32)
    run = run_ref[...]
    rank = jnp.sum(jnp.where(col == gsel, incl + run - 1.0, 0.0), axis=-1, keepdims=True)
    meta = meta + jnp.where(col == META_RANK, rank, 0.0)
    meta_ref[...] = meta
    hx_ref[:, D_MODEL:XW] = meta
    run = run + incl[rows - 1:rows, :]
    run_ref[...] = run
    cnt_ref[...] = run


def _router_weights(w_rg, b_rg, w_re, b_re):
    pad = 128 - N_GROUPS - N_EXPERTS
    wr = jnp.concatenate([w_rg, w_re, jnp.zeros((DEPTH, D_MODEL, pad), F32)], axis=2)
    w_hi = wr.astype(BF16)
    w_lo = (wr - w_hi.astype(F32)).astype(BF16)
    br = jnp.concatenate([b_rg, b_re, jnp.zeros((DEPTH, pad), F32)], axis=1).reshape(DEPTH, 1, 128)
    return jnp.concatenate([w_hi, w_lo], axis=2), w_hi, br


def _route(x, g, mod, whl, w_hi, br, l):
    return pl.pallas_call(
        _route_kernel,
        grid=(N_TILES,),
        in_specs=[
            pl.BlockSpec((TM, D_MODEL), lambda i: (i, 0)),
            pl.BlockSpec((1, D_MODEL), lambda i: (0, 0)),
            _mod_spec(l, "sc2"),
            _mod_spec(l, "sh2"),
            pl.BlockSpec((None, D_MODEL, 256), lambda i: (l, 0, 0)),
            pl.BlockSpec((None, D_MODEL, 128), lambda i: (l, 0, 0)),
            pl.BlockSpec((None, 1, 128), lambda i: (l, 0, 0)),
        ],
        out_specs=[
            pl.BlockSpec((TM, XW), lambda i: (i, 0)),
            pl.BlockSpec((TM, 128), lambda i: (i, 0)),
            pl.BlockSpec((1, 128), lambda i: (0, 0)),
        ],
        out_shape=[
            jax.ShapeDtypeStruct((N_ROWS, XW), F32),
            jax.ShapeDtypeStruct((N_ROWS, 128), F32),
            jax.ShapeDtypeStruct((1, 128), F32),
        ],
        scratch_shapes=[pltpu.VMEM((1, 128), F32),
                        pltpu.VMEM((N_SLABS, D_MODEL), F32), pltpu.VMEM((N_SLABS, D_MODEL), F32),
                        pltpu.VMEM((TM, D_MODEL), BF16), pltpu.VMEM((TM, D_MODEL), BF16)],
        compiler_params=_cparams(("arbitrary",)),
        name="route",
    )(x, g.reshape(1, D_MODEL), mod, mod, whl, w_hi, br)


def _moe_positions(meta, cnt):
    i32 = jnp.int32
    counts = cnt[0, :N_GROUPS].astype(i32)
    ntile = (counts + TM - 1) // TM
    tstart = jnp.cumsum(ntile) - ntile
    gsel = meta[:, META_GSEL].astype(i32)
    rank = meta[:, META_RANK].astype(i32)
    first_row = jnp.zeros_like(gsel)
    for g in range(N_GROUPS):
        first_row = jnp.where(gsel == g, tstart[g] * TM, first_row)
    pos = first_row + rank
    pos_all = jnp.concatenate([pos[:NP_ROWS], jnp.repeat(pos[NP_ROWS::SROWS], SROWS)])
    plan = jnp.concatenate([tstart, ntile]).astype(i32)
    return pos_all, plan


def _scatter_kernel(pos_ref, src_ref, buf_hbm, out_hbm, sem):
    del buf_hbm
    i = pl.program_id(0)
    base = i * TM

    def row_copy(r):
        return pltpu.make_async_copy(src_ref.at[pl.ds(r, 1), :],
                                     out_hbm.at[pl.ds(pos_ref[base + r], 1), :], sem)

    def run(n, stride):
        def issue(k, c):
            row_copy(k * stride).start()
            return c

        def drain(k, c):
            row_copy(k * stride).wait()
            return c

        lax.fori_loop(0, n, issue, 0, unroll=8)
        lax.fori_loop(0, n, drain, 0, unroll=8)

    @pl.when(i < PROMPT_TILES)
    def _():
        run(TM, 1)

    @pl.when(i == PROMPT_TILES)
    def _():
        run(DEC_BATCH, SROWS)


def _moe_scatter(hx, buf, pos_all):
    grid_spec = pltpu.PrefetchScalarGridSpec(
        num_scalar_prefetch=1,
        grid=(N_TILES,),
        in_specs=[pl.BlockSpec((TM, XW), lambda i, pos: (i, 0)), pl.BlockSpec(memory_space=pl.ANY)],
        out_specs=pl.BlockSpec(memory_space=pl.ANY),
        scratch_shapes=[pltpu.SemaphoreType.DMA(())],
    )
    return pl.pallas_call(
        _scatter_kernel,
        grid_spec=grid_spec,
        out_shape=jax.ShapeDtypeStruct((MOE_ROWS, XW), F32),
        input_output_aliases={2: 0},
        compiler_params=_cparams(("arbitrary",)),
        name="moe_scatter",
    )(pos_all, hx, buf)


def _work_item(w, plan_ref, per_tile):
    i32 = jnp.int32
    nt = [plan_ref[N_GROUPS + g] for g in range(N_GROUPS)]
    ts = [plan_ref[g] for g in range(N_GROUPS)]
    ends = []
    acc = 0
    for g in range(N_GROUPS):
        acc = acc + per_tile * nt[g]
        ends.append(acc)
    valid = w < ends[-1]
    wc = jnp.maximum(jnp.minimum(w, ends[-1] - 1), 0)
    g = sum((wc >= ends[k]).astype(i32) for k in range(N_GROUPS - 1))

    def pick(vals):
        out = vals[N_GROUPS - 1]
        for k in range(N_GROUPS - 2, -1, -1):
            out = jnp.where(g == k, vals[k], out)
        return out

    r = wc - pick([0] + ends[:-1])
    ntg = pick(nt)
    c = sum((r >= m * ntg).astype(i32) for m in range(1, per_tile))
    t_in = r - c * ntg
    t = pick(ts) + t_in
    extra = jnp.maximum(w - ends[-1], 0)
    used = ts[N_GROUPS - 1] + nt[N_GROUPS - 1]
    t_out = jnp.where(valid, t, used + extra // per_tile)
    c_out = jnp.where(valid, c, extra % per_tile)
    return g, c, t, t_in == 0, valid, t_out, c_out


HID_EXPERTS = 2
HID_NCOL = EXP_PER_GROUP // HID_EXPERTS
HID_WORK = MOE_TILES * HID_NCOL
HID_VMEM_LIMIT = 56 * 1024 * 1024


def _hidden_kernel(plan_ref, x_ref, w1_ref, w3_ref, o_ref, w1b, w3b):
    _, c, _, first, valid, _, _ = _work_item(pl.program_id(0), plan_ref, HID_NCOL)

    @pl.when(jnp.logical_not(valid))
    def _():
        o_ref[...] = jnp.zeros_like(o_ref)

    @pl.when(valid & first)
    def _():
        for j in range(HID_EXPERTS):
            cs = slice(j * D_EXPERT, (j + 1) * D_EXPERT)
            w1b[:, cs] = w1_ref[j].astype(BF16)
            w3b[:, cs] = w3_ref[j].astype(BF16)

    @pl.when(valid)
    def _():
        x = x_ref[:, 0:D_MODEL].astype(BF16)
        m = x_ref[:, D_MODEL:XW]
        col = lax.broadcasted_iota(jnp.int32, m.shape, 1)
        for j in range(HID_EXPERTS):
            cs = slice(j * D_EXPERT, (j + 1) * D_EXPERT)
            a = jnp.dot(x, w1b[:, cs], preferred_element_type=F32)
            b = jnp.dot(x, w3b[:, cs], preferred_element_type=F32)
            lane = META_GATE + c * HID_EXPERTS + j
            gate = jnp.sum(jnp.where(col == lane, m, 0.0), axis=-1, keepdims=True)
            o_ref[:, cs] = (_silu(a) * b * gate).astype(o_ref.dtype)


def _moe_hidden(xs, w1, w3, plan, l):
    def item(w, plan):
        return _work_item(w, plan, HID_NCOL)

    def w_map(w, plan):
        it = item(w, plan)
        return (l, it[0] * HID_NCOL + it[1], 0, 0)

    wide = HID_EXPERTS * D_EXPERT
    grid_spec = pltpu.PrefetchScalarGridSpec(
        num_scalar_prefetch=1,
        grid=(HID_WORK,),
        in_specs=[
            pl.BlockSpec((TM, XW), lambda w, plan: (item(w, plan)[2], 0)),
            pl.BlockSpec((None, HID_EXPERTS, D_MODEL, D_EXPERT), w_map),
            pl.BlockSpec((None, HID_EXPERTS, D_MODEL, D_EXPERT), w_map),
        ],
        out_specs=pl.BlockSpec((TM, wide), lambda w, plan: item(w, plan)[5:7]),
        scratch_shapes=[pltpu.VMEM((D_MODEL, wide), BF16), pltpu.VMEM((D_MODEL, wide), BF16)],
    )
    return pl.pallas_call(
        _hidden_kernel,
        grid_spec=grid_spec,
        out_shape=jax.ShapeDtypeStruct((MOE_ROWS, EXP_PER_GROUP * D_EXPERT), BF16),
        compiler_params=pltpu.CompilerParams(dimension_semantics=("arbitrary",),
                                             vmem_limit_bytes=HID_VMEM_LIMIT),
        name="moe_hidden",
    )(plan, xs, w1, w3)


def _down_kernel(plan_ref, h_ref, w_ref, o_ref, wb):
    _, _, _, first, valid, _, _ = _work_item(pl.program_id(0), plan_ref, DOWN_NCOL)

    @pl.when(jnp.logical_not(valid))
    def _():
        o_ref[...] = jnp.zeros_like(o_ref)

    @pl.when(valid & first)
    def _():
        wb[...] = w_ref[...].astype(BF16)

    @pl.when(valid)
    def _():
        o_ref[...] = jnp.dot(h_ref[...], wb[...], preferred_element_type=F32)


def _moe_down(hid, w2g, plan, l):
    kdim = EXP_PER_GROUP * D_EXPERT

    def item(w, plan):
        return _work_item(w, plan, DOWN_NCOL)

    grid_spec = pltpu.PrefetchScalarGridSpec(
        num_scalar_prefetch=1,
        grid=(DOWN_WORK,),
        in_specs=[
            pl.BlockSpec((TM, kdim), lambda w, plan: (item(w, plan)[2], 0)),
            pl.BlockSpec((None, None, kdim, DOWN_TN),
                         lambda w, plan: (l, item(w, plan)[0], 0, item(w, plan)[1])),
        ],
        out_specs=pl.BlockSpec((TM, DOWN_TN), lambda w, plan: item(w, plan)[5:7]),
        scratch_shapes=[pltpu.VMEM((kdim, DOWN_TN), BF16)],
    )
    return pl.pallas_call(
        _down_kernel,
        grid_spec=grid_spec,
        out_shape=jax.ShapeDtypeStruct((MOE_ROWS, D_MODEL), F32),
        compiler_params=pltpu.CompilerParams(dimension_semantics=("arbitrary",),
                                             vmem_limit_bytes=HID_VMEM_LIMIT),
        name="moe_down",
    )(plan, hid, w2g)


def _res_kernel(pos_ref, x_ref, y_hbm, g2_ref, *rest, with_norm):
    if with_norm:
        lg_ref, sc_ref, sh_ref, xo_ref, ho_ref, ybuf, sem, g2_st, sc_st, sh_st = rest
    else:
        yp_ref, ys_ref, ybuf, sem, g2_st = rest
    i = pl.program_id(0)
    slot = i % 2

    def row_copy(tile, k, sl):
        return pltpu.make_async_copy(y_hbm.at[pl.ds(pos_ref[tile * TM + k], 1), :],
                                     ybuf.at[sl, pl.ds(k, 1), :], sem.at[sl])

    def issue_tile(tile, sl):
        def issue(k, c):
            row_copy(tile, k, sl).start()
            return c
        lax.fori_loop(0, TM, issue, 0, unroll=8)

    @pl.when(i == 0)
    def _():
        issue_tile(0, 0)

    @pl.when(i + 1 < N_TILES)
    def _():
        issue_tile(i + 1, 1 - slot)

    _stage_mod(g2_ref, g2_st, i)
    if with_norm:
        _stage_mod(sc_ref, sc_st, i)
        _stage_mod(sh_ref, sh_st, i)

    def drain(k, c):
        row_copy(i, k, slot).wait()
        return c

    lax.fori_loop(0, TM, drain, 0, unroll=8)

    def new_x(s):
        r = _slab(s)
        return x_ref[r, :] + g2_st[s:s + 1, :] * ybuf[slot, r, :]

    if with_norm:
        lg = lg_ref[...]
        for s in range(N_SLABS):
            xn = new_x(s)
            xo_ref[_slab(s), :] = xn
            hn = _norm_mod_slab(xn, lg, sc_st[s:s + 1, :], sh_st[s:s + 1, :])
            ho_ref[_slab(s), :] = hn.astype(ho_ref.dtype)
    else:
        @pl.when(i < PROMPT_TILES)
        def _():
            for s in range(N_SLABS):
                yp_ref[_slab(s), :] = new_x(s)

        @pl.when(i == PROMPT_TILES)
        def _():
            for s in range(N_SLABS):
                ys_ref[s:s + 1, :] = new_x(s)[0:1, :]


def _residual(x, y_sorted, pos_all, mod, l, next_g=None):
    with_norm = next_g is not None
    row_spec = pl.BlockSpec((TM, D_MODEL), lambda i, pos: (i, 0))
    in_specs = [row_spec, pl.BlockSpec(memory_space=pl.ANY), _mod_spec(l, "g2")]
    args = [pos_all, x, y_sorted, mod]
    stage = pltpu.VMEM((N_SLABS, D_MODEL), F32)
    scratch = [pltpu.VMEM((2, TM, D_MODEL), F32), pltpu.SemaphoreType.DMA((2,)), stage]
    if with_norm:
        in_specs += [pl.BlockSpec((1, D_MODEL), lambda i, pos: (0, 0)),
                     _mod_spec(l + 1, "sc1"), _mod_spec(l + 1, "sh1")]
        args += [next_g.reshape(1, D_MODEL), mod, mod]
        out_specs = [row_spec, row_spec]
        out_shape = [jax.ShapeDtypeStruct((N_ROWS, D_MODEL), F32),
                     jax.ShapeDtypeStruct((N_ROWS, D_MODEL), BF16)]
        scratch += [stage, stage]
    else:
        out_specs = [pl.BlockSpec((TM, D_MODEL), lambda i, pos: (jnp.minimum(i, PROMPT_TILES - 1), 0)),
                     pl.BlockSpec((DEC_BATCH, D_MODEL), lambda i, pos: (0, 0))]
        out_shape = [jax.ShapeDtypeStruct((NP_ROWS, D_MODEL), F32),
                     jax.ShapeDtypeStruct((DEC_BATCH, D_MODEL), F32)]
    grid_spec = pltpu.PrefetchScalarGridSpec(
        num_scalar_prefetch=1,
        grid=(N_TILES,),
        in_specs=in_specs,
        out_specs=out_specs,
        scratch_shapes=scratch,
    )
    return pl.pallas_call(
        functools.partial(_res_kernel, with_norm=with_norm),
        grid_spec=grid_spec,
        out_shape=out_shape,
        compiler_params=_cparams(("arbitrary",)),
        name="residual_norm" if with_norm else "residual",
    )(*args)


def kernel(x_prompt, x_sample, c_prompt, c_sample, cache_swa_k, cache_swa_v, state_ret, state_hgrn,
           w_ada, b_ada, ln1_g, w_in, qn_g, kn_g, attn_sinks, ret_gn_g, ret_gn_b, hgrn_lb, hgrn_ng,
           w_out, ln2_g, w_rg, b_rg, w_re, b_re, w1, w3, w2):
    i32 = jnp.int32
    lbp = jax.nn.softmax(hgrn_lb.astype(F32), axis=0)
    lb_all = jnp.cumsum(lbp, axis=0) - lbp[0]

    assert TM // SROWS == DEC_BATCH and NS_ROWS == TM
    c_all = jnp.concatenate([c_prompt, jnp.zeros((C_SAMPLE0 - BATCH, D_MODEL), F32), c_sample], axis=0)
    mod = _ada(c_all, w_ada, b_ada)

    ret_tab_p = _ret_tables(RET_CHUNK, RET_CHUNK, 0.0, SEQ)
    ret_tab_s = _ret_tables(SROWS, 1, float(PAST_LEN), SROWS)
    w2g = w2.reshape(DEPTH, N_GROUPS, EXP_PER_GROUP * D_EXPERT, D_MODEL)
    whl_all, whi_all, br_all = _router_weights(w_rg, b_rg, w_re, b_re)
    st_s_all = jnp.swapaxes(state_hgrn, -1, -2).reshape(DEPTH, DEC_BATCH, C_HEADS * C_DV, C_DK)
    ck_all = cache_swa_k.reshape(DEPTH, DEC_BATCH * WINDOW, A_KV * A_HD)
    cv_all = cache_swa_v.reshape(DEPTH, DEC_BATCH * WINDOW, A_KV * A_HD)
    xs_buf = jnp.zeros((MOE_ROWS, XW), F32)

    x, h = _norm_mod_first(x_prompt.reshape(NP_ROWS, D_MODEL), x_sample.reshape(DEC_BATCH, D_MODEL),
                           ln1_g[0], mod, 0)

    outs = {k: [] for k in ("kp", "vp", "rp", "hp", "ks", "vs", "rs", "hs")}
    for l in range(DEPTH):
        proj = _mm_in(h, w_in, l)

        oa, kp, vp = _swa(proj, proj, proj, attn_sinks[l], qn_g[l], kn_g[l], tq=WINDOW, nb=SWA_NB,
                          n_seq=BATCH, steps_per_seq=SEQ // (WINDOW * SWA_NB), row_block0=0, chain=True)
        srows = SAMPLE_NB * SROWS
        oa, ks, vs = _swa(proj, ck_all, cv_all, attn_sinks[l], qn_g[l], kn_g[l], tq=SROWS, nb=SAMPLE_NB,
                          n_seq=DEC_BATCH, steps_per_seq=1, row_block0=NP_ROWS // srows, chain=False,
                          layer=l, alias_buf=oa)
        zero_ret = jnp.zeros((BATCH, B_HEADS, B_DK, B_DV), F32)
        ob, rp = _ret(proj, zero_ret, ret_gn_g[l], ret_gn_b[l], ret_tab_p, chunk=RET_CHUNK, nb=1,
                      n_seq=BATCH, chunks_per_seq=SEQ // RET_CHUNK, row_block0=0, table_per_chunk=True)
        ob, rs = _ret(proj, state_ret[l], ret_gn_g[l], ret_gn_b[l], ret_tab_s, chunk=SROWS, nb=SAMPLE_NB,
                      n_seq=DEC_BATCH, chunks_per_seq=1, row_block0=NP_ROWS // srows,
                      table_per_chunk=False, alias_buf=ob)
        zero_hg = jnp.zeros((BATCH, C_HEADS * C_DV, C_DK), F32)
        oc, hp_t = _hgrn(proj, zero_hg, lb_all[l], hgrn_ng[l], rows=HG_TILE, seq_rows=HG_TILE,
                         tv=HG_TILE, n_seq=BATCH, tiles_per_seq=SEQ // HG_TILE, row_block0=0)
        oc, hs_t = _hgrn(proj, st_s_all[l], lb_all[l], hgrn_ng[l], rows=srows, seq_rows=SROWS, tv=1,
                         n_seq=DEC_BATCH, tiles_per_seq=1, row_block0=NP_ROWS // srows, alias_buf=oc)

        x1 = _mm_out(oa, ob, oc, w_out, x, mod, l)

        hx, meta, cnt = _route(x1, ln2_g[l], mod, whl_all, whi_all, br_all, l)
        pos_all, plan = _moe_positions(meta, cnt)
        xs_buf = _moe_scatter(hx, xs_buf, pos_all)
        hid = _moe_hidden(xs_buf, w1, w3, plan, l)
        y_sorted = _moe_down(hid, w2g, plan, l)
        if l + 1 < DEPTH:
            x, h = _residual(x1, y_sorted, pos_all, mod, l, ln1_g[l + 1])
        else:
            y_prompt, y_sample = _residual(x1, y_sorted, pos_all, mod, l)

        outs["kp"].append(kp.reshape(BATCH, WINDOW, A_KV, A_HD))
        outs["vp"].append(vp.reshape(BATCH, WINDOW, A_KV, A_HD))
        outs["rp"].append(rp)
        outs["hp"].append(hp_t.reshape(BATCH, C_HEADS, C_DV, C_DK))
        outs["ks"].append(ks.reshape(DEC_BATCH, 1, A_KV, A_HD))
        outs["vs"].append(vs.reshape(DEC_BATCH, 1, A_KV, A_HD))
        outs["rs"].append(rs)
        outs["hs"].append(hs_t.reshape(DEC_BATCH, C_HEADS, C_DV, C_DK))

    y_prompt = y_prompt.reshape(BATCH, SEQ, D_MODEL)
    y_sample = y_sample.reshape(DEC_BATCH, 1, D_MODEL)
    st = lambda k: jnp.stack(outs[k])
    un_t = lambda k: jnp.swapaxes(st(k), -1, -2)
    roll_in = lambda cache, k: jnp.concatenate([cache[:, :, 1:], st(k)], axis=2)
    return (y_prompt, y_sample, st("kp"), st("vp"), st("rp"), un_t("hp"),
            roll_in(cache_swa_k, "ks"), roll_in(cache_swa_v, "vs"), st("rs"), un_t("hs"))
```

```python
import functools

import numpy as np
import jax
import jax.numpy as jnp
from jax import lax
from jax.experimental import pallas as pl
from jax.experimental.pallas import tpu as pltpu

F32 = jnp.float32
BF16 = jnp.bfloat16

D_MODEL = 2048
BATCH = 2
SEQ = 4096
DEPTH = 4
DEC_BATCH = 32
PAST_LEN = 16384
A_HD = 128
A_HEADS = 8
A_KV = 2
A_GROUP = 4
WINDOW = 128
B_DV = 128
B_DK = 64
B_HEADS = 4
C_DK = 128
C_DV = 128
C_HEADS = 4
ROPE_BASE = 10000.0
N_GROUPS = 4
EXP_PER_GROUP = 4
N_EXPERTS = 16
D_EXPERT = 512
EPS = 1e-6
NEG_BIG = -1e30
IN_WIDTH = 5120

SROWS = 16
NP_ROWS = BATCH * SEQ
NS_ROWS = DEC_BATCH * SROWS
N_ROWS = NP_ROWS + NS_ROWS
TM = 512
N_TILES = N_ROWS // TM
MOD_ROWS = 16
N_TOK = NP_ROWS + DEC_BATCH
MOE_TILES = (N_TOK + N_GROUPS * (TM - 1)) // TM
MOE_ROWS = MOE_TILES * TM
DOWN_TN = 2048
DOWN_NCOL = D_MODEL // DOWN_TN
DOWN_WORK = MOE_TILES * DOWN_NCOL

RET_CHUNK = 256
HG_TILE = 256
HG_BLK = 16
SAMPLE_NB = 8
SWA_NB = 2

VMEM_LIMIT = 48 * 1024 * 1024
BIG_VMEM_LIMIT = 56 * 1024 * 1024


def _cparams(sem, limit=VMEM_LIMIT):
    return pltpu.CompilerParams(dimension_semantics=sem, vmem_limit_bytes=limit)


def _sigmoid(x):
    return 1.0 / (1.0 + jnp.exp(-x))


def _silu(x):
    return x * _sigmoid(x)


C_ROWS = 40
C_SAMPLE0 = 8
PROMPT_TILES = NP_ROWS // TM
MOD_COL = dict(sh1=0, sc1=1, g1=2, sh2=3, sc2=4, g2=5)


def _tile_mod(m_ref, i):
    n = m_ref.shape[-1]
    seq = jnp.minimum(i // (SEQ // TM), BATCH - 1)
    prow = jnp.broadcast_to(m_ref[pl.ds(seq, 1), :], (DEC_BATCH, n))
    srows = m_ref[C_SAMPLE0:C_SAMPLE0 + DEC_BATCH, :]
    m = jnp.where(i < PROMPT_TILES, prow, srows)
    return jnp.broadcast_to(m[:, None, :], (DEC_BATCH, SROWS, n)).reshape(TM, n)


N_SLABS = TM // SROWS


def _stage_mod(m_ref, stage_ref, i):
    n = m_ref.shape[-1]
    seq = jnp.minimum(i // (SEQ // TM), BATCH - 1)
    prow = jnp.broadcast_to(m_ref[pl.ds(seq, 1), :], (N_SLABS, n))
    srows = m_ref[C_SAMPLE0:C_SAMPLE0 + DEC_BATCH, :]
    stage_ref[...] = jnp.where(i < PROMPT_TILES, prow, srows)


def _slab(s):
    return slice(s * SROWS, (s + 1) * SROWS)


def _mod_spec(l, name, tn=D_MODEL, colmap=None):
    base = MOD_COL[name] * (D_MODEL // tn)
    if colmap is None:
        return pl.BlockSpec((None, C_ROWS, tn), lambda *ids: (l, 0, base))
    return pl.BlockSpec((None, C_ROWS, tn), lambda *ids: (l, 0, base + colmap(*ids)))


def _ada_kernel(c_ref, w_ref, b_ref, o_ref):
    c = _silu(c_ref[...])
    o_ref[...] = jnp.dot(c.astype(BF16), w_ref[...].astype(BF16),
                         preferred_element_type=F32) + b_ref[...]


def _ada(c_all, w_ada, b_ada):
    rows = c_all.shape[0]
    tn = 2048
    ncol = w_ada.shape[2] // tn
    return pl.pallas_call(
        _ada_kernel,
        grid=(DEPTH, ncol),
        in_specs=[
            pl.BlockSpec((rows, D_MODEL), lambda l, j: (0, 0)),
            pl.BlockSpec((None, D_MODEL, tn), lambda l, j: (l, 0, j)),
            pl.BlockSpec((None, 1, tn), lambda l, j: (l, 0, j)),
        ],
        out_specs=pl.BlockSpec((None, rows, tn), lambda l, j: (l, 0, j)),
        out_shape=jax.ShapeDtypeStruct((DEPTH, rows, w_ada.shape[2]), F32),
        compiler_params=_cparams(("arbitrary", "arbitrary")),
        name="ada",
    )(c_all, w_ada, b_ada.reshape(DEPTH, 1, -1))


def _norm_mod_rows(x, g, sc_ref, sh_ref, i):
    y = x * lax.rsqrt(jnp.mean(x * x, axis=-1, keepdims=True) + EPS) * g
    return y * (1.0 + _tile_mod(sc_ref, i)) + _tile_mod(sh_ref, i)


def _norm_mod_slab(x, g, sc_row, sh_row):
    y = x * lax.rsqrt(jnp.mean(x * x, axis=-1, keepdims=True) + EPS) * g
    return y * (1.0 + sc_row) + sh_row


def _norm_kernel(xp_ref, xs_ref, g_ref, sc_ref, sh_ref, x_ref, o_ref):
    i = pl.program_id(0)
    xs = xs_ref[...]
    rid = lax.broadcasted_iota(jnp.int32, (DEC_BATCH, SROWS, 1), 1)
    blk = jnp.where(rid == 0, jnp.broadcast_to(xs[:, None, :], (DEC_BATCH, SROWS, D_MODEL)), 0.0)
    x = jnp.where(i < PROMPT_TILES, xp_ref[...], blk.reshape(TM, D_MODEL))
    x_ref[...] = x
    o_ref[...] = _norm_mod_rows(x, g_ref[...], sc_ref, sh_ref, i).astype(o_ref.dtype)


def _norm_mod_first(x_prompt, x_sample, g, mod, l):
    return pl.pallas_call(
        _norm_kernel,
        grid=(N_TILES,),
        in_specs=[
            pl.BlockSpec((TM, D_MODEL), lambda i: (jnp.minimum(i, PROMPT_TILES - 1), 0)),
            pl.BlockSpec((DEC_BATCH, D_MODEL), lambda i: (0, 0)),
            pl.BlockSpec((1, D_MODEL), lambda i: (0, 0)),
            _mod_spec(l, "sc1"),
            _mod_spec(l, "sh1"),
        ],
        out_specs=[
            pl.BlockSpec((TM, D_MODEL), lambda i: (i, 0)),
            pl.BlockSpec((TM, D_MODEL), lambda i: (i, 0)),
        ],
        out_shape=[
            jax.ShapeDtypeStruct((N_ROWS, D_MODEL), F32),
            jax.ShapeDtypeStruct((N_ROWS, D_MODEL), BF16),
        ],
        compiler_params=_cparams(("arbitrary",)),
        name="norm_first",
    )(x_prompt, x_sample, g.reshape(1, D_MODEL), mod, mod)


def _mm_in_kernel(x_ref, w_ref, o_ref, wb_ref):
    @pl.when(pl.program_id(1) == 0)
    def _():
        wb_ref[...] = w_ref[...].astype(BF16)

    o_ref[...] = jnp.dot(x_ref[...], wb_ref[...], preferred_element_type=F32)


MM_IN_TILES = 8


def _mm_in(h, w_in, l):
    tn = 1280
    ncol = IN_WIDTH // tn
    tm = N_ROWS // MM_IN_TILES
    return pl.pallas_call(
        _mm_in_kernel,
        grid=(ncol, MM_IN_TILES),
        in_specs=[
            pl.BlockSpec((tm, D_MODEL), lambda j, i: (i, 0)),
            pl.BlockSpec((None, D_MODEL, tn), lambda j, i: (l, 0, j)),
        ],
        out_specs=pl.BlockSpec((tm, tn), lambda j, i: (i, j)),
        out_shape=jax.ShapeDtypeStruct((N_ROWS, IN_WIDTH), F32),
        scratch_shapes=[pltpu.VMEM((D_MODEL, tn), BF16)],
        compiler_params=_cparams(("arbitrary", "arbitrary"), BIG_VMEM_LIMIT),
        name="mm_in",
    )(h, w_in)


def _rms(x, g):
    return x * lax.rsqrt(jnp.mean(x * x, axis=-1, keepdims=True) + EPS) * g


def _swa_kernel(sink_ref, q_ref, k_ref, v_ref, pk_ref, pv_ref, qg_ref, kg_ref,
                *rest, tq, nb, chain):
    o_ref, kc_ref, vc_ref = rest[-3:]
    qg = qg_ref[...]
    kg = kg_ref[...]
    nk = WINDOW + tq
    nq = A_GROUP * tq
    row = lax.broadcasted_iota(jnp.int32, (nq, nk), 0) & (tq - 1)
    col = lax.broadcasted_iota(jnp.int32, (nq, nk), 1)
    vis_prev = (col < WINDOW) & (col >= row)
    vis_own = (col >= WINDOW) & ((col - WINDOW) <= row)
    mask_all = vis_prev | vis_own
    if chain:
        mask_first = (vis_prev & (pl.program_id(1) > 0)) | vis_own
        last_step = pl.program_id(1) == pl.num_programs(1) - 1
    rgrp = lax.broadcasted_iota(jnp.int32, (nq, 1), 0) // tq
    last = [None] * A_KV
    for sb in range(nb):
        rs = slice(sb * tq, (sb + 1) * tq)
        mask = mask_first if (chain and sb == 0) else mask_all
        for kv in range(A_KV):
            ks = slice(kv * A_HD, (kv + 1) * A_HD)
            kn = _rms(k_ref[rs, ks], kg)
            vn = v_ref[rs, ks]
            if chain and sb > 0:
                pk, pv = last[kv]
            elif chain:
                pk = _rms(pk_ref[:, ks], kg)
                pv = pv_ref[:, ks]
            else:
                ps = slice(sb * WINDOW, (sb + 1) * WINDOW)
                pk = pk_ref[ps, ks]
                pv = pv_ref[ps, ks]
            if chain:
                if sb == nb - 1:
                    @pl.when(last_step)
                    def _():
                        kc_ref[:, ks] = kn
                        vc_ref[:, ks] = vn
            else:
                kc_ref[sb:sb + 1, ks] = kn[0:1, :]
                vc_ref[sb:sb + 1, ks] = vn[0:1, :]
            last[kv] = (kn, vn)
            keys = jnp.concatenate([pk, kn], axis=0).astype(BF16)
            vals = jnp.concatenate([pv, vn], axis=0).astype(BF16)
            h0 = kv * A_GROUP
            qs = jnp.concatenate(
                [_rms(q_ref[rs, (h0 + g) * A_HD:(h0 + g + 1) * A_HD], qg) for g in range(A_GROUP)],
                axis=0).astype(BF16)
            s = lax.dot_general(qs, keys, (((1,), (1,)), ((), ())),
                                preferred_element_type=F32) * (A_HD ** -0.5)
            s = jnp.where(mask, s, NEG_BIG)
            snk = jnp.full((nq, 1), sink_ref[h0 + A_GROUP - 1], F32)
            for g in range(A_GROUP - 2, -1, -1):
                snk = jnp.where(rgrp == g, sink_ref[h0 + g], snk)
            m = jnp.maximum(jnp.max(s, axis=-1, keepdims=True), snk)
            p = jnp.exp(s - m)
            den = jnp.sum(p, axis=-1, keepdims=True) + jnp.exp(snk - m)
            o = jnp.dot(p.astype(BF16), vals, preferred_element_type=F32) / den
            for g in range(A_GROUP):
                o_ref[rs, (h0 + g) * A_HD:(h0 + g + 1) * A_HD] = o[g * tq:(g + 1) * tq].astype(o_ref.dtype)


def _swa(proj, prev_k, prev_v, sinks, qn_g, kn_g, *, tq, nb, n_seq, steps_per_seq, row_block0,
         chain, layer=None):
    qcol, kcol, vcol = 0, 1024 // 256, 1280 // 256
    assert chain or steps_per_seq == 1
    rows = tq * nb

    def rb(s, n):
        return row_block0 + s * steps_per_seq + n

    if chain:
        def before(s, n):
            return jnp.maximum(rb(s, n) * nb - 1, rb(s, 0) * nb)
        prev_specs = [pl.BlockSpec((WINDOW, 256), lambda s, n: (before(s, n), kcol)),
                      pl.BlockSpec((WINDOW, 256), lambda s, n: (before(s, n), vcol))]
        cache_spec = pl.BlockSpec((None, WINDOW, 256), lambda s, n: (s, 0, 0))
        cache_shape = jax.ShapeDtypeStruct((n_seq, WINDOW, 256), F32)
    else:
        prev_specs = [pl.BlockSpec((None, WINDOW * nb, 256), lambda s, n: (layer, s, 0))] * 2
        cache_spec = pl.BlockSpec((nb, 256), lambda s, n: (s, 0))
        cache_shape = jax.ShapeDtypeStruct((n_seq, 256), F32)

    in_specs = [
        pl.BlockSpec(memory_space=pltpu.SMEM),
        pl.BlockSpec((rows, 1024), lambda s, n: (rb(s, n), qcol)),
        pl.BlockSpec((rows, 256), lambda s, n: (rb(s, n), kcol)),
        pl.BlockSpec((rows, 256), lambda s, n: (rb(s, n), vcol)),
        *prev_specs,
        pl.BlockSpec((1, A_HD), lambda s, n: (0, 0)),
        pl.BlockSpec((1, A_HD), lambda s, n: (0, 0)),
    ]
    args = [sinks, proj, proj, proj, prev_k, prev_v, qn_g.reshape(1, A_HD), kn_g.reshape(1, A_HD)]
    n_steps0 = n_seq if chain else n_seq // nb
    return pl.pallas_call(
        functools.partial(_swa_kernel, tq=tq, nb=nb, chain=chain),
        grid=(n_steps0, steps_per_seq),
        in_specs=in_specs,
        out_specs=[pl.BlockSpec((rows, 1024), lambda s, n: (s * steps_per_seq + n, 0)),
                   cache_spec, cache_spec],
        out_shape=[jax.ShapeDtypeStruct((n_steps0 * steps_per_seq * rows, 1024), BF16),
                   cache_shape, cache_shape],
        compiler_params=_cparams(("arbitrary", "arbitrary")),
        name="swa_prompt" if chain else "swa_sample",
    )(*args)


def _ret_tables(chunk, tv, pos0, t_len):
    lg = np.log1p(-(2.0 ** (-5.0 - np.arange(B_HEADS, dtype=np.float64))))
    idx = np.arange(chunk, dtype=np.float64)
    valid = idx < tv
    diff = idx[:, None] - idx[None, :]
    dmat = np.where((diff >= 0) & valid[:, None] & valid[None, :],
                    np.exp(np.maximum(diff, 0.0)[None] * lg[:, None, None]), 0.0)
    din = np.exp((idx + 1.0)[:, None] * lg[None])
    din = np.repeat(din, B_DV, axis=1)
    kfac = np.where(valid[:, None], np.exp((tv - 1.0 - idx)[:, None] * lg[None]), 0.0)
    kfac = np.repeat(kfac, B_DK, axis=1)
    gc = np.repeat(np.exp(tv * lg), B_DK)[:, None] * np.ones((1, B_DV))
    half = B_DK // 2
    inv = ROPE_BASE ** (-np.arange(half, dtype=np.float64) / half)
    ang = (pos0 + np.arange(t_len, dtype=np.float64))[:, None] * inv[None]
    cos = np.tile(np.cos(ang), (1, 2 * B_HEADS))
    sin = np.tile(np.concatenate([-np.sin(ang), np.sin(ang)], axis=1), (1, B_HEADS))
    f = lambda a: jnp.asarray(a, dtype=F32)
    return f(dmat), f(din), f(kfac), f(gc), f(cos), f(sin)


def _ret_kernel(q_ref, k_ref, v_ref, g_ref, cos_ref, sin_ref, dmat_ref, din_ref, kfac_ref,
                gc_ref, gng_ref, gnb_ref, s0_ref, *rest, nb, chunk):
    o_ref, s_ref = rest[-2], rest[-1]

    @pl.when(pl.program_id(1) == 0)
    def _():
        s_ref[...] = s0_ref[...]

    cos = cos_ref[...]
    sin = sin_ref[...]
    width = B_HEADS * B_DK
    lane = lax.broadcasted_iota(jnp.int32, (1, width), 1)
    first_half = (lane % B_DK) < (B_DK // 2)
    lane_head = lane // B_DK

    def rope(x):
        partner = jnp.where(first_half, pltpu.roll(x, width - B_DK // 2, 1),
                            pltpu.roll(x, B_DK // 2, 1))
        return x * cos + partner * sin

    gc = gc_ref[...]
    for sb in range(nb):
        rws = slice(sb * chunk, (sb + 1) * chunk)
        qr = rope(q_ref[rws, :])
        kr = rope(k_ref[rws, :]) * (B_DK ** -0.5)
        kb = kr.astype(BF16)
        kdec = (kr * kfac_ref[...]).astype(BF16)
        s_old = s_ref[sb]
        s_b = s_old.astype(BF16)
        for h in range(B_HEADS):
            vs = slice(h * B_DV, (h + 1) * B_DV)
            rs = slice(h * B_DK, (h + 1) * B_DK)
            qm = jnp.where(lane_head == h, qr, 0.0).astype(BF16)
            a = lax.dot_general(qm, kb, (((1,), (1,)), ((), ())),
                                preferred_element_type=F32) * dmat_ref[h]
            vh = v_ref[rws, vs].astype(BF16)
            o = jnp.dot(a.astype(BF16), vh, preferred_element_type=F32)
            o = o + jnp.dot(qm, s_b, preferred_element_type=F32) * din_ref[:, vs]
            u = lax.dot_general(kdec, vh, (((0,), (0,)), ((), ())), preferred_element_type=F32)
            s_ref[sb, rs, :] = gc[rs, :] * s_old[rs, :] + u[rs, :]
            mu = jnp.mean(o, axis=-1, keepdims=True)
            oc = o - mu
            var = jnp.mean(oc * oc, axis=-1, keepdims=True)
            y = oc * lax.rsqrt(var + EPS) * gng_ref[:, vs] + gnb_ref[:, vs]
            y = y * _silu(g_ref[rws, vs])
            o_ref[rws, vs] = y.astype(o_ref.dtype)


def _ret(proj, s0, gn_g, gn_b, tables, *, chunk, nb, n_seq, chunks_per_seq, row_block0,
         table_per_chunk):
    dmat, din, kfac, gc, cos, sin = tables
    qcol, kcol, vcol, gcol = 1536 // 256, 1792 // 256, 2048 // 512, 2560 // 512
    assert nb == 1 or chunks_per_seq == 1
    rows = nb * chunk

    def rb(s, c):
        return row_block0 + s * chunks_per_seq + c

    tmap = (lambda s, c: (c, 0)) if table_per_chunk else (lambda s, c: (0, 0))
    const2 = lambda s, c: (0, 0)
    in_specs = [
        pl.BlockSpec((rows, 256), lambda s, c: (rb(s, c), qcol)),
        pl.BlockSpec((rows, 256), lambda s, c: (rb(s, c), kcol)),
        pl.BlockSpec((rows, 512), lambda s, c: (rb(s, c), vcol)),
        pl.BlockSpec((rows, 512), lambda s, c: (rb(s, c), gcol)),
        pl.BlockSpec((chunk, 256), tmap),
        pl.BlockSpec((chunk, 256), tmap),
        pl.BlockSpec((B_HEADS, chunk, chunk), lambda s, c: (0, 0, 0)),
        pl.BlockSpec((chunk, 512), const2),
        pl.BlockSpec((chunk, 256), const2),
        pl.BlockSpec((256, B_DV), const2),
        pl.BlockSpec((1, 512), const2),
        pl.BlockSpec((1, 512), const2),
        pl.BlockSpec((nb, 256, B_DV), lambda s, c: (s, 0, 0)),
    ]
    args = [proj, proj, proj, proj, cos, sin, dmat, din, kfac, gc,
            gn_g.reshape(1, 512), gn_b.reshape(1, 512), s0.reshape(n_seq, 256, B_DV)]
    o, s_new = pl.pallas_call(
        functools.partial(_ret_kernel, nb=nb, chunk=chunk),
        grid=(n_seq // nb, chunks_per_seq),
        in_specs=in_specs,
        out_specs=[
            pl.BlockSpec((rows, 512), lambda s, c: (s * chunks_per_seq + c, 0)),
            pl.BlockSpec((nb, 256, B_DV), lambda s, c: (s, 0, 0)),
        ],
        out_shape=[
            jax.ShapeDtypeStruct((n_seq // nb * chunks_per_seq * rows, 512), BF16),
            jax.ShapeDtypeStruct((n_seq, 256, B_DV), F32),
        ],
        compiler_params=_cparams(("arbitrary", "arbitrary")),
        name="ret_prompt" if nb == 1 else "ret_sample",
    )(*args)
    return o, s_new.reshape(n_seq, B_HEADS, B_DK, B_DV)


HG_FAST = 32
HG_LIM = 60.0
HG_HDR = 32


def _hgrn_kernel(q_ref, f_ref, i_ref, g_ref, lb_ref, ng_ref, st0_ref, *rest, rows, seq_rows, tv):
    use_fast = seq_rows % HG_FAST == 0
    if use_fast:
        o_ref, st_ref, q_s, k_s, cs_s, qd_s, ki_s, kd_s, eb_s, stb_s = rest[-10:]
    else:
        o_ref, st_ref, q_s, k_s, cs_s = rest[-5:]
    nseq = rows // seq_rows

    @pl.when(pl.program_id(1) == 0)
    def _():
        st_ref[...] = st0_ref[...]

    z = f_ref[...]
    lb = lb_ref[...]
    sig = _sigmoid(z)
    lf = jnp.log(lb + (1.0 - lb) * sig)
    k = (1.0 - lb) * _sigmoid(-z)
    if tv < seq_rows:
        rvalid = (lax.broadcasted_iota(jnp.int32, (rows, 1), 0) % seq_rows) < tv
        lf = jnp.where(rvalid, lf, 0.0)
        k = jnp.where(rvalid, k, 0.0)
    ri = lax.broadcasted_iota(jnp.int32, (rows, rows), 0)
    ci = lax.broadcasted_iota(jnp.int32, (rows, rows), 1)
    tri = (ci <= ri).astype(F32)
    cs = jnp.dot(tri, lf, precision=lax.Precision.HIGHEST, preferred_element_type=F32)
    q_s[...] = _silu(q_ref[...])
    k_s[...] = k
    cs_s[0:HG_HDR, :] = jnp.zeros((HG_HDR, C_HEADS * C_DK), F32)
    cs_s[HG_HDR:HG_HDR + rows, :] = cs
    ng = ng_ref[...]

    nt_dims = (((1,), (1,)), ((), ()))
    tn_dims = (((0,), (0,)), ((), ()))

    def finish(o, rsl, hs):
        y = o * lax.rsqrt(jnp.mean(o * o, axis=-1, keepdims=True) + EPS) * ng
        return (y * _silu(g_ref[rsl, hs])).astype(o_ref.dtype)

    def pairwise_block(blk, carry):
        size = HG_BLK
        r0 = pl.multiple_of(blk * size, size)
        rsl = pl.ds(r0, size)
        sidx = 0 if nseq == 1 else blk // (seq_rows // size)
        rowi = lax.broadcasted_iota(jnp.int32, (size, 1), 0)
        for h in range(C_HEADS):
            hs = slice(h * C_DK, (h + 1) * C_DK)
            vsl = slice(h * C_DV, (h + 1) * C_DV)
            qb = q_s[rsl, hs]
            kb = k_s[rsl, hs]
            vb = i_ref[rsl, hs]
            prev = cs_s[pl.ds(HG_HDR + r0 - 8, 8), hs][7:8, :]
            bc = cs_s[pl.ds(HG_HDR + r0, size), hs] - prev
            blast = bc[size - 1:size, :]
            qd = (qb * jnp.exp(bc)).astype(BF16)
            kd = (kb * jnp.exp(blast - bc)).astype(BF16)
            st = st_ref[sidx, vsl, :]
            o = lax.dot_general(qd, st.astype(BF16), nt_dims, preferred_element_type=F32)
            for j in range(size):
                dec = jnp.exp(jnp.minimum(bc - bc[j:j + 1, :], 0.0))
                a = jnp.sum(qb * kb[j:j + 1, :] * dec, axis=-1, keepdims=True)
                a = jnp.where(rowi >= j, a, 0.0)
                o = o + a * vb[j:j + 1, :]
            u = lax.dot_general(vb.astype(BF16), kd, tn_dims, preferred_element_type=F32)
            st_ref[sidx, vsl, :] = st * jnp.exp(blast) + u
            o_ref[rsl, vsl] = finish(o, rsl, hs)
        return carry

    def factorised_tile():
        nblk = rows // HG_FAST
        for b in range(nblk):
            r = slice(b * HG_FAST, (b + 1) * HG_FAST)
            lo = HG_HDR + b * HG_FAST
            prev = cs_s[lo - 8:lo, :][7:8, :]
            bc = cs_s[lo:lo + HG_FAST, :] - prev
            blast = bc[HG_FAST - 1:HG_FAST, :]
            qb = q_s[r, :]
            kb = k_s[r, :]
            qd_s[r, :] = (qb * jnp.exp(bc)).astype(BF16)
            ki_s[r, :] = (kb * jnp.exp(-bc)).astype(BF16)
            kd_s[r, :] = (kb * jnp.exp(blast - bc)).astype(BF16)
            eb_s[b:b + 1, :] = jnp.exp(blast)
        same = (ri // HG_FAST) == (ci // HG_FAST)
        keep = same & (ci <= ri)
        for h in range(C_HEADS):
            hs = slice(h * C_DK, (h + 1) * C_DK)
            vsl = slice(h * C_DV, (h + 1) * C_DV)
            vh = i_ref[:, hs].astype(BF16)
            st = st_ref[0, vsl, :]
            for b in range(nblk):
                r = slice(b * HG_FAST, (b + 1) * HG_FAST)
                stb_s[h, b] = st.astype(BF16)
                u = lax.dot_general(vh[r, :], kd_s[r, hs], tn_dims, preferred_element_type=F32)
                st = st * eb_s[b:b + 1, hs] + u
            st_ref[0, vsl, :] = st
            a = lax.dot_general(qd_s[:, hs], ki_s[:, hs], nt_dims, preferred_element_type=F32)
            a = jnp.where(keep, a, 0.0)
            o = jnp.dot(a.astype(BF16), vh, preferred_element_type=F32)
            inter = [lax.dot_general(qd_s[b * HG_FAST:(b + 1) * HG_FAST, hs], stb_s[h, b], nt_dims,
                                     preferred_element_type=F32) for b in range(nblk)]
            o = o + jnp.concatenate(inter, axis=0)
            o_ref[:, vsl] = finish(o, slice(None), hs)

    def pairwise_tile():
        lax.fori_loop(0, rows // HG_BLK, pairwise_block, 0)

    if use_fast:
        worst = jnp.max(cs_s[HG_HDR - HG_FAST:HG_HDR - HG_FAST + rows, :] - cs)
        bounded = worst < HG_LIM
        pl.when(bounded)(factorised_tile)
        pl.when(jnp.logical_not(bounded))(pairwise_tile)
    else:
        pairwise_tile()


def _hgrn(proj, st0, lb, ng, *, rows, seq_rows, tv, n_seq, tiles_per_seq, row_block0):
    qcol, fcol, icol, gcol = 3072 // 512, 3584 // 512, 4096 // 512, 4608 // 512
    nst = max(rows // seq_rows, 1)
    assert nst == 1 or tiles_per_seq == 1
    width = C_HEADS * C_DK
    scratch = [pltpu.VMEM((rows, width), F32), pltpu.VMEM((rows, width), F32),
               pltpu.VMEM((HG_HDR + rows, width), F32)]
    if seq_rows % HG_FAST == 0:
        scratch += [pltpu.VMEM((rows, width), BF16) for _ in range(3)]
        scratch += [pltpu.VMEM((rows // HG_FAST, width), F32),
                    pltpu.VMEM((C_HEADS, rows // HG_FAST, C_DV, C_DK), BF16)]

    def rb(s, c):
        return row_block0 + s * tiles_per_seq + c

    const2 = lambda s, c: (0, 0)
    in_specs = [
        pl.BlockSpec((rows, 512), lambda s, c: (rb(s, c), qcol)),
        pl.BlockSpec((rows, 512), lambda s, c: (rb(s, c), fcol)),
        pl.BlockSpec((rows, 512), lambda s, c: (rb(s, c), icol)),
        pl.BlockSpec((rows, 512), lambda s, c: (rb(s, c), gcol)),
        pl.BlockSpec((1, 512), const2),
        pl.BlockSpec((1, C_DV), const2),
        pl.BlockSpec((nst, 512, C_DK), lambda s, c: (s, 0, 0)),
    ]
    args = [proj, proj, proj, proj, lb.reshape(1, 512), ng.reshape(1, C_DV), st0]
    return pl.pallas_call(
        functools.partial(_hgrn_kernel, rows=rows, seq_rows=seq_rows, tv=tv),
        grid=(n_seq // nst, tiles_per_seq),
        in_specs=in_specs,
        out_specs=[
            pl.BlockSpec((rows, 512), lambda s, c: (s * tiles_per_seq + c, 0)),
            pl.BlockSpec((nst, 512, C_DK), lambda s, c: (s, 0, 0)),
        ],
        out_shape=[
            jax.ShapeDtypeStruct((n_seq // nst * tiles_per_seq * rows, 512), BF16),
            jax.ShapeDtypeStruct((n_seq, 512, C_DK), F32),
        ],
        scratch_shapes=scratch,
        compiler_params=_cparams(("arbitrary", "arbitrary")),
        name="hgrn_prompt" if nst == 1 else "hgrn_sample",
    )(*args)


def _mm_out_kernel(pa_ref, pb_ref, pc_ref, sa_ref, sb_ref, sc_ref, w_ref, x_ref, g_ref, o_ref,
                   wb_ref, acc_ref):
    i = pl.program_id(1)

    @pl.when(i == 0)
    def _():
        wb_ref[...] = w_ref[...].astype(BF16)

    def mix(oa_ref, ob_ref, oc_ref):
        acc = jnp.dot(oa_ref[...], wb_ref[0:1024, :], preferred_element_type=F32)
        acc = acc + jnp.dot(ob_ref[...], wb_ref[1024:1536, :], preferred_element_type=F32)
        acc_ref[...] = acc + jnp.dot(oc_ref[...], wb_ref[1536:2048, :], preferred_element_type=F32)

    @pl.when(i < PROMPT_TILES)
    def _():
        mix(pa_ref, pb_ref, pc_ref)

    @pl.when(i == PROMPT_TILES)
    def _():
        mix(sa_ref, sb_ref, sc_ref)

    o_ref[...] = x_ref[...] + _tile_mod(g_ref, i) * acc_ref[...]


def _mm_out(prompt_mix, sample_mix, w_out, x, mod, l):
    tn = 1024
    ncol = D_MODEL // tn
    pmap = lambda j, i: (jnp.minimum(i, PROMPT_TILES - 1), 0)
    smap = lambda j, i: (0, 0)
    return pl.pallas_call(
        _mm_out_kernel,
        grid=(ncol, N_TILES),
        in_specs=[
            pl.BlockSpec((TM, 1024), pmap), pl.BlockSpec((TM, 512), pmap), pl.BlockSpec((TM, 512), pmap),
            pl.BlockSpec((TM, 1024), smap), pl.BlockSpec((TM, 512), smap), pl.BlockSpec((TM, 512), smap),
            pl.BlockSpec((None, D_MODEL, tn), lambda j, i: (l, 0, j)),
            pl.BlockSpec((TM, tn), lambda j, i: (i, j)),
            _mod_spec(l, "g1", tn, lambda j, i: j),
        ],
        out_specs=pl.BlockSpec((TM, tn), lambda j, i: (i, j)),
        out_shape=jax.ShapeDtypeStruct((N_ROWS, D_MODEL), F32),
        scratch_shapes=[pltpu.VMEM((D_MODEL, tn), BF16), pltpu.VMEM((TM, tn), F32)],
        compiler_params=_cparams(("arbitrary", "arbitrary")),
        name="mm_out",
    )(*prompt_mix, *sample_mix, w_out, x, mod)


META_GSEL = 0
META_RANK = 1
META_GATE = 20
XW = D_MODEL + 128


def _route_kernel(x_ref, g_ref, sc_ref, sh_ref, whl_ref, wh_ref, br_ref, hx_ref, meta_ref, cnt_ref,
                  run_ref, sc_st, sh_st, hi_s, lo_s):
    i = pl.program_id(0)

    @pl.when(i == 0)
    def _():
        run_ref[...] = jnp.zeros_like(run_ref)

    _stage_mod(sc_ref, sc_st, i)
    _stage_mod(sh_ref, sh_st, i)
    lg = g_ref[...]
    for s in range(N_SLABS):
        r = _slab(s)
        h = _norm_mod_slab(x_ref[r, :], lg, sc_st[s:s + 1, :], sh_st[s:s + 1, :])
        hx_ref[r, 0:D_MODEL] = h
        hi = h.astype(BF16)
        hi_s[r, :] = hi
        lo_s[r, :] = (h - hi.astype(F32)).astype(BF16)
    l1 = jnp.dot(hi_s[...], whl_ref[...], preferred_element_type=F32)
    l2 = jnp.dot(lo_s[...], wh_ref[...], preferred_element_type=F32)
    logits = l1[:, 0:128] + l1[:, 128:256] + l2 + br_ref[...]
    col = lax.broadcasted_iota(jnp.int32, logits.shape, 1).astype(F32)
    big = 1e9
    is_g = col < N_GROUPS
    gl = jnp.where(is_g, logits, -jnp.inf)
    gmax = jnp.max(gl, axis=-1, keepdims=True)
    gsel = jnp.min(jnp.where(gl == gmax, col, big), axis=-1, keepdims=True)
    gden = jnp.sum(jnp.where(is_g, jnp.exp(logits - gmax), 0.0), axis=-1, keepdims=True)
    g_w = 1.0 / gden
    e0 = N_GROUPS + EXP_PER_GROUP * gsel
    in_grp = (col >= e0) & (col < e0 + EXP_PER_GROUP)
    el = jnp.where(in_grp, logits, -jnp.inf)
    v1 = jnp.max(el, axis=-1, keepdims=True)
    i1 = jnp.min(jnp.where(el == v1, col, big), axis=-1, keepdims=True)
    el2 = jnp.where(col == i1, -jnp.inf, el)
    v2 = jnp.max(el2, axis=-1, keepdims=True)
    i2 = jnp.min(jnp.where(el2 == v2, col, big), axis=-1, keepdims=True)
    t = jnp.exp(v2 - v1)
    p1 = 1.0 / (1.0 + t)
    a1 = p1 * g_w
    a2 = t * p1 * g_w
    meta = jnp.where(col == META_GATE + (i1 - e0), a1, 0.0)
    meta = meta + jnp.where(col == META_GATE + (i2 - e0), a2, 0.0)
    meta = meta + jnp.where(col == META_GSEL, gsel, 0.0)
    rows = logits.shape[0]
    rowi = lax.broadcasted_iota(jnp.int32, (rows, 1), 0)
    is_tok = (i < PROMPT_TILES) | (rowi % SROWS == 0)
    onehot = jnp.where((col == gsel) & is_tok, 1.0, 0.0)
    ri = lax.broadcasted_iota(jnp.int32, (rows, rows), 0)
    ci = lax.broadcasted_iota(jnp.int32, (rows, rows), 1)
    tri = jnp.where(ci <= ri, 1.0, 0.0).astype(BF16)
    incl = jnp.dot(tri, onehot.astype(BF16), preferred_element_type=F32)
    run = run_ref[...]
    rank = jnp.sum(jnp.where(col == gsel, incl + run - 1.0, 0.0), axis=-1, keepdims=True)
    meta = meta + jnp.where(col == META_RANK, rank, 0.0)
    meta_ref[...] = meta
    hx_ref[:, D_MODEL:XW] = meta
    run = run + incl[rows - 1:rows, :]
    run_ref[...] = run
    cnt_ref[...] = run


def _router_weights(w_rg, b_rg, w_re, b_re):
    pad = 128 - N_GROUPS - N_EXPERTS
    wr = jnp.concatenate([w_rg, w_re, jnp.zeros((DEPTH, D_MODEL, pad), F32)], axis=2)
    w_hi = wr.astype(BF16)
    w_lo = (wr - w_hi.astype(F32)).astype(BF16)
    br = jnp.concatenate([b_rg, b_re, jnp.zeros((DEPTH, pad), F32)], axis=1).reshape(DEPTH, 1, 128)
    return jnp.concatenate([w_hi, w_lo], axis=2), w_hi, br


def _route(x, g, mod, whl, w_hi, br, l):
    return pl.pallas_call(
        _route_kernel,
        grid=(N_TILES,),
        in_specs=[
            pl.BlockSpec((TM, D_MODEL), lambda i: (i, 0)),
            pl.BlockSpec((1, D_MODEL), lambda i: (0, 0)),
            _mod_spec(l, "sc2"),
            _mod_spec(l, "sh2"),
            pl.BlockSpec((None, D_MODEL, 256), lambda i: (l, 0, 0)),
            pl.BlockSpec((None, D_MODEL, 128), lambda i: (l, 0, 0)),
            pl.BlockSpec((None, 1, 128), lambda i: (l, 0, 0)),
        ],
        out_specs=[
            pl.BlockSpec((TM, XW), lambda i: (i, 0)),
            pl.BlockSpec((TM, 128), lambda i: (i, 0)),
            pl.BlockSpec((1, 128), lambda i: (0, 0)),
        ],
        out_shape=[
            jax.ShapeDtypeStruct((N_ROWS, XW), F32),
            jax.ShapeDtypeStruct((N_ROWS, 128), F32),
            jax.ShapeDtypeStruct((1, 128), F32),
        ],
        scratch_shapes=[pltpu.VMEM((1, 128), F32),
                        pltpu.VMEM((N_SLABS, D_MODEL), F32), pltpu.VMEM((N_SLABS, D_MODEL), F32),
                        pltpu.VMEM((TM, D_MODEL), BF16), pltpu.VMEM((TM, D_MODEL), BF16)],
        compiler_params=_cparams(("arbitrary",)),
        name="route",
    )(x, g.reshape(1, D_MODEL), mod, mod, whl, w_hi, br)


def _moe_positions(meta, cnt):
    i32 = jnp.int32
    counts = cnt[0, :N_GROUPS].astype(i32)
    ntile = (counts + TM - 1) // TM
    tstart = jnp.cumsum(ntile) - ntile
    gsel = meta[:, META_GSEL].astype(i32)
    rank = meta[:, META_RANK].astype(i32)
    first_row = jnp.zeros_like(gsel)
    for g in range(N_GROUPS):
        first_row = jnp.where(gsel == g, tstart[g] * TM, first_row)
    pos = first_row + rank
    pos_all = jnp.concatenate([pos[:NP_ROWS], jnp.repeat(pos[NP_ROWS::SROWS], SROWS)])
    plan = jnp.concatenate([tstart, ntile]).astype(i32)
    return pos_all, plan


def _scatter_kernel(pos_ref, src_ref, buf_hbm, out_hbm, sem):
    del buf_hbm
    i = pl.program_id(0)
    base = i * TM

    def row_copy(r):
        return pltpu.make_async_copy(src_ref.at[pl.ds(r, 1), :],
                                     out_hbm.at[pl.ds(pos_ref[base + r], 1), :], sem)

    def run(n, stride):
        def issue(k, c):
            row_copy(k * stride).start()
            return c

        def drain(k, c):
            row_copy(k * stride).wait()
            return c

        lax.fori_loop(0, n, issue, 0, unroll=8)
        lax.fori_loop(0, n, drain, 0, unroll=8)

    @pl.when(i < PROMPT_TILES)
    def _():
        run(TM, 1)

    @pl.when(i == PROMPT_TILES)
    def _():
        run(DEC_BATCH, SROWS)


def _moe_scatter(hx, buf, pos_all):
    grid_spec = pltpu.PrefetchScalarGridSpec(
        num_scalar_prefetch=1,
        grid=(N_TILES,),
        in_specs=[pl.BlockSpec((TM, XW), lambda i, pos: (i, 0)), pl.BlockSpec(memory_space=pl.ANY)],
        out_specs=pl.BlockSpec(memory_space=pl.ANY),
        scratch_shapes=[pltpu.SemaphoreType.DMA(())],
    )
    return pl.pallas_call(
        _scatter_kernel,
        grid_spec=grid_spec,
        out_shape=jax.ShapeDtypeStruct((MOE_ROWS, XW), F32),
        input_output_aliases={2: 0},
        compiler_params=_cparams(("arbitrary",)),
        name="moe_scatter",
    )(pos_all, hx, buf)


def _work_item(w, plan_ref, per_tile):
    i32 = jnp.int32
    nt = [plan_ref[N_GROUPS + g] for g in range(N_GROUPS)]
    ts = [plan_ref[g] for g in range(N_GROUPS)]
    ends = []
    acc = 0
    for g in range(N_GROUPS):
        acc = acc + per_tile * nt[g]
        ends.append(acc)
    valid = w < ends[-1]
    wc = jnp.maximum(jnp.minimum(w, ends[-1] - 1), 0)
    g = sum((wc >= ends[k]).astype(i32) for k in range(N_GROUPS - 1))

    def pick(vals):
        out = vals[N_GROUPS - 1]
        for k in range(N_GROUPS - 2, -1, -1):
            out = jnp.where(g == k, vals[k], out)
        return out

    r = wc - pick([0] + ends[:-1])
    ntg = pick(nt)
    c = sum((r >= m * ntg).astype(i32) for m in range(1, per_tile))
    t_in = r - c * ntg
    t = pick(ts) + t_in
    extra = jnp.maximum(w - ends[-1], 0)
    used = ts[N_GROUPS - 1] + nt[N_GROUPS - 1]
    t_out = jnp.where(valid, t, used + extra // per_tile)
    c_out = jnp.where(valid, c, extra % per_tile)
    return g, c, t, t_in == 0, valid, t_out, c_out


HID_EXPERTS = 2
HID_NCOL = EXP_PER_GROUP // HID_EXPERTS
HID_WORK = MOE_TILES * HID_NCOL


def _hidden_kernel(plan_ref, x_ref, w1_ref, w3_ref, o_ref, w1b, w3b):
    _, c, _, first, valid, _, _ = _work_item(pl.program_id(0), plan_ref, HID_NCOL)

    @pl.when(jnp.logical_not(valid))
    def _():
        o_ref[...] = jnp.zeros_like(o_ref)

    @pl.when(valid & first)
    def _():
        for j in range(HID_EXPERTS):
            cs = slice(j * D_EXPERT, (j + 1) * D_EXPERT)
            w1b[:, cs] = w1_ref[j].astype(BF16)
            w3b[:, cs] = w3_ref[j].astype(BF16)

    @pl.when(valid)
    def _():
        x = x_ref[:, 0:D_MODEL].astype(BF16)
        m = x_ref[:, D_MODEL:XW]
        col = lax.broadcasted_iota(jnp.int32, m.shape, 1)
        for j in range(HID_EXPERTS):
            cs = slice(j * D_EXPERT, (j + 1) * D_EXPERT)
            a = jnp.dot(x, w1b[:, cs], preferred_element_type=F32)
            b = jnp.dot(x, w3b[:, cs], preferred_element_type=F32)
            lane = META_GATE + c * HID_EXPERTS + j
            gate = jnp.sum(jnp.where(col == lane, m, 0.0), axis=-1, keepdims=True)
            o_ref[:, cs] = (_silu(a) * b * gate).astype(o_ref.dtype)


def _moe_hidden(xs, w1, w3, plan, l):
    def item(w, plan):
        return _work_item(w, plan, HID_NCOL)

    def w_map(w, plan):
        it = item(w, plan)
        return (l, it[0] * HID_NCOL + it[1], 0, 0)

    wide = HID_EXPERTS * D_EXPERT
    grid_spec = pltpu.PrefetchScalarGridSpec(
        num_scalar_prefetch=1,
        grid=(HID_WORK,),
        in_specs=[
            pl.BlockSpec((TM, XW), lambda w, plan: (item(w, plan)[2], 0)),
            pl.BlockSpec((None, HID_EXPERTS, D_MODEL, D_EXPERT), w_map),
            pl.BlockSpec((None, HID_EXPERTS, D_MODEL, D_EXPERT), w_map),
        ],
        out_specs=pl.BlockSpec((TM, wide), lambda w, plan: item(w, plan)[5:7]),
        scratch_shapes=[pltpu.VMEM((D_MODEL, wide), BF16), pltpu.VMEM((D_MODEL, wide), BF16)],
    )
    return pl.pallas_call(
        _hidden_kernel,
        grid_spec=grid_spec,
        out_shape=jax.ShapeDtypeStruct((MOE_ROWS, EXP_PER_GROUP * D_EXPERT), BF16),
        compiler_params=_cparams(("arbitrary",), BIG_VMEM_LIMIT),
        name="moe_hidden",
    )(plan, xs, w1, w3)


def _down_kernel(plan_ref, h_ref, w_ref, o_ref, wb):
    _, _, _, first, valid, _, _ = _work_item(pl.program_id(0), plan_ref, DOWN_NCOL)

    @pl.when(jnp.logical_not(valid))
    def _():
        o_ref[...] = jnp.zeros_like(o_ref)

    @pl.when(valid & first)
    def _():
        wb[...] = w_ref[...].astype(BF16)

    @pl.when(valid)
    def _():
        o_ref[...] = jnp.dot(h_ref[...], wb[...], preferred_element_type=F32)


def _moe_down(hid, w2g, plan, l):
    kdim = EXP_PER_GROUP * D_EXPERT

    def item(w, plan):
        return _work_item(w, plan, DOWN_NCOL)

    grid_spec = pltpu.PrefetchScalarGridSpec(
        num_scalar_prefetch=1,
        grid=(DOWN_WORK,),
        in_specs=[
            pl.BlockSpec((TM, kdim), lambda w, plan: (item(w, plan)[2], 0)),
            pl.BlockSpec((None, None, kdim, DOWN_TN),
                         lambda w, plan: (l, item(w, plan)[0], 0, item(w, plan)[1])),
        ],
        out_specs=pl.BlockSpec((TM, DOWN_TN), lambda w, plan: item(w, plan)[5:7]),
        scratch_shapes=[pltpu.VMEM((kdim, DOWN_TN), BF16)],
    )
    return pl.pallas_call(
        _down_kernel,
        grid_spec=grid_spec,
        out_shape=jax.ShapeDtypeStruct((MOE_ROWS, D_MODEL), F32),
        compiler_params=_cparams(("arbitrary",), BIG_VMEM_LIMIT),
        name="moe_down",
    )(plan, hid, w2g)


def _res_kernel(pos_ref, x_ref, y_hbm, g2_ref, *rest, with_norm):
    if with_norm:
        lg_ref, sc_ref, sh_ref, xo_ref, ho_ref, ybuf, sem, g2_st, sc_st, sh_st = rest
    else:
        yp_ref, ys_ref, ybuf, sem, g2_st = rest
    i = pl.program_id(0)
    slot = i % 2

    def row_copy(tile, k, sl):
        return pltpu.make_async_copy(y_hbm.at[pl.ds(pos_ref[tile * TM + k], 1), :],
                                     ybuf.at[sl, pl.ds(k, 1), :], sem.at[sl])

    def issue_tile(tile, sl):
        def issue(k, c):
            row_copy(tile, k, sl).start()
            return c
        lax.fori_loop(0, TM, issue, 0, unroll=8)

    @pl.when(i == 0)
    def _():
        issue_tile(0, 0)

    @pl.when(i + 1 < N_TILES)
    def _():
        issue_tile(i + 1, 1 - slot)

    _stage_mod(g2_ref, g2_st, i)
    if with_norm:
        _stage_mod(sc_ref, sc_st, i)
        _stage_mod(sh_ref, sh_st, i)

    def drain(k, c):
        row_copy(i, k, slot).wait()
        return c

    lax.fori_loop(0, TM, drain, 0, unroll=8)

    def new_x(s):
        r = _slab(s)
        return x_ref[r, :] + g2_st[s:s + 1, :] * ybuf[slot, r, :]

    if with_norm:
        lg = lg_ref[...]
        for s in range(N_SLABS):
            xn = new_x(s)
            xo_ref[_slab(s), :] = xn
            hn = _norm_mod_slab(xn, lg, sc_st[s:s + 1, :], sh_st[s:s + 1, :])
            ho_ref[_slab(s), :] = hn.astype(ho_ref.dtype)
    else:
        @pl.when(i < PROMPT_TILES)
        def _():
            for s in range(N_SLABS):
                yp_ref[_slab(s), :] = new_x(s)

        @pl.when(i == PROMPT_TILES)
        def _():
            for s in range(N_SLABS):
                ys_ref[s:s + 1, :] = new_x(s)[0:1, :]


def _residual(x, y_sorted, pos_all, mod, l, next_g=None):
    with_norm = next_g is not None
    row_spec = pl.BlockSpec((TM, D_MODEL), lambda i, pos: (i, 0))
    in_specs = [row_spec, pl.BlockSpec(memory_space=pl.ANY), _mod_spec(l, "g2")]
    args = [pos_all, x, y_sorted, mod]
    stage = pltpu.VMEM((N_SLABS, D_MODEL), F32)
    scratch = [pltpu.VMEM((2, TM, D_MODEL), F32), pltpu.SemaphoreType.DMA((2,)), stage]
    if with_norm:
        in_specs += [pl.BlockSpec((1, D_MODEL), lambda i, pos: (0, 0)),
                     _mod_spec(l + 1, "sc1"), _mod_spec(l + 1, "sh1")]
        args += [next_g.reshape(1, D_MODEL), mod, mod]
        out_specs = [row_spec, row_spec]
        out_shape = [jax.ShapeDtypeStruct((N_ROWS, D_MODEL), F32),
                     jax.ShapeDtypeStruct((N_ROWS, D_MODEL), BF16)]
        scratch += [stage, stage]
    else:
        out_specs = [pl.BlockSpec((TM, D_MODEL), lambda i, pos: (jnp.minimum(i, PROMPT_TILES - 1), 0)),
                     pl.BlockSpec((DEC_BATCH, D_MODEL), lambda i, pos: (0, 0))]
        out_shape = [jax.ShapeDtypeStruct((NP_ROWS, D_MODEL), F32),
                     jax.ShapeDtypeStruct((DEC_BATCH, D_MODEL), F32)]
    grid_spec = pltpu.PrefetchScalarGridSpec(
        num_scalar_prefetch=1,
        grid=(N_TILES,),
        in_specs=in_specs,
        out_specs=out_specs,
        scratch_shapes=scratch,
    )
    return pl.pallas_call(
        functools.partial(_res_kernel, with_norm=with_norm),
        grid_spec=grid_spec,
        out_shape=out_shape,
        compiler_params=_cparams(("arbitrary",)),
        name="residual_norm" if with_norm else "residual",
    )(*args)


def kernel(x_prompt, x_sample, c_prompt, c_sample, cache_swa_k, cache_swa_v, state_ret, state_hgrn,
           w_ada, b_ada, ln1_g, w_in, qn_g, kn_g, attn_sinks, ret_gn_g, ret_gn_b, hgrn_lb, hgrn_ng,
           w_out, ln2_g, w_rg, b_rg, w_re, b_re, w1, w3, w2):
    i32 = jnp.int32
    lbp = jax.nn.softmax(hgrn_lb.astype(F32), axis=0)
    lb_all = jnp.cumsum(lbp, axis=0) - lbp[0]

    assert TM // SROWS == DEC_BATCH and NS_ROWS == TM
    c_all = jnp.concatenate([c_prompt, jnp.zeros((C_SAMPLE0 - BATCH, D_MODEL), F32), c_sample], axis=0)
    mod = _ada(c_all, w_ada, b_ada)

    ret_tab_p = _ret_tables(RET_CHUNK, RET_CHUNK, 0.0, SEQ)
    ret_tab_s = _ret_tables(SROWS, 1, float(PAST_LEN), SROWS)
    w2g = w2.reshape(DEPTH, N_GROUPS, EXP_PER_GROUP * D_EXPERT, D_MODEL)
    whl_all, whi_all, br_all = _router_weights(w_rg, b_rg, w_re, b_re)
    st_s_all = jnp.swapaxes(state_hgrn, -1, -2).reshape(DEPTH, DEC_BATCH, C_HEADS * C_DV, C_DK)
    ck_all = cache_swa_k.reshape(DEPTH, DEC_BATCH * WINDOW, A_KV * A_HD)
    cv_all = cache_swa_v.reshape(DEPTH, DEC_BATCH * WINDOW, A_KV * A_HD)
    xs_buf = jnp.zeros((MOE_ROWS, XW), F32)

    x, h = _norm_mod_first(x_prompt.reshape(NP_ROWS, D_MODEL), x_sample.reshape(DEC_BATCH, D_MODEL),
                           ln1_g[0], mod, 0)

    outs = {k: [] for k in ("kp", "vp", "rp", "hp", "ks", "vs", "rs", "hs")}
    for l in range(DEPTH):
        proj = _mm_in(h, w_in, l)

        oa_p, kp, vp = _swa(proj, proj, proj, attn_sinks[l], qn_g[l], kn_g[l], tq=WINDOW, nb=SWA_NB,
                            n_seq=BATCH, steps_per_seq=SEQ // (WINDOW * SWA_NB), row_block0=0, chain=True)
        srows = SAMPLE_NB * SROWS
        oa_s, ks, vs = _swa(proj, ck_all, cv_all, attn_sinks[l], qn_g[l], kn_g[l], tq=SROWS, nb=SAMPLE_NB,
                            n_seq=DEC_BATCH, steps_per_seq=1, row_block0=NP_ROWS // srows, chain=False,
                            layer=l)
        zero_ret = jnp.zeros((BATCH, B_HEADS, B_DK, B_DV), F32)
        ob_p, rp = _ret(proj, zero_ret, ret_gn_g[l], ret_gn_b[l], ret_tab_p, chunk=RET_CHUNK, nb=1,
                        n_seq=BATCH, chunks_per_seq=SEQ // RET_CHUNK, row_block0=0, table_per_chunk=True)
        ob_s, rs = _ret(proj, state_ret[l], ret_gn_g[l], ret_gn_b[l], ret_tab_s, chunk=SROWS, nb=SAMPLE_NB,
                        n_seq=DEC_BATCH, chunks_per_seq=1, row_block0=NP_ROWS // srows,
                        table_per_chunk=False)
        zero_hg = jnp.zeros((BATCH, C_HEADS * C_DV, C_DK), F32)
        oc_p, hp_t = _hgrn(proj, zero_hg, lb_all[l], hgrn_ng[l], rows=HG_TILE, seq_rows=HG_TILE,
                           tv=HG_TILE, n_seq=BATCH, tiles_per_seq=SEQ // HG_TILE, row_block0=0)
        oc_s, hs_t = _hgrn(proj, st_s_all[l], lb_all[l], hgrn_ng[l], rows=srows, seq_rows=SROWS, tv=1,
                           n_seq=DEC_BATCH, tiles_per_seq=1, row_block0=NP_ROWS // srows)

        x1 = _mm_out((oa_p, ob_p, oc_p), (oa_s, ob_s, oc_s), w_out, x, mod, l)

        hx, meta, cnt = _route(x1, ln2_g[l], mod, whl_all, whi_all, br_all, l)
        pos_all, plan = _moe_positions(meta, cnt)
        xs_buf = _moe_scatter(hx, xs_buf, pos_all)
        hid = _moe_hidden(xs_buf, w1, w3, plan, l)
        y_sorted = _moe_down(hid, w2g, plan, l)
        if l + 1 < DEPTH:
            x, h = _residual(x1, y_sorted, pos_all, mod, l, ln1_g[l + 1])
        else:
            y_prompt, y_sample = _residual(x1, y_sorted, pos_all, mod, l)

        outs["kp"].append(kp.reshape(BATCH, WINDOW, A_KV, A_HD))
        outs["vp"].append(vp.reshape(BATCH, WINDOW, A_KV, A_HD))
        outs["rp"].append(rp)
        outs["hp"].append(hp_t.reshape(BATCH, C_HEADS, C_DV, C_DK))
        outs["ks"].append(ks.reshape(DEC_BATCH, 1, A_KV, A_HD))
        outs["vs"].append(vs.reshape(DEC_BATCH, 1, A_KV, A_HD))
        outs["rs"].append(rs)
        outs["hs"].append(hs_t.reshape(DEC_BATCH, C_HEADS, C_DV, C_DK))

    y_prompt = y_prompt.reshape(BATCH, SEQ, D_MODEL)
    y_sample = y_sample.reshape(DEC_BATCH, 1, D_MODEL)
    st = lambda k: jnp.stack(outs[k])
    un_t = lambda k: jnp.swapaxes(st(k), -1, -2)
    roll_in = lambda cache, k: jnp.concatenate([cache[:, :, 1:], st(k)], axis=2)
    return (y_prompt, y_sample, st("kp"), st("vp"), st("rp"), un_t("hp"),
            roll_in(cache_swa_k, "ks"), roll_in(cache_swa_v, "vs"), st("rs"), un_t("hs"))
```

```python
import functools

import numpy as np
import jax
import jax.numpy as jnp
from jax import lax
from jax.experimental import pallas as pl
from jax.experimental.pallas import tpu as pltpu

F32 = jnp.float32
BF16 = jnp.bfloat16

D_MODEL = 2048
BATCH = 2
SEQ = 4096
DEPTH = 4
DEC_BATCH = 32
PAST_LEN = 16384
A_HD = 128
A_HEADS = 8
A_KV = 2
A_GROUP = 4
WINDOW = 128
B_DV = 128
B_DK = 64
B_HEADS = 4
C_DK = 128
C_DV = 128
C_HEADS = 4
ROPE_BASE = 10000.0
N_GROUPS = 4
EXP_PER_GROUP = 4
N_EXPERTS = 16
D_EXPERT = 512
EPS = 1e-6
NEG_BIG = -1e30
IN_WIDTH = 5120

SROWS = 16
NP_ROWS = BATCH * SEQ
NS_ROWS = DEC_BATCH * SROWS
N_ROWS = NP_ROWS + NS_ROWS
TM = 512
N_TILES = N_ROWS // TM
MOD_ROWS = 16
N_TOK = NP_ROWS + DEC_BATCH
MOE_TILES = (N_TOK + N_GROUPS * (TM - 1)) // TM
MOE_ROWS = MOE_TILES * TM
DOWN_TN = 2048
DOWN_NCOL = D_MODEL // DOWN_TN
DOWN_WORK = MOE_TILES * DOWN_NCOL

RET_CHUNK = 256
HG_TILE = 256
HG_BLK = 16
SAMPLE_NB = 8
SWA_NB = 2

VMEM_LIMIT = 48 * 1024 * 1024
BIG_VMEM_LIMIT = 56 * 1024 * 1024


def _cparams(sem, limit=VMEM_LIMIT):
    return pltpu.CompilerParams(dimension_semantics=sem, vmem_limit_bytes=limit)


def _sigmoid(x):
    return 1.0 / (1.0 + jnp.exp(-x))


def _silu(x):
    return x * _sigmoid(x)


C_ROWS = 40
C_SAMPLE0 = 8
PROMPT_TILES = NP_ROWS // TM
MOD_COL = dict(sh1=0, sc1=1, g1=2, sh2=3, sc2=4, g2=5)


def _tile_mod(m_ref, i):
    n = m_ref.shape[-1]
    seq = jnp.minimum(i // (SEQ // TM), BATCH - 1)
    prow = jnp.broadcast_to(m_ref[pl.ds(seq, 1), :], (DEC_BATCH, n))
    srows = m_ref[C_SAMPLE0:C_SAMPLE0 + DEC_BATCH, :]
    m = jnp.where(i < PROMPT_TILES, prow, srows)
    return jnp.broadcast_to(m[:, None, :], (DEC_BATCH, SROWS, n)).reshape(TM, n)


N_SLABS = TM // SROWS


def _stage_mod(m_ref, stage_ref, i):
    n = m_ref.shape[-1]
    seq = jnp.minimum(i // (SEQ // TM), BATCH - 1)
    prow = jnp.broadcast_to(m_ref[pl.ds(seq, 1), :], (N_SLABS, n))
    srows = m_ref[C_SAMPLE0:C_SAMPLE0 + DEC_BATCH, :]
    stage_ref[...] = jnp.where(i < PROMPT_TILES, prow, srows)


def _slab(s):
    return slice(s * SROWS, (s + 1) * SROWS)


def _mod_spec(l, name, tn=D_MODEL, colmap=None):
    base = MOD_COL[name] * (D_MODEL // tn)
    if colmap is None:
        return pl.BlockSpec((None, C_ROWS, tn), lambda *ids: (l, 0, base))
    return pl.BlockSpec((None, C_ROWS, tn), lambda *ids: (l, 0, base + colmap(*ids)))


def _ada_kernel(c_ref, w_ref, b_ref, o_ref):
    c = _silu(c_ref[...])
    o_ref[...] = jnp.dot(c.astype(BF16), w_ref[...].astype(BF16),
                         preferred_element_type=F32) + b_ref[...]


def _ada(c_all, w_ada, b_ada):
    rows = c_all.shape[0]
    tn = 2048
    ncol = w_ada.shape[2] // tn
    return pl.pallas_call(
        _ada_kernel,
        grid=(DEPTH, ncol),
        in_specs=[
            pl.BlockSpec((rows, D_MODEL), lambda l, j: (0, 0)),
            pl.BlockSpec((None, D_MODEL, tn), lambda l, j: (l, 0, j)),
            pl.BlockSpec((None, 1, tn), lambda l, j: (l, 0, j)),
        ],
        out_specs=pl.BlockSpec((None, rows, tn), lambda l, j: (l, 0, j)),
        out_shape=jax.ShapeDtypeStruct((DEPTH, rows, w_ada.shape[2]), F32),
        compiler_params=_cparams(("arbitrary", "arbitrary")),
        name="ada",
    )(c_all, w_ada, b_ada.reshape(DEPTH, 1, -1))


def _norm_mod_rows(x, g, sc_ref, sh_ref, i):
    y = x * lax.rsqrt(jnp.mean(x * x, axis=-1, keepdims=True) + EPS) * g
    return y * (1.0 + _tile_mod(sc_ref, i)) + _tile_mod(sh_ref, i)


def _norm_mod_slab(x, g, sc_row, sh_row):
    y = x * lax.rsqrt(jnp.mean(x * x, axis=-1, keepdims=True) + EPS) * g
    return y * (1.0 + sc_row) + sh_row


def _norm_kernel(xp_ref, xs_ref, g_ref, sc_ref, sh_ref, x_ref, o_ref):
    i = pl.program_id(0)
    xs = xs_ref[...]
    rid = lax.broadcasted_iota(jnp.int32, (DEC_BATCH, SROWS, 1), 1)
    blk = jnp.where(rid == 0, jnp.broadcast_to(xs[:, None, :], (DEC_BATCH, SROWS, D_MODEL)), 0.0)
    x = jnp.where(i < PROMPT_TILES, xp_ref[...], blk.reshape(TM, D_MODEL))
    x_ref[...] = x
    o_ref[...] = _norm_mod_rows(x, g_ref[...], sc_ref, sh_ref, i).astype(o_ref.dtype)


def _norm_mod_first(x_prompt, x_sample, g, mod, l):
    return pl.pallas_call(
        _norm_kernel,
        grid=(N_TILES,),
        in_specs=[
            pl.BlockSpec((TM, D_MODEL), lambda i: (jnp.minimum(i, PROMPT_TILES - 1), 0)),
            pl.BlockSpec((DEC_BATCH, D_MODEL), lambda i: (0, 0)),
            pl.BlockSpec((1, D_MODEL), lambda i: (0, 0)),
            _mod_spec(l, "sc1"),
            _mod_spec(l, "sh1"),
        ],
        out_specs=[
            pl.BlockSpec((TM, D_MODEL), lambda i: (i, 0)),
            pl.BlockSpec((TM, D_MODEL), lambda i: (i, 0)),
        ],
        out_shape=[
            jax.ShapeDtypeStruct((N_ROWS, D_MODEL), F32),
            jax.ShapeDtypeStruct((N_ROWS, D_MODEL), BF16),
        ],
        compiler_params=_cparams(("arbitrary",)),
        name="norm_first",
    )(x_prompt, x_sample, g.reshape(1, D_MODEL), mod, mod)


def _mm_in_kernel(x_ref, w_ref, o_ref, wb_ref):
    @pl.when(pl.program_id(1) == 0)
    def _():
        wb_ref[...] = w_ref[...].astype(BF16)

    o_ref[...] = jnp.dot(x_ref[...], wb_ref[...], preferred_element_type=F32)


MM_IN_TILES = 8


def _mm_in(h, w_in, l):
    tn = 1280
    ncol = IN_WIDTH // tn
    tm = N_ROWS // MM_IN_TILES
    return pl.pallas_call(
        _mm_in_kernel,
        grid=(ncol, MM_IN_TILES),
        in_specs=[
            pl.BlockSpec((tm, D_MODEL), lambda j, i: (i, 0)),
            pl.BlockSpec((None, D_MODEL, tn), lambda j, i: (l, 0, j)),
        ],
        out_specs=pl.BlockSpec((tm, tn), lambda j, i: (i, j)),
        out_shape=jax.ShapeDtypeStruct((N_ROWS, IN_WIDTH), F32),
        scratch_shapes=[pltpu.VMEM((D_MODEL, tn), BF16)],
        compiler_params=_cparams(("arbitrary", "arbitrary"), BIG_VMEM_LIMIT),
        name="mm_in",
    )(h, w_in)


def _rms(x, g):
    return x * lax.rsqrt(jnp.mean(x * x, axis=-1, keepdims=True) + EPS) * g


def _swa_kernel(sink_ref, q_ref, k_ref, v_ref, pk_ref, pv_ref, qg_ref, kg_ref,
                *rest, tq, nb, chain):
    o_ref, kc_ref, vc_ref = rest[-3:]
    qg = qg_ref[...]
    kg = kg_ref[...]
    nk = WINDOW + tq
    nq = A_GROUP * tq
    row = lax.broadcasted_iota(jnp.int32, (nq, nk), 0) & (tq - 1)
    col = lax.broadcasted_iota(jnp.int32, (nq, nk), 1)
    vis_prev = (col < WINDOW) & (col >= row)
    vis_own = (col >= WINDOW) & ((col - WINDOW) <= row)
    mask_all = vis_prev | vis_own
    if chain:
        mask_first = (vis_prev & (pl.program_id(1) > 0)) | vis_own
        last_step = pl.program_id(1) == pl.num_programs(1) - 1
    rgrp = lax.broadcasted_iota(jnp.int32, (nq, 1), 0) // tq
    last = [None] * A_KV
    for sb in range(nb):
        rs = slice(sb * tq, (sb + 1) * tq)
        mask = mask_first if (chain and sb == 0) else mask_all
        for kv in range(A_KV):
            ks = slice(kv * A_HD, (kv + 1) * A_HD)
            kn = _rms(k_ref[rs, ks], kg)
            vn = v_ref[rs, ks]
            if chain and sb > 0:
                pk, pv = last[kv]
            elif chain:
                pk = _rms(pk_ref[:, ks], kg)
                pv = pv_ref[:, ks]
            else:
                ps = slice(sb * WINDOW, (sb + 1) * WINDOW)
                pk = pk_ref[ps, ks]
                pv = pv_ref[ps, ks]
            if chain:
                if sb == nb - 1:
                    @pl.when(last_step)
                    def _():
                        kc_ref[:, ks] = kn
                        vc_ref[:, ks] = vn
            else:
                kc_ref[sb:sb + 1, ks] = kn[0:1, :]
                vc_ref[sb:sb + 1, ks] = vn[0:1, :]
            last[kv] = (kn, vn)
            keys = jnp.concatenate([pk, kn], axis=0).astype(BF16)
            vals = jnp.concatenate([pv, vn], axis=0).astype(BF16)
            h0 = kv * A_GROUP
            qs = jnp.concatenate(
                [_rms(q_ref[rs, (h0 + g) * A_HD:(h0 + g + 1) * A_HD], qg) for g in range(A_GROUP)],
                axis=0).astype(BF16)
            s = lax.dot_general(qs, keys, (((1,), (1,)), ((), ())),
                                preferred_element_type=F32) * (A_HD ** -0.5)
            s = jnp.where(mask, s, NEG_BIG)
            snk = jnp.full((nq, 1), sink_ref[h0 + A_GROUP - 1], F32)
            for g in range(A_GROUP - 2, -1, -1):
                snk = jnp.where(rgrp == g, sink_ref[h0 + g], snk)
            m = jnp.maximum(jnp.max(s, axis=-1, keepdims=True), snk)
            p = jnp.exp(s - m)
            den = jnp.sum(p, axis=-1, keepdims=True) + jnp.exp(snk - m)
            o = jnp.dot(p.astype(BF16), vals, preferred_element_type=F32) / den
            for g in range(A_GROUP):
                o_ref[rs, (h0 + g) * A_HD:(h0 + g + 1) * A_HD] = o[g * tq:(g + 1) * tq].astype(o_ref.dtype)


def _swa(proj, prev_k, prev_v, sinks, qn_g, kn_g, *, tq, nb, n_seq, steps_per_seq, row_block0,
         chain, layer=None):
    qcol, kcol, vcol = 0, 1024 // 256, 1280 // 256
    assert chain or steps_per_seq == 1
    rows = tq * nb

    def rb(s, n):
        return row_block0 + s * steps_per_seq + n

    if chain:
        def before(s, n):
            return jnp.maximum(rb(s, n) * nb - 1, rb(s, 0) * nb)
        prev_specs = [pl.BlockSpec((WINDOW, 256), lambda s, n: (before(s, n), kcol)),
                      pl.BlockSpec((WINDOW, 256), lambda s, n: (before(s, n), vcol))]
        cache_spec = pl.BlockSpec((None, WINDOW, 256), lambda s, n: (s, 0, 0))
        cache_shape = jax.ShapeDtypeStruct((n_seq, WINDOW, 256), F32)
    else:
        prev_specs = [pl.BlockSpec((None, WINDOW * nb, 256), lambda s, n: (layer, s, 0))] * 2
        cache_spec = pl.BlockSpec((nb, 256), lambda s, n: (s, 0))
        cache_shape = jax.ShapeDtypeStruct((n_seq, 256), F32)

    in_specs = [
        pl.BlockSpec(memory_space=pltpu.SMEM),
        pl.BlockSpec((rows, 1024), lambda s, n: (rb(s, n), qcol)),
        pl.BlockSpec((rows, 256), lambda s, n: (rb(s, n), kcol)),
        pl.BlockSpec((rows, 256), lambda s, n: (rb(s, n), vcol)),
        *prev_specs,
        pl.BlockSpec((1, A_HD), lambda s, n: (0, 0)),
        pl.BlockSpec((1, A_HD), lambda s, n: (0, 0)),
    ]
    args = [sinks, proj, proj, proj, prev_k, prev_v, qn_g.reshape(1, A_HD), kn_g.reshape(1, A_HD)]
    n_steps0 = n_seq if chain else n_seq // nb
    return pl.pallas_call(
        functools.partial(_swa_kernel, tq=tq, nb=nb, chain=chain),
        grid=(n_steps0, steps_per_seq),
        in_specs=in_specs,
        out_specs=[pl.BlockSpec((rows, 1024), lambda s, n: (s * steps_per_seq + n, 0)),
                   cache_spec, cache_spec],
        out_shape=[jax.ShapeDtypeStruct((n_steps0 * steps_per_seq * rows, 1024), BF16),
                   cache_shape, cache_shape],
        compiler_params=_cparams(("arbitrary", "arbitrary")),
        name="swa_prompt" if chain else "swa_sample",
    )(*args)


def _ret_tables(chunk, tv, pos0, t_len):
    lg = np.log1p(-(2.0 ** (-5.0 - np.arange(B_HEADS, dtype=np.float64))))
    idx = np.arange(chunk, dtype=np.float64)
    valid = idx < tv
    diff = idx[:, None] - idx[None, :]
    dmat = np.where((diff >= 0) & valid[:, None] & valid[None, :],
                    np.exp(np.maximum(diff, 0.0)[None] * lg[:, None, None]), 0.0)
    din = np.exp((idx + 1.0)[:, None] * lg[None])
    din = np.repeat(din, B_DV, axis=1)
    kfac = np.where(valid[:, None], np.exp((tv - 1.0 - idx)[:, None] * lg[None]), 0.0)
    kfac = np.repeat(kfac, B_DK, axis=1)
    gc = np.repeat(np.exp(tv * lg), B_DK)[:, None] * np.ones((1, B_DV))
    half = B_DK // 2
    inv = ROPE_BASE ** (-np.arange(half, dtype=np.float64) / half)
    ang = (pos0 + np.arange(t_len, dtype=np.float64))[:, None] * inv[None]
    cos = np.tile(np.cos(ang), (1, 2 * B_HEADS))
    sin = np.tile(np.concatenate([-np.sin(ang), np.sin(ang)], axis=1), (1, B_HEADS))
    f = lambda a: jnp.asarray(a, dtype=F32)
    return f(dmat), f(din), f(kfac), f(gc), f(cos), f(sin)


def _ret_kernel(q_ref, k_ref, v_ref, g_ref, cos_ref, sin_ref, dmat_ref, din_ref, kfac_ref,
                gc_ref, gng_ref, gnb_ref, s0_ref, *rest, nb, chunk):
    o_ref, s_ref = rest[-2], rest[-1]

    @pl.when(pl.program_id(1) == 0)
    def _():
        s_ref[...] = s0_ref[...]

    cos = cos_ref[...]
    sin = sin_ref[...]
    width = B_HEADS * B_DK
    lane = lax.broadcasted_iota(jnp.int32, (1, width), 1)
    first_half = (lane % B_DK) < (B_DK // 2)
    lane_head = lane // B_DK

    def rope(x):
        partner = jnp.where(first_half, pltpu.roll(x, width - B_DK // 2, 1),
                            pltpu.roll(x, B_DK // 2, 1))
        return x * cos + partner * sin

    gc = gc_ref[...]
    for sb in range(nb):
        rws = slice(sb * chunk, (sb + 1) * chunk)
        qr = rope(q_ref[rws, :])
        kr = rope(k_ref[rws, :]) * (B_DK ** -0.5)
        kb = kr.astype(BF16)
        kdec = (kr * kfac_ref[...]).astype(BF16)
        s_old = s_ref[sb]
        s_b = s_old.astype(BF16)
        for h in range(B_HEADS):
            vs = slice(h * B_DV, (h + 1) * B_DV)
            rs = slice(h * B_DK, (h + 1) * B_DK)
            qm = jnp.where(lane_head == h, qr, 0.0).astype(BF16)
            a = lax.dot_general(qm, kb, (((1,), (1,)), ((), ())),
                                preferred_element_type=F32) * dmat_ref[h]
            vh = v_ref[rws, vs].astype(BF16)
            o = jnp.dot(a.astype(BF16), vh, preferred_element_type=F32)
            o = o + jnp.dot(qm, s_b, preferred_element_type=F32) * din_ref[:, vs]
            u = lax.dot_general(kdec, vh, (((0,), (0,)), ((), ())), preferred_element_type=F32)
            s_ref[sb, rs, :] = gc[rs, :] * s_old[rs, :] + u[rs, :]
            mu = jnp.mean(o, axis=-1, keepdims=True)
            oc = o - mu
            var = jnp.mean(oc * oc, axis=-1, keepdims=True)
            y = oc * lax.rsqrt(var + EPS) * gng_ref[:, vs] + gnb_ref[:, vs]
            y = y * _silu(g_ref[rws, vs])
            o_ref[rws, vs] = y.astype(o_ref.dtype)


def _ret(proj, s0, gn_g, gn_b, tables, *, chunk, nb, n_seq, chunks_per_seq, row_block0,
         table_per_chunk):
    dmat, din, kfac, gc, cos, sin = tables
    qcol, kcol, vcol, gcol = 1536 // 256, 1792 // 256, 2048 // 512, 2560 // 512
    assert nb == 1 or chunks_per_seq == 1
    rows = nb * chunk

    def rb(s, c):
        return row_block0 + s * chunks_per_seq + c

    tmap = (lambda s, c: (c, 0)) if table_per_chunk else (lambda s, c: (0, 0))
    const2 = lambda s, c: (0, 0)
    in_specs = [
        pl.BlockSpec((rows, 256), lambda s, c: (rb(s, c), qcol)),
        pl.BlockSpec((rows, 256), lambda s, c: (rb(s, c), kcol)),
        pl.BlockSpec((rows, 512), lambda s, c: (rb(s, c), vcol)),
        pl.BlockSpec((rows, 512), lambda s, c: (rb(s, c), gcol)),
        pl.BlockSpec((chunk, 256), tmap),
        pl.BlockSpec((chunk, 256), tmap),
        pl.BlockSpec((B_HEADS, chunk, chunk), lambda s, c: (0, 0, 0)),
        pl.BlockSpec((chunk, 512), const2),
        pl.BlockSpec((chunk, 256), const2),
        pl.BlockSpec((256, B_DV), const2),
        pl.BlockSpec((1, 512), const2),
        pl.BlockSpec((1, 512), const2),
        pl.BlockSpec((nb, 256, B_DV), lambda s, c: (s, 0, 0)),
    ]
    args = [proj, proj, proj, proj, cos, sin, dmat, din, kfac, gc,
            gn_g.reshape(1, 512), gn_b.reshape(1, 512), s0.reshape(n_seq, 256, B_DV)]
    o, s_new = pl.pallas_call(
        functools.partial(_ret_kernel, nb=nb, chunk=chunk),
        grid=(n_seq // nb, chunks_per_seq),
        in_specs=in_specs,
        out_specs=[
            pl.BlockSpec((rows, 512), lambda s, c: (s * chunks_per_seq + c, 0)),
            pl.BlockSpec((nb, 256, B_DV), lambda s, c: (s, 0, 0)),
        ],
        out_shape=[
            jax.ShapeDtypeStruct((n_seq // nb * chunks_per_seq * rows, 512), BF16),
            jax.ShapeDtypeStruct((n_seq, 256, B_DV), F32),
        ],
        compiler_params=_cparams(("arbitrary", "arbitrary")),
        name="ret_prompt" if nb == 1 else "ret_sample",
    )(*args)
    return o, s_new.reshape(n_seq, B_HEADS, B_DK, B_DV)


HG_FAST = 32
HG_LIM = 60.0
HG_HDR = 32


def _hgrn_kernel(q_ref, f_ref, i_ref, g_ref, lb_ref, ng_ref, st0_ref, *rest, rows, seq_rows, tv):
    use_fast = seq_rows % HG_FAST == 0
    if use_fast:
        o_ref, st_ref, q_s, k_s, cs_s, qd_s, ki_s, kd_s, eb_s, stb_s = rest[-10:]
    else:
        o_ref, st_ref, q_s, k_s, cs_s = rest[-5:]
    nseq = rows // seq_rows

    @pl.when(pl.program_id(1) == 0)
    def _():
        st_ref[...] = st0_ref[...]

    z = f_ref[...]
    lb = lb_ref[...]
    sig = _sigmoid(z)
    lf = jnp.log(lb + (1.0 - lb) * sig)
    k = (1.0 - lb) * _sigmoid(-z)
    if tv < seq_rows:
        rvalid = (lax.broadcasted_iota(jnp.int32, (rows, 1), 0) % seq_rows) < tv
        lf = jnp.where(rvalid, lf, 0.0)
        k = jnp.where(rvalid, k, 0.0)
    ri = lax.broadcasted_iota(jnp.int32, (rows, rows), 0)
    ci = lax.broadcasted_iota(jnp.int32, (rows, rows), 1)
    tri = jnp.where(ci <= ri, 1.0, 0.0).astype(BF16)
    cs = jnp.zeros((rows, C_HEADS * C_DK), F32)
    rem = lf
    for _ in range(3):
        part = rem.astype(BF16)
        cs = cs + jnp.dot(tri, part, preferred_element_type=F32)
        rem = rem - part.astype(F32)
    q_s[...] = _silu(q_ref[...])
    k_s[...] = k
    cs_s[0:HG_HDR, :] = jnp.zeros((HG_HDR, C_HEADS * C_DK), F32)
    cs_s[HG_HDR:HG_HDR + rows, :] = cs
    ng = ng_ref[...]

    nt_dims = (((1,), (1,)), ((), ()))
    tn_dims = (((0,), (0,)), ((), ()))

    def finish(o, rsl, hs):
        y = o * lax.rsqrt(jnp.mean(o * o, axis=-1, keepdims=True) + EPS) * ng
        return (y * _silu(g_ref[rsl, hs])).astype(o_ref.dtype)

    def pairwise_block(blk, carry):
        size = HG_BLK
        r0 = pl.multiple_of(blk * size, size)
        rsl = pl.ds(r0, size)
        sidx = 0 if nseq == 1 else blk // (seq_rows // size)
        rowi = lax.broadcasted_iota(jnp.int32, (size, 1), 0)
        for h in range(C_HEADS):
            hs = slice(h * C_DK, (h + 1) * C_DK)
            vsl = slice(h * C_DV, (h + 1) * C_DV)
            qb = q_s[rsl, hs]
            kb = k_s[rsl, hs]
            vb = i_ref[rsl, hs]
            prev = cs_s[pl.ds(HG_HDR + r0 - 8, 8), hs][7:8, :]
            bc = cs_s[pl.ds(HG_HDR + r0, size), hs] - prev
            blast = bc[size - 1:size, :]
            qd = (qb * jnp.exp(bc)).astype(BF16)
            kd = (kb * jnp.exp(blast - bc)).astype(BF16)
            st = st_ref[sidx, vsl, :]
            o = lax.dot_general(qd, st.astype(BF16), nt_dims, preferred_element_type=F32)
            for j in range(size):
                dec = jnp.exp(jnp.minimum(bc - bc[j:j + 1, :], 0.0))
                a = jnp.sum(qb * kb[j:j + 1, :] * dec, axis=-1, keepdims=True)
                a = jnp.where(rowi >= j, a, 0.0)
                o = o + a * vb[j:j + 1, :]
            u = lax.dot_general(vb.astype(BF16), kd, tn_dims, preferred_element_type=F32)
            st_ref[sidx, vsl, :] = st * jnp.exp(blast) + u
            o_ref[rsl, vsl] = finish(o, rsl, hs)
        return carry

    def factorised_tile():
        nblk = rows // HG_FAST
        for b in range(nblk):
            r = slice(b * HG_FAST, (b + 1) * HG_FAST)
            lo = HG_HDR + b * HG_FAST
            prev = cs_s[lo - 8:lo, :][7:8, :]
            bc = cs_s[lo:lo + HG_FAST, :] - prev
            blast = bc[HG_FAST - 1:HG_FAST, :]
            qb = q_s[r, :]
            kb = k_s[r, :]
            qd_s[r, :] = (qb * jnp.exp(bc)).astype(BF16)
            ki_s[r, :] = (kb * jnp.exp(-bc)).astype(BF16)
            kd_s[r, :] = (kb * jnp.exp(blast - bc)).astype(BF16)
            eb_s[b:b + 1, :] = jnp.exp(blast)
        same = (ri // HG_FAST) == (ci // HG_FAST)
        keep = same & (ci <= ri)
        for h in range(C_HEADS):
            hs = slice(h * C_DK, (h + 1) * C_DK)
            vsl = slice(h * C_DV, (h + 1) * C_DV)
            vh = i_ref[:, hs].astype(BF16)
            st = st_ref[0, vsl, :]
            for b in range(nblk):
                r = slice(b * HG_FAST, (b + 1) * HG_FAST)
                stb_s[h, b] = st.astype(BF16)
                u = lax.dot_general(vh[r, :], kd_s[r, hs], tn_dims, preferred_element_type=F32)
                st = st * eb_s[b:b + 1, hs] + u
            st_ref[0, vsl, :] = st
            a = lax.dot_general(qd_s[:, hs], ki_s[:, hs], nt_dims, preferred_element_type=F32)
            a = jnp.where(keep, a, 0.0)
            o = jnp.dot(a.astype(BF16), vh, preferred_element_type=F32)
            inter = [lax.dot_general(qd_s[b * HG_FAST:(b + 1) * HG_FAST, hs], stb_s[h, b], nt_dims,
                                     preferred_element_type=F32) for b in range(nblk)]
            o = o + jnp.concatenate(inter, axis=0)
            o_ref[:, vsl] = finish(o, slice(None), hs)

    def pairwise_tile():
        lax.fori_loop(0, rows // HG_BLK, pairwise_block, 0)

    if use_fast:
        worst = jnp.max(cs_s[HG_HDR - HG_FAST:HG_HDR - HG_FAST + rows, :] - cs)
        bounded = worst < HG_LIM
        pl.when(bounded)(factorised_tile)
        pl.when(jnp.logical_not(bounded))(pairwise_tile)
    else:
        pairwise_tile()


def _hgrn(proj, st0, lb, ng, *, rows, seq_rows, tv, n_seq, tiles_per_seq, row_block0):
    qcol, fcol, icol, gcol = 3072 // 512, 3584 // 512, 4096 // 512, 4608 // 512
    nst = max(rows // seq_rows, 1)
    assert nst == 1 or tiles_per_seq == 1
    width = C_HEADS * C_DK
    scratch = [pltpu.VMEM((rows, width), F32), pltpu.VMEM((rows, width), F32),
               pltpu.VMEM((HG_HDR + rows, width), F32)]
    if seq_rows % HG_FAST == 0:
        scratch += [pltpu.VMEM((rows, width), BF16) for _ in range(3)]
        scratch += [pltpu.VMEM((rows // HG_FAST, width), F32),
                    pltpu.VMEM((C_HEADS, rows // HG_FAST, C_DV, C_DK), BF16)]

    def rb(s, c):
        return row_block0 + s * tiles_per_seq + c

    const2 = lambda s, c: (0, 0)
    in_specs = [
        pl.BlockSpec((rows, 512), lambda s, c: (rb(s, c), qcol)),
        pl.BlockSpec((rows, 512), lambda s, c: (rb(s, c), fcol)),
        pl.BlockSpec((rows, 512), lambda s, c: (rb(s, c), icol)),
        pl.BlockSpec((rows, 512), lambda s, c: (rb(s, c), gcol)),
        pl.BlockSpec((1, 512), const2),
        pl.BlockSpec((1, C_DV), const2),
        pl.BlockSpec((nst, 512, C_DK), lambda s, c: (s, 0, 0)),
    ]
    args = [proj, proj, proj, proj, lb.reshape(1, 512), ng.reshape(1, C_DV), st0]
    return pl.pallas_call(
        functools.partial(_hgrn_kernel, rows=rows, seq_rows=seq_rows, tv=tv),
        grid=(n_seq // nst, tiles_per_seq),
        in_specs=in_specs,
        out_specs=[
            pl.BlockSpec((rows, 512), lambda s, c: (s * tiles_per_seq + c, 0)),
            pl.BlockSpec((nst, 512, C_DK), lambda s, c: (s, 0, 0)),
        ],
        out_shape=[
            jax.ShapeDtypeStruct((n_seq // nst * tiles_per_seq * rows, 512), BF16),
            jax.ShapeDtypeStruct((n_seq, 512, C_DK), F32),
        ],
        scratch_shapes=scratch,
        compiler_params=_cparams(("arbitrary", "arbitrary")),
        name="hgrn_prompt" if nst == 1 else "hgrn_sample",
    )(*args)


def _mm_out_kernel(pa_ref, pb_ref, pc_ref, sa_ref, sb_ref, sc_ref, w_ref, x_ref, g_ref, o_ref, wb_ref):
    i = pl.program_id(1)

    @pl.when(i == 0)
    def _():
        wb_ref[...] = w_ref[...].astype(BF16)

    def mix(oa_ref, ob_ref, oc_ref):
        acc = jnp.dot(oa_ref[...], wb_ref[0:1024, :], preferred_element_type=F32)
        acc = acc + jnp.dot(ob_ref[...], wb_ref[1024:1536, :], preferred_element_type=F32)
        acc = acc + jnp.dot(oc_ref[...], wb_ref[1536:2048, :], preferred_element_type=F32)
        o_ref[...] = x_ref[...] + _tile_mod(g_ref, i) * acc

    @pl.when(i < PROMPT_TILES)
    def _():
        mix(pa_ref, pb_ref, pc_ref)

    @pl.when(i == PROMPT_TILES)
    def _():
        mix(sa_ref, sb_ref, sc_ref)


def _mm_out(prompt_mix, sample_mix, w_out, x, mod, l):
    tn = 1024
    ncol = D_MODEL // tn
    pmap = lambda j, i: (jnp.minimum(i, PROMPT_TILES - 1), 0)
    smap = lambda j, i: (0, 0)
    return pl.pallas_call(
        _mm_out_kernel,
        grid=(ncol, N_TILES),
        in_specs=[
            pl.BlockSpec((TM, 1024), pmap), pl.BlockSpec((TM, 512), pmap), pl.BlockSpec((TM, 512), pmap),
            pl.BlockSpec((TM, 1024), smap), pl.BlockSpec((TM, 512), smap), pl.BlockSpec((TM, 512), smap),
            pl.BlockSpec((None, D_MODEL, tn), lambda j, i: (l, 0, j)),
            pl.BlockSpec((TM, tn), lambda j, i: (i, j)),
            _mod_spec(l, "g1", tn, lambda j, i: j),
        ],
        out_specs=pl.BlockSpec((TM, tn), lambda j, i: (i, j)),
        out_shape=jax.ShapeDtypeStruct((N_ROWS, D_MODEL), F32),
        scratch_shapes=[pltpu.VMEM((D_MODEL, tn), BF16)],
        compiler_params=_cparams(("arbitrary", "arbitrary")),
        name="mm_out",
    )(*prompt_mix, *sample_mix, w_out, x, mod)


META_GSEL = 0
META_RANK = 1
META_GATE = 20
XW = D_MODEL + 128


def _route_kernel(x_ref, g_ref, sc_ref, sh_ref, whl_ref, wh_ref, br_ref, hx_ref, meta_ref, cnt_ref,
                  run_ref, sc_st, sh_st, hi_s, lo_s):
    i = pl.program_id(0)

    @pl.when(i == 0)
    def _():
        run_ref[...] = jnp.zeros_like(run_ref)

    _stage_mod(sc_ref, sc_st, i)
    _stage_mod(sh_ref, sh_st, i)
    lg = g_ref[...]
    for s in range(N_SLABS):
        r = _slab(s)
        h = _norm_mod_slab(x_ref[r, :], lg, sc_st[s:s + 1, :], sh_st[s:s + 1, :])
        hx_ref[r, 0:D_MODEL] = h
        hi = h.astype(BF16)
        hi_s[r, :] = hi
        lo_s[r, :] = (h - hi.astype(F32)).astype(BF16)
    l1 = jnp.dot(hi_s[...], whl_ref[...], preferred_element_type=F32)
    l2 = jnp.dot(lo_s[...], wh_ref[...], preferred_element_type=F32)
    logits = l1[:, 0:128] + l1[:, 128:256] + l2 + br_ref[...]
    col = lax.broadcasted_iota(jnp.int32, logits.shape, 1).astype(F32)
    big = 1e9
    is_g = col < N_GROUPS
    gl = jnp.where(is_g, logits, -jnp.inf)
    gmax = jnp.max(gl, axis=-1, keepdims=True)
    gsel = jnp.min(jnp.where(gl == gmax, col, big), axis=-1, keepdims=True)
    gden = jnp.sum(jnp.where(is_g, jnp.exp(logits - gmax), 0.0), axis=-1, keepdims=True)
    g_w = 1.0 / gden
    e0 = N_GROUPS + EXP_PER_GROUP * gsel
    in_grp = (col >= e0) & (col < e0 + EXP_PER_GROUP)
    el = jnp.where(in_grp, logits, -jnp.inf)
    v1 = jnp.max(el, axis=-1, keepdims=True)
    i1 = jnp.min(jnp.where(el == v1, col, big), axis=-1, keepdims=True)
    el2 = jnp.where(col == i1, -jnp.inf, el)
    v2 = jnp.max(el2, axis=-1, keepdims=True)
    i2 = jnp.min(jnp.where(el2 == v2, col, big), axis=-1, keepdims=True)
    t = jnp.exp(v2 - v1)
    p1 = 1.0 / (1.0 + t)
    a1 = p1 * g_w
    a2 = t * p1 * g_w
    meta = jnp.where(col == META_GATE + (i1 - e0), a1, 0.0)
    meta = meta + jnp.where(col == META_GATE + (i2 - e0), a2, 0.0)
    meta = meta + jnp.where(col == META_GSEL, gsel, 0.0)
    rows = logits.shape[0]
    rowi = lax.broadcasted_iota(jnp.int32, (rows, 1), 0)
    is_tok = (i < PROMPT_TILES) | (rowi % SROWS == 0)
    onehot = jnp.where((col == gsel) & is_tok, 1.0, 0.0)
    ri = lax.broadcasted_iota(jnp.int32, (rows, rows), 0)
    ci = lax.broadcasted_iota(jnp.int32, (rows, rows), 1)
    tri = jnp.where(ci <= ri, 1.0, 0.0).astype(BF16)
    incl = jnp.dot(tri, onehot.astype(BF16), preferred_element_type=F32)
    run = run_ref[...]
    rank = jnp.sum(jnp.where(col == gsel, incl + run - 1.0, 0.0), axis=-1, keepdims=True)
    meta = meta + jnp.where(col == META_RANK, rank, 0.0)
    meta_ref[...] = meta
    hx_ref[:, D_MODEL:XW] = meta
    run = run + incl[rows - 1:rows, :]
    run_ref[...] = run
    cnt_ref[...] = run


def _router_weights(w_rg, b_rg, w_re, b_re):
    pad = 128 - N_GROUPS - N_EXPERTS
    wr = jnp.concatenate([w_rg, w_re, jnp.zeros((DEPTH, D_MODEL, pad), F32)], axis=2)
    w_hi = wr.astype(BF16)
    w_lo = (wr - w_hi.astype(F32)).astype(BF16)
    br = jnp.concatenate([b_rg, b_re, jnp.zeros((DEPTH, pad), F32)], axis=1).reshape(DEPTH, 1, 128)
    return jnp.concatenate([w_hi, w_lo], axis=2), w_hi, br


def _route(x, g, mod, whl, w_hi, br, l):
    return pl.pallas_call(
        _route_kernel,
        grid=(N_TILES,),
        in_specs=[
            pl.BlockSpec((TM, D_MODEL), lambda i: (i, 0)),
            pl.BlockSpec((1, D_MODEL), lambda i: (0, 0)),
            _mod_spec(l, "sc2"),
            _mod_spec(l, "sh2"),
            pl.BlockSpec((None, D_MODEL, 256), lambda i: (l, 0, 0)),
            pl.BlockSpec((None, D_MODEL, 128), lambda i: (l, 0, 0)),
            pl.BlockSpec((None, 1, 128), lambda i: (l, 0, 0)),
        ],
        out_specs=[
            pl.BlockSpec((TM, XW), lambda i: (i, 0)),
            pl.BlockSpec((TM, 128), lambda i: (i, 0)),
            pl.BlockSpec((1, 128), lambda i: (0, 0)),
        ],
        out_shape=[
            jax.ShapeDtypeStruct((N_ROWS, XW), F32),
            jax.ShapeDtypeStruct((N_ROWS, 128), F32),
            jax.ShapeDtypeStruct((1, 128), F32),
        ],
        scratch_shapes=[pltpu.VMEM((1, 128), F32),
                        pltpu.VMEM((N_SLABS, D_MODEL), F32), pltpu.VMEM((N_SLABS, D_MODEL), F32),
                        pltpu.VMEM((TM, D_MODEL), BF16), pltpu.VMEM((TM, D_MODEL), BF16)],
        compiler_params=_cparams(("arbitrary",)),
        name="route",
    )(x, g.reshape(1, D_MODEL), mod, mod, whl, w_hi, br)


def _moe_positions(meta, cnt):
    i32 = jnp.int32
    counts = cnt[0, :N_GROUPS].astype(i32)
    ntile = (counts + TM - 1) // TM
    tstart = jnp.cumsum(ntile) - ntile
    gsel = meta[:, META_GSEL].astype(i32)
    rank = meta[:, META_RANK].astype(i32)
    first_row = jnp.zeros_like(gsel)
    for g in range(N_GROUPS):
        first_row = jnp.where(gsel == g, tstart[g] * TM, first_row)
    pos = first_row + rank
    pos_all = jnp.concatenate([pos[:NP_ROWS], jnp.repeat(pos[NP_ROWS::SROWS], SROWS)])
    plan = jnp.concatenate([tstart, ntile]).astype(i32)
    return pos_all, plan


def _scatter_kernel(pos_ref, src_ref, buf_hbm, out_hbm, sem):
    del buf_hbm
    i = pl.program_id(0)
    base = i * TM

    def row_copy(r):
        return pltpu.make_async_copy(src_ref.at[pl.ds(r, 1), :],
                                     out_hbm.at[pl.ds(pos_ref[base + r], 1), :], sem)

    def run(n, stride):
        def issue(k, c):
            row_copy(k * stride).start()
            return c

        def drain(k, c):
            row_copy(k * stride).wait()
            return c

        lax.fori_loop(0, n, issue, 0, unroll=8)
        lax.fori_loop(0, n, drain, 0, unroll=8)

    @pl.when(i < PROMPT_TILES)
    def _():
        run(TM, 1)

    @pl.when(i == PROMPT_TILES)
    def _():
        run(DEC_BATCH, SROWS)


def _moe_scatter(hx, buf, pos_all):
    grid_spec = pltpu.PrefetchScalarGridSpec(
        num_scalar_prefetch=1,
        grid=(N_TILES,),
        in_specs=[pl.BlockSpec((TM, XW), lambda i, pos: (i, 0)), pl.BlockSpec(memory_space=pl.ANY)],
        out_specs=pl.BlockSpec(memory_space=pl.ANY),
        scratch_shapes=[pltpu.SemaphoreType.DMA(())],
    )
    return pl.pallas_call(
        _scatter_kernel,
        grid_spec=grid_spec,
        out_shape=jax.ShapeDtypeStruct((MOE_ROWS, XW), F32),
        input_output_aliases={2: 0},
        compiler_params=_cparams(("arbitrary",)),
        name="moe_scatter",
    )(pos_all, hx, buf)


def _work_item(w, plan_ref, per_tile):
    i32 = jnp.int32
    nt = [plan_ref[N_GROUPS + g] for g in range(N_GROUPS)]
    ts = [plan_ref[g] for g in range(N_GROUPS)]
    ends = []
    acc = 0
    for g in range(N_GROUPS):
        acc = acc + per_tile * nt[g]
        ends.append(acc)
    valid = w < ends[-1]
    wc = jnp.maximum(jnp.minimum(w, ends[-1] - 1), 0)
    g = sum((wc >= ends[k]).astype(i32) for k in range(N_GROUPS - 1))

    def pick(vals):
        out = vals[N_GROUPS - 1]
        for k in range(N_GROUPS - 2, -1, -1):
            out = jnp.where(g == k, vals[k], out)
        return out

    r = wc - pick([0] + ends[:-1])
    ntg = pick(nt)
    c = sum((r >= m * ntg).astype(i32) for m in range(1, per_tile))
    t_in = r - c * ntg
    t = pick(ts) + t_in
    extra = jnp.maximum(w - ends[-1], 0)
    used = ts[N_GROUPS - 1] + nt[N_GROUPS - 1]
    t_out = jnp.where(valid, t, used + extra // per_tile)
    c_out = jnp.where(valid, c, extra % per_tile)
    return g, c, t, t_in == 0, valid, t_out, c_out


HID_EXPERTS = 2
HID_NCOL = EXP_PER_GROUP // HID_EXPERTS
HID_WORK = MOE_TILES * HID_NCOL


def _hidden_kernel(plan_ref, x_ref, w1_ref, w3_ref, o_ref, w1b, w3b):
    _, c, _, first, valid, _, _ = _work_item(pl.program_id(0), plan_ref, HID_NCOL)

    @pl.when(jnp.logical_not(valid))
    def _():
        o_ref[...] = jnp.zeros_like(o_ref)

    @pl.when(valid & first)
    def _():
        for j in range(HID_EXPERTS):
            cs = slice(j * D_EXPERT, (j + 1) * D_EXPERT)
            w1b[:, cs] = w1_ref[j].astype(BF16)
            w3b[:, cs] = w3_ref[j].astype(BF16)

    @pl.when(valid)
    def _():
        x = x_ref[:, 0:D_MODEL].astype(BF16)
        m = x_ref[:, D_MODEL:XW]
        col = lax.broadcasted_iota(jnp.int32, m.shape, 1)
        for j in range(HID_EXPERTS):
            cs = slice(j * D_EXPERT, (j + 1) * D_EXPERT)
            a = jnp.dot(x, w1b[:, cs], preferred_element_type=F32)
            b = jnp.dot(x, w3b[:, cs], preferred_element_type=F32)
            lane = META_GATE + c * HID_EXPERTS + j
            gate = jnp.sum(jnp.where(col == lane, m, 0.0), axis=-1, keepdims=True)
            o_ref[:, cs] = (_silu(a) * b * gate).astype(o_ref.dtype)


def _moe_hidden(xs, w1, w3, plan, l):
    def item(w, plan):
        return _work_item(w, plan, HID_NCOL)

    def w_map(w, plan):
        it = item(w, plan)
        return (l, it[0] * HID_NCOL + it[1], 0, 0)

    wide = HID_EXPERTS * D_EXPERT
    grid_spec = pltpu.PrefetchScalarGridSpec(
        num_scalar_prefetch=1,
        grid=(HID_WORK,),
        in_specs=[
            pl.BlockSpec((TM, XW), lambda w, plan: (item(w, plan)[2], 0)),
            pl.BlockSpec((None, HID_EXPERTS, D_MODEL, D_EXPERT), w_map),
            pl.BlockSpec((None, HID_EXPERTS, D_MODEL, D_EXPERT), w_map),
        ],
        out_specs=pl.BlockSpec((TM, wide), lambda w, plan: item(w, plan)[5:7]),
        scratch_shapes=[pltpu.VMEM((D_MODEL, wide), BF16), pltpu.VMEM((D_MODEL, wide), BF16)],
    )
    return pl.pallas_call(
        _hidden_kernel,
        grid_spec=grid_spec,
        out_shape=jax.ShapeDtypeStruct((MOE_ROWS, EXP_PER_GROUP * D_EXPERT), BF16),
        compiler_params=_cparams(("arbitrary",), BIG_VMEM_LIMIT),
        name="moe_hidden",
    )(plan, xs, w1, w3)


def _down_kernel(plan_ref, h_ref, w_ref, o_ref, wb):
    _, _, _, first, valid, _, _ = _work_item(pl.program_id(0), plan_ref, DOWN_NCOL)

    @pl.when(jnp.logical_not(valid))
    def _():
        o_ref[...] = jnp.zeros_like(o_ref)

    @pl.when(valid & first)
    def _():
        wb[...] = w_ref[...].astype(BF16)

    @pl.when(valid)
    def _():
        o_ref[...] = jnp.dot(h_ref[...], wb[...], preferred_element_type=F32)


def _moe_down(hid, w2g, plan, l):
    kdim = EXP_PER_GROUP * D_EXPERT

    def item(w, plan):
        return _work_item(w, plan, DOWN_NCOL)

    grid_spec = pltpu.PrefetchScalarGridSpec(
        num_scalar_prefetch=1,
        grid=(DOWN_WORK,),
        in_specs=[
            pl.BlockSpec((TM, kdim), lambda w, plan: (item(w, plan)[2], 0)),
            pl.BlockSpec((None, None, kdim, DOWN_TN),
                         lambda w, plan: (l, item(w, plan)[0], 0, item(w, plan)[1])),
        ],
        out_specs=pl.BlockSpec((TM, DOWN_TN), lambda w, plan: item(w, plan)[5:7]),
        scratch_shapes=[pltpu.VMEM((kdim, DOWN_TN), BF16)],
    )
    return pl.pallas_call(
        _down_kernel,
        grid_spec=grid_spec,
        out_shape=jax.ShapeDtypeStruct((MOE_ROWS, D_MODEL), F32),
        compiler_params=_cparams(("arbitrary",), BIG_VMEM_LIMIT),
        name="moe_down",
    )(plan, hid, w2g)


def _res_kernel(pos_ref, x_ref, y_hbm, g2_ref, *rest, with_norm):
    if with_norm:
        lg_ref, sc_ref, sh_ref, xo_ref, ho_ref, ybuf, sem, g2_st, sc_st, sh_st = rest
    else:
        yp_ref, ys_ref, ybuf, sem, g2_st = rest
    i = pl.program_id(0)
    slot = i % 2

    def row_copy(tile, k, sl):
        return pltpu.make_async_copy(y_hbm.at[pl.ds(pos_ref[tile * TM + k], 1), :],
                                     ybuf.at[sl, pl.ds(k, 1), :], sem.at[sl])

    def issue_tile(tile, sl):
        def issue(k, c):
            row_copy(tile, k, sl).start()
            return c
        lax.fori_loop(0, TM, issue, 0, unroll=8)

    @pl.when(i == 0)
    def _():
        issue_tile(0, 0)

    @pl.when(i + 1 < N_TILES)
    def _():
        issue_tile(i + 1, 1 - slot)

    _stage_mod(g2_ref, g2_st, i)
    if with_norm:
        _stage_mod(sc_ref, sc_st, i)
        _stage_mod(sh_ref, sh_st, i)

    def drain(k, c):
        row_copy(i, k, slot).wait()
        return c

    lax.fori_loop(0, TM, drain, 0, unroll=8)

    def new_x(s):
        r = _slab(s)
        return x_ref[r, :] + g2_st[s:s + 1, :] * ybuf[slot, r, :]

    if with_norm:
        lg = lg_ref[...]
        for s in range(N_SLABS):
            xn = new_x(s)
            xo_ref[_slab(s), :] = xn
            hn = _norm_mod_slab(xn, lg, sc_st[s:s + 1, :], sh_st[s:s + 1, :])
            ho_ref[_slab(s), :] = hn.astype(ho_ref.dtype)
    else:
        @pl.when(i < PROMPT_TILES)
        def _():
            for s in range(N_SLABS):
                yp_ref[_slab(s), :] = new_x(s)

        @pl.when(i == PROMPT_TILES)
        def _():
            for s in range(N_SLABS):
                ys_ref[s:s + 1, :] = new_x(s)[0:1, :]


def _residual(x, y_sorted, pos_all, mod, l, next_g=None):
    with_norm = next_g is not None
    row_spec = pl.BlockSpec((TM, D_MODEL), lambda i, pos: (i, 0))
    in_specs = [row_spec, pl.BlockSpec(memory_space=pl.ANY), _mod_spec(l, "g2")]
    args = [pos_all, x, y_sorted, mod]
    stage = pltpu.VMEM((N_SLABS, D_MODEL), F32)
    scratch = [pltpu.VMEM((2, TM, D_MODEL), F32), pltpu.SemaphoreType.DMA((2,)), stage]
    if with_norm:
        in_specs += [pl.BlockSpec((1, D_MODEL), lambda i, pos: (0, 0)),
                     _mod_spec(l + 1, "sc1"), _mod_spec(l + 1, "sh1")]
        args += [next_g.reshape(1, D_MODEL), mod, mod]
        out_specs = [row_spec, row_spec]
        out_shape = [jax.ShapeDtypeStruct((N_ROWS, D_MODEL), F32),
                     jax.ShapeDtypeStruct((N_ROWS, D_MODEL), BF16)]
        scratch += [stage, stage]
    else:
        out_specs = [pl.BlockSpec((TM, D_MODEL), lambda i, pos: (jnp.minimum(i, PROMPT_TILES - 1), 0)),
                     pl.BlockSpec((DEC_BATCH, D_MODEL), lambda i, pos: (0, 0))]
        out_shape = [jax.ShapeDtypeStruct((NP_ROWS, D_MODEL), F32),
                     jax.ShapeDtypeStruct((DEC_BATCH, D_MODEL), F32)]
    grid_spec = pltpu.PrefetchScalarGridSpec(
        num_scalar_prefetch=1,
        grid=(N_TILES,),
        in_specs=in_specs,
        out_specs=out_specs,
        scratch_shapes=scratch,
    )
    return pl.pallas_call(
        functools.partial(_res_kernel, with_norm=with_norm),
        grid_spec=grid_spec,
        out_shape=out_shape,
        compiler_params=_cparams(("arbitrary",)),
        name="residual_norm" if with_norm else "residual",
    )(*args)


def kernel(x_prompt, x_sample, c_prompt, c_sample, cache_swa_k, cache_swa_v, state_ret, state_hgrn,
           w_ada, b_ada, ln1_g, w_in, qn_g, kn_g, attn_sinks, ret_gn_g, ret_gn_b, hgrn_lb, hgrn_ng,
           w_out, ln2_g, w_rg, b_rg, w_re, b_re, w1, w3, w2):
    i32 = jnp.int32
    lbp = jax.nn.softmax(hgrn_lb.astype(F32), axis=0)
    lb_all = jnp.cumsum(lbp, axis=0) - lbp[0]

    assert TM // SROWS == DEC_BATCH and NS_ROWS == TM
    c_all = jnp.concatenate([c_prompt, jnp.zeros((C_SAMPLE0 - BATCH, D_MODEL), F32), c_sample], axis=0)
    mod = _ada(c_all, w_ada, b_ada)

    ret_tab_p = _ret_tables(RET_CHUNK, RET_CHUNK, 0.0, SEQ)
    ret_tab_s = _ret_tables(SROWS, 1, float(PAST_LEN), SROWS)
    w2g = w2.reshape(DEPTH, N_GROUPS, EXP_PER_GROUP * D_EXPERT, D_MODEL)
    whl_all, whi_all, br_all = _router_weights(w_rg, b_rg, w_re, b_re)
    st_s_all = jnp.swapaxes(state_hgrn, -1, -2).reshape(DEPTH, DEC_BATCH, C_HEADS * C_DV, C_DK)
    ck_all = cache_swa_k.reshape(DEPTH, DEC_BATCH * WINDOW, A_KV * A_HD)
    cv_all = cache_swa_v.reshape(DEPTH, DEC_BATCH * WINDOW, A_KV * A_HD)
    xs_buf = jnp.zeros((MOE_ROWS, XW), F32)

    x, h = _norm_mod_first(x_prompt.reshape(NP_ROWS, D_MODEL), x_sample.reshape(DEC_BATCH, D_MODEL),
                           ln1_g[0], mod, 0)

    outs = {k: [] for k in ("kp", "vp", "rp", "hp", "ks", "vs", "rs", "hs")}
    for l in range(DEPTH):
        proj = _mm_in(h, w_in, l)

        oa_p, kp, vp = _swa(proj, proj, proj, attn_sinks[l], qn_g[l], kn_g[l], tq=WINDOW, nb=SWA_NB,
                            n_seq=BATCH, steps_per_seq=SEQ // (WINDOW * SWA_NB), row_block0=0, chain=True)
        srows = SAMPLE_NB * SROWS
        oa_s, ks, vs = _swa(proj, ck_all, cv_all, attn_sinks[l], qn_g[l], kn_g[l], tq=SROWS, nb=SAMPLE_NB,
                            n_seq=DEC_BATCH, steps_per_seq=1, row_block0=NP_ROWS // srows, chain=False,
                            layer=l)
        zero_ret = jnp.zeros((BATCH, B_HEADS, B_DK, B_DV), F32)
        ob_p, rp = _ret(proj, zero_ret, ret_gn_g[l], ret_gn_b[l], ret_tab_p, chunk=RET_CHUNK, nb=1,
                        n_seq=BATCH, chunks_per_seq=SEQ // RET_CHUNK, row_block0=0, table_per_chunk=True)
        ob_s, rs = _ret(proj, state_ret[l], ret_gn_g[l], ret_gn_b[l], ret_tab_s, chunk=SROWS, nb=SAMPLE_NB,
                        n_seq=DEC_BATCH, chunks_per_seq=1, row_block0=NP_ROWS // srows,
                        table_per_chunk=False)
        zero_hg = jnp.zeros((BATCH, C_HEADS * C_DV, C_DK), F32)
        oc_p, hp_t = _hgrn(proj, zero_hg, lb_all[l], hgrn_ng[l], rows=HG_TILE, seq_rows=HG_TILE,
                           tv=HG_TILE, n_seq=BATCH, tiles_per_seq=SEQ // HG_TILE, row_block0=0)
        oc_s, hs_t = _hgrn(proj, st_s_all[l], lb_all[l], hgrn_ng[l], rows=srows, seq_rows=SROWS, tv=1,
                           n_seq=DEC_BATCH, tiles_per_seq=1, row_block0=NP_ROWS // srows)

        x1 = _mm_out((oa_p, ob_p, oc_p), (oa_s, ob_s, oc_s), w_out, x, mod, l)

        hx, meta, cnt = _route(x1, ln2_g[l], mod, whl_all, whi_all, br_all, l)
        pos_all, plan = _moe_positions(meta, cnt)
        xs_buf = _moe_scatter(hx, xs_buf, pos_all)
        hid = _moe_hidden(xs_buf, w1, w3, plan, l)
        y_sorted = _moe_down(hid, w2g, plan, l)
        if l + 1 < DEPTH:
            x, h = _residual(x1, y_sorted, pos_all, mod, l, ln1_g[l + 1])
        else:
            y_prompt, y_sample = _residual(x1, y_sorted, pos_all, mod, l)

        outs["kp"].append(kp.reshape(BATCH, WINDOW, A_KV, A_HD))
        outs["vp"].append(vp.reshape(BATCH, WINDOW, A_KV, A_HD))
        outs["rp"].append(rp)
        outs["hp"].append(hp_t.reshape(BATCH, C_HEADS, C_DV, C_DK))
        outs["ks"].append(ks.reshape(DEC_BATCH, 1, A_KV, A_HD))
        outs["vs"].append(vs.reshape(DEC_BATCH, 1, A_KV, A_HD))
        outs["rs"].append(rs)
        outs["hs"].append(hs_t.reshape(DEC_BATCH, C_HEADS, C_DV, C_DK))

    y_prompt = y_prompt.reshape(BATCH, SEQ, D_MODEL)
    y_sample = y_sample.reshape(DEC_BATCH, 1, D_MODEL)
    st = lambda k: jnp.stack(outs[k])
    un_t = lambda k: jnp.swapaxes(st(k), -1, -2)
    roll_in = lambda cache, k: jnp.concatenate([cache[:, :, 1:], st(k)], axis=2)
    return (y_prompt, y_sample, st("kp"), st("vp"), st("rp"), un_t("hp"),
            roll_in(cache_swa_k, "ks"), roll_in(cache_swa_v, "vs"), st("rs"), un_t("hs"))
```

```python
import functools

import numpy as np
import jax
import jax.numpy as jnp
from jax import lax
from jax.experimental import pallas as pl
from jax.experimental.pallas import tpu as pltpu

F32 = jnp.float32
BF16 = jnp.bfloat16

D_MODEL = 2048
BATCH = 2
SEQ = 4096
DEPTH = 4
DEC_BATCH = 32
PAST_LEN = 16384
A_HD = 128
A_HEADS = 8
A_KV = 2
A_GROUP = 4
WINDOW = 128
B_DV = 128
B_DK = 64
B_HEADS = 4
C_DK = 128
C_DV = 128
C_HEADS = 4
ROPE_BASE = 10000.0
N_GROUPS = 4
EXP_PER_GROUP = 4
N_EXPERTS = 16
D_EXPERT = 512
EPS = 1e-6
NEG_BIG = -1e30
IN_WIDTH = 5120

SROWS = 16
NP_ROWS = BATCH * SEQ
NS_ROWS = DEC_BATCH * SROWS
N_ROWS = NP_ROWS + NS_ROWS
TM = 512
N_TILES = N_ROWS // TM
MOD_ROWS = 16
N_TOK = NP_ROWS + DEC_BATCH
MOE_TILES = (N_TOK + N_GROUPS * (TM - 1)) // TM
MOE_ROWS = MOE_TILES * TM
DOWN_TN = 2048
DOWN_NCOL = D_MODEL // DOWN_TN
DOWN_WORK = MOE_TILES * DOWN_NCOL

RET_CHUNK = 256
HG_TILE = 256
HG_BLK = 16
SAMPLE_NB = 8
SWA_NB = 4

VMEM_LIMIT = 48 * 1024 * 1024
BIG_VMEM_LIMIT = 56 * 1024 * 1024


def _cparams(sem, limit=VMEM_LIMIT):
    return pltpu.CompilerParams(dimension_semantics=sem, vmem_limit_bytes=limit)


def _sigmoid(x):
    return 1.0 / (1.0 + jnp.exp(-x))


def _silu(x):
    return x * _sigmoid(x)


C_ROWS = 40
C_SAMPLE0 = 8
PROMPT_TILES = NP_ROWS // TM
MOD_COL = dict(sh1=0, sc1=1, g1=2, sh2=3, sc2=4, g2=5)


def _tile_mod(m_ref, i):
    n = m_ref.shape[-1]
    seq = jnp.minimum(i // (SEQ // TM), BATCH - 1)
    prow = jnp.broadcast_to(m_ref[pl.ds(seq, 1), :], (DEC_BATCH, n))
    srows = m_ref[C_SAMPLE0:C_SAMPLE0 + DEC_BATCH, :]
    m = jnp.where(i < PROMPT_TILES, prow, srows)
    return jnp.broadcast_to(m[:, None, :], (DEC_BATCH, SROWS, n)).reshape(TM, n)


N_SLABS = TM // SROWS


def _stage_mod(m_ref, stage_ref, i):
    n = m_ref.shape[-1]
    seq = jnp.minimum(i // (SEQ // TM), BATCH - 1)
    prow = jnp.broadcast_to(m_ref[pl.ds(seq, 1), :], (N_SLABS, n))
    srows = m_ref[C_SAMPLE0:C_SAMPLE0 + DEC_BATCH, :]
    stage_ref[...] = jnp.where(i < PROMPT_TILES, prow, srows)


def _slab(s):
    return slice(s * SROWS, (s + 1) * SROWS)


def _mod_spec(l, name, tn=D_MODEL, colmap=None):
    base = MOD_COL[name] * (D_MODEL // tn)
    if colmap is None:
        return pl.BlockSpec((None, C_ROWS, tn), lambda *ids: (l, 0, base))
    return pl.BlockSpec((None, C_ROWS, tn), lambda *ids: (l, 0, base + colmap(*ids)))


def _ada_kernel(c_ref, w_ref, b_ref, o_ref):
    c = _silu(c_ref[...])
    o_ref[...] = jnp.dot(c.astype(BF16), w_ref[...].astype(BF16),
                         preferred_element_type=F32) + b_ref[...]


def _ada(c_all, w_ada, b_ada):
    rows = c_all.shape[0]
    tn = 2048
    ncol = w_ada.shape[2] // tn
    return pl.pallas_call(
        _ada_kernel,
        grid=(DEPTH, ncol),
        in_specs=[
            pl.BlockSpec((rows, D_MODEL), lambda l, j: (0, 0)),
            pl.BlockSpec((None, D_MODEL, tn), lambda l, j: (l, 0, j)),
            pl.BlockSpec((None, 1, tn), lambda l, j: (l, 0, j)),
        ],
        out_specs=pl.BlockSpec((None, rows, tn), lambda l, j: (l, 0, j)),
        out_shape=jax.ShapeDtypeStruct((DEPTH, rows, w_ada.shape[2]), F32),
        compiler_params=_cparams(("arbitrary", "arbitrary")),
        name="ada",
    )(c_all, w_ada, b_ada.reshape(DEPTH, 1, -1))


def _norm_mod_rows(x, g, sc_ref, sh_ref, i):
    y = x * lax.rsqrt(jnp.mean(x * x, axis=-1, keepdims=True) + EPS) * g
    return y * (1.0 + _tile_mod(sc_ref, i)) + _tile_mod(sh_ref, i)


def _norm_mod_slab(x, g, sc_row, sh_row):
    y = x * lax.rsqrt(jnp.mean(x * x, axis=-1, keepdims=True) + EPS) * g
    return y * (1.0 + sc_row) + sh_row


def _norm_kernel(xp_ref, xs_ref, g_ref, sc_ref, sh_ref, x_ref, o_ref):
    i = pl.program_id(0)
    xs = xs_ref[...]
    rid = lax.broadcasted_iota(jnp.int32, (DEC_BATCH, SROWS, 1), 1)
    blk = jnp.where(rid == 0, jnp.broadcast_to(xs[:, None, :], (DEC_BATCH, SROWS, D_MODEL)), 0.0)
    x = jnp.where(i < PROMPT_TILES, xp_ref[...], blk.reshape(TM, D_MODEL))
    x_ref[...] = x
    o_ref[...] = _norm_mod_rows(x, g_ref[...], sc_ref, sh_ref, i).astype(o_ref.dtype)


def _norm_mod_first(x_prompt, x_sample, g, mod, l):
    return pl.pallas_call(
        _norm_kernel,
        grid=(N_TILES,),
        in_specs=[
            pl.BlockSpec((TM, D_MODEL), lambda i: (jnp.minimum(i, PROMPT_TILES - 1), 0)),
            pl.BlockSpec((DEC_BATCH, D_MODEL), lambda i: (0, 0)),
            pl.BlockSpec((1, D_MODEL), lambda i: (0, 0)),
            _mod_spec(l, "sc1"),
            _mod_spec(l, "sh1"),
        ],
        out_specs=[
            pl.BlockSpec((TM, D_MODEL), lambda i: (i, 0)),
            pl.BlockSpec((TM, D_MODEL), lambda i: (i, 0)),
        ],
        out_shape=[
            jax.ShapeDtypeStruct((N_ROWS, D_MODEL), F32),
            jax.ShapeDtypeStruct((N_ROWS, D_MODEL), BF16),
        ],
        compiler_params=_cparams(("arbitrary",)),
        name="norm_first",
    )(x_prompt, x_sample, g.reshape(1, D_MODEL), mod, mod)


def _mm_in_kernel(x_ref, w_ref, o_ref, wb_ref):
    @pl.when(pl.program_id(1) == 0)
    def _():
        wb_ref[...] = w_ref[...].astype(BF16)

    o_ref[...] = jnp.dot(x_ref[...], wb_ref[...], preferred_element_type=F32)


MM_IN_TILES = 8


def _mm_in(h, w_in, l):
    tn = 1280
    ncol = IN_WIDTH // tn
    tm = N_ROWS // MM_IN_TILES
    return pl.pallas_call(
        _mm_in_kernel,
        grid=(ncol, MM_IN_TILES),
        in_specs=[
            pl.BlockSpec((tm, D_MODEL), lambda j, i: (i, 0)),
            pl.BlockSpec((None, D_MODEL, tn), lambda j, i: (l, 0, j)),
        ],
        out_specs=pl.BlockSpec((tm, tn), lambda j, i: (i, j)),
        out_shape=jax.ShapeDtypeStruct((N_ROWS, IN_WIDTH), F32),
        scratch_shapes=[pltpu.VMEM((D_MODEL, tn), BF16)],
        compiler_params=_cparams(("arbitrary", "arbitrary"), BIG_VMEM_LIMIT),
        name="mm_in",
    )(h, w_in)


def _rms(x, g):
    return x * lax.rsqrt(jnp.mean(x * x, axis=-1, keepdims=True) + EPS) * g


def _swa_kernel(sink_ref, q_ref, k_ref, v_ref, pk_ref, pv_ref, qg_ref, kg_ref,
                *rest, tq, nb, chain):
    o_ref, kc_ref, vc_ref = rest[-3:]
    qg = qg_ref[...]
    kg = kg_ref[...]
    nk = WINDOW + tq
    nq = A_GROUP * tq
    row = lax.broadcasted_iota(jnp.int32, (nq, nk), 0) & (tq - 1)
    col = lax.broadcasted_iota(jnp.int32, (nq, nk), 1)
    vis_prev = (col < WINDOW) & (col >= row)
    vis_own = (col >= WINDOW) & ((col - WINDOW) <= row)
    mask_all = vis_prev | vis_own
    if chain:
        mask_first = (vis_prev & (pl.program_id(1) > 0)) | vis_own
        last_step = pl.program_id(1) == pl.num_programs(1) - 1
    rgrp = lax.broadcasted_iota(jnp.int32, (nq, 1), 0) // tq
    last = [None] * A_KV
    for sb in range(nb):
        rs = slice(sb * tq, (sb + 1) * tq)
        mask = mask_first if (chain and sb == 0) else mask_all
        for kv in range(A_KV):
            ks = slice(kv * A_HD, (kv + 1) * A_HD)
            kn = _rms(k_ref[rs, ks], kg)
            vn = v_ref[rs, ks]
            if chain and sb > 0:
                pk, pv = last[kv]
            elif chain:
                pk = _rms(pk_ref[:, ks], kg)
                pv = pv_ref[:, ks]
            else:
                ps = slice(sb * WINDOW, (sb + 1) * WINDOW)
                pk = pk_ref[ps, ks]
                pv = pv_ref[ps, ks]
            if chain:
                if sb == nb - 1:
                    @pl.when(last_step)
                    def _():
                        kc_ref[:, ks] = kn
                        vc_ref[:, ks] = vn
            else:
                kc_ref[sb:sb + 1, ks] = kn[0:1, :]
                vc_ref[sb:sb + 1, ks] = vn[0:1, :]
            last[kv] = (kn, vn)
            keys = jnp.concatenate([pk, kn], axis=0).astype(BF16)
            vals = jnp.concatenate([pv, vn], axis=0).astype(BF16)
            h0 = kv * A_GROUP
            qs = jnp.concatenate(
                [_rms(q_ref[rs, (h0 + g) * A_HD:(h0 + g + 1) * A_HD], qg) for g in range(A_GROUP)],
                axis=0).astype(BF16)
            s = lax.dot_general(qs, keys, (((1,), (1,)), ((), ())),
                                preferred_element_type=F32) * (A_HD ** -0.5)
            s = jnp.where(mask, s, NEG_BIG)
            snk = jnp.full((nq, 1), sink_ref[h0 + A_GROUP - 1], F32)
            for g in range(A_GROUP - 2, -1, -1):
                snk = jnp.where(rgrp == g, sink_ref[h0 + g], snk)
            m = jnp.maximum(jnp.max(s, axis=-1, keepdims=True), snk)
            p = jnp.exp(s - m)
            den = jnp.sum(p, axis=-1, keepdims=True) + jnp.exp(snk - m)
            o = jnp.dot(p.astype(BF16), vals, preferred_element_type=F32) / den
            for g in range(A_GROUP):
                o_ref[rs, (h0 + g) * A_HD:(h0 + g + 1) * A_HD] = o[g * tq:(g + 1) * tq].astype(o_ref.dtype)


def _swa(proj, prev_k, prev_v, sinks, qn_g, kn_g, *, tq, nb, n_seq, steps_per_seq, row_block0,
         chain, layer=None):
    qcol, kcol, vcol = 0, 1024 // 256, 1280 // 256
    assert chain or steps_per_seq == 1
    rows = tq * nb

    def rb(s, n):
        return row_block0 + s * steps_per_seq + n

    if chain:
        def before(s, n):
            return jnp.maximum(rb(s, n) * nb - 1, rb(s, 0) * nb)
        prev_specs = [pl.BlockSpec((WINDOW, 256), lambda s, n: (before(s, n), kcol)),
                      pl.BlockSpec((WINDOW, 256), lambda s, n: (before(s, n), vcol))]
        cache_spec = pl.BlockSpec((None, WINDOW, 256), lambda s, n: (s, 0, 0))
        cache_shape = jax.ShapeDtypeStruct((n_seq, WINDOW, 256), F32)
    else:
        prev_specs = [pl.BlockSpec((None, WINDOW * nb, 256), lambda s, n: (layer, s, 0))] * 2
        cache_spec = pl.BlockSpec((nb, 256), lambda s, n: (s, 0))
        cache_shape = jax.ShapeDtypeStruct((n_seq, 256), F32)

    in_specs = [
        pl.BlockSpec(memory_space=pltpu.SMEM),
        pl.BlockSpec((rows, 1024), lambda s, n: (rb(s, n), qcol)),
        pl.BlockSpec((rows, 256), lambda s, n: (rb(s, n), kcol)),
        pl.BlockSpec((rows, 256), lambda s, n: (rb(s, n), vcol)),
        *prev_specs,
        pl.BlockSpec((1, A_HD), lambda s, n: (0, 0)),
        pl.BlockSpec((1, A_HD), lambda s, n: (0, 0)),
    ]
    args = [sinks, proj, proj, proj, prev_k, prev_v, qn_g.reshape(1, A_HD), kn_g.reshape(1, A_HD)]
    n_steps0 = n_seq if chain else n_seq // nb
    return pl.pallas_call(
        functools.partial(_swa_kernel, tq=tq, nb=nb, chain=chain),
        grid=(n_steps0, steps_per_seq),
        in_specs=in_specs,
        out_specs=[pl.BlockSpec((rows, 1024), lambda s, n: (s * steps_per_seq + n, 0)),
                   cache_spec, cache_spec],
        out_shape=[jax.ShapeDtypeStruct((n_steps0 * steps_per_seq * rows, 1024), BF16),
                   cache_shape, cache_shape],
        compiler_params=_cparams(("arbitrary", "arbitrary")),
        name="swa_prompt" if chain else "swa_sample",
    )(*args)


def _ret_tables(chunk, tv, pos0, t_len):
    lg = np.log1p(-(2.0 ** (-5.0 - np.arange(B_HEADS, dtype=np.float64))))
    idx = np.arange(chunk, dtype=np.float64)
    valid = idx < tv
    diff = idx[:, None] - idx[None, :]
    dmat = np.where((diff >= 0) & valid[:, None] & valid[None, :],
                    np.exp(np.maximum(diff, 0.0)[None] * lg[:, None, None]), 0.0)
    din = np.exp((idx + 1.0)[:, None] * lg[None])
    din = np.repeat(din, B_DV, axis=1)
    kfac = np.where(valid[:, None], np.exp((tv - 1.0 - idx)[:, None] * lg[None]), 0.0)
    kfac = np.repeat(kfac, B_DK, axis=1)
    gc = np.repeat(np.exp(tv * lg), B_DK)[:, None] * np.ones((1, B_DV))
    half = B_DK // 2
    inv = ROPE_BASE ** (-np.arange(half, dtype=np.float64) / half)
    ang = (pos0 + np.arange(t_len, dtype=np.float64))[:, None] * inv[None]
    cos = np.tile(np.cos(ang), (1, 2 * B_HEADS))
    sin = np.tile(np.concatenate([-np.sin(ang), np.sin(ang)], axis=1), (1, B_HEADS))
    f = lambda a: jnp.asarray(a, dtype=F32)
    return f(dmat), f(din), f(kfac), f(gc), f(cos), f(sin)


def _ret_kernel(q_ref, k_ref, v_ref, g_ref, cos_ref, sin_ref, dmat_ref, din_ref, kfac_ref,
                gc_ref, gng_ref, gnb_ref, s0_ref, *rest, nb, chunk):
    o_ref, s_ref = rest[-2], rest[-1]

    @pl.when(pl.program_id(1) == 0)
    def _():
        s_ref[...] = s0_ref[...]

    cos = cos_ref[...]
    sin = sin_ref[...]
    width = B_HEADS * B_DK
    lane = lax.broadcasted_iota(jnp.int32, (1, width), 1)
    first_half = (lane % B_DK) < (B_DK // 2)
    lane_head = lane // B_DK

    def rope(x):
        partner = jnp.where(first_half, pltpu.roll(x, width - B_DK // 2, 1),
                            pltpu.roll(x, B_DK // 2, 1))
        return x * cos + partner * sin

    gc = gc_ref[...]
    for sb in range(nb):
        rws = slice(sb * chunk, (sb + 1) * chunk)
        qr = rope(q_ref[rws, :])
        kr = rope(k_ref[rws, :]) * (B_DK ** -0.5)
        kb = kr.astype(BF16)
        kdec = (kr * kfac_ref[...]).astype(BF16)
        s_old = s_ref[sb]
        s_b = s_old.astype(BF16)
        for h in range(B_HEADS):
            vs = slice(h * B_DV, (h + 1) * B_DV)
            rs = slice(h * B_DK, (h + 1) * B_DK)
            qm = jnp.where(lane_head == h, qr, 0.0).astype(BF16)
            a = lax.dot_general(qm, kb, (((1,), (1,)), ((), ())),
                                preferred_element_type=F32) * dmat_ref[h]
            vh = v_ref[rws, vs].astype(BF16)
            o = jnp.dot(a.astype(BF16), vh, preferred_element_type=F32)
            o = o + jnp.dot(qm, s_b, preferred_element_type=F32) * din_ref[:, vs]
            u = lax.dot_general(kdec, vh, (((0,), (0,)), ((), ())), preferred_element_type=F32)
            s_ref[sb, rs, :] = gc[rs, :] * s_old[rs, :] + u[rs, :]
            mu = jnp.mean(o, axis=-1, keepdims=True)
            oc = o - mu
            var = jnp.mean(oc * oc, axis=-1, keepdims=True)
            y = oc * lax.rsqrt(var + EPS) * gng_ref[:, vs] + gnb_ref[:, vs]
            y = y * _silu(g_ref[rws, vs])
            o_ref[rws, vs] = y.astype(o_ref.dtype)


def _ret(proj, s0, gn_g, gn_b, tables, *, chunk, nb, n_seq, chunks_per_seq, row_block0,
         table_per_chunk):
    dmat, din, kfac, gc, cos, sin = tables
    qcol, kcol, vcol, gcol = 1536 // 256, 1792 // 256, 2048 // 512, 2560 // 512
    assert nb == 1 or chunks_per_seq == 1
    rows = nb * chunk

    def rb(s, c):
        return row_block0 + s * chunks_per_seq + c

    tmap = (lambda s, c: (c, 0)) if table_per_chunk else (lambda s, c: (0, 0))
    const2 = lambda s, c: (0, 0)
    in_specs = [
        pl.BlockSpec((rows, 256), lambda s, c: (rb(s, c), qcol)),
        pl.BlockSpec((rows, 256), lambda s, c: (rb(s, c), kcol)),
        pl.BlockSpec((rows, 512), lambda s, c: (rb(s, c), vcol)),
        pl.BlockSpec((rows, 512), lambda s, c: (rb(s, c), gcol)),
        pl.BlockSpec((chunk, 256), tmap),
        pl.BlockSpec((chunk, 256), tmap),
        pl.BlockSpec((B_HEADS, chunk, chunk), lambda s, c: (0, 0, 0)),
        pl.BlockSpec((chunk, 512), const2),
        pl.BlockSpec((chunk, 256), const2),
        pl.BlockSpec((256, B_DV), const2),
        pl.BlockSpec((1, 512), const2),
        pl.BlockSpec((1, 512), const2),
        pl.BlockSpec((nb, 256, B_DV), lambda s, c: (s, 0, 0)),
    ]
    args = [proj, proj, proj, proj, cos, sin, dmat, din, kfac, gc,
            gn_g.reshape(1, 512), gn_b.reshape(1, 512), s0.reshape(n_seq, 256, B_DV)]
    o, s_new = pl.pallas_call(
        functools.partial(_ret_kernel, nb=nb, chunk=chunk),
        grid=(n_seq // nb, chunks_per_seq),
        in_specs=in_specs,
        out_specs=[
            pl.BlockSpec((rows, 512), lambda s, c: (s * chunks_per_seq + c, 0)),
            pl.BlockSpec((nb, 256, B_DV), lambda s, c: (s, 0, 0)),
        ],
        out_shape=[
            jax.ShapeDtypeStruct((n_seq // nb * chunks_per_seq * rows, 512), BF16),
            jax.ShapeDtypeStruct((n_seq, 256, B_DV), F32),
        ],
        compiler_params=_cparams(("arbitrary", "arbitrary")),
        name="ret_prompt" if nb == 1 else "ret_sample",
    )(*args)
    return o, s_new.reshape(n_seq, B_HEADS, B_DK, B_DV)


HG_FAST = 32
HG_LIM = 60.0
HG_HDR = 32


def _hgrn_kernel(q_ref, f_ref, i_ref, g_ref, lb_ref, ng_ref, st0_ref, *rest, rows, seq_rows, tv):
    use_fast = seq_rows % HG_FAST == 0
    if use_fast:
        o_ref, st_ref, q_s, k_s, cs_s, qd_s, ki_s, kd_s, eb_s, stb_s = rest[-10:]
    else:
        o_ref, st_ref, q_s, k_s, cs_s = rest[-5:]
    nseq = rows // seq_rows

    @pl.when(pl.program_id(1) == 0)
    def _():
        st_ref[...] = st0_ref[...]

    z = f_ref[...]
    lb = lb_ref[...]
    sig = _sigmoid(z)
    lf = jnp.log(lb + (1.0 - lb) * sig)
    k = (1.0 - lb) * _sigmoid(-z)
    if tv < seq_rows:
        rvalid = (lax.broadcasted_iota(jnp.int32, (rows, 1), 0) % seq_rows) < tv
        lf = jnp.where(rvalid, lf, 0.0)
        k = jnp.where(rvalid, k, 0.0)
    ri = lax.broadcasted_iota(jnp.int32, (rows, rows), 0)
    ci = lax.broadcasted_iota(jnp.int32, (rows, rows), 1)
    tri = jnp.where(ci <= ri, 1.0, 0.0).astype(BF16)
    cs = jnp.zeros((rows, C_HEADS * C_DK), F32)
    rem = lf
    for _ in range(3):
        part = rem.astype(BF16)
        cs = cs + jnp.dot(tri, part, preferred_element_type=F32)
        rem = rem - part.astype(F32)
    q_s[...] = _silu(q_ref[...])
    k_s[...] = k
    cs_s[0:HG_HDR, :] = jnp.zeros((HG_HDR, C_HEADS * C_DK), F32)
    cs_s[HG_HDR:HG_HDR + rows, :] = cs
    ng = ng_ref[...]

    nt_dims = (((1,), (1,)), ((), ()))
    tn_dims = (((0,), (0,)), ((), ()))

    def finish(o, rsl, hs):
        y = o * lax.rsqrt(jnp.mean(o * o, axis=-1, keepdims=True) + EPS) * ng
        return (y * _silu(g_ref[rsl, hs])).astype(o_ref.dtype)

    def pairwise_block(blk, carry):
        size = HG_BLK
        r0 = pl.multiple_of(blk * size, size)
        rsl = pl.ds(r0, size)
        sidx = 0 if nseq == 1 else blk // (seq_rows // size)
        rowi = lax.broadcasted_iota(jnp.int32, (size, 1), 0)
        for h in range(C_HEADS):
            hs = slice(h * C_DK, (h + 1) * C_DK)
            vsl = slice(h * C_DV, (h + 1) * C_DV)
            qb = q_s[rsl, hs]
            kb = k_s[rsl, hs]
            vb = i_ref[rsl, hs]
            prev = cs_s[pl.ds(HG_HDR + r0 - 8, 8), hs][7:8, :]
            bc = cs_s[pl.ds(HG_HDR + r0, size), hs] - prev
            blast = bc[size - 1:size, :]
            qd = (qb * jnp.exp(bc)).astype(BF16)
            kd = (kb * jnp.exp(blast - bc)).astype(BF16)
            st = st_ref[sidx, vsl, :]
            o = lax.dot_general(qd, st.astype(BF16), nt_dims, preferred_element_type=F32)
            for j in range(size):
                dec = jnp.exp(jnp.minimum(bc - bc[j:j + 1, :], 0.0))
                a = jnp.sum(qb * kb[j:j + 1, :] * dec, axis=-1, keepdims=True)
                a = jnp.where(rowi >= j, a, 0.0)
                o = o + a * vb[j:j + 1, :]
            u = lax.dot_general(vb.astype(BF16), kd, tn_dims, preferred_element_type=F32)
            st_ref[sidx, vsl, :] = st * jnp.exp(blast) + u
            o_ref[rsl, vsl] = finish(o, rsl, hs)
        return carry

    def factorised_tile():
        nblk = rows // HG_FAST
        for b in range(nblk):
            r = slice(b * HG_FAST, (b + 1) * HG_FAST)
            lo = HG_HDR + b * HG_FAST
            prev = cs_s[lo - 8:lo, :][7:8, :]
            bc = cs_s[lo:lo + HG_FAST, :] - prev
            blast = bc[HG_FAST - 1:HG_FAST, :]
            qb = q_s[r, :]
            kb = k_s[r, :]
            qd_s[r, :] = (qb * jnp.exp(bc)).astype(BF16)
            ki_s[r, :] = (kb * jnp.exp(-bc)).astype(BF16)
            kd_s[r, :] = (kb * jnp.exp(blast - bc)).astype(BF16)
            eb_s[b:b + 1, :] = jnp.exp(blast)
        same = (ri // HG_FAST) == (ci // HG_FAST)
        keep = same & (ci <= ri)
        for h in range(C_HEADS):
            hs = slice(h * C_DK, (h + 1) * C_DK)
            vsl = slice(h * C_DV, (h + 1) * C_DV)
            vh = i_ref[:, hs].astype(BF16)
            st = st_ref[0, vsl, :]
            for b in range(nblk):
                r = slice(b * HG_FAST, (b + 1) * HG_FAST)
                stb_s[h, b] = st.astype(BF16)
                u = lax.dot_general(vh[r, :], kd_s[r, hs], tn_dims, preferred_element_type=F32)
                st = st * eb_s[b:b + 1, hs] + u
            st_ref[0, vsl, :] = st
            a = lax.dot_general(qd_s[:, hs], ki_s[:, hs], nt_dims, preferred_element_type=F32)
            a = jnp.where(keep, a, 0.0)
            o = jnp.dot(a.astype(BF16), vh, preferred_element_type=F32)
            inter = [lax.dot_general(qd_s[b * HG_FAST:(b + 1) * HG_FAST, hs], stb_s[h, b], nt_dims,
                                     preferred_element_type=F32) for b in range(nblk)]
            o = o + jnp.concatenate(inter, axis=0)
            o_ref[:, vsl] = finish(o, slice(None), hs)

    def pairwise_tile():
        lax.fori_loop(0, rows // HG_BLK, pairwise_block, 0)

    if use_fast:
        worst = jnp.max(cs_s[HG_HDR - HG_FAST:HG_HDR - HG_FAST + rows, :] - cs)
        bounded = worst < HG_LIM
        pl.when(bounded)(factorised_tile)
        pl.when(jnp.logical_not(bounded))(pairwise_tile)
    else:
        pairwise_tile()


def _hgrn(proj, st0, lb, ng, *, rows, seq_rows, tv, n_seq, tiles_per_seq, row_block0):
    qcol, fcol, icol, gcol = 3072 // 512, 3584 // 512, 4096 // 512, 4608 // 512
    nst = max(rows // seq_rows, 1)
    assert nst == 1 or tiles_per_seq == 1
    width = C_HEADS * C_DK
    scratch = [pltpu.VMEM((rows, width), F32), pltpu.VMEM((rows, width), F32),
               pltpu.VMEM((HG_HDR + rows, width), F32)]
    if seq_rows % HG_FAST == 0:
        scratch += [pltpu.VMEM((rows, width), BF16) for _ in range(3)]
        scratch += [pltpu.VMEM((rows // HG_FAST, width), F32),
                    pltpu.VMEM((C_HEADS, rows // HG_FAST, C_DV, C_DK), BF16)]

    def rb(s, c):
        return row_block0 + s * tiles_per_seq + c

    const2 = lambda s, c: (0, 0)
    in_specs = [
        pl.BlockSpec((rows, 512), lambda s, c: (rb(s, c), qcol)),
        pl.BlockSpec((rows, 512), lambda s, c: (rb(s, c), fcol)),
        pl.BlockSpec((rows, 512), lambda s, c: (rb(s, c), icol)),
        pl.BlockSpec((rows, 512), lambda s, c: (rb(s, c), gcol)),
        pl.BlockSpec((1, 512), const2),
        pl.BlockSpec((1, C_DV), const2),
        pl.BlockSpec((nst, 512, C_DK), lambda s, c: (s, 0, 0)),
    ]
    args = [proj, proj, proj, proj, lb.reshape(1, 512), ng.reshape(1, C_DV), st0]
    return pl.pallas_call(
        functools.partial(_hgrn_kernel, rows=rows, seq_rows=seq_rows, tv=tv),
        grid=(n_seq // nst, tiles_per_seq),
        in_specs=in_specs,
        out_specs=[
            pl.BlockSpec((rows, 512), lambda s, c: (s * tiles_per_seq + c, 0)),
            pl.BlockSpec((nst, 512, C_DK), lambda s, c: (s, 0, 0)),
        ],
        out_shape=[
            jax.ShapeDtypeStruct((n_seq // nst * tiles_per_seq * rows, 512), BF16),
            jax.ShapeDtypeStruct((n_seq, 512, C_DK), F32),
        ],
        scratch_shapes=scratch,
        compiler_params=_cparams(("arbitrary", "arbitrary")),
        name="hgrn_prompt" if nst == 1 else "hgrn_sample",
    )(*args)


def _mm_out_kernel(pa_ref, pb_ref, pc_ref, sa_ref, sb_ref, sc_ref, w_ref, x_ref, g_ref, o_ref, wb_ref):
    i = pl.program_id(1)

    @pl.when(i == 0)
    def _():
        wb_ref[...] = w_ref[...].astype(BF16)

    def mix(oa_ref, ob_ref, oc_ref):
        acc = jnp.dot(oa_ref[...], wb_ref[0:1024, :], preferred_element_type=F32)
        acc = acc + jnp.dot(ob_ref[...], wb_ref[1024:1536, :], preferred_element_type=F32)
        acc = acc + jnp.dot(oc_ref[...], wb_ref[1536:2048, :], preferred_element_type=F32)
        o_ref[...] = x_ref[...] + _tile_mod(g_ref, i) * acc

    @pl.when(i < PROMPT_TILES)
    def _():
        mix(pa_ref, pb_ref, pc_ref)

    @pl.when(i == PROMPT_TILES)
    def _():
        mix(sa_ref, sb_ref, sc_ref)


def _mm_out(prompt_mix, sample_mix, w_out, x, mod, l):
    tn = 1024
    ncol = D_MODEL // tn
    pmap = lambda j, i: (jnp.minimum(i, PROMPT_TILES - 1), 0)
    smap = lambda j, i: (0, 0)
    return pl.pallas_call(
        _mm_out_kernel,
        grid=(ncol, N_TILES),
        in_specs=[
            pl.BlockSpec((TM, 1024), pmap), pl.BlockSpec((TM, 512), pmap), pl.BlockSpec((TM, 512), pmap),
            pl.BlockSpec((TM, 1024), smap), pl.BlockSpec((TM, 512), smap), pl.BlockSpec((TM, 512), smap),
            pl.BlockSpec((None, D_MODEL, tn), lambda j, i: (l, 0, j)),
            pl.BlockSpec((TM, tn), lambda j, i: (i, j)),
            _mod_spec(l, "g1", tn, lambda j, i: j),
        ],
        out_specs=pl.BlockSpec((TM, tn), lambda j, i: (i, j)),
        out_shape=jax.ShapeDtypeStruct((N_ROWS, D_MODEL), F32),
        scratch_shapes=[pltpu.VMEM((D_MODEL, tn), BF16)],
        compiler_params=_cparams(("arbitrary", "arbitrary")),
        name="mm_out",
    )(*prompt_mix, *sample_mix, w_out, x, mod)


META_GSEL = 0
META_RANK = 1
META_GATE = 20
XW = D_MODEL + 128


def _route_kernel(x_ref, g_ref, sc_ref, sh_ref, whl_ref, wh_ref, br_ref, hx_ref, meta_ref, cnt_ref,
                  run_ref, sc_st, sh_st, hi_s, lo_s):
    i = pl.program_id(0)

    @pl.when(i == 0)
    def _():
        run_ref[...] = jnp.zeros_like(run_ref)

    _stage_mod(sc_ref, sc_st, i)
    _stage_mod(sh_ref, sh_st, i)
    lg = g_ref[...]
    for s in range(N_SLABS):
        r = _slab(s)
        h = _norm_mod_slab(x_ref[r, :], lg, sc_st[s:s + 1, :], sh_st[s:s + 1, :])
        hx_ref[r, 0:D_MODEL] = h
        hi = h.astype(BF16)
        hi_s[r, :] = hi
        lo_s[r, :] = (h - hi.astype(F32)).astype(BF16)
    l1 = jnp.dot(hi_s[...], whl_ref[...], preferred_element_type=F32)
    l2 = jnp.dot(lo_s[...], wh_ref[...], preferred_element_type=F32)
    logits = l1[:, 0:128] + l1[:, 128:256] + l2 + br_ref[...]
    col = lax.broadcasted_iota(jnp.int32, logits.shape, 1).astype(F32)
    big = 1e9
    is_g = col < N_GROUPS
    gl = jnp.where(is_g, logits, -jnp.inf)
    gmax = jnp.max(gl, axis=-1, keepdims=True)
    gsel = jnp.min(jnp.where(gl == gmax, col, big), axis=-1, keepdims=True)
    gden = jnp.sum(jnp.where(is_g, jnp.exp(logits - gmax), 0.0), axis=-1, keepdims=True)
    g_w = 1.0 / gden
    e0 = N_GROUPS + EXP_PER_GROUP * gsel
    in_grp = (col >= e0) & (col < e0 + EXP_PER_GROUP)
    el = jnp.where(in_grp, logits, -jnp.inf)
    v1 = jnp.max(el, axis=-1, keepdims=True)
    i1 = jnp.min(jnp.where(el == v1, col, big), axis=-1, keepdims=True)
    el2 = jnp.where(col == i1, -jnp.inf, el)
    v2 = jnp.max(el2, axis=-1, keepdims=True)
    i2 = jnp.min(jnp.where(el2 == v2, col, big), axis=-1, keepdims=True)
    t = jnp.exp(v2 - v1)
    p1 = 1.0 / (1.0 + t)
    a1 = p1 * g_w
    a2 = t * p1 * g_w
    meta = jnp.where(col == META_GATE + (i1 - e0), a1, 0.0)
    meta = meta + jnp.where(col == META_GATE + (i2 - e0), a2, 0.0)
    meta = meta + jnp.where(col == META_GSEL, gsel, 0.0)
    rows = logits.shape[0]
    rowi = lax.broadcasted_iota(jnp.int32, (rows, 1), 0)
    is_tok = (i < PROMPT_TILES) | (rowi % SROWS == 0)
    onehot = jnp.where((col == gsel) & is_tok, 1.0, 0.0)
    ri = lax.broadcasted_iota(jnp.int32, (rows, rows), 0)
    ci = lax.broadcasted_iota(jnp.int32, (rows, rows), 1)
    tri = jnp.where(ci <= ri, 1.0, 0.0).astype(BF16)
    incl = jnp.dot(tri, onehot.astype(BF16), preferred_element_type=F32)
    run = run_ref[...]
    rank = jnp.sum(jnp.where(col == gsel, incl + run - 1.0, 0.0), axis=-1, keepdims=True)
    meta = meta + jnp.where(col == META_RANK, rank, 0.0)
    meta_ref[...] = meta
    hx_ref[:, D_MODEL:XW] = meta
    run = run + incl[rows - 1:rows, :]
    run_ref[...] = run
    cnt_ref[...] = run


def _router_weights(w_rg, b_rg, w_re, b_re):
    pad = 128 - N_GROUPS - N_EXPERTS
    wr = jnp.concatenate([w_rg, w_re, jnp.zeros((DEPTH, D_MODEL, pad), F32)], axis=2)
    w_hi = wr.astype(BF16)
    w_lo = (wr - w_hi.astype(F32)).astype(BF16)
    br = jnp.concatenate([b_rg, b_re, jnp.zeros((DEPTH, pad), F32)], axis=1).reshape(DEPTH, 1, 128)
    return jnp.concatenate([w_hi, w_lo], axis=2), w_hi, br


def _route(x, g, mod, whl, w_hi, br, l):
    return pl.pallas_call(
        _route_kernel,
        grid=(N_TILES,),
        in_specs=[
            pl.BlockSpec((TM, D_MODEL), lambda i: (i, 0)),
            pl.BlockSpec((1, D_MODEL), lambda i: (0, 0)),
            _mod_spec(l, "sc2"),
            _mod_spec(l, "sh2"),
            pl.BlockSpec((None, D_MODEL, 256), lambda i: (l, 0, 0)),
            pl.BlockSpec((None, D_MODEL, 128), lambda i: (l, 0, 0)),
            pl.BlockSpec((None, 1, 128), lambda i: (l, 0, 0)),
        ],
        out_specs=[
            pl.BlockSpec((TM, XW), lambda i: (i, 0)),
            pl.BlockSpec((TM, 128), lambda i: (i, 0)),
            pl.BlockSpec((1, 128), lambda i: (0, 0)),
        ],
        out_shape=[
            jax.ShapeDtypeStruct((N_ROWS, XW), F32),
            jax.ShapeDtypeStruct((N_ROWS, 128), F32),
            jax.ShapeDtypeStruct((1, 128), F32),
        ],
        scratch_shapes=[pltpu.VMEM((1, 128), F32),
                        pltpu.VMEM((N_SLABS, D_MODEL), F32), pltpu.VMEM((N_SLABS, D_MODEL), F32),
                        pltpu.VMEM((TM, D_MODEL), BF16), pltpu.VMEM((TM, D_MODEL), BF16)],
        compiler_params=_cparams(("arbitrary",)),
        name="route",
    )(x, g.reshape(1, D_MODEL), mod, mod, whl, w_hi, br)


def _moe_positions(meta, cnt):
    i32 = jnp.int32
    counts = cnt[0, :N_GROUPS].astype(i32)
    ntile = (counts + TM - 1) // TM
    tstart = jnp.cumsum(ntile) - ntile
    gsel = meta[:, META_GSEL].astype(i32)
    rank = meta[:, META_RANK].astype(i32)
    first_row = jnp.zeros_like(gsel)
    for g in range(N_GROUPS):
        first_row = jnp.where(gsel == g, tstart[g] * TM, first_row)
    pos = first_row + rank
    pos_all = jnp.concatenate([pos[:NP_ROWS], jnp.repeat(pos[NP_ROWS::SROWS], SROWS)])
    plan = jnp.concatenate([tstart, ntile]).astype(i32)
    return pos_all, plan


def _scatter_kernel(pos_ref, src_ref, buf_hbm, out_hbm, sem):
    del buf_hbm
    i = pl.program_id(0)
    base = i * TM

    def row_copy(r):
        return pltpu.make_async_copy(src_ref.at[pl.ds(r, 1), :],
                                     out_hbm.at[pl.ds(pos_ref[base + r], 1), :], sem)

    def run(n, stride):
        def issue(k, c):
            row_copy(k * stride).start()
            return c

        def drain(k, c):
            row_copy(k * stride).wait()
            return c

        lax.fori_loop(0, n, issue, 0, unroll=8)
        lax.fori_loop(0, n, drain, 0, unroll=8)

    @pl.when(i < PROMPT_TILES)
    def _():
        run(TM, 1)

    @pl.when(i == PROMPT_TILES)
    def _():
        run(DEC_BATCH, SROWS)


def _moe_scatter(hx, buf, pos_all):
    grid_spec = pltpu.PrefetchScalarGridSpec(
        num_scalar_prefetch=1,
        grid=(N_TILES,),
        in_specs=[pl.BlockSpec((TM, XW), lambda i, pos: (i, 0)), pl.BlockSpec(memory_space=pl.ANY)],
        out_specs=pl.BlockSpec(memory_space=pl.ANY),
        scratch_shapes=[pltpu.SemaphoreType.DMA(())],
    )
    return pl.pallas_call(
        _scatter_kernel,
        grid_spec=grid_spec,
        out_shape=jax.ShapeDtypeStruct((MOE_ROWS, XW), F32),
        input_output_aliases={2: 0},
        compiler_params=_cparams(("arbitrary",)),
        name="moe_scatter",
    )(pos_all, hx, buf)


def _work_item(w, plan_ref, per_tile):
    i32 = jnp.int32
    nt = [plan_ref[N_GROUPS + g] for g in range(N_GROUPS)]
    ts = [plan_ref[g] for g in range(N_GROUPS)]
    ends = []
    acc = 0
    for g in range(N_GROUPS):
        acc = acc + per_tile * nt[g]
        ends.append(acc)
    valid = w < ends[-1]
    wc = jnp.maximum(jnp.minimum(w, ends[-1] - 1), 0)
    g = sum((wc >= ends[k]).astype(i32) for k in range(N_GROUPS - 1))

    def pick(vals):
        out = vals[N_GROUPS - 1]
        for k in range(N_GROUPS - 2, -1, -1):
            out = jnp.where(g == k, vals[k], out)
        return out

    r = wc - pick([0] + ends[:-1])
    ntg = pick(nt)
    c = sum((r >= m * ntg).astype(i32) for m in range(1, per_tile))
    t_in = r - c * ntg
    t = pick(ts) + t_in
    extra = jnp.maximum(w - ends[-1], 0)
    used = ts[N_GROUPS - 1] + nt[N_GROUPS - 1]
    t_out = jnp.where(valid, t, used + extra // per_tile)
    c_out = jnp.where(valid, c, extra % per_tile)
    return g, c, t, t_in == 0, valid, t_out, c_out


HID_EXPERTS = 2
HID_NCOL = EXP_PER_GROUP // HID_EXPERTS
HID_WORK = MOE_TILES * HID_NCOL


def _hidden_kernel(plan_ref, x_ref, w1_ref, w3_ref, o_ref, w1b, w3b):
    _, c, _, first, valid, _, _ = _work_item(pl.program_id(0), plan_ref, HID_NCOL)

    @pl.when(jnp.logical_not(valid))
    def _():
        o_ref[...] = jnp.zeros_like(o_ref)

    @pl.when(valid & first)
    def _():
        for j in range(HID_EXPERTS):
            cs = slice(j * D_EXPERT, (j + 1) * D_EXPERT)
            w1b[:, cs] = w1_ref[j].astype(BF16)
            w3b[:, cs] = w3_ref[j].astype(BF16)

    @pl.when(valid)
    def _():
        x = x_ref[:, 0:D_MODEL].astype(BF16)
        m = x_ref[:, D_MODEL:XW]
        col = lax.broadcasted_iota(jnp.int32, m.shape, 1)
        for j in range(HID_EXPERTS):
            cs = slice(j * D_EXPERT, (j + 1) * D_EXPERT)
            a = jnp.dot(x, w1b[:, cs], preferred_element_type=F32)
            b = jnp.dot(x, w3b[:, cs], preferred_element_type=F32)
            lane = META_GATE + c * HID_EXPERTS + j
            gate = jnp.sum(jnp.where(col == lane, m, 0.0), axis=-1, keepdims=True)
            o_ref[:, cs] = (_silu(a) * b * gate).astype(o_ref.dtype)


def _moe_hidden(xs, w1, w3, plan, l):
    def item(w, plan):
        return _work_item(w, plan, HID_NCOL)

    def w_map(w, plan):
        it = item(w, plan)
        return (l, it[0] * HID_NCOL + it[1], 0, 0)

    wide = HID_EXPERTS * D_EXPERT
    grid_spec = pltpu.PrefetchScalarGridSpec(
        num_scalar_prefetch=1,
        grid=(HID_WORK,),
        in_specs=[
            pl.BlockSpec((TM, XW), lambda w, plan: (item(w, plan)[2], 0)),
            pl.BlockSpec((None, HID_EXPERTS, D_MODEL, D_EXPERT), w_map),
            pl.BlockSpec((None, HID_EXPERTS, D_MODEL, D_EXPERT), w_map),
        ],
        out_specs=pl.BlockSpec((TM, wide), lambda w, plan: item(w, plan)[5:7]),
        scratch_shapes=[pltpu.VMEM((D_MODEL, wide), BF16), pltpu.VMEM((D_MODEL, wide), BF16)],
    )
    return pl.pallas_call(
        _hidden_kernel,
        grid_spec=grid_spec,
        out_shape=jax.ShapeDtypeStruct((MOE_ROWS, EXP_PER_GROUP * D_EXPERT), BF16),
        compiler_params=_cparams(("arbitrary",), BIG_VMEM_LIMIT),
        name="moe_hidden",
    )(plan, xs, w1, w3)


def _down_kernel(plan_ref, h_ref, w_ref, o_ref, wb):
    _, _, _, first, valid, _, _ = _work_item(pl.program_id(0), plan_ref, DOWN_NCOL)

    @pl.when(jnp.logical_not(valid))
    def _():
        o_ref[...] = jnp.zeros_like(o_ref)

    @pl.when(valid & first)
    def _():
        wb[...] = w_ref[...].astype(BF16)

    @pl.when(valid)
    def _():
        o_ref[...] = jnp.dot(h_ref[...], wb[...], preferred_element_type=F32)


def _moe_down(hid, w2g, plan, l):
    kdim = EXP_PER_GROUP * D_EXPERT

    def item(w, plan):
        return _work_item(w, plan, DOWN_NCOL)

    grid_spec = pltpu.PrefetchScalarGridSpec(
        num_scalar_prefetch=1,
        grid=(DOWN_WORK,),
        in_specs=[
            pl.BlockSpec((TM, kdim), lambda w, plan: (item(w, plan)[2], 0)),
            pl.BlockSpec((None, None, kdim, DOWN_TN),
                         lambda w, plan: (l, item(w, plan)[0], 0, item(w, plan)[1])),
        ],
        out_specs=pl.BlockSpec((TM, DOWN_TN), lambda w, plan: item(w, plan)[5:7]),
        scratch_shapes=[pltpu.VMEM((kdim, DOWN_TN), BF16)],
    )
    return pl.pallas_call(
        _down_kernel,
        grid_spec=grid_spec,
        out_shape=jax.ShapeDtypeStruct((MOE_ROWS, D_MODEL), F32),
        compiler_params=_cparams(("arbitrary",), BIG_VMEM_LIMIT),
        name="moe_down",
    )(plan, hid, w2g)


def _res_kernel(pos_ref, x_ref, y_hbm, g2_ref, *rest, with_norm):
    if with_norm:
        lg_ref, sc_ref, sh_ref, xo_ref, ho_ref, ybuf, sem, g2_st, sc_st, sh_st = rest
    else:
        yp_ref, ys_ref, ybuf, sem, g2_st = rest
    i = pl.program_id(0)
    slot = i % 2

    def row_copy(tile, k, sl):
        return pltpu.make_async_copy(y_hbm.at[pl.ds(pos_ref[tile * TM + k], 1), :],
                                     ybuf.at[sl, pl.ds(k, 1), :], sem.at[sl])

    def issue_tile(tile, sl):
        def issue(k, c):
            row_copy(tile, k, sl).start()
            return c
        lax.fori_loop(0, TM, issue, 0, unroll=8)

    @pl.when(i == 0)
    def _():
        issue_tile(0, 0)

    @pl.when(i + 1 < N_TILES)
    def _():
        issue_tile(i + 1, 1 - slot)

    _stage_mod(g2_ref, g2_st, i)
    if with_norm:
        _stage_mod(sc_ref, sc_st, i)
        _stage_mod(sh_ref, sh_st, i)

    def drain(k, c):
        row_copy(i, k, slot).wait()
        return c

    lax.fori_loop(0, TM, drain, 0, unroll=8)

    def new_x(s):
        r = _slab(s)
        return x_ref[r, :] + g2_st[s:s + 1, :] * ybuf[slot, r, :]

    if with_norm:
        lg = lg_ref[...]
        for s in range(N_SLABS):
            xn = new_x(s)
            xo_ref[_slab(s), :] = xn
            hn = _norm_mod_slab(xn, lg, sc_st[s:s + 1, :], sh_st[s:s + 1, :])
            ho_ref[_slab(s), :] = hn.astype(ho_ref.dtype)
    else:
        @pl.when(i < PROMPT_TILES)
        def _():
            for s in range(N_SLABS):
                yp_ref[_slab(s), :] = new_x(s)

        @pl.when(i == PROMPT_TILES)
        def _():
            for s in range(N_SLABS):
                ys_ref[s:s + 1, :] = new_x(s)[0:1, :]


def _residual(x, y_sorted, pos_all, mod, l, next_g=None):
    with_norm = next_g is not None
    row_spec = pl.BlockSpec((TM, D_MODEL), lambda i, pos: (i, 0))
    in_specs = [row_spec, pl.BlockSpec(memory_space=pl.ANY), _mod_spec(l, "g2")]
    args = [pos_all, x, y_sorted, mod]
    stage = pltpu.VMEM((N_SLABS, D_MODEL), F32)
    scratch = [pltpu.VMEM((2, TM, D_MODEL), F32), pltpu.SemaphoreType.DMA((2,)), stage]
    if with_norm:
        in_specs += [pl.BlockSpec((1, D_MODEL), lambda i, pos: (0, 0)),
                     _mod_spec(l + 1, "sc1"), _mod_spec(l + 1, "sh1")]
        args += [next_g.reshape(1, D_MODEL), mod, mod]
        out_specs = [row_spec, row_spec]
        out_shape = [jax.ShapeDtypeStruct((N_ROWS, D_MODEL), F32),
                     jax.ShapeDtypeStruct((N_ROWS, D_MODEL), BF16)]
        scratch += [stage, stage]
    else:
        out_specs = [pl.BlockSpec((TM, D_MODEL), lambda i, pos: (jnp.minimum(i, PROMPT_TILES - 1), 0)),
                     pl.BlockSpec((DEC_BATCH, D_MODEL), lambda i, pos: (0, 0))]
        out_shape = [jax.ShapeDtypeStruct((NP_ROWS, D_MODEL), F32),
                     jax.ShapeDtypeStruct((DEC_BATCH, D_MODEL), F32)]
    grid_spec = pltpu.PrefetchScalarGridSpec(
        num_scalar_prefetch=1,
        grid=(N_TILES,),
        in_specs=in_specs,
        out_specs=out_specs,
        scratch_shapes=scratch,
    )
    return pl.pallas_call(
        functools.partial(_res_kernel, with_norm=with_norm),
        grid_spec=grid_spec,
        out_shape=out_shape,
        compiler_params=_cparams(("arbitrary",)),
        name="residual_norm" if with_norm else "residual",
    )(*args)


def kernel(x_prompt, x_sample, c_prompt, c_sample, cache_swa_k, cache_swa_v, state_ret, state_hgrn,
           w_ada, b_ada, ln1_g, w_in, qn_g, kn_g, attn_sinks, ret_gn_g, ret_gn_b, hgrn_lb, hgrn_ng,
           w_out, ln2_g, w_rg, b_rg, w_re, b_re, w1, w3, w2):
    i32 = jnp.int32
    lbp = jax.nn.softmax(hgrn_lb.astype(F32), axis=0)
    lb_all = jnp.cumsum(lbp, axis=0) - lbp[0]

    assert TM // SROWS == DEC_BATCH and NS_ROWS == TM
    c_all = jnp.concatenate([c_prompt, jnp.zeros((C_SAMPLE0 - BATCH, D_MODEL), F32), c_sample], axis=0)
    mod = _ada(c_all, w_ada, b_ada)

    ret_tab_p = _ret_tables(RET_CHUNK, RET_CHUNK, 0.0, SEQ)
    ret_tab_s = _ret_tables(SROWS, 1, float(PAST_LEN), SROWS)
    w2g = w2.reshape(DEPTH, N_GROUPS, EXP_PER_GROUP * D_EXPERT, D_MODEL)
    whl_all, whi_all, br_all = _router_weights(w_rg, b_rg, w_re, b_re)
    st_s_all = jnp.swapaxes(state_hgrn, -1, -2).reshape(DEPTH, DEC_BATCH, C_HEADS * C_DV, C_DK)
    ck_all = cache_swa_k.reshape(DEPTH, DEC_BATCH * WINDOW, A_KV * A_HD)
    cv_all = cache_swa_v.reshape(DEPTH, DEC_BATCH * WINDOW, A_KV * A_HD)
    xs_buf = jnp.zeros((MOE_ROWS, XW), F32)

    x, h = _norm_mod_first(x_prompt.reshape(NP_ROWS, D_MODEL), x_sample.reshape(DEC_BATCH, D_MODEL),
                           ln1_g[0], mod, 0)

    outs = {k: [] for k in ("kp", "vp", "rp", "hp", "ks", "vs", "rs", "hs")}
    for l in range(DEPTH):
        proj = _mm_in(h, w_in, l)

        oa_p, kp, vp = _swa(proj, proj, proj, attn_sinks[l], qn_g[l], kn_g[l], tq=WINDOW, nb=SWA_NB,
                            n_seq=BATCH, steps_per_seq=SEQ // (WINDOW * SWA_NB), row_block0=0, chain=True)
        srows = SAMPLE_NB * SROWS
        oa_s, ks, vs = _swa(proj, ck_all, cv_all, attn_sinks[l], qn_g[l], kn_g[l], tq=SROWS, nb=SAMPLE_NB,
                            n_seq=DEC_BATCH, steps_per_seq=1, row_block0=NP_ROWS // srows, chain=False,
                            layer=l)
        zero_ret = jnp.zeros((BATCH, B_HEADS, B_DK, B_DV), F32)
        ob_p, rp = _ret(proj, zero_ret, ret_gn_g[l], ret_gn_b[l], ret_tab_p, chunk=RET_CHUNK, nb=1,
                        n_seq=BATCH, chunks_per_seq=SEQ // RET_CHUNK, row_block0=0, table_per_chunk=True)
        ob_s, rs = _ret(proj, state_ret[l], ret_gn_g[l], ret_gn_b[l], ret_tab_s, chunk=SROWS, nb=SAMPLE_NB,
                        n_seq=DEC_BATCH, chunks_per_seq=1, row_block0=NP_ROWS // srows,
                        table_per_chunk=False)
        zero_hg = jnp.zeros((BATCH, C_HEADS * C_DV, C_DK), F32)
        oc_p, hp_t = _hgrn(proj, zero_hg, lb_all[l], hgrn_ng[l], rows=HG_TILE, seq_rows=HG_TILE,
                           tv=HG_TILE, n_seq=BATCH, tiles_per_seq=SEQ // HG_TILE, row_block0=0)
        oc_s, hs_t = _hgrn(proj, st_s_all[l], lb_all[l], hgrn_ng[l], rows=srows, seq_rows=SROWS, tv=1,
                           n_seq=DEC_BATCH, tiles_per_seq=1, row_block0=NP_ROWS // srows)

        x1 = _mm_out((oa_p, ob_p, oc_p), (oa_s, ob_s, oc_s), w_out, x, mod, l)

        hx, meta, cnt = _route(x1, ln2_g[l], mod, whl_all, whi_all, br_all, l)
        pos_all, plan = _moe_positions(meta, cnt)
        xs_buf = _moe_scatter(hx, xs_buf, pos_all)
        hid = _moe_hidden(xs_buf, w1, w3, plan, l)
        y_sorted = _moe_down(hid, w2g, plan, l)
        if l + 1 < DEPTH:
            x, h = _residual(x1, y_sorted, pos_all, mod, l, ln1_g[l + 1])
        else:
            y_prompt, y_sample = _residual(x1, y_sorted, pos_all, mod, l)

        outs["kp"].append(kp.reshape(BATCH, WINDOW, A_KV, A_HD))
        outs["vp"].append(vp.reshape(BATCH, WINDOW, A_KV, A_HD))
        outs["rp"].append(rp)
        outs["hp"].append(hp_t.reshape(BATCH, C_HEADS, C_DV, C_DK))
        outs["ks"].append(ks.reshape(DEC_BATCH, 1, A_KV, A_HD))
        outs["vs"].append(vs.reshape(DEC_BATCH, 1, A_KV, A_HD))
        outs["rs"].append(rs)
        outs["hs"].append(hs_t.reshape(DEC_BATCH, C_HEADS, C_DV, C_DK))

    y_prompt = y_prompt.reshape(BATCH, SEQ, D_MODEL)
    y_sample = y_sample.reshape(DEC_BATCH, 1, D_MODEL)
    st = lambda k: jnp.stack(outs[k])
    un_t = lambda k: jnp.swapaxes(st(k), -1, -2)
    roll_in = lambda cache, k: jnp.concatenate([cache[:, :, 1:], st(k)], axis=2)
    return (y_prompt, y_sample, st("kp"), st("vp"), st("rp"), un_t("hp"),
            roll_in(cache_swa_k, "ks"), roll_in(cache_swa_v, "vs"), st("rs"), un_t("hs"))
```
